```python
import jax, jax.numpy as jnp
from jax import lax
import numpy as np

D_MODEL = 4096
BATCH = 8
SEQ = 4096
DEPTH = 1

CHUNK = 64
Q_BLOCK = 128
PLE_DIM = 256
RET_HEADS = 8
RET_HEAD_DIM = D_MODEL // (2 * RET_HEADS)
RET_WIDTH = RET_HEADS * RET_HEAD_DIM
MLA_HEADS = 16
MLA_NOPE = 128
MLA_ROPE = 64
MLA_V = (D_MODEL - RET_WIDTH) // MLA_HEADS
Q_LORA = D_MODEL // 4
KV_LORA = D_MODEL // 8
D_FF = ((8 * D_MODEL // 3 + 255) // 256) * 256
CONV_WIDTH = 3
ROPE_BASE = 10000.0
EPS = 1e-6
IN_WIDTH = 4 * RET_WIDTH + Q_LORA + KV_LORA + MLA_ROPE
IN_SPLITS = (RET_WIDTH, 2 * RET_WIDTH, 3 * RET_WIDTH, 4 * RET_WIDTH,
             4 * RET_WIDTH + Q_LORA, 4 * RET_WIDTH + Q_LORA + KV_LORA)

kernel_name = "hybrid_retention_mla_convffn_ple"


def rms_norm(x, g):
    xf = x.astype(jnp.float32)
    y = xf * lax.rsqrt(jnp.mean(xf * xf, axis=-1, keepdims=True) + EPS)
    return (y * g.astype(jnp.float32)).astype(x.dtype)


def head_norm(o):
    of = o.astype(jnp.float32)
    mu = jnp.mean(of, axis=-1, keepdims=True)
    var = jnp.mean(jnp.square(of - mu), axis=-1, keepdims=True)
    return ((of - mu) * lax.rsqrt(var + EPS)).astype(o.dtype)


def rope_tables(seq, dim):
    inv = 1.0 / (ROPE_BASE ** (jnp.arange(0, dim, 2, dtype=jnp.float32) / dim))
    ang = jnp.arange(seq, dtype=jnp.float32)[:, None] * inv[None, :]
    return jnp.cos(ang), jnp.sin(ang)


def apply_rope(x, cos, sin):
    x1, x2 = jnp.split(x, 2, axis=-1)
    c = cos[None, :, None, :].astype(x.dtype)
    s = sin[None, :, None, :].astype(x.dtype)
    return jnp.concatenate([x1 * c - x2 * s, x2 * c + x1 * s], axis=-1)


def retention(q, k, v):
    B, S, H, dk = q.shape
    dv = v.shape[-1]
    nC = S // CHUNK
    dt = q.dtype
    log_g = jnp.log1p(-jnp.exp2(-5.0 - jnp.arange(H, dtype=jnp.float32)))
    idx = jnp.arange(CHUNK, dtype=jnp.float32)
    inner = jnp.exp(log_g[:, None, None] * jnp.abs(idx[:, None] - idx[None, :])).astype(dt)
    q_dec = jnp.exp(log_g[:, None] * (idx + 1.0))[..., None].astype(dt)
    k_dec = jnp.exp(log_g[:, None] * (CHUNK - 1.0 - idx))[..., None].astype(dt)
    s_dec = jnp.exp(log_g * CHUNK)[:, None, None].astype(dt)
    k = k * (dk ** -0.5)

    def to_chunks(t):
        return t.reshape(B, nC, CHUNK, H, t.shape[-1]).transpose(1, 0, 3, 2, 4)

    def step(state, qkv):
        qc, kc, vc = qkv
        scores = jnp.einsum('bhnd,bhmd->bhnm', qc, kc) * inner
        out = (jnp.einsum('bhnm,bhmv->bhnv', scores, vc)
               + jnp.einsum('bhnd,bhdv->bhnv', qc * q_dec, state))
        state = state * s_dec + jnp.einsum('bhmd,bhmv->bhdv', kc * k_dec, vc)
        return state, out

    s0 = jnp.zeros((B, H, dk, dv), dt)
    _, out = lax.scan(step, s0, (to_chunks(q), to_chunks(k), to_chunks(v)))
    return out.transpose(1, 0, 3, 2, 4).reshape(B, S, H, dv)


def mla_attention(q_nope, q_rope, k_nope, k_rope, v):
    B, S, H, _ = q_nope.shape
    nQ = S // Q_BLOCK
    scale = (MLA_NOPE + MLA_ROPE) ** -0.5
    key_chunk = jnp.arange(S) // CHUNK

    def blocks(t):
        return t.reshape(B, nQ, Q_BLOCK, *t.shape[2:]).swapaxes(0, 1)

    def one_block(args):
        qn, qr, b = args
        s = (jnp.einsum('bqhd,bkhd->bhqk', qn, k_nope)
             + jnp.einsum('bqhr,bkr->bhqk', qr, k_rope)).astype(jnp.float32) * scale
        q_chunk = (b * Q_BLOCK + jnp.arange(Q_BLOCK)) // CHUNK
        mask = key_chunk[None, :] <= q_chunk[:, None]
        s = jnp.where(mask[None, None], s, -jnp.inf)
        w = jax.nn.softmax(s, axis=-1).astype(v.dtype)
        return jnp.einsum('bhqk,bkhv->bqhv', w, v)

    out = lax.map(one_block, (blocks(q_nope), blocks(q_rope), jnp.arange(nQ)))
    return out.swapaxes(0, 1).reshape(B, S, H, v.shape[-1])


def causal_dwconv(h, w, b):
    S = h.shape[1]
    hp = jnp.pad(h, ((0, 0), (CONV_WIDTH - 1, 0), (0, 0)))
    out = b
    for j in range(CONV_WIDTH):
        out = out + hp[:, j:j + S, :] * w[j]
    return out


def _fwd_setup_inputs(seed: int = 0) -> dict:
    key = jax.random.key(seed)
    ks = jax.random.split(key, 20)

    def nrm(k, shape, fan_in):
        return jax.random.normal(k, shape, jnp.float32) * (fan_in ** -0.5)

    def gain(k, shape):
        return 1.0 + 0.01 * jax.random.normal(k, shape, jnp.float32)

    L = DEPTH
    return {
        "x": jax.random.normal(ks[0], (BATCH, SEQ, D_MODEL), jnp.float32),
        "p": jax.random.normal(ks[1], (DEPTH, BATCH, SEQ, PLE_DIM), jnp.float32),
        "w_in": nrm(ks[2], (L, D_MODEL, IN_WIDTH), D_MODEL),
        "g_attn": gain(ks[3], (L, D_MODEL)),
        "g_q_lora": gain(ks[4], (L, Q_LORA)),
        "g_kv_lora": gain(ks[5], (L, KV_LORA)),
        "w_uq": nrm(ks[6], (L, Q_LORA, MLA_HEADS * (MLA_NOPE + MLA_ROPE)), Q_LORA),
        "w_ukv": nrm(ks[7], (L, KV_LORA, MLA_HEADS * (MLA_NOPE + MLA_V)), KV_LORA),
        "w_o": nrm(ks[8], (L, RET_WIDTH + MLA_HEADS * MLA_V, D_MODEL), D_MODEL),
        "g_ffn": gain(ks[9], (L, D_MODEL)),
        "w_ffn_gate": nrm(ks[10], (L, D_MODEL, D_FF), D_MODEL),
        "w_ffn_up": nrm(ks[11], (L, D_MODEL, D_FF), D_MODEL),
        "conv_w": nrm(ks[12], (L, CONV_WIDTH, D_FF), CONV_WIDTH),
        "conv_b": 0.01 * jax.random.normal(ks[13], (L, D_FF), jnp.float32),
        "w_ffn_down": nrm(ks[14], (L, D_FF, D_MODEL), D_FF),
        "g_ple": gain(ks[15], (L, D_MODEL)),
        "w_ple_gate": nrm(ks[16], (L, D_MODEL, D_MODEL), D_MODEL),
        "w_ple_proj": nrm(ks[17], (L, PLE_DIM, D_MODEL), PLE_DIM),
        "g_final": gain(ks[18], (D_MODEL,)),
    }


def _fwd_reference(x, p, w_in, g_attn, g_q_lora, g_kv_lora, w_uq, w_ukv, w_o, g_ffn,
              w_ffn_gate, w_ffn_up, conv_w, conv_b, w_ffn_down, g_ple,
              w_ple_gate, w_ple_proj, g_final):
    B, S, _ = x.shape
    cos_r, sin_r = rope_tables(S, RET_HEAD_DIM)
    cos_m, sin_m = rope_tables(S, MLA_ROPE)
    h = x
    for i in range(DEPTH):
        hn = rms_norm(h, g_attn[i])
        proj = hn @ w_in[i]
        rq, rk, rv, rg, cq, ckv, kr = jnp.split(proj, IN_SPLITS, axis=-1)

        rq = apply_rope(rq.reshape(B, S, RET_HEADS, RET_HEAD_DIM), cos_r, sin_r)
        rk = apply_rope(rk.reshape(B, S, RET_HEADS, RET_HEAD_DIM), cos_r, sin_r)
        rv = rv.reshape(B, S, RET_HEADS, RET_HEAD_DIM)
        ro = head_norm(retention(rq, rk, rv)).reshape(B, S, RET_WIDTH)
        ro = jax.nn.silu(rg) * ro

        q = (rms_norm(cq, g_q_lora[i]) @ w_uq[i]).reshape(B, S, MLA_HEADS, MLA_NOPE + MLA_ROPE)
        q_nope = q[..., :MLA_NOPE]
        q_rope = apply_rope(q[..., MLA_NOPE:], cos_m, sin_m)
        kv = (rms_norm(ckv, g_kv_lora[i]) @ w_ukv[i]).reshape(B, S, MLA_HEADS, MLA_NOPE + MLA_V)
        k_nope = kv[..., :MLA_NOPE]
        v = kv[..., MLA_NOPE:]
        k_rope = apply_rope(kr[:, :, None, :], cos_m, sin_m)[:, :, 0, :]
        mo = mla_attention(q_nope, q_rope, k_nope, k_rope, v).reshape(B, S, MLA_HEADS * MLA_V)

        h = h + jnp.concatenate([ro, mo], axis=-1) @ w_o[i]

        hn = rms_norm(h, g_ffn[i])
        a = causal_dwconv(hn @ w_ffn_gate[i], conv_w[i], conv_b[i])
        h = h + (jax.nn.silu(a) * (hn @ w_ffn_up[i])) @ w_ffn_down[i]

        gate = jax.nn.sigmoid(rms_norm(h, g_ple[i]) @ w_ple_gate[i])
        h = h + gate * (p[i] @ w_ple_proj[i])
    return rms_norm(h, g_final)


import jax as _jax
import jax.numpy as _jnp

TWIN_FORMAT = 'train_step'
FWD_PARAMS = ['x', 'p', 'w_in', 'g_attn', 'g_q_lora', 'g_kv_lora', 'w_uq', 'w_ukv', 'w_o', 'g_ffn', 'w_ffn_gate', 'w_ffn_up', 'conv_w', 'conv_b', 'w_ffn_down', 'g_ple', 'w_ple_gate', 'w_ple_proj', 'g_final']
TWIN_WEIGHTS = ['w_in', 'g_attn', 'g_q_lora', 'g_kv_lora', 'w_uq', 'w_ukv', 'w_o', 'g_ffn', 'w_ffn_gate', 'w_ffn_up', 'conv_w', 'conv_b', 'w_ffn_down', 'g_ple', 'w_ple_gate', 'w_ple_proj', 'g_final']
TWIN_DIFF_INPUT = 'x'
TWIN_INPUTS = ['x', 'p', 'w_in', 'g_attn', 'g_q_lora', 'g_kv_lora', 'w_uq', 'w_ukv', 'w_o', 'g_ffn', 'w_ffn_gate', 'w_ffn_up', 'conv_w', 'conv_b', 'w_ffn_down', 'g_ple', 'w_ple_gate', 'w_ple_proj', 'g_final', 'loss_target', 'm_w_in', 'm_g_attn', 'm_g_q_lora', 'm_g_kv_lora', 'm_w_uq', 'm_w_ukv', 'm_w_o', 'm_g_ffn', 'm_w_ffn_gate', 'm_w_ffn_up', 'm_conv_w', 'm_conv_b', 'm_w_ffn_down', 'm_g_ple', 'm_w_ple_gate', 'm_w_ple_proj', 'm_g_final', 'v_w_in', 'v_g_attn', 'v_g_q_lora', 'v_g_kv_lora', 'v_w_uq', 'v_w_ukv', 'v_w_o', 'v_g_ffn', 'v_w_ffn_gate', 'v_w_ffn_up', 'v_conv_w', 'v_conv_b', 'v_w_ffn_down', 'v_g_ple', 'v_w_ple_gate', 'v_w_ple_proj', 'v_g_final']
TWIN_OUTPUTS = ['loss', 'grad_x', 'grad_w_in', 'grad_g_attn', 'grad_g_q_lora', 'grad_g_kv_lora', 'grad_w_uq', 'grad_w_ukv', 'grad_w_o', 'grad_g_ffn', 'grad_w_ffn_gate', 'grad_w_ffn_up', 'grad_conv_w', 'grad_conv_b', 'grad_w_ffn_down', 'grad_g_ple', 'grad_w_ple_gate', 'grad_w_ple_proj', 'grad_g_final', 'delta_w_in', 'delta_g_attn', 'delta_g_q_lora', 'delta_g_kv_lora', 'delta_w_uq', 'delta_w_ukv', 'delta_w_o', 'delta_g_ffn', 'delta_w_ffn_gate', 'delta_w_ffn_up', 'delta_conv_w', 'delta_conv_b', 'delta_w_ffn_down', 'delta_g_ple', 'delta_w_ple_gate', 'delta_w_ple_proj', 'delta_g_final', 'new_m_w_in', 'new_m_g_attn', 'new_m_g_q_lora', 'new_m_g_kv_lora', 'new_m_w_uq', 'new_m_w_ukv', 'new_m_w_o', 'new_m_g_ffn', 'new_m_w_ffn_gate', 'new_m_w_ffn_up', 'new_m_conv_w', 'new_m_conv_b', 'new_m_w_ffn_down', 'new_m_g_ple', 'new_m_w_ple_gate', 'new_m_w_ple_proj', 'new_m_g_final', 'new_v_w_in', 'new_v_g_attn', 'new_v_g_q_lora', 'new_v_g_kv_lora', 'new_v_w_uq', 'new_v_w_ukv', 'new_v_w_o', 'new_v_g_ffn', 'new_v_w_ffn_gate', 'new_v_w_ffn_up', 'new_v_conv_w', 'new_v_conv_b', 'new_v_w_ffn_down', 'new_v_g_ple', 'new_v_w_ple_gate', 'new_v_w_ple_proj', 'new_v_g_final']
TWIN_LEAF_KINDS = {'loss': 'loss', 'grad_x': 'grad_x', 'grad_w_in': 'grad_w', 'grad_g_attn': 'grad_w', 'grad_g_q_lora': 'grad_w', 'grad_g_kv_lora': 'grad_w', 'grad_w_uq': 'grad_w', 'grad_w_ukv': 'grad_w', 'grad_w_o': 'grad_w', 'grad_g_ffn': 'grad_w', 'grad_w_ffn_gate': 'grad_w', 'grad_w_ffn_up': 'grad_w', 'grad_conv_w': 'grad_w', 'grad_conv_b': 'grad_w', 'grad_w_ffn_down': 'grad_w', 'grad_g_ple': 'grad_w', 'grad_w_ple_gate': 'grad_w', 'grad_w_ple_proj': 'grad_w', 'grad_g_final': 'grad_w', 'delta_w_in': 'delta_w', 'delta_g_attn': 'delta_w', 'delta_g_q_lora': 'delta_w', 'delta_g_kv_lora': 'delta_w', 'delta_w_uq': 'delta_w', 'delta_w_ukv': 'delta_w', 'delta_w_o': 'delta_w', 'delta_g_ffn': 'delta_w', 'delta_w_ffn_gate': 'delta_w', 'delta_w_ffn_up': 'delta_w', 'delta_conv_w': 'delta_w', 'delta_conv_b': 'delta_w', 'delta_w_ffn_down': 'delta_w', 'delta_g_ple': 'delta_w', 'delta_w_ple_gate': 'delta_w', 'delta_w_ple_proj': 'delta_w', 'delta_g_final': 'delta_w', 'new_m_w_in': 'new_m', 'new_m_g_attn': 'new_m', 'new_m_g_q_lora': 'new_m', 'new_m_g_kv_lora': 'new_m', 'new_m_w_uq': 'new_m', 'new_m_w_ukv': 'new_m', 'new_m_w_o': 'new_m', 'new_m_g_ffn': 'new_m', 'new_m_w_ffn_gate': 'new_m', 'new_m_w_ffn_up': 'new_m', 'new_m_conv_w': 'new_m', 'new_m_conv_b': 'new_m', 'new_m_w_ffn_down': 'new_m', 'new_m_g_ple': 'new_m', 'new_m_w_ple_gate': 'new_m', 'new_m_w_ple_proj': 'new_m', 'new_m_g_final': 'new_m', 'new_v_w_in': 'new_v', 'new_v_g_attn': 'new_v', 'new_v_g_q_lora': 'new_v', 'new_v_g_kv_lora': 'new_v', 'new_v_w_uq': 'new_v', 'new_v_w_ukv': 'new_v', 'new_v_w_o': 'new_v', 'new_v_g_ffn': 'new_v', 'new_v_w_ffn_gate': 'new_v', 'new_v_w_ffn_up': 'new_v', 'new_v_conv_w': 'new_v', 'new_v_conv_b': 'new_v', 'new_v_w_ffn_down': 'new_v', 'new_v_g_ple': 'new_v', 'new_v_w_ple_gate': 'new_v', 'new_v_w_ple_proj': 'new_v', 'new_v_g_final': 'new_v'}


def _forward(args):
    return _fwd_reference(*[args[k] for k in FWD_PARAMS])


def _output_shape():
    out = _jax.eval_shape(lambda: _forward(_fwd_setup_inputs(0)))
    return out.shape, out.dtype

N_MICROBATCH = 1
ADAM_LR = 0.001
ADAM_B1 = 0.9
ADAM_B2 = 0.999
ADAM_EPS = 1e-08
ADAM_WD = 0.01
ADAM_STEP = 10
PER_EXAMPLE_BATCH_AXIS = {'x': 0, 'p': 1, 'loss_target': 0}
SHARED_INPUTS = []
_WEIGHT_DTYPES = {'w_in': _jnp.float32, 'g_attn': _jnp.float32, 'g_q_lora': _jnp.float32, 'g_kv_lora': _jnp.float32, 'w_uq': _jnp.float32, 'w_ukv': _jnp.float32, 'w_o': _jnp.float32, 'g_ffn': _jnp.float32, 'w_ffn_gate': _jnp.float32, 'w_ffn_up': _jnp.float32, 'conv_w': _jnp.float32, 'conv_b': _jnp.float32, 'w_ffn_down': _jnp.float32, 'g_ple': _jnp.float32, 'w_ple_gate': _jnp.float32, 'w_ple_proj': _jnp.float32, 'g_final': _jnp.float32}
MOMENT_SCALE = {'w_in': 2.386614e-02, 'g_attn': 3.762103e-02, 'g_q_lora': 8.404679e-03, 'g_kv_lora': 1.741244e-02, 'w_uq': 4.941285e-03, 'w_ukv': 5.681739e-03, 'w_o': 1.838694e-02, 'g_ffn': 2.905864e-02, 'w_ffn_gate': 1.280620e-02, 'w_ffn_up': 1.241159e-02, 'conv_w': 1.278235e-02, 'conv_b': 1.209007e-02, 'w_ffn_down': 2.036044e-02, 'g_ple': 6.954928e-03, 'w_ple_gate': 6.886364e-03, 'w_ple_proj': 1.765340e-02, 'g_final': 7.983673e+00}


def _to_microbatches(a, axis):
    t = _jnp.moveaxis(a, axis, 0)
    t = t.reshape((N_MICROBATCH, t.shape[0] // N_MICROBATCH) + t.shape[1:])
    return _jnp.moveaxis(t, 1, axis + 1)


def setup_inputs(seed: int = 0) -> dict:
    inp = _fwd_setup_inputs(seed)
    key = _jax.random.fold_in(_jax.random.key(seed), 7919)
    shape, _ = _output_shape()
    out = dict(inp)
    out["loss_target"] = _jax.random.normal(_jax.random.fold_in(key, 0), shape, _jnp.float32)
    for i, name in enumerate(TWIN_WEIGHTS):
        w = inp[name].astype(_jnp.float32)
        if MOMENT_SCALE is None:
            s = _jnp.sqrt(_jnp.mean(_jnp.square(w)) + 1e-30)
        else:
            s = MOMENT_SCALE[name]
        km, kv = _jax.random.split(_jax.random.fold_in(key, i + 1))
        out[name] = w
        out["m_" + name] = s * _jax.random.normal(km, w.shape, _jnp.float32)
        out["v_" + name] = (s * s) * _jax.random.uniform(kv, w.shape, _jnp.float32, 0.5, 1.5)
    if N_MICROBATCH > 1:
        for name, axis in PER_EXAMPLE_BATCH_AXIS.items():
            out[name] = _to_microbatches(out[name], axis)
    return {'x': out['x'], 'p': out['p'], 'w_in': out['w_in'], 'g_attn': out['g_attn'], 'g_q_lora': out['g_q_lora'], 'g_kv_lora': out['g_kv_lora'], 'w_uq': out['w_uq'], 'w_ukv': out['w_ukv'], 'w_o': out['w_o'], 'g_ffn': out['g_ffn'], 'w_ffn_gate': out['w_ffn_gate'], 'w_ffn_up': out['w_ffn_up'], 'conv_w': out['conv_w'], 'conv_b': out['conv_b'], 'w_ffn_down': out['w_ffn_down'], 'g_ple': out['g_ple'], 'w_ple_gate': out['w_ple_gate'], 'w_ple_proj': out['w_ple_proj'], 'g_final': out['g_final'], 'loss_target': out['loss_target'], 'm_w_in': out['m_w_in'], 'm_g_attn': out['m_g_attn'], 'm_g_q_lora': out['m_g_q_lora'], 'm_g_kv_lora': out['m_g_kv_lora'], 'm_w_uq': out['m_w_uq'], 'm_w_ukv': out['m_w_ukv'], 'm_w_o': out['m_w_o'], 'm_g_ffn': out['m_g_ffn'], 'm_w_ffn_gate': out['m_w_ffn_gate'], 'm_w_ffn_up': out['m_w_ffn_up'], 'm_conv_w': out['m_conv_w'], 'm_conv_b': out['m_conv_b'], 'm_w_ffn_down': out['m_w_ffn_down'], 'm_g_ple': out['m_g_ple'], 'm_w_ple_gate': out['m_w_ple_gate'], 'm_w_ple_proj': out['m_w_ple_proj'], 'm_g_final': out['m_g_final'], 'v_w_in': out['v_w_in'], 'v_g_attn': out['v_g_attn'], 'v_g_q_lora': out['v_g_q_lora'], 'v_g_kv_lora': out['v_g_kv_lora'], 'v_w_uq': out['v_w_uq'], 'v_w_ukv': out['v_w_ukv'], 'v_w_o': out['v_w_o'], 'v_g_ffn': out['v_g_ffn'], 'v_w_ffn_gate': out['v_w_ffn_gate'], 'v_w_ffn_up': out['v_w_ffn_up'], 'v_conv_w': out['v_conv_w'], 'v_conv_b': out['v_conv_b'], 'v_w_ffn_down': out['v_w_ffn_down'], 'v_g_ple': out['v_g_ple'], 'v_w_ple_gate': out['v_w_ple_gate'], 'v_w_ple_proj': out['v_w_ple_proj'], 'v_g_final': out['v_g_final']}


def _loss(weights, diff, rest, loss_target):
    with _jax.named_scope("forward"):
        args = {**rest, TWIN_DIFF_INPUT: diff, **{k: w.astype(_WEIGHT_DTYPES[k]) for k, w in weights.items()}}
        y = _forward(args)
    with _jax.named_scope("loss_head"):
        err = _jnp.square(y.astype(_jnp.float32) - loss_target)
        return 0.5 * _jnp.sum(_jnp.mean(err, axis=-1)) if err.ndim else 0.5 * err


def _adamw(w, g, m, v):
    m = ADAM_B1 * m + (1.0 - ADAM_B1) * g
    v = ADAM_B2 * v + (1.0 - ADAM_B2) * _jnp.square(g)
    m_hat = m / (1.0 - ADAM_B1 ** ADAM_STEP)
    v_hat = v / (1.0 - ADAM_B2 ** ADAM_STEP)
    delta = -ADAM_LR * (m_hat / (_jnp.sqrt(v_hat) + ADAM_EPS) + ADAM_WD * w)
    return delta, m, v


def reference(x, p, w_in, g_attn, g_q_lora, g_kv_lora, w_uq, w_ukv, w_o, g_ffn, w_ffn_gate, w_ffn_up, conv_w, conv_b, w_ffn_down, g_ple, w_ple_gate, w_ple_proj, g_final, loss_target, m_w_in, m_g_attn, m_g_q_lora, m_g_kv_lora, m_w_uq, m_w_ukv, m_w_o, m_g_ffn, m_w_ffn_gate, m_w_ffn_up, m_conv_w, m_conv_b, m_w_ffn_down, m_g_ple, m_w_ple_gate, m_w_ple_proj, m_g_final, v_w_in, v_g_attn, v_g_q_lora, v_g_kv_lora, v_w_uq, v_w_ukv, v_w_o, v_g_ffn, v_w_ffn_gate, v_w_ffn_up, v_conv_w, v_conv_b, v_w_ffn_down, v_g_ple, v_w_ple_gate, v_w_ple_proj, v_g_final):
    given = dict(x=x, p=p, w_in=w_in, g_attn=g_attn, g_q_lora=g_q_lora, g_kv_lora=g_kv_lora, w_uq=w_uq, w_ukv=w_ukv, w_o=w_o, g_ffn=g_ffn, w_ffn_gate=w_ffn_gate, w_ffn_up=w_ffn_up, conv_w=conv_w, conv_b=conv_b, w_ffn_down=w_ffn_down, g_ple=g_ple, w_ple_gate=w_ple_gate, w_ple_proj=w_ple_proj, g_final=g_final, loss_target=loss_target, m_w_in=m_w_in, m_g_attn=m_g_attn, m_g_q_lora=m_g_q_lora, m_g_kv_lora=m_g_kv_lora, m_w_uq=m_w_uq, m_w_ukv=m_w_ukv, m_w_o=m_w_o, m_g_ffn=m_g_ffn, m_w_ffn_gate=m_w_ffn_gate, m_w_ffn_up=m_w_ffn_up, m_conv_w=m_conv_w, m_conv_b=m_conv_b, m_w_ffn_down=m_w_ffn_down, m_g_ple=m_g_ple, m_w_ple_gate=m_w_ple_gate, m_w_ple_proj=m_w_ple_proj, m_g_final=m_g_final, v_w_in=v_w_in, v_g_attn=v_g_attn, v_g_q_lora=v_g_q_lora, v_g_kv_lora=v_g_kv_lora, v_w_uq=v_w_uq, v_w_ukv=v_w_ukv, v_w_o=v_w_o, v_g_ffn=v_g_ffn, v_w_ffn_gate=v_w_ffn_gate, v_w_ffn_up=v_w_ffn_up, v_conv_w=v_conv_w, v_conv_b=v_conv_b, v_w_ffn_down=v_w_ffn_down, v_g_ple=v_g_ple, v_w_ple_gate=v_w_ple_gate, v_w_ple_proj=v_w_ple_proj, v_g_final=v_g_final)
    weights = {n: given[n] for n in TWIN_WEIGHTS}
    shared = {n: given[n] for n in SHARED_INPUTS}
    per_example = {n: given[n] for n in ['x', 'p']}
    grad_fn = _jax.value_and_grad(_loss, argnums=(0, 1))

    def one_microbatch(ex, loss_target):
        ex = dict(ex)
        diff = ex.pop(TWIN_DIFF_INPUT)
        return grad_fn(weights, diff, {**shared, **ex}, loss_target)

    if N_MICROBATCH == 1:
        loss, (grad_w, grad_x) = one_microbatch(per_example, given["loss_target"])
    else:
        def body(carry, xs):
            loss_sum, grad_sum = carry
            l_k, (gw_k, gx_k) = one_microbatch(xs[0], xs[1])
            with _jax.named_scope("update"):
                return (loss_sum + l_k, _jax.tree.map(_jnp.add, grad_sum, gw_k)), gx_k

        init = (_jnp.zeros((), _jnp.float32), _jax.tree.map(_jnp.zeros_like, weights))
        (loss, grad_w), grad_x = _jax.lax.scan(body, init, (per_example, given["loss_target"]))
    with _jax.named_scope("update"):
        delta_w, new_m, new_v = {}, {}, {}
        for n in TWIN_WEIGHTS:
            delta_w[n], new_m[n], new_v[n] = _adamw(weights[n], grad_w[n], given["m_" + n], given["v_" + n])
    return (loss, grad_x, *[grad_w[n] for n in TWIN_WEIGHTS], *[delta_w[n] for n in TWIN_WEIGHTS],
            *[new_m[n] for n in TWIN_WEIGHTS], *[new_v[n] for n in TWIN_WEIGHTS])
```

```python
import functools
import math

import jax
import jax.numpy as jnp
from jax import lax
from jax.experimental import pallas as pl
from jax.experimental.pallas import tpu as pltpu

F32 = jnp.float32
BF16 = jnp.bfloat16

LANE = 128
VMEM_LIMIT = 56 * 1024 * 1024
ROWWISE_BLOCK_BYTES = 5 * 1024 * 1024

N_CHIP = 4
MESH = pl.DeviceIdType.MESH

CHUNK = 64
RET_HEADS = 8
MLA_HEADS = 16
MLA_NOPE = 128
MLA_ROPE = 64
MLA_QK_PAD = 256
PLE_DIM = 256
CONV_WIDTH = 3
ROPE_BASE = 10000.0
EPS = 1e-6
RET_BLOCK = 256

ADAM_LR = 0.001
ADAM_B1 = 0.9
ADAM_B2 = 0.999
ADAM_EPS = 1e-08
ADAM_WD = 0.01
ADAM_STEP = 10

WEIGHTS = ['w_in', 'g_attn', 'g_q_lora', 'g_kv_lora', 'w_uq', 'w_ukv', 'w_o', 'g_ffn', 'w_ffn_gate', 'w_ffn_up',
           'conv_w', 'conv_b', 'w_ffn_down', 'g_ple', 'w_ple_gate', 'w_ple_proj', 'g_final']
COL_SHARDED = ('w_in', 'w_uq', 'w_ukv', 'w_ffn_gate', 'w_ffn_up', 'w_ple_proj')
ROW_SHARDED = ('w_o', 'w_ffn_down', 'w_ple_gate')
BIG = COL_SHARDED + ROW_SHARDED
SMALL_REPLICATED = ('g_attn', 'g_q_lora', 'g_kv_lora', 'g_ffn', 'conv_b', 'g_ple', 'g_final')


def _round_up(n, m):
    return (n + m - 1) // m * m


def _tile(dim, cap, align=LANE):
    if dim <= cap:
        return dim
    t = cap // align * align
    while t >= align:
        if dim % t == 0:
            return t
        t -= align
    return dim


def _params(sem):
    return pltpu.CompilerParams(dimension_semantics=sem, vmem_limit_bytes=VMEM_LIMIT)


def _mm_call(name, dims, grid, in_specs, out_specs, out_shape, acc_shape, operands, has_res):
    nsteps = grid[2]
    n_out = len(out_shape)

    def body(*refs):
        a_ref, b_ref = refs[0], refs[1]
        res_ref = refs[2] if has_res else None
        outs = refs[2 + has_res:2 + has_res + n_out]
        acc = refs[2 + has_res + n_out]
        k = pl.program_id(2)

        @pl.when(k == 0)
        def _():
            acc[...] = jnp.zeros_like(acc)

        acc[...] += lax.dot_general(a_ref[...], b_ref[...], (dims, ((), ())), preferred_element_type=F32)

        @pl.when(k == nsteps - 1)
        def _():
            r = acc[...]
            if has_res:
                r = r + res_ref[...]
            for o in outs:
                o[...] = r.astype(o.dtype)

    return pl.pallas_call(
        body, name=name, grid=grid, in_specs=in_specs, out_specs=out_specs, out_shape=out_shape,
        scratch_shapes=[pltpu.VMEM(acc_shape, F32)],
        compiler_params=_params(("parallel", "parallel", "arbitrary")),
    )(*operands)


def _mm_nn(a, w, *, name, res=None, out_dtype=F32):
    M, K = a.shape
    G, _, n = w.shape
    tm, tn, tk = _tile(M, 1024), _tile(n, 1408), _tile(K, 1024)
    npg = n // tn
    grid = (M // tm, G * npg, K // tk)
    in_specs = [pl.BlockSpec((tm, tk), lambda i, j, k: (i, k)),
                pl.BlockSpec((None, tk, tn), lambda i, j, k: (j // npg, k, j % npg))]
    operands = [a, w]
    if res is not None:
        in_specs.append(pl.BlockSpec((tm, tn), lambda i, j, k: (i, j)))
        operands.append(res)
    out_specs = [pl.BlockSpec((tm, tn), lambda i, j, k: (i, j))]
    out_shape = [jax.ShapeDtypeStruct((M, G * n), out_dtype)]
    return _mm_call(name, ((1,), (0,)), grid, in_specs, out_specs, out_shape, (tm, tn), operands, res is not None)[0]


def _mm_nt(g, w, *, name, res=None, out_dtype=F32):
    M, _ = g.shape
    G, K, n = w.shape
    tm, tko, tn = _tile(M, 1024), _tile(K, 1024), _tile(n, 1408)
    npg = n // tn
    grid = (M // tm, K // tko, G * npg)
    in_specs = [pl.BlockSpec((tm, tn), lambda i, j, k: (i, k)),
                pl.BlockSpec((None, tko, tn), lambda i, j, k: (k // npg, j, k % npg))]
    operands = [g, w]
    if res is not None:
        in_specs.append(pl.BlockSpec((tm, tko), lambda i, j, k: (i, j)))
        operands.append(res)
    out_specs = [pl.BlockSpec((tm, tko), lambda i, j, k: (i, j))]
    out_shape = [jax.ShapeDtypeStruct((M, K), out_dtype)]
    return _mm_call(name, ((1,), (1,)), grid, in_specs, out_specs, out_shape, (tm, tko), operands, res is not None)[0]


def _mm_tn(a, g, groups, *, name):
    M, K = a.shape
    n = g.shape[1] // groups
    tm, tko, tn = _tile(M, 1024), _tile(K, 1024), _tile(n, 1408)
    npg = n // tn
    grid = (K // tko, groups * npg, M // tm)
    in_specs = [pl.BlockSpec((tm, tko), lambda i, j, k: (k, i)),
                pl.BlockSpec((tm, tn), lambda i, j, k: (k, j))]
    out_spec = pl.BlockSpec((None, tko, tn), lambda i, j, k: (j // npg, i, j % npg))
    out_shape = [jax.ShapeDtypeStruct((groups, K, n), F32), jax.ShapeDtypeStruct((groups, K, n), BF16)]
    return _mm_call(name, ((0,), (0,)), grid, in_specs, [out_spec, out_spec], out_shape, (tko, tn), [a, g], False)


def _rowwise(fn, ins, outs, accs=(), *, name):
    R = next(a.shape[0] for a, kind in ins if kind == 'row')
    row_bytes = sum(a.shape[1] * a.dtype.itemsize for a, kind in ins if kind != 'bcast')
    row_bytes += sum(w * jnp.dtype(dt).itemsize for w, dt in outs)
    ts = 8
    while ts * 2 <= 512 and ts * 2 * row_bytes <= ROWWISE_BLOCK_BYTES:
        ts *= 2
    ts = min(ts, R)
    for a, kind in ins:
        if kind == 'per':
            ts = math.gcd(ts, a.shape[0])
    while R % ts:
        ts //= 2
    in_specs = []
    for a, kind in ins:
        w = a.shape[1]
        if kind == 'row':
            in_specs.append(pl.BlockSpec((ts, w), lambda i: (i, 0)))
        elif kind == 'bcast':
            in_specs.append(pl.BlockSpec((1, w), lambda i: (0, 0)))
        else:
            nper = a.shape[0] // ts
            in_specs.append(pl.BlockSpec((ts, w), lambda i, nper=nper: (i % nper, 0)))
    out_specs = [pl.BlockSpec((ts, w), lambda i: (i, 0)) for w, _ in outs]
    out_specs += [pl.BlockSpec((1, w), lambda i: (0, 0)) for w in accs]
    out_shape = [jax.ShapeDtypeStruct((R, w), dt) for w, dt in outs]
    out_shape += [jax.ShapeDtypeStruct((1, w), F32) for w in accs]
    n_in, n_out = len(ins), len(outs)

    def body(*refs):
        vals = [r[...] for r in refs[:n_in]]
        res = fn(*vals)
        for r, v in zip(refs[n_in:n_in + n_out], res[:n_out]):
            r[...] = v.astype(r.dtype)
        if accs:
            first = pl.program_id(0) == 0
            for r, v in zip(refs[n_in + n_out:], res[n_out:]):
                @pl.when(first)
                def _(r=r, v=v):
                    r[...] = v

                @pl.when(jnp.logical_not(first))
                def _(r=r, v=v):
                    r[...] += v

    return pl.pallas_call(
        body, name=name, grid=(R // ts,), in_specs=in_specs, out_specs=out_specs, out_shape=out_shape,
        compiler_params=_params(("arbitrary",) if accs else ("parallel",)),
    )(*[a for a, _ in ins])


def _mean(v):
    return jnp.mean(v, axis=-1, keepdims=True)


def _colsum(v):
    return jnp.sum(v, axis=0, keepdims=True)


def _sigmoid(v):
    return 1.0 / (1.0 + jnp.exp(-v))


def _rms_fwd(x, g, *, name):
    def fn(xv, gv):
        r = lax.rsqrt(_mean(xv * xv) + EPS)
        return [xv * r * gv, r]
    w = x.shape[1]
    return _rowwise(fn, [(x, 'row'), (g, 'bcast')], [(w, BF16), (1, F32)], name=name)


def _rms_bwd(x, r, dhn, g, dres, *, name):
    def fn(xv, rv, dv, gv, *rest):
        n = xv * rv
        dn = dv * gv
        dx = rv * (dn - n * _mean(dn * n))
        if rest:
            dx = dx + rest[0]
        return [dx, dx, _colsum(dv * n)]
    w = x.shape[1]
    ins = [(x, 'row'), (r, 'row'), (dhn, 'row'), (g, 'bcast')]
    if dres is not None:
        ins.append((dres, 'row'))
    return _rowwise(fn, ins, [(w, F32), (w, BF16)], [w], name=name)


def _rope(x, cos, sin_a, sin_b, shift, scale, out_dtype, *, name):
    w = x.shape[1]

    def fn(xv, cv, sav, sbv):
        y = xv * cv + pltpu.roll(xv, w - shift, 1) * sav + pltpu.roll(xv, shift, 1) * sbv
        return [y * scale]
    return _rowwise(fn, [(x, 'row'), (cos, 'per'), (sin_a, 'per'), (sin_b, 'per')], [(w, out_dtype)], name=name)[0]


def _dot(a, b, ca, cb):
    return lax.dot_general(a, b, (((ca,), (cb,)), ((), ())), preferred_element_type=F32)


def _ret_fwd(q, k, v, dec, dq_, dk_, ds_):
    H, S, d = q.shape
    T = dec.shape[1]
    nC = S // T

    def body(q_ref, k_ref, v_ref, m_ref, qd_ref, kd_ref, sd_ref, o_ref, st_ref, state):
        @pl.when(pl.program_id(1) == 0)
        def _():
            state[...] = jnp.zeros_like(state)

        st = state[...]
        st_ref[...] = st
        qv, kv, vv = q_ref[...], k_ref[...], v_ref[...]
        p = (_dot(qv, kv, 1, 1) * m_ref[...]).astype(BF16)
        qs = (qv.astype(F32) * qd_ref[...]).astype(BF16)
        o_ref[...] = _dot(p, vv, 1, 0) + _dot(qs, st.astype(BF16), 1, 0)
        ks = (kv.astype(F32) * kd_ref[...]).astype(BF16)
        state[...] = st * sd_ref[...] + _dot(ks, vv, 0, 0)

    blk = pl.BlockSpec((None, T, d), lambda h, c: (h, c, 0))
    return pl.pallas_call(
        body, name="ret_fwd", grid=(H, nC),
        in_specs=[blk, blk, blk,
                  pl.BlockSpec((None, T, T), lambda h, c: (h, 0, 0)),
                  pl.BlockSpec((None, T, 1), lambda h, c: (h, 0, 0)),
                  pl.BlockSpec((None, T, 1), lambda h, c: (h, 0, 0)),
                  pl.BlockSpec((None, 1, 1), lambda h, c: (h, 0, 0))],
        out_specs=[blk, pl.BlockSpec((None, None, d, d), lambda h, c: (h, c, 0, 0))],
        out_shape=[jax.ShapeDtypeStruct((H, S, d), F32), jax.ShapeDtypeStruct((H, nC, d, d), F32)],
        scratch_shapes=[pltpu.VMEM((d, d), F32)],
        compiler_params=_params(("parallel", "arbitrary")),
    )(q, k, v, dec, dq_, dk_, ds_)


def _ret_bwd(q, k, v, do, states, dec, dq_, dk_, ds_):
    H, S, d = q.shape
    T = dec.shape[1]
    nC = S // T

    def body(q_ref, k_ref, v_ref, do_ref, st_ref, m_ref, qd_ref, kd_ref, sd_ref, gq_ref, gk_ref, gv_ref, dstate):
        @pl.when(pl.program_id(1) == 0)
        def _():
            dstate[...] = jnp.zeros_like(dstate)

        qv, kv, vv, dov = q_ref[...], k_ref[...], v_ref[...], do_ref[...]
        m = m_ref[...]
        qd, kd = qd_ref[...], kd_ref[...]
        ds = dstate[...]
        dsb = ds.astype(BF16)
        sb = st_ref[...].astype(BF16)
        p = (_dot(qv, kv, 1, 1) * m).astype(BF16)
        da = (_dot(dov, vv, 1, 1) * m).astype(BF16)
        qs = (qv.astype(F32) * qd).astype(BF16)
        ks = (kv.astype(F32) * kd).astype(BF16)
        gq_ref[...] = _dot(da, kv, 1, 0) + _dot(dov, sb, 1, 1) * qd
        gk_ref[...] = _dot(da, qv, 0, 0) + _dot(vv, dsb, 1, 1) * kd
        gv_ref[...] = _dot(p, dov, 0, 0) + _dot(ks, dsb, 1, 0)
        dstate[...] = ds * sd_ref[...] + _dot(qs, dov, 0, 0)

    blk = pl.BlockSpec((None, T, d), lambda h, c: (h, nC - 1 - c, 0))
    out = jax.ShapeDtypeStruct((H, S, d), F32)
    return pl.pallas_call(
        body, name="ret_bwd", grid=(H, nC),
        in_specs=[blk, blk, blk, blk,
                  pl.BlockSpec((None, None, d, d), lambda h, c: (h, nC - 1 - c, 0, 0)),
                  pl.BlockSpec((None, T, T), lambda h, c: (h, 0, 0)),
                  pl.BlockSpec((None, T, 1), lambda h, c: (h, 0, 0)),
                  pl.BlockSpec((None, T, 1), lambda h, c: (h, 0, 0)),
                  pl.BlockSpec((None, 1, 1), lambda h, c: (h, 0, 0))],
        out_specs=[blk, blk, blk], out_shape=[out, out, out],
        scratch_shapes=[pltpu.VMEM((d, d), F32)],
        compiler_params=_params(("parallel", "arbitrary")),
    )(q, k, v, do, states, dec, dq_, dk_, ds_)


def _ret_tables(T, d):
    h = jnp.arange(RET_HEADS, dtype=F32)
    log_g = jnp.log1p(-jnp.exp2(-5.0 - h))
    idx = jnp.arange(T, dtype=F32)
    diff = idx[:, None] - idx[None, :]
    same = (jnp.arange(T)[:, None] // CHUNK) == (jnp.arange(T)[None, :] // CHUNK)
    earlier = (jnp.arange(T)[None, :] // CHUNK) < (jnp.arange(T)[:, None] // CHUNK)
    expo = jnp.where(same, jnp.abs(diff), diff)
    dec = jnp.where(same | earlier, jnp.exp(log_g[:, None, None] * expo[None]), 0.0)
    q_dec = jnp.exp(log_g[:, None] * (idx + 1.0))[..., None]
    k_dec = jnp.exp(log_g[:, None] * (T - 1.0 - idx))[..., None]
    s_dec = jnp.exp(log_g * T)[:, None, None]
    return dec.astype(F32), q_dec, k_dec, s_dec


NEG = -1e30


def _scores(q_ref, k_ref, qi, ki, T, scale):
    s = _dot(q_ref[...], k_ref[...], 1, 1) * scale
    qc = (qi * T + lax.broadcasted_iota(jnp.int32, (T, T), 0)) // CHUNK
    kc = (ki * T + lax.broadcasted_iota(jnp.int32, (T, T), 1)) // CHUNK
    return jnp.where(kc <= qc, s, NEG)


def _flash_fwd(q, k, v, scale):
    H, S, dk = q.shape
    dv = v.shape[2]
    T = _tile(S, 512)
    n = S // T

    def body(q_ref, k_ref, v_ref, o_ref, lse_ref, m_s, l_s, acc):
        qi, ki = pl.program_id(1), pl.program_id(2)

        @pl.when(ki == 0)
        def _():
            m_s[...] = jnp.full_like(m_s, NEG)
            l_s[...] = jnp.zeros_like(l_s)
            acc[...] = jnp.zeros_like(acc)

        @pl.when(ki <= qi)
        def _():
            s = _scores(q_ref, k_ref, qi, ki, T, scale)
            m_old = m_s[...]
            m_new = jnp.maximum(m_old, jnp.max(s, axis=1, keepdims=True))
            p = jnp.exp(s - m_new)
            alpha = jnp.exp(m_old - m_new)
            l_s[...] = alpha * l_s[...] + jnp.sum(p, axis=1, keepdims=True)
            acc[...] = alpha * acc[...] + _dot(p.astype(BF16), v_ref[...], 1, 0)
            m_s[...] = m_new

        @pl.when(ki == qi)
        def _():
            o_ref[...] = acc[...] / l_s[...]
            lse_ref[...] = m_s[...] + jnp.log(l_s[...])

    kv_map = lambda h, i, j: (h, jnp.minimum(i, j), 0)
    return pl.pallas_call(
        body, name="mla_fwd", grid=(H, n, n),
        in_specs=[pl.BlockSpec((None, T, dk), lambda h, i, j: (h, i, 0)),
                  pl.BlockSpec((None, T, dk), kv_map),
                  pl.BlockSpec((None, T, dv), kv_map)],
        out_specs=[pl.BlockSpec((None, T, dv), lambda h, i, j: (h, i, 0)),
                   pl.BlockSpec((None, T, 1), lambda h, i, j: (h, i, 0))],
        out_shape=[jax.ShapeDtypeStruct((H, S, dv), F32), jax.ShapeDtypeStruct((H, S, 1), F32)],
        scratch_shapes=[pltpu.VMEM((T, 1), F32), pltpu.VMEM((T, 1), F32), pltpu.VMEM((T, dv), F32)],
        compiler_params=_params(("parallel", "parallel", "arbitrary")),
    )(q, k, v)


def _flash_dq(q, k, v, do, lse, dlt, scale):
    H, S, dk = q.shape
    dv = v.shape[2]
    T = _tile(S, 512)
    n = S // T

    def body(q_ref, k_ref, v_ref, do_ref, lse_ref, dlt_ref, dq_ref, acc):
        qi, ki = pl.program_id(1), pl.program_id(2)

        @pl.when(ki == 0)
        def _():
            acc[...] = jnp.zeros_like(acc)

        @pl.when(ki <= qi)
        def _():
            p = jnp.exp(_scores(q_ref, k_ref, qi, ki, T, scale) - lse_ref[...])
            dp = _dot(do_ref[...], v_ref[...], 1, 1)
            ds = (p * (dp - dlt_ref[...]) * scale).astype(BF16)
            acc[...] += _dot(ds, k_ref[...], 1, 0)

        @pl.when(ki == qi)
        def _():
            dq_ref[...] = acc[...]

    kv_map = lambda h, i, j: (h, jnp.minimum(i, j), 0)
    q_map = lambda h, i, j: (h, i, 0)
    return pl.pallas_call(
        body, name="mla_dq", grid=(H, n, n),
        in_specs=[pl.BlockSpec((None, T, dk), q_map), pl.BlockSpec((None, T, dk), kv_map),
                  pl.BlockSpec((None, T, dv), kv_map), pl.BlockSpec((None, T, dv), q_map),
                  pl.BlockSpec((None, T, 1), q_map), pl.BlockSpec((None, T, 1), q_map)],
        out_specs=pl.BlockSpec((None, T, dk), q_map),
        out_shape=jax.ShapeDtypeStruct((H, S, dk), F32),
        scratch_shapes=[pltpu.VMEM((T, dk), F32)],
        compiler_params=_params(("parallel", "parallel", "arbitrary")),
    )(q, k, v, do, lse, dlt)


def _flash_dkv(q, k, v, do, lse, dlt, scale):
    H, S, dk = q.shape
    dv = v.shape[2]
    T = _tile(S, 512)
    n = S // T

    def body(q_ref, k_ref, v_ref, do_ref, lse_ref, dlt_ref, dk_ref, dv_ref, acc_k, acc_v):
        ki, qi = pl.program_id(1), pl.program_id(2)

        @pl.when(qi == 0)
        def _():
            acc_k[...] = jnp.zeros_like(acc_k)
            acc_v[...] = jnp.zeros_like(acc_v)

        @pl.when(qi >= ki)
        def _():
            p = jnp.exp(_scores(q_ref, k_ref, qi, ki, T, scale) - lse_ref[...])
            dov = do_ref[...]
            acc_v[...] += _dot(p.astype(BF16), dov, 0, 0)
            dp = _dot(dov, v_ref[...], 1, 1)
            ds = (p * (dp - dlt_ref[...]) * scale).astype(BF16)
            acc_k[...] += _dot(ds, q_ref[...], 0, 0)

        @pl.when(qi == n - 1)
        def _():
            dk_ref[...] = acc_k[...]
            dv_ref[...] = acc_v[...]

    q_map = lambda h, j, i: (h, jnp.maximum(i, j), 0)
    kv_map = lambda h, j, i: (h, j, 0)
    return pl.pallas_call(
        body, name="mla_dkv", grid=(H, n, n),
        in_specs=[pl.BlockSpec((None, T, dk), q_map), pl.BlockSpec((None, T, dk), kv_map),
                  pl.BlockSpec((None, T, dv), kv_map), pl.BlockSpec((None, T, dv), q_map),
                  pl.BlockSpec((None, T, 1), q_map), pl.BlockSpec((None, T, 1), q_map)],
        out_specs=[pl.BlockSpec((None, T, dk), kv_map), pl.BlockSpec((None, T, dv), kv_map)],
        out_shape=[jax.ShapeDtypeStruct((H, S, dk), F32), jax.ShapeDtypeStruct((H, S, dv), F32)],
        scratch_shapes=[pltpu.VMEM((T, dk), F32), pltpu.VMEM((T, dv), F32)],
        compiler_params=_params(("parallel", "parallel", "arbitrary")),
    )(q, k, v, do, lse, dlt)


def _shift_down(cur, prev, by):
    rows = lax.broadcasted_iota(jnp.int32, cur.shape, 0)
    return jnp.where(rows < by, pltpu.roll(prev, by, 0), pltpu.roll(cur, by, 0))


def _shift_up(cur, nxt, by):
    ts = cur.shape[0]
    rows = lax.broadcasted_iota(jnp.int32, cur.shape, 0)
    return jnp.where(rows >= ts - by, pltpu.roll(nxt, ts - by, 0), pltpu.roll(cur, ts - by, 0))


def _ffn_tiles(S, F):
    return _tile(S, 256, 8), _tile(F, 1024)


def _ffn_fwd(G, U, cw, cb):
    S, F = G.shape
    ts, tc = _ffn_tiles(S, F)

    def body(g_ref, gp_ref, u_ref, cw_ref, cb_ref, a_ref, act_ref):
        cur = g_ref[...]
        prev = gp_ref[...] * (pl.program_id(1) > 0).astype(F32)
        a = (cb_ref[...] + cw_ref[0:1, :] * _shift_down(cur, prev, 2) + cw_ref[1:2, :] * _shift_down(cur, prev, 1)
             + cw_ref[2:3, :] * cur)
        a_ref[...] = a
        act_ref[...] = (a * _sigmoid(a) * u_ref[...]).astype(BF16)

    cur_spec = pl.BlockSpec((ts, tc), lambda j, i: (i, j))
    return pl.pallas_call(
        body, name="ffn_act_fwd", grid=(F // tc, S // ts),
        in_specs=[cur_spec, pl.BlockSpec((ts, tc), lambda j, i: (jnp.maximum(i - 1, 0), j)), cur_spec,
                  pl.BlockSpec((CONV_WIDTH, tc), lambda j, i: (0, j)), pl.BlockSpec((1, tc), lambda j, i: (0, j))],
        out_specs=[cur_spec, cur_spec],
        out_shape=[jax.ShapeDtypeStruct((S, F), F32), jax.ShapeDtypeStruct((S, F), BF16)],
        compiler_params=_params(("parallel", "parallel")),
    )(G, G, U, cw, cb)


def _ffn_bwd_act(a, U, dact):
    S, F = a.shape
    ts, tc = _ffn_tiles(S, F)

    def body(a_ref, u_ref, d_ref, da_ref, du_ref, db_ref):
        av, dv = a_ref[...], d_ref[...]
        sg = _sigmoid(av)
        du_ref[...] = (dv * av * sg).astype(BF16)
        da = dv * u_ref[...] * sg * (1.0 + av * (1.0 - sg))
        da_ref[...] = da

        @pl.when(pl.program_id(1) == 0)
        def _():
            db_ref[...] = jnp.zeros_like(db_ref)

        db_ref[...] += _colsum(da)

    cur_spec = pl.BlockSpec((ts, tc), lambda j, i: (i, j))
    return pl.pallas_call(
        body, name="ffn_act_bwd", grid=(F // tc, S // ts),
        in_specs=[cur_spec, cur_spec, cur_spec],
        out_specs=[cur_spec, cur_spec, pl.BlockSpec((1, tc), lambda j, i: (0, j))],
        out_shape=[jax.ShapeDtypeStruct((S, F), F32), jax.ShapeDtypeStruct((S, F), BF16),
                   jax.ShapeDtypeStruct((1, F), F32)],
        compiler_params=_params(("parallel", "arbitrary")),
    )(a, U, dact)


def _ffn_bwd_conv(da, G, cw):
    S, F = da.shape
    ts, tc = _ffn_tiles(S, F)
    n = S // ts

    def body(d_ref, dn_ref, g_ref, gp_ref, cw_ref, dg_ref, dw_ref):
        i = pl.program_id(1)
        dcur = d_ref[...]
        dnxt = dn_ref[...] * (i < n - 1).astype(F32)
        cur = g_ref[...]
        prev = gp_ref[...] * (i > 0).astype(F32)
        dg = (cw_ref[2:3, :] * dcur + cw_ref[1:2, :] * _shift_up(dcur, dnxt, 1)
              + cw_ref[0:1, :] * _shift_up(dcur, dnxt, 2))
        dg_ref[...] = dg.astype(BF16)

        @pl.when(i == 0)
        def _():
            dw_ref[...] = jnp.zeros_like(dw_ref)

        dw_ref[0:1, :] += _colsum(dcur * _shift_down(cur, prev, 2))
        dw_ref[1:2, :] += _colsum(dcur * _shift_down(cur, prev, 1))
        dw_ref[2:3, :] += _colsum(dcur * cur)

    cur_spec = pl.BlockSpec((ts, tc), lambda j, i: (i, j))
    return pl.pallas_call(
        body, name="ffn_conv_bwd", grid=(F // tc, n),
        in_specs=[cur_spec, pl.BlockSpec((ts, tc), lambda j, i: (jnp.minimum(i + 1, n - 1), j)),
                  cur_spec, pl.BlockSpec((ts, tc), lambda j, i: (jnp.maximum(i - 1, 0), j)),
                  pl.BlockSpec((CONV_WIDTH, tc), lambda j, i: (0, j))],
        out_specs=[cur_spec, pl.BlockSpec((CONV_WIDTH, tc), lambda j, i: (0, j))],
        out_shape=[jax.ShapeDtypeStruct((S, F), BF16), jax.ShapeDtypeStruct((CONV_WIDTH, F), F32)],
        compiler_params=_params(("parallel", "arbitrary")),
    )(da, da, G, G, cw)


class _Cfg:
    def __init__(self, S, D):
        self.S, self.D = S, D
        self.RD = D // (2 * RET_HEADS)
        self.RW = RET_HEADS * self.RD
        self.MV = (D - self.RW) // MLA_HEADS
        self.QL, self.KVL = D // 4, D // 8
        self.F = ((8 * D // 3 + 255) // 256) * 256
        self.IN = 4 * self.RW + self.QL + self.KVL + MLA_ROPE
        self.INs = self.IN // N_CHIP
        self.INp = _round_up(self.INs, LANE)
        self.Fs = self.F // N_CHIP
        self.Fp = _round_up(self.Fs, LANE)
        self.F4 = N_CHIP * self.Fp
        self.QK = MLA_NOPE + MLA_ROPE
        self.KVH = MLA_NOPE + self.MV


def _head_major(t, H, d):
    S = t.shape[0]
    return t.reshape(S, H, d).transpose(1, 0, 2).reshape(H * S, d)


def _seq_major(t, H, d):
    S = t.shape[0] // H
    return t.reshape(H, S, d).transpose(1, 0, 2).reshape(S, H * d)


def _rope_tables(cfg):
    S = cfg.S

    def cs(dim):
        inv = 1.0 / (ROPE_BASE ** (jnp.arange(0, dim, 2, dtype=F32) / dim))
        ang = jnp.arange(S, dtype=F32)[:, None] * inv[None, :]
        return jnp.cos(ang), jnp.sin(ang)

    c, s = cs(cfg.RD)
    z = jnp.zeros_like(s)
    ret = (jnp.concatenate([c, c], 1), jnp.concatenate([-s, z], 1), jnp.concatenate([z, s], 1))
    c, s = cs(MLA_ROPE)
    one = jnp.ones((S, MLA_NOPE), F32)
    zn = jnp.zeros((S, MLA_NOPE), F32)
    tail1 = jnp.ones((S, MLA_QK_PAD - cfg.QK), F32)
    tail0 = jnp.zeros((S, MLA_QK_PAD - cfg.QK), F32)
    z = jnp.zeros_like(s)
    mla = (jnp.concatenate([one, c, c, tail1], 1), jnp.concatenate([zn, -s, z, tail0], 1),
           jnp.concatenate([zn, z, s, tail0], 1))
    return ret, mla


def _local_step(cfg, x, p, tgt, W, sp):
    S, D, RD, RW, MV = cfg.S, cfg.D, cfg.RD, cfg.RW, cfg.MV
    H, MH = RET_HEADS, MLA_HEADS
    (rc, rsa, rsb), (mc, msa, msb) = _rope_tables(cfg)
    dec, q_dec, k_dec, s_dec = _ret_tables(min(RET_BLOCK, S), RD)
    k_scale = RD ** -0.5
    a_scale = cfg.QK ** -0.5
    gb, gs = {}, {}

    hn1, r1 = _rms_fwd(x, sp['g_attn'], name="rms_attn")
    proj = _mm_nn(hn1, W['w_in'], name="in_proj")
    proj = proj.reshape(S, N_CHIP, cfg.INp)[:, :, :cfg.INs].reshape(S, cfg.IN)
    cuts = [RW, 2 * RW, 3 * RW, 4 * RW, 4 * RW + cfg.QL, 4 * RW + cfg.QL + cfg.KVL]
    rq, rk, rv, rg, cq, ckv, kr = jnp.split(proj, cuts, axis=1)

    rq_h, rk_h, rv_h, rg_h = (_head_major(t, H, RD) for t in (rq, rk, rv, rg))
    q_r = _rope(rq_h, rc, rsa, rsb, RD // 2, 1.0, BF16, name="ret_rope_q").reshape(H, S, RD)
    k_r = _rope(rk_h, rc, rsa, rsb, RD // 2, k_scale, BF16, name="ret_rope_k").reshape(H, S, RD)
    v_r = rv_h.astype(BF16).reshape(H, S, RD)
    o_h, states = _ret_fwd(q_r, k_r, v_r, dec, q_dec, k_dec, s_dec)
    o_h = o_h.reshape(H * S, RD)

    def gate_fn(ov, gv):
        oc = ov - _mean(ov)
        ron = oc * lax.rsqrt(_mean(oc * oc) + EPS)
        return [gv * _sigmoid(gv) * ron]
    ro_h = _rowwise(gate_fn, [(o_h, 'row'), (rg_h, 'row')], [(RD, BF16)], name="ret_gate")[0]
    ro = _seq_major(ro_h, H, RD)

    cqn, rcq = _rms_fwd(cq, sp['g_q_lora'], name="rms_q")
    ckvn, rckv = _rms_fwd(ckv, sp['g_kv_lora'], name="rms_kv")
    Q = _mm_nn(cqn, W['w_uq'], name="q_up")
    KV = _mm_nn(ckvn, W['w_ukv'], name="kv_up")
    pad = jnp.zeros((MH, S, MLA_QK_PAD - cfg.QK), F32)
    q_raw = jnp.concatenate([Q.reshape(S, MH, cfg.QK).transpose(1, 0, 2), pad], 2).reshape(MH * S, MLA_QK_PAD)
    KV3 = KV.reshape(S, MH, cfg.KVH).transpose(1, 0, 2)
    k_raw = jnp.concatenate([KV3[:, :, :MLA_NOPE], jnp.broadcast_to(kr[None], (MH, S, MLA_ROPE)), pad], 2)
    k_raw = k_raw.reshape(MH * S, MLA_QK_PAD)
    v_m = KV3[:, :, MLA_NOPE:].astype(BF16)
    q_m = _rope(q_raw, mc, msa, msb, MLA_ROPE // 2, 1.0, BF16, name="mla_rope_q").reshape(MH, S, MLA_QK_PAD)
    k_m = _rope(k_raw, mc, msa, msb, MLA_ROPE // 2, 1.0, BF16, name="mla_rope_k").reshape(MH, S, MLA_QK_PAD)
    mo_h, lse = _flash_fwd(q_m, k_m, v_m, a_scale)
    mo = _seq_major(mo_h.reshape(MH * S, MV), MH, MV)

    cat = jnp.concatenate([ro, mo.astype(BF16)], axis=1)
    h1 = _mm_nn(cat, W['w_o'], name="out_proj", res=x)

    hn2, r2 = _rms_fwd(h1, sp['g_ffn'], name="rms_ffn")
    G = _mm_nn(hn2, W['w_ffn_gate'], name="ffn_gate")
    U = _mm_nn(hn2, W['w_ffn_up'], name="ffn_up")
    a, act = _ffn_fwd(G, U, sp['conv_w'], sp['conv_b'])
    h2 = _mm_nn(act, W['w_ffn_down'], name="ffn_down", res=h1)

    hn3, r3 = _rms_fwd(h2, sp['g_ple'], name="rms_ple")
    Z = _mm_nn(hn3, W['w_ple_gate'], name="ple_gate")
    p_b = p.astype(BF16)
    PP = _mm_nn(p_b, W['w_ple_proj'], name="ple_proj")

    def head_fn(h2v, zv, ppv, tv, gv):
        gate = _sigmoid(zv)
        h3 = h2v + gate * ppv
        r4 = lax.rsqrt(_mean(h3 * h3) + EPS)
        n4 = h3 * r4
        e = n4 * gv - tv
        dy = e * (1.0 / D)
        dn = dy * gv
        dh3 = r4 * (dn - n4 * _mean(dn * n4))
        dpp = dh3 * gate
        dz = dh3 * ppv * gate * (1.0 - gate)
        loss = jnp.sum(0.5 * _mean(e * e), axis=0, keepdims=True)
        return [dh3, dz, dpp, _colsum(dy * n4), jnp.broadcast_to(loss, (1, LANE))]
    dh3, dZ, dPP, dgf, loss = _rowwise(
        head_fn, [(h2, 'row'), (Z, 'row'), (PP, 'row'), (tgt, 'row'), (sp['g_final'], 'bcast')],
        [(D, F32), (D, BF16), (D, BF16)], [D, LANE], name="ple_loss_head")
    gs['g_final'] = dgf
    loss = loss[0, 0]

    gb['w_ple_proj'] = _mm_tn(p_b, dPP, N_CHIP, name="ple_proj_dw")
    gb['w_ple_gate'] = _mm_tn(hn3, dZ, 1, name="ple_gate_dw")
    dhn3 = _mm_nt(dZ, W['w_ple_gate'], name="ple_gate_dx")
    dh2, dh2_b, gs['g_ple'] = _rms_bwd(h2, r3, dhn3, sp['g_ple'], dh3, name="rms_ple_bwd")

    dact = _mm_nt(dh2_b, W['w_ffn_down'], name="ffn_down_dx")
    gb['w_ffn_down'] = _mm_tn(act, dh2_b, 1, name="ffn_down_dw")
    da, dU, gs['conv_b'] = _ffn_bwd_act(a, U, dact)
    dG, gs['conv_w'] = _ffn_bwd_conv(da, G, sp['conv_w'])
    gb['w_ffn_gate'] = _mm_tn(hn2, dG, N_CHIP, name="ffn_gate_dw")
    gb['w_ffn_up'] = _mm_tn(hn2, dU, N_CHIP, name="ffn_up_dw")
    dhn2 = _mm_nt(dG, W['w_ffn_gate'], name="ffn_gate_dx")
    dhn2 = _mm_nt(dU, W['w_ffn_up'], name="ffn_up_dx", res=dhn2)
    dh1, dh1_b, gs['g_ffn'] = _rms_bwd(h1, r2, dhn2, sp['g_ffn'], dh2, name="rms_ffn_bwd")

    dcat = _mm_nt(dh1_b, W['w_o'], name="out_proj_dx")
    gb['w_o'] = _mm_tn(cat, dh1_b, 1, name="out_proj_dw")
    dro_h = _head_major(dcat[:, :RW], H, RD)

    def gate_bwd_fn(ov, gv, dv):
        oc = ov - _mean(ov)
        rs = lax.rsqrt(_mean(oc * oc) + EPS)
        ron = oc * rs
        sg = _sigmoid(gv)
        dron = dv * gv * sg
        drg = dv * ron * sg * (1.0 + gv * (1.0 - sg))
        do = rs * (dron - _mean(dron) - ron * _mean(dron * ron))
        return [do, drg]
    do_h, drg_h = _rowwise(gate_bwd_fn, [(o_h, 'row'), (rg_h, 'row'), (dro_h, 'row')],
                           [(RD, BF16), (RD, F32)], name="ret_gate_bwd")
    gq_r, gk_r, gv_r = _ret_bwd(q_r, k_r, v_r, do_h.reshape(H, S, RD), states, dec, q_dec, k_dec, s_dec)
    drq_h = _rope(gq_r.reshape(H * S, RD), rc, -rsa, -rsb, RD // 2, 1.0, F32, name="ret_rope_q_bwd")
    drk_h = _rope(gk_r.reshape(H * S, RD), rc, -rsa, -rsb, RD // 2, k_scale, F32, name="ret_rope_k_bwd")
    drq, drk, drv, drg = (_seq_major(t, H, RD) for t in (drq_h, drk_h, gv_r.reshape(H * S, RD), drg_h))

    dmo_h = _head_major(dcat[:, RW:], MH, MV)
    dlt = _rowwise(lambda ov, dv: [jnp.sum(ov * dv, axis=1, keepdims=True)],
                   [(mo_h.reshape(MH * S, MV), 'row'), (dmo_h, 'row')], [(1, F32)], name="mla_delta")[0]
    dmo_b = dmo_h.astype(BF16).reshape(MH, S, MV)
    dlt = dlt.reshape(MH, S, 1)
    gq_m = _flash_dq(q_m, k_m, v_m, dmo_b, lse, dlt, a_scale)
    gk_m, gv_m = _flash_dkv(q_m, k_m, v_m, dmo_b, lse, dlt, a_scale)
    dq_raw = _rope(gq_m.reshape(MH * S, MLA_QK_PAD), mc, -msa, -msb, MLA_ROPE // 2, 1.0, F32, name="mla_rope_q_bwd")
    dk_raw = _rope(gk_m.reshape(MH * S, MLA_QK_PAD), mc, -msa, -msb, MLA_ROPE // 2, 1.0, F32, name="mla_rope_k_bwd")
    dQ = dq_raw.reshape(MH, S, MLA_QK_PAD)[:, :, :cfg.QK].transpose(1, 0, 2).reshape(S, MH * cfg.QK).astype(BF16)
    dk3 = dk_raw.reshape(MH, S, MLA_QK_PAD)
    dKV = jnp.concatenate([dk3[:, :, :MLA_NOPE], gv_m], 2).transpose(1, 0, 2).reshape(S, MH * cfg.KVH).astype(BF16)
    dkr_heads = [(dk3[h, :, MLA_NOPE:cfg.QK], 'row') for h in range(MH)]
    dkr = _rowwise(lambda *v: [functools.reduce(lambda s, t: s + t, v)], dkr_heads, [(MLA_ROPE, F32)],
                   name="mla_rope_k_heads")[0]

    gb['w_uq'] = _mm_tn(cqn, dQ, N_CHIP, name="q_up_dw")
    dcqn = _mm_nt(dQ, W['w_uq'], name="q_up_dx")
    dcq, _, gs['g_q_lora'] = _rms_bwd(cq, rcq, dcqn, sp['g_q_lora'], None, name="rms_q_bwd")
    gb['w_ukv'] = _mm_tn(ckvn, dKV, N_CHIP, name="kv_up_dw")
    dckvn = _mm_nt(dKV, W['w_ukv'], name="kv_up_dx")
    dckv, _, gs['g_kv_lora'] = _rms_bwd(ckv, rckv, dckvn, sp['g_kv_lora'], None, name="rms_kv_bwd")

    dproj = jnp.concatenate([drq, drk, drv, drg, dcq, dckv, dkr], axis=1).astype(BF16)
    dproj = jnp.pad(dproj.reshape(S, N_CHIP, cfg.INs), ((0, 0), (0, 0), (0, cfg.INp - cfg.INs)))
    dproj = dproj.reshape(S, N_CHIP * cfg.INp)
    gb['w_in'] = _mm_tn(hn1, dproj, N_CHIP, name="in_proj_dw")
    dhn1 = _mm_nt(dproj, W['w_in'], name="in_proj_dx")
    dx, _, gs['g_attn'] = _rms_bwd(x, r1, dhn1, sp['g_attn'], dh1, name="rms_attn_bwd")
    return loss, dx, gb, gs


def _padded_shape(name, shape):
    K, n = shape
    if name in COL_SHARDED:
        return K, _round_up(n, LANE)
    return _round_up(K, LANE), n


def _pad_shard(name, w):
    K, n = _padded_shape(name, w.shape)
    return jnp.pad(w.astype(BF16), ((0, K - w.shape[0]), (0, n - w.shape[1])))


def _as_operand(name, gathered):
    if name in COL_SHARDED:
        return gathered
    return gathered.reshape(1, gathered.shape[0] * gathered.shape[1], gathered.shape[2])


def _grad_pieces(name, g):
    if name in COL_SHARDED:
        return g
    return g.reshape(N_CHIP, g.shape[1] // N_CHIP, g.shape[2])


def _channels_padded(v, cfg):
    r = v.shape[0]
    return jnp.pad(v.reshape(r, N_CHIP, cfg.Fs), ((0, 0), (0, 0), (0, cfg.Fp - cfg.Fs))).reshape(r, cfg.F4)


def _channels_unpadded(v, cfg):
    r = v.shape[0]
    return v.reshape(r, N_CHIP, cfg.Fp)[:, :, :cfg.Fs].reshape(r, cfg.F)


HBM = pl.BlockSpec(memory_space=pltpu.HBM)


def _mesh_pos():
    return lax.axis_index("x"), lax.axis_index("y"), lax.axis_index("c")


def _other_chips(x, y):
    return [(1 - x, y), (x, 1 - y), (1 - x, 1 - y)]


def _chip_id(cx, cy):
    return 2 * cx + cy


def _half(ref_rows, core):
    half = ref_rows // 2
    return pl.ds(core * half, half)


def _comm_call(body, name, ins, out_shape, sems):
    return pl.pallas_call(
        body, name=name, in_specs=[HBM] * len(ins), out_specs=[HBM] * len(out_shape), out_shape=out_shape,
        scratch_shapes=sems,
    )(*ins)


def _gather_weights(shards):
    n = len(shards)

    def body(*refs):
        srcs, outs = refs[:n], refs[n:2 * n]
        ici_send, ici_recv, d2d_send, d2d_recv, loc_sem = refs[2 * n:]
        x, y, c = _mesh_pos()
        mine = _chip_id(x, y)
        others = _other_chips(x, y)
        local = [pltpu.make_async_copy(srcs[w], outs[w].at[mine], loc_sem.at[w]) for w in range(n)]
        for cp in local:
            cp.start()

        def over_ici(w, j, chip):
            rows = _half(srcs[w].shape[0], c)
            return pltpu.make_async_remote_copy(
                src_ref=srcs[w].at[rows], dst_ref=outs[w].at[chip, rows],
                send_sem=ici_send.at[w, j], recv_sem=ici_recv.at[w, j],
                device_id=(*others[j], c), device_id_type=MESH)

        def over_d2d(w, j, core):
            rows = _half(srcs[w].shape[0], core)
            slab = outs[w].at[_chip_id(*others[j]), rows]
            return pltpu.make_async_remote_copy(
                src_ref=slab, dst_ref=slab, send_sem=d2d_send.at[w, j], recv_sem=d2d_recv.at[w, j],
                device_id=(x, y, 1 - c), device_id_type=MESH)

        sends = [over_ici(w, j, mine) for w in range(n) for j in range(3)]
        for cp in sends:
            cp.start()
        for w in range(n):
            for j in range(3):
                over_ici(w, j, _chip_id(*others[j])).wait_recv()
                fwd = over_d2d(w, j, c)
                fwd.start()
                sends.append(fwd)
        for w in range(n):
            for j in range(3):
                over_d2d(w, j, 1 - c).wait_recv()
        for cp in sends:
            cp.wait_send()
        for cp in local:
            cp.wait()

    out_shape = [jax.ShapeDtypeStruct((N_CHIP,) + s.shape, s.dtype) for s in shards]
    sems = [pltpu.SemaphoreType.DMA((n, 3))] * 4 + [pltpu.SemaphoreType.DMA((n,))]
    return _comm_call(body, "gather_weights", shards, out_shape, sems)


def _pair_exchange(grads):
    n = len(grads)

    def body(*refs):
        srcs, outs = refs[:n], refs[n:2 * n]
        send, recv = refs[2 * n:]
        x, y, c = _mesh_pos()
        copies = []
        for w in range(n):
            rows = _half(srcs[w].shape[1], 1 - c)
            copies.append(pltpu.make_async_remote_copy(
                src_ref=srcs[w].at[:, rows, :], dst_ref=outs[w], send_sem=send.at[w], recv_sem=recv.at[w],
                device_id=(x, y, 1 - c), device_id_type=MESH))
        for cp in copies:
            cp.start()
        for cp in copies:
            cp.wait()

    out_shape = [jax.ShapeDtypeStruct((g.shape[0], g.shape[1] // 2, g.shape[2]), g.dtype) for g in grads]
    return _comm_call(body, "grad_pair_exchange", grads, out_shape, [pltpu.SemaphoreType.DMA((n,))] * 2)


def _chip_exchange(parts):
    n = len(parts)

    def body(*refs):
        srcs, outs = refs[:n], refs[n:2 * n]
        send, recv = refs[2 * n:]
        x, y, c = _mesh_pos()
        others = _other_chips(x, y)
        copies = []
        for w in range(n):
            for j in range(3):
                copies.append(pltpu.make_async_remote_copy(
                    src_ref=srcs[w].at[_chip_id(*others[j])], dst_ref=outs[w].at[j],
                    send_sem=send.at[w, j], recv_sem=recv.at[w, j],
                    device_id=(*others[j], c), device_id_type=MESH))
        for cp in copies:
            cp.start()
        for cp in copies:
            cp.wait()

    out_shape = [jax.ShapeDtypeStruct((3,) + p.shape[1:], p.dtype) for p in parts]
    return _comm_call(body, "grad_chip_exchange", parts, out_shape, [pltpu.SemaphoreType.DMA((n, 3))] * 2)


def _sibling_share(halves):
    n = len(halves)

    def body(*refs):
        srcs, outs = refs[:n], refs[n:2 * n]
        send, recv, loc_sem = refs[2 * n:]
        x, y, c = _mesh_pos()
        local, remote = [], []
        for w in range(n):
            rows = _half(outs[w].shape[0], c)
            local.append(pltpu.make_async_copy(srcs[w], outs[w].at[rows], loc_sem.at[w]))
            remote.append(pltpu.make_async_remote_copy(
                src_ref=srcs[w], dst_ref=outs[w].at[rows], send_sem=send.at[w], recv_sem=recv.at[w],
                device_id=(x, y, 1 - c), device_id_type=MESH))
        for cp in local + remote:
            cp.start()
        for w in range(n):
            theirs = outs[w].at[_half(outs[w].shape[0], 1 - c)]
            pltpu.make_async_remote_copy(
                src_ref=srcs[w], dst_ref=theirs, send_sem=send.at[w], recv_sem=recv.at[w],
                device_id=(x, y, 1 - c), device_id_type=MESH).wait_recv()
        for cp in remote:
            cp.wait_send()
        for cp in local:
            cp.wait()

    out_shape = [jax.ShapeDtypeStruct((2 * h.shape[0], h.shape[1]), h.dtype) for h in halves]
    return _comm_call(body, "grad_sibling_share", halves, out_shape, [pltpu.SemaphoreType.DMA((n,))] * 3)


N_DEV = 8


def _gather_small(v):
    r, width = v.shape

    def body(v_ref, out_ref, send_sems, recv_sems, local_sem):
        x, y, c = _mesh_pos()
        me, sibling = (x, y, c), (x, y, 1 - c)
        chips = _other_chips(x, y)

        def rows(px, py, pc):
            return out_ref.at[pl.ds((4 * px + 2 * py + pc) * r, r), :]

        def copy(k, block, to, src=None):
            return pltpu.make_async_remote_copy(
                src_ref=rows(*block) if src is None else src, dst_ref=rows(*block),
                send_sem=send_sems.at[k], recv_sem=recv_sems.at[k], device_id=to, device_id_type=MESH)

        mine = pltpu.make_async_copy(v_ref, rows(*me), local_sem)
        mine.start()
        first = [copy(0, me, sibling, src=v_ref)]
        first += [copy(1 + j, me, (*chip, c), src=v_ref) for j, chip in enumerate(chips)]
        for cp in first:
            cp.start()
        passed = [copy(4 + j, (*chip, c), sibling) for j, chip in enumerate(chips)]
        for j, chip in enumerate(chips):
            copy(1 + j, (*chip, c), me).wait_recv()
            passed[j].start()
        copy(0, sibling, me).wait_recv()
        for j, chip in enumerate(chips):
            copy(4 + j, (*chip, 1 - c), me).wait_recv()
        for cp in first + passed:
            cp.wait_send()
        mine.wait()

    vmem = pl.BlockSpec(memory_space=pltpu.VMEM)
    return pl.pallas_call(
        body, name="gather_small", out_shape=jax.ShapeDtypeStruct((N_DEV * r, width), v.dtype),
        in_specs=[vmem], out_specs=vmem,
        scratch_shapes=[pltpu.SemaphoreType.DMA((7,)), pltpu.SemaphoreType.DMA((7,)), pltpu.SemaphoreType.DMA],
    )(v)


def _pack(arrays):
    flat = jnp.concatenate([a.reshape(-1) for a in arrays])
    size = _round_up(flat.shape[0], 8 * LANE)
    return jnp.pad(flat, (0, size - flat.shape[0])).reshape(size // LANE, LANE)


def _unpack(packed, shapes):
    flat = packed.reshape(-1)
    out, at = [], 0
    for s in shapes:
        size = math.prod(s)
        out.append(flat[at:at + size].reshape(s))
        at += size
    return out


def _sum_devices(gathered):
    r = gathered.shape[0] // N_DEV
    blocks = [(gathered[d * r:(d + 1) * r], 'row') for d in range(N_DEV)]
    return _rowwise(lambda *v: [functools.reduce(lambda s, t: s + t, v)], blocks, [(LANE, F32)],
                    name="small_grad_sum")[0]


def _reduce_tiles(half, n):
    return _tile(half, 128, 8)


def _pair_add(g32, r1, pos):
    G, K, n = g32.shape
    half = K // 2
    tr = _reduce_tiles(half, n)
    nrt = half // tr

    def body(pos_ref, g_ref, r_ref, s32_ref, sb_ref):
        s = g_ref[...] + r_ref[...].astype(F32)
        s32_ref[...] = s
        sb_ref[...] = s.astype(BF16)

    blk = pl.BlockSpec((None, tr, n), lambda k, i, pos: (k, i, 0))
    grid_spec = pltpu.PrefetchScalarGridSpec(
        num_scalar_prefetch=1, grid=(G, nrt),
        in_specs=[pl.BlockSpec((None, tr, n), lambda k, i, pos: (k, pos[0] * nrt + i, 0)), blk],
        out_specs=[blk, blk])
    return pl.pallas_call(
        body, name="grad_pair_add", grid_spec=grid_spec,
        out_shape=[jax.ShapeDtypeStruct((G, half, n), F32), jax.ShapeDtypeStruct((G, half, n), BF16)],
        compiler_params=_params(("parallel", "parallel")),
    )(pos, g32, r1)


def _chip_add(s32, r2, pos):
    _, half, n = s32.shape
    tr = _reduce_tiles(half, n)

    def body(pos_ref, s_ref, a_ref, b_ref, c_ref, o_ref):
        o_ref[...] = ((s_ref[...] + a_ref[...].astype(F32)) + b_ref[...].astype(F32)) + c_ref[...].astype(F32)

    def piece(j):
        return pl.BlockSpec((None, tr, n), lambda i, pos, j=j: (j, i, 0))

    grid_spec = pltpu.PrefetchScalarGridSpec(
        num_scalar_prefetch=1, grid=(half // tr,),
        in_specs=[pl.BlockSpec((None, tr, n), lambda i, pos: (pos[1], i, 0)), piece(0), piece(1), piece(2)],
        out_specs=pl.BlockSpec((tr, n), lambda i, pos: (i, 0)))
    return pl.pallas_call(
        body, name="grad_chip_add", grid_spec=grid_spec,
        out_shape=jax.ShapeDtypeStruct((half, n), F32),
        compiler_params=_params(("parallel",)),
    )(pos, s32, r2, r2, r2)


def _adamw(w, g, m, v, *, name):
    def fn(wv, gv, mv, vv):
        m2 = ADAM_B1 * mv + (1.0 - ADAM_B1) * gv
        v2 = ADAM_B2 * vv + (1.0 - ADAM_B2) * (gv * gv)
        m_hat = m2 / (1.0 - ADAM_B1 ** ADAM_STEP)
        v_hat = v2 / (1.0 - ADAM_B2 ** ADAM_STEP)
        delta = -ADAM_LR * (m_hat / (jnp.sqrt(v_hat) + ADAM_EPS) + ADAM_WD * wv)
        return [delta, m2, v2]
    width = w.shape[1]
    return _rowwise(fn, [(w, 'row'), (g, 'row'), (m, 'row'), (v, 'row')], [(width, F32)] * 3, name=name)


def kernel(x, p, w_in, g_attn, g_q_lora, g_kv_lora, w_uq, w_ukv, w_o, g_ffn, w_ffn_gate, w_ffn_up, conv_w, conv_b, w_ffn_down, g_ple, w_ple_gate, w_ple_proj, g_final, loss_target, m_w_in, m_g_attn, m_g_q_lora, m_g_kv_lora, m_w_uq, m_w_ukv, m_w_o, m_g_ffn, m_w_ffn_gate, m_w_ffn_up, m_conv_w, m_conv_b, m_w_ffn_down, m_g_ple, m_w_ple_gate, m_w_ple_proj, m_g_final, v_w_in, v_g_attn, v_g_q_lora, v_g_kv_lora, v_w_uq, v_w_ukv, v_w_o, v_g_ffn, v_w_ffn_gate, v_w_ffn_up, v_conv_w, v_conv_b, v_w_ffn_down, v_g_ple, v_w_ple_gate, v_w_ple_proj, v_g_final):
    weights = dict(w_in=w_in, g_attn=g_attn, g_q_lora=g_q_lora, g_kv_lora=g_kv_lora, w_uq=w_uq, w_ukv=w_ukv, w_o=w_o,
                   g_ffn=g_ffn, w_ffn_gate=w_ffn_gate, w_ffn_up=w_ffn_up, conv_w=conv_w, conv_b=conv_b,
                   w_ffn_down=w_ffn_down, g_ple=g_ple, w_ple_gate=w_ple_gate, w_ple_proj=w_ple_proj, g_final=g_final)
    mom1 = dict(w_in=m_w_in, g_attn=m_g_attn, g_q_lora=m_g_q_lora, g_kv_lora=m_g_kv_lora, w_uq=m_w_uq, w_ukv=m_w_ukv,
                w_o=m_w_o, g_ffn=m_g_ffn, w_ffn_gate=m_w_ffn_gate, w_ffn_up=m_w_ffn_up, conv_w=m_conv_w,
                conv_b=m_conv_b, w_ffn_down=m_w_ffn_down, g_ple=m_g_ple, w_ple_gate=m_w_ple_gate,
                w_ple_proj=m_w_ple_proj, g_final=m_g_final)
    mom2 = dict(w_in=v_w_in, g_attn=v_g_attn, g_q_lora=v_g_q_lora, g_kv_lora=v_g_kv_lora, w_uq=v_w_uq, w_ukv=v_w_ukv,
                w_o=v_w_o, g_ffn=v_g_ffn, w_ffn_gate=v_w_ffn_gate, w_ffn_up=v_w_ffn_up, conv_w=v_conv_w,
                conv_b=v_conv_b, w_ffn_down=v_w_ffn_down, g_ple=v_g_ple, w_ple_gate=v_w_ple_gate,
                w_ple_proj=v_w_ple_proj, g_final=v_g_final)
    _, S, D = x.shape
    cfg = _Cfg(S, D)
    cx, cy, cc = _mesh_pos()
    pos = jnp.stack([cc, _chip_id(cx, cy)]).astype(jnp.int32)

    shards = [_pad_shard(name, weights[name][0]) for name in BIG]
    gathered = _gather_weights(shards)
    W = {name: _as_operand(name, g) for name, g in zip(BIG, gathered)}
    cw_all = _gather_small(_pack([jnp.pad(conv_w[0], ((0, 0), (0, cfg.Fp - cfg.Fs)))]))
    r_cw = cw_all.shape[0] // N_DEV
    cw_chips = [_unpack(cw_all[2 * k * r_cw:(2 * k + 1) * r_cw], [(CONV_WIDTH, cfg.Fp)])[0] for k in range(N_CHIP)]
    sp = dict(g_attn=g_attn, g_q_lora=g_q_lora, g_kv_lora=g_kv_lora, g_ffn=g_ffn, g_ple=g_ple,
              g_final=g_final.reshape(1, D), conv_w=jnp.concatenate(cw_chips, axis=1),
              conv_b=_channels_padded(conv_b, cfg))

    loss, dx, gb, gs = _local_step(cfg, x[0], p[0, 0], loss_target[0], W, sp)
    loss = lax.psum(loss, ("x", "y", "c"))

    g32 = [_grad_pieces(name, gb[name][0]) for name in BIG]
    gbf = [_grad_pieces(name, gb[name][1]) for name in BIG]
    from_sibling = _pair_exchange(gbf)
    pair = [_pair_add(g, r, pos) for g, r in zip(g32, from_sibling)]
    from_chips = _chip_exchange([s[1] for s in pair])
    halves = [_chip_add(s[0], r, pos) for s, r in zip(pair, from_chips)]
    whole = _sibling_share(halves)
    grads = {}
    for name, g in zip(BIG, whole):
        K, n = weights[name].shape[1:]
        grads[name] = g[:K, :n].reshape(1, K, n)

    small_names = ['g_attn', 'g_q_lora', 'g_kv_lora', 'g_ffn', 'g_ple', 'g_final', 'conv_b', 'conv_w']
    small_sum = _sum_devices(_gather_small(_pack([gs[name] for name in small_names])))
    for name, g in zip(small_names, _unpack(small_sum, [gs[name].shape for name in small_names])):
        grads[name] = g
    grads['g_final'] = grads['g_final'].reshape(D)
    grads['conv_b'] = _channels_unpadded(grads['conv_b'], cfg)
    mine = _chip_id(cx, cy)
    grads['conv_w'] = lax.dynamic_slice_in_dim(grads['conv_w'], mine * cfg.Fp, cfg.Fp, axis=1)[:, :cfg.Fs]
    grads['conv_w'] = grads['conv_w'].reshape(1, CONV_WIDTH, cfg.Fs)

    delta, new_m, new_v = {}, {}, {}
    for name in WEIGHTS:
        shape = weights[name].shape
        flat = (shape[-2], shape[-1]) if len(shape) == 3 else (1, shape[-1])
        d, m2, v2 = _adamw(weights[name].reshape(flat), grads[name].reshape(flat), mom1[name].reshape(flat),
                           mom2[name].reshape(flat), name="adamw_" + name)
        delta[name], new_m[name], new_v[name] = d.reshape(shape), m2.reshape(shape), v2.reshape(shape)

    return (loss, dx.reshape(1, S, D), *[grads[n] for n in WEIGHTS], *[delta[n] for n in WEIGHTS],
            *[new_m[n] for n in WEIGHTS], *[new_v[n] for n in WEIGHTS])
```

```python
import functools
import math

import jax
import jax.numpy as jnp
from jax import lax
from jax.experimental import pallas as pl
from jax.experimental.pallas import tpu as pltpu

F32 = jnp.float32
BF16 = jnp.bfloat16

LANE = 128
VMEM_LIMIT = 56 * 1024 * 1024
ROWWISE_BLOCK_BYTES = 5 * 1024 * 1024

N_CHIP = 4
MESH = pl.DeviceIdType.MESH

CHUNK = 64
RET_HEADS = 8
MLA_HEADS = 16
MLA_NOPE = 128
MLA_ROPE = 64
MLA_QK_PAD = 256
PLE_DIM = 256
CONV_WIDTH = 3
ROPE_BASE = 10000.0
EPS = 1e-6
RET_BLOCK = 256

ADAM_LR = 0.001
ADAM_B1 = 0.9
ADAM_B2 = 0.999
ADAM_EPS = 1e-08
ADAM_WD = 0.01
ADAM_STEP = 10

WEIGHTS = ['w_in', 'g_attn', 'g_q_lora', 'g_kv_lora', 'w_uq', 'w_ukv', 'w_o', 'g_ffn', 'w_ffn_gate', 'w_ffn_up',
           'conv_w', 'conv_b', 'w_ffn_down', 'g_ple', 'w_ple_gate', 'w_ple_proj', 'g_final']
COL_SHARDED = ('w_in', 'w_uq', 'w_ukv', 'w_ffn_gate', 'w_ffn_up', 'w_ple_proj')
ROW_SHARDED = ('w_o', 'w_ffn_down', 'w_ple_gate')
BIG = COL_SHARDED + ROW_SHARDED
SMALL_REPLICATED = ('g_attn', 'g_q_lora', 'g_kv_lora', 'g_ffn', 'conv_b', 'g_ple', 'g_final')


def _round_up(n, m):
    return (n + m - 1) // m * m


def _tile(dim, cap, align=LANE):
    if dim <= cap:
        return dim
    t = cap // align * align
    while t >= align:
        if dim % t == 0:
            return t
        t -= align
    return dim


def _params(sem):
    return pltpu.CompilerParams(dimension_semantics=sem, vmem_limit_bytes=VMEM_LIMIT)


def _mm_call(name, dims, grid, in_specs, out_specs, out_shape, acc_shape, operands, has_res):
    nsteps = grid[2]
    n_out = len(out_shape)

    def body(*refs):
        a_ref, b_ref = refs[0], refs[1]
        res_ref = refs[2] if has_res else None
        outs = refs[2 + has_res:2 + has_res + n_out]
        acc = refs[2 + has_res + n_out]
        k = pl.program_id(2)

        @pl.when(k == 0)
        def _():
            acc[...] = jnp.zeros_like(acc)

        acc[...] += lax.dot_general(a_ref[...], b_ref[...], (dims, ((), ())), preferred_element_type=F32)

        @pl.when(k == nsteps - 1)
        def _():
            r = acc[...]
            if has_res:
                r = r + res_ref[...]
            for o in outs:
                o[...] = r.astype(o.dtype)

    return pl.pallas_call(
        body, name=name, grid=grid, in_specs=in_specs, out_specs=out_specs, out_shape=out_shape,
        scratch_shapes=[pltpu.VMEM(acc_shape, F32)],
        compiler_params=_params(("parallel", "parallel", "arbitrary")),
    )(*operands)


def _mm_nn(a, w, *, name, res=None, out_dtype=F32):
    M, K = a.shape
    G, _, n = w.shape
    tm, tn, tk = _tile(M, 1024), _tile(n, 1408), _tile(K, 1024)
    npg = n // tn
    grid = (M // tm, G * npg, K // tk)
    in_specs = [pl.BlockSpec((tm, tk), lambda i, j, k: (i, k)),
                pl.BlockSpec((None, tk, tn), lambda i, j, k: (j // npg, k, j % npg))]
    operands = [a, w]
    if res is not None:
        in_specs.append(pl.BlockSpec((tm, tn), lambda i, j, k: (i, j)))
        operands.append(res)
    out_specs = [pl.BlockSpec((tm, tn), lambda i, j, k: (i, j))]
    out_shape = [jax.ShapeDtypeStruct((M, G * n), out_dtype)]
    return _mm_call(name, ((1,), (0,)), grid, in_specs, out_specs, out_shape, (tm, tn), operands, res is not None)[0]


def _mm_nt(g, w, *, name, res=None, out_dtype=F32):
    M, _ = g.shape
    G, K, n = w.shape
    tm, tko, tn = _tile(M, 1024), _tile(K, 1024), _tile(n, 1408)
    npg = n // tn
    grid = (M // tm, K // tko, G * npg)
    in_specs = [pl.BlockSpec((tm, tn), lambda i, j, k: (i, k)),
                pl.BlockSpec((None, tko, tn), lambda i, j, k: (k // npg, j, k % npg))]
    operands = [g, w]
    if res is not None:
        in_specs.append(pl.BlockSpec((tm, tko), lambda i, j, k: (i, j)))
        operands.append(res)
    out_specs = [pl.BlockSpec((tm, tko), lambda i, j, k: (i, j))]
    out_shape = [jax.ShapeDtypeStruct((M, K), out_dtype)]
    return _mm_call(name, ((1,), (1,)), grid, in_specs, out_specs, out_shape, (tm, tko), operands, res is not None)[0]


def _mm_tn(a, g, groups, *, name):
    M, K = a.shape
    n = g.shape[1] // groups
    tm, tko, tn = _tile(M, 1024), _tile(K, 1024), _tile(n, 1408)
    npg = n // tn
    grid = (K // tko, groups * npg, M // tm)
    in_specs = [pl.BlockSpec((tm, tko), lambda i, j, k: (k, i)),
                pl.BlockSpec((tm, tn), lambda i, j, k: (k, j))]
    out_spec = pl.BlockSpec((None, tko, tn), lambda i, j, k: (j // npg, i, j % npg))
    out_shape = [jax.ShapeDtypeStruct((groups, K, n), F32), jax.ShapeDtypeStruct((groups, K, n), BF16)]
    return _mm_call(name, ((0,), (0,)), grid, in_specs, [out_spec, out_spec], out_shape, (tko, tn), [a, g], False)


def _rowwise(fn, ins, outs, accs=(), *, name):
    R = next(a.shape[0] for a, kind in ins if kind == 'row')
    row_bytes = sum(a.shape[1] * a.dtype.itemsize for a, kind in ins if kind != 'bcast')
    row_bytes += sum(w * jnp.dtype(dt).itemsize for w, dt in outs)
    ts = 8
    while ts * 2 <= 512 and ts * 2 * row_bytes <= ROWWISE_BLOCK_BYTES:
        ts *= 2
    ts = min(ts, R)
    for a, kind in ins:
        if kind == 'per':
            ts = math.gcd(ts, a.shape[0])
    while R % ts:
        ts //= 2
    in_specs = []
    for a, kind in ins:
        w = a.shape[1]
        if kind == 'row':
            in_specs.append(pl.BlockSpec((ts, w), lambda i: (i, 0)))
        elif kind == 'bcast':
            in_specs.append(pl.BlockSpec((1, w), lambda i: (0, 0)))
        else:
            nper = a.shape[0] // ts
            in_specs.append(pl.BlockSpec((ts, w), lambda i, nper=nper: (i % nper, 0)))
    out_specs = [pl.BlockSpec((ts, w), lambda i: (i, 0)) for w, _ in outs]
    out_specs += [pl.BlockSpec((1, w), lambda i: (0, 0)) for w in accs]
    out_shape = [jax.ShapeDtypeStruct((R, w), dt) for w, dt in outs]
    out_shape += [jax.ShapeDtypeStruct((1, w), F32) for w in accs]
    n_in, n_out = len(ins), len(outs)

    def body(*refs):
        vals = [r[...] for r in refs[:n_in]]
        res = fn(*vals)
        for r, v in zip(refs[n_in:n_in + n_out], res[:n_out]):
            r[...] = v.astype(r.dtype)
        if accs:
            first = pl.program_id(0) == 0
            for r, v in zip(refs[n_in + n_out:], res[n_out:]):
                @pl.when(first)
                def _(r=r, v=v):
                    r[...] = v

                @pl.when(jnp.logical_not(first))
                def _(r=r, v=v):
                    r[...] += v

    return pl.pallas_call(
        body, name=name, grid=(R // ts,), in_specs=in_specs, out_specs=out_specs, out_shape=out_shape,
        compiler_params=_params(("arbitrary",) if accs else ("parallel",)),
    )(*[a for a, _ in ins])


def _mean(v):
    return jnp.mean(v, axis=-1, keepdims=True)


def _colsum(v):
    return jnp.sum(v, axis=0, keepdims=True)


def _sigmoid(v):
    return 1.0 / (1.0 + jnp.exp(-v))


def _rms_fwd(x, g, *, name):
    def fn(xv, gv):
        r = lax.rsqrt(_mean(xv * xv) + EPS)
        return [xv * r * gv, r]
    w = x.shape[1]
    return _rowwise(fn, [(x, 'row'), (g, 'bcast')], [(w, BF16), (1, F32)], name=name)


def _rms_bwd(x, r, dhn, g, dres, *, name):
    def fn(xv, rv, dv, gv, *rest):
        n = xv * rv
        dn = dv * gv
        dx = rv * (dn - n * _mean(dn * n))
        if rest:
            dx = dx + rest[0]
        return [dx, dx, _colsum(dv * n)]
    w = x.shape[1]
    ins = [(x, 'row'), (r, 'row'), (dhn, 'row'), (g, 'bcast')]
    if dres is not None:
        ins.append((dres, 'row'))
    return _rowwise(fn, ins, [(w, F32), (w, BF16)], [w], name=name)


def _rope(x, cos, sin_a, sin_b, shift, scale, out_dtype, *, name):
    w = x.shape[1]
    wr = cos.shape[1]
    lo = w - wr

    def fn(xv, cv, sav, sbv):
        xr = xv[:, lo:] if lo else xv
        y = xr * cv + pltpu.roll(xr, wr - shift, 1) * sav + pltpu.roll(xr, shift, 1) * sbv
        if lo:
            y = jnp.concatenate([xv[:, :lo], y], axis=1)
        return [y * scale]
    return _rowwise(fn, [(x, 'row'), (cos, 'per'), (sin_a, 'per'), (sin_b, 'per')], [(w, out_dtype)], name=name)[0]


def _dot(a, b, ca, cb):
    return lax.dot_general(a, b, (((ca,), (cb,)), ((), ())), preferred_element_type=F32)


def _ret_fwd(q, k, v, dec, dq_, dk_, ds_):
    H, S, d = q.shape
    T = dec.shape[1]
    nC = S // T

    def body(q_ref, k_ref, v_ref, m_ref, qd_ref, kd_ref, sd_ref, o_ref, st_ref, state):
        @pl.when(pl.program_id(1) == 0)
        def _():
            state[...] = jnp.zeros_like(state)

        st = state[...]
        st_ref[...] = st
        qv, kv, vv = q_ref[...], k_ref[...], v_ref[...]
        p = (_dot(qv, kv, 1, 1) * m_ref[...]).astype(BF16)
        qs = (qv.astype(F32) * qd_ref[...]).astype(BF16)
        o_ref[...] = _dot(p, vv, 1, 0) + _dot(qs, st.astype(BF16), 1, 0)
        ks = (kv.astype(F32) * kd_ref[...]).astype(BF16)
        state[...] = st * sd_ref[...] + _dot(ks, vv, 0, 0)

    blk = pl.BlockSpec((None, T, d), lambda h, c: (h, c, 0))
    return pl.pallas_call(
        body, name="ret_fwd", grid=(H, nC),
        in_specs=[blk, blk, blk,
                  pl.BlockSpec((None, T, T), lambda h, c: (h, 0, 0)),
                  pl.BlockSpec((None, T, 1), lambda h, c: (h, 0, 0)),
                  pl.BlockSpec((None, T, 1), lambda h, c: (h, 0, 0)),
                  pl.BlockSpec((None, 1, 1), lambda h, c: (h, 0, 0))],
        out_specs=[blk, pl.BlockSpec((None, None, d, d), lambda h, c: (h, c, 0, 0))],
        out_shape=[jax.ShapeDtypeStruct((H, S, d), F32), jax.ShapeDtypeStruct((H, nC, d, d), F32)],
        scratch_shapes=[pltpu.VMEM((d, d), F32)],
        compiler_params=_params(("parallel", "arbitrary")),
    )(q, k, v, dec, dq_, dk_, ds_)


def _ret_bwd(q, k, v, do, states, dec, dq_, dk_, ds_):
    H, S, d = q.shape
    T = dec.shape[1]
    nC = S // T

    def body(q_ref, k_ref, v_ref, do_ref, st_ref, m_ref, qd_ref, kd_ref, sd_ref, gq_ref, gk_ref, gv_ref, dstate):
        @pl.when(pl.program_id(1) == 0)
        def _():
            dstate[...] = jnp.zeros_like(dstate)

        qv, kv, vv, dov = q_ref[...], k_ref[...], v_ref[...], do_ref[...]
        m = m_ref[...]
        qd, kd = qd_ref[...], kd_ref[...]
        ds = dstate[...]
        dsb = ds.astype(BF16)
        sb = st_ref[...].astype(BF16)
        p = (_dot(qv, kv, 1, 1) * m).astype(BF16)
        da = (_dot(dov, vv, 1, 1) * m).astype(BF16)
        qs = (qv.astype(F32) * qd).astype(BF16)
        ks = (kv.astype(F32) * kd).astype(BF16)
        gq_ref[...] = _dot(da, kv, 1, 0) + _dot(dov, sb, 1, 1) * qd
        gk_ref[...] = _dot(da, qv, 0, 0) + _dot(vv, dsb, 1, 1) * kd
        gv_ref[...] = _dot(p, dov, 0, 0) + _dot(ks, dsb, 1, 0)
        dstate[...] = ds * sd_ref[...] + _dot(qs, dov, 0, 0)

    blk = pl.BlockSpec((None, T, d), lambda h, c: (h, nC - 1 - c, 0))
    out = jax.ShapeDtypeStruct((H, S, d), F32)
    return pl.pallas_call(
        body, name="ret_bwd", grid=(H, nC),
        in_specs=[blk, blk, blk, blk,
                  pl.BlockSpec((None, None, d, d), lambda h, c: (h, nC - 1 - c, 0, 0)),
                  pl.BlockSpec((None, T, T), lambda h, c: (h, 0, 0)),
                  pl.BlockSpec((None, T, 1), lambda h, c: (h, 0, 0)),
                  pl.BlockSpec((None, T, 1), lambda h, c: (h, 0, 0)),
                  pl.BlockSpec((None, 1, 1), lambda h, c: (h, 0, 0))],
        out_specs=[blk, blk, blk], out_shape=[out, out, out],
        scratch_shapes=[pltpu.VMEM((d, d), F32)],
        compiler_params=_params(("parallel", "arbitrary")),
    )(q, k, v, do, states, dec, dq_, dk_, ds_)


def _ret_tables(T, d):
    h = jnp.arange(RET_HEADS, dtype=F32)
    log_g = jnp.log1p(-jnp.exp2(-5.0 - h))
    idx = jnp.arange(T, dtype=F32)
    diff = idx[:, None] - idx[None, :]
    same = (jnp.arange(T)[:, None] // CHUNK) == (jnp.arange(T)[None, :] // CHUNK)
    earlier = (jnp.arange(T)[None, :] // CHUNK) < (jnp.arange(T)[:, None] // CHUNK)
    expo = jnp.where(same, jnp.abs(diff), diff)
    dec = jnp.where(same | earlier, jnp.exp(log_g[:, None, None] * expo[None]), 0.0)
    q_dec = jnp.exp(log_g[:, None] * (idx + 1.0))[..., None]
    k_dec = jnp.exp(log_g[:, None] * (T - 1.0 - idx))[..., None]
    s_dec = jnp.exp(log_g * T)[:, None, None]
    return dec.astype(F32), q_dec, k_dec, s_dec


NEG = -1e30


ROW_GROUP = 128


def _scores(q, kv, r, diagonal, scale):
    s = _dot(q, kv, 1, 1) * scale
    if diagonal:
        rg, T = s.shape
        qc = (r * rg + lax.broadcasted_iota(jnp.int32, (rg, T), 0)) // CHUNK
        kc = lax.broadcasted_iota(jnp.int32, (rg, T), 1) // CHUNK
        s = jnp.where(kc <= qc, s, NEG)
    return s


def _flash_fwd(q, k, v, scale):
    H, S, dk = q.shape
    dv = v.shape[2]
    T = _tile(S, 512)
    n = S // T
    rg = min(ROW_GROUP, T)

    def body(q_ref, k_ref, v_ref, o_ref, lse_ref, m_s, l_s, acc):
        qi, ki = pl.program_id(1), pl.program_id(2)

        @pl.when(ki == 0)
        def _():
            m_s[...] = jnp.full_like(m_s, NEG)
            l_s[...] = jnp.zeros_like(l_s)
            acc[...] = jnp.zeros_like(acc)

        def step(diagonal):
            kv, vv = k_ref[...], v_ref[...]
            for r in range(T // rg):
                rows = pl.ds(r * rg, rg)
                s = _scores(q_ref[rows, :], kv, r, diagonal, scale)
                m_old = m_s[rows, :]
                m_new = jnp.maximum(m_old, jnp.max(s, axis=1, keepdims=True))
                p = jnp.exp(s - m_new)
                alpha = jnp.exp(m_old - m_new)
                l_s[rows, :] = alpha * l_s[rows, :] + jnp.sum(p, axis=1, keepdims=True)
                acc[rows, :] = alpha * acc[rows, :] + _dot(p.astype(BF16), vv, 1, 0)
                m_s[rows, :] = m_new

        @pl.when(ki < qi)
        def _():
            step(False)

        @pl.when(ki == qi)
        def _():
            step(True)
            o_ref[...] = acc[...] / l_s[...]
            lse_ref[...] = m_s[...] + jnp.log(l_s[...])

    kv_map = lambda h, i, j: (h, jnp.minimum(i, j), 0)
    return pl.pallas_call(
        body, name="mla_fwd", grid=(H, n, n),
        in_specs=[pl.BlockSpec((None, T, dk), lambda h, i, j: (h, i, 0)),
                  pl.BlockSpec((None, T, dk), kv_map),
                  pl.BlockSpec((None, T, dv), kv_map)],
        out_specs=[pl.BlockSpec((None, T, dv), lambda h, i, j: (h, i, 0)),
                   pl.BlockSpec((None, T, 1), lambda h, i, j: (h, i, 0))],
        out_shape=[jax.ShapeDtypeStruct((H, S, dv), F32), jax.ShapeDtypeStruct((H, S, 1), F32)],
        scratch_shapes=[pltpu.VMEM((T, 1), F32), pltpu.VMEM((T, 1), F32), pltpu.VMEM((T, dv), F32)],
        compiler_params=_params(("parallel", "parallel", "arbitrary")),
    )(q, k, v)


def _flash_dq(q, k, v, do, lse, dlt, scale):
    H, S, dk = q.shape
    dv = v.shape[2]
    T = _tile(S, 512)
    n = S // T
    rg = min(ROW_GROUP, T)

    def body(q_ref, k_ref, v_ref, do_ref, lse_ref, dlt_ref, dq_ref, acc):
        qi, ki = pl.program_id(1), pl.program_id(2)

        @pl.when(ki == 0)
        def _():
            acc[...] = jnp.zeros_like(acc)

        def step(diagonal):
            kv, vv = k_ref[...], v_ref[...]
            for r in range(T // rg):
                rows = pl.ds(r * rg, rg)
                p = jnp.exp(_scores(q_ref[rows, :], kv, r, diagonal, scale) - lse_ref[rows, :])
                dp = _dot(do_ref[rows, :], vv, 1, 1)
                ds = (p * (dp - dlt_ref[rows, :]) * scale).astype(BF16)
                acc[rows, :] += _dot(ds, kv, 1, 0)

        @pl.when(ki < qi)
        def _():
            step(False)

        @pl.when(ki == qi)
        def _():
            step(True)
            dq_ref[...] = acc[...]

    kv_map = lambda h, i, j: (h, jnp.minimum(i, j), 0)
    q_map = lambda h, i, j: (h, i, 0)
    return pl.pallas_call(
        body, name="mla_dq", grid=(H, n, n),
        in_specs=[pl.BlockSpec((None, T, dk), q_map), pl.BlockSpec((None, T, dk), kv_map),
                  pl.BlockSpec((None, T, dv), kv_map), pl.BlockSpec((None, T, dv), q_map),
                  pl.BlockSpec((None, T, 1), q_map), pl.BlockSpec((None, T, 1), q_map)],
        out_specs=pl.BlockSpec((None, T, dk), q_map),
        out_shape=jax.ShapeDtypeStruct((H, S, dk), F32),
        scratch_shapes=[pltpu.VMEM((T, dk), F32)],
        compiler_params=_params(("parallel", "parallel", "arbitrary")),
    )(q, k, v, do, lse, dlt)


def _flash_dkv(q, k, v, do, lse, dlt, scale):
    H, S, dk = q.shape
    dv = v.shape[2]
    T = _tile(S, 512)
    n = S // T
    rg = min(ROW_GROUP, T)

    def body(q_ref, k_ref, v_ref, do_ref, lse_ref, dlt_ref, dk_ref, dv_ref, acc_k, acc_v):
        ki, qi = pl.program_id(1), pl.program_id(2)

        @pl.when(qi == 0)
        def _():
            acc_k[...] = jnp.zeros_like(acc_k)
            acc_v[...] = jnp.zeros_like(acc_v)

        def step(diagonal):
            kv, vv = k_ref[...], v_ref[...]
            for r in range(T // rg):
                rows = pl.ds(r * rg, rg)
                qv, dov = q_ref[rows, :], do_ref[rows, :]
                p = jnp.exp(_scores(qv, kv, r, diagonal, scale) - lse_ref[rows, :])
                acc_v[...] += _dot(p.astype(BF16), dov, 0, 0)
                dp = _dot(dov, vv, 1, 1)
                ds = (p * (dp - dlt_ref[rows, :]) * scale).astype(BF16)
                acc_k[...] += _dot(ds, qv, 0, 0)

        @pl.when(qi > ki)
        def _():
            step(False)

        @pl.when(qi == ki)
        def _():
            step(True)

        @pl.when(qi == n - 1)
        def _():
            dk_ref[...] = acc_k[...]
            dv_ref[...] = acc_v[...]

    q_map = lambda h, j, i: (h, jnp.maximum(i, j), 0)
    kv_map = lambda h, j, i: (h, j, 0)
    return pl.pallas_call(
        body, name="mla_dkv", grid=(H, n, n),
        in_specs=[pl.BlockSpec((None, T, dk), q_map), pl.BlockSpec((None, T, dk), kv_map),
                  pl.BlockSpec((None, T, dv), kv_map), pl.BlockSpec((None, T, dv), q_map),
                  pl.BlockSpec((None, T, 1), q_map), pl.BlockSpec((None, T, 1), q_map)],
        out_specs=[pl.BlockSpec((None, T, dk), kv_map), pl.BlockSpec((None, T, dv), kv_map)],
        out_shape=[jax.ShapeDtypeStruct((H, S, dk), F32), jax.ShapeDtypeStruct((H, S, dv), F32)],
        scratch_shapes=[pltpu.VMEM((T, dk), F32), pltpu.VMEM((T, dv), F32)],
        compiler_params=_params(("parallel", "parallel", "arbitrary")),
    )(q, k, v, do, lse, dlt)


def _shift_down(cur, prev, by):
    rows = lax.broadcasted_iota(jnp.int32, cur.shape, 0)
    return jnp.where(rows < by, pltpu.roll(prev, by, 0), pltpu.roll(cur, by, 0))


def _shift_up(cur, nxt, by):
    ts = cur.shape[0]
    rows = lax.broadcasted_iota(jnp.int32, cur.shape, 0)
    return jnp.where(rows >= ts - by, pltpu.roll(nxt, ts - by, 0), pltpu.roll(cur, ts - by, 0))


def _ffn_tiles(S, F):
    return _tile(S, 256, 8), _tile(F, 1024)


def _ffn_fwd(G, U, cw, cb):
    S, F = G.shape
    ts, tc = _ffn_tiles(S, F)

    def body(g_ref, gp_ref, u_ref, cw_ref, cb_ref, a_ref, act_ref):
        cur = g_ref[...]
        prev = gp_ref[...] * (pl.program_id(1) > 0).astype(F32)
        a = (cb_ref[...] + cw_ref[0:1, :] * _shift_down(cur, prev, 2) + cw_ref[1:2, :] * _shift_down(cur, prev, 1)
             + cw_ref[2:3, :] * cur)
        a_ref[...] = a
        act_ref[...] = (a * _sigmoid(a) * u_ref[...]).astype(BF16)

    cur_spec = pl.BlockSpec((ts, tc), lambda j, i: (i, j))
    return pl.pallas_call(
        body, name="ffn_act_fwd", grid=(F // tc, S // ts),
        in_specs=[cur_spec, pl.BlockSpec((ts, tc), lambda j, i: (jnp.maximum(i - 1, 0), j)), cur_spec,
                  pl.BlockSpec((CONV_WIDTH, tc), lambda j, i: (0, j)), pl.BlockSpec((1, tc), lambda j, i: (0, j))],
        out_specs=[cur_spec, cur_spec],
        out_shape=[jax.ShapeDtypeStruct((S, F), F32), jax.ShapeDtypeStruct((S, F), BF16)],
        compiler_params=_params(("parallel", "parallel")),
    )(G, G, U, cw, cb)


def _ffn_bwd_act(a, U, dact):
    S, F = a.shape
    ts, tc = _ffn_tiles(S, F)

    def body(a_ref, u_ref, d_ref, da_ref, du_ref, db_ref):
        av, dv = a_ref[...], d_ref[...]
        sg = _sigmoid(av)
        du_ref[...] = (dv * av * sg).astype(BF16)
        da = dv * u_ref[...] * sg * (1.0 + av * (1.0 - sg))
        da_ref[...] = da

        @pl.when(pl.program_id(1) == 0)
        def _():
            db_ref[...] = jnp.zeros_like(db_ref)

        db_ref[...] += _colsum(da)

    cur_spec = pl.BlockSpec((ts, tc), lambda j, i: (i, j))
    return pl.pallas_call(
        body, name="ffn_act_bwd", grid=(F // tc, S // ts),
        in_specs=[cur_spec, cur_spec, cur_spec],
        out_specs=[cur_spec, cur_spec, pl.BlockSpec((1, tc), lambda j, i: (0, j))],
        out_shape=[jax.ShapeDtypeStruct((S, F), F32), jax.ShapeDtypeStruct((S, F), BF16),
                   jax.ShapeDtypeStruct((1, F), F32)],
        compiler_params=_params(("parallel", "arbitrary")),
    )(a, U, dact)


def _ffn_bwd_conv(da, G, cw):
    S, F = da.shape
    ts, tc = _ffn_tiles(S, F)
    n = S // ts

    def body(d_ref, dn_ref, g_ref, gp_ref, cw_ref, dg_ref, dw_ref):
        i = pl.program_id(1)
        dcur = d_ref[...]
        dnxt = dn_ref[...] * (i < n - 1).astype(F32)
        cur = g_ref[...]
        prev = gp_ref[...] * (i > 0).astype(F32)
        dg = (cw_ref[2:3, :] * dcur + cw_ref[1:2, :] * _shift_up(dcur, dnxt, 1)
              + cw_ref[0:1, :] * _shift_up(dcur, dnxt, 2))
        dg_ref[...] = dg.astype(BF16)

        @pl.when(i == 0)
        def _():
            dw_ref[...] = jnp.zeros_like(dw_ref)

        dw_ref[0:1, :] += _colsum(dcur * _shift_down(cur, prev, 2))
        dw_ref[1:2, :] += _colsum(dcur * _shift_down(cur, prev, 1))
        dw_ref[2:3, :] += _colsum(dcur * cur)

    cur_spec = pl.BlockSpec((ts, tc), lambda j, i: (i, j))
    return pl.pallas_call(
        body, name="ffn_conv_bwd", grid=(F // tc, n),
        in_specs=[cur_spec, pl.BlockSpec((ts, tc), lambda j, i: (jnp.minimum(i + 1, n - 1), j)),
                  cur_spec, pl.BlockSpec((ts, tc), lambda j, i: (jnp.maximum(i - 1, 0), j)),
                  pl.BlockSpec((CONV_WIDTH, tc), lambda j, i: (0, j))],
        out_specs=[cur_spec, pl.BlockSpec((CONV_WIDTH, tc), lambda j, i: (0, j))],
        out_shape=[jax.ShapeDtypeStruct((S, F), BF16), jax.ShapeDtypeStruct((CONV_WIDTH, F), F32)],
        compiler_params=_params(("parallel", "arbitrary")),
    )(da, da, G, G, cw)


class _Cfg:
    def __init__(self, S, D):
        self.S, self.D = S, D
        self.RD = D // (2 * RET_HEADS)
        self.RW = RET_HEADS * self.RD
        self.MV = (D - self.RW) // MLA_HEADS
        self.QL, self.KVL = D // 4, D // 8
        self.F = ((8 * D // 3 + 255) // 256) * 256
        self.IN = 4 * self.RW + self.QL + self.KVL + MLA_ROPE
        self.INs = self.IN // N_CHIP
        self.INp = _round_up(self.INs, LANE)
        self.Fs = self.F // N_CHIP
        self.Fp = _round_up(self.Fs, LANE)
        self.F4 = N_CHIP * self.Fp
        self.QK = MLA_NOPE + MLA_ROPE
        self.KVH = MLA_NOPE + self.MV


def _head_major(t, H, d):
    S = t.shape[0]
    return t.reshape(S, H, d).transpose(1, 0, 2).reshape(H * S, d)


def _seq_major(t, H, d):
    S = t.shape[0] // H
    return t.reshape(H, S, d).transpose(1, 0, 2).reshape(S, H * d)


def _rope_tables(cfg):
    S = cfg.S

    def cs(dim):
        inv = 1.0 / (ROPE_BASE ** (jnp.arange(0, dim, 2, dtype=F32) / dim))
        ang = jnp.arange(S, dtype=F32)[:, None] * inv[None, :]
        return jnp.cos(ang), jnp.sin(ang)

    c, s = cs(cfg.RD)
    z = jnp.zeros_like(s)
    ret = (jnp.concatenate([c, c], 1), jnp.concatenate([-s, z], 1), jnp.concatenate([z, s], 1))
    c, s = cs(MLA_ROPE)
    tail1 = jnp.ones((S, MLA_QK_PAD - cfg.QK), F32)
    tail0 = jnp.zeros((S, MLA_QK_PAD - cfg.QK), F32)
    z = jnp.zeros_like(s)
    mla = (jnp.concatenate([c, c, tail1], 1), jnp.concatenate([-s, z, tail0], 1), jnp.concatenate([z, s, tail0], 1))
    return ret, mla


def _local_step(cfg, x, p, tgt, W, sp):
    S, D, RD, RW, MV = cfg.S, cfg.D, cfg.RD, cfg.RW, cfg.MV
    H, MH = RET_HEADS, MLA_HEADS
    (rc, rsa, rsb), (mc, msa, msb) = _rope_tables(cfg)
    dec, q_dec, k_dec, s_dec = _ret_tables(min(RET_BLOCK, S), RD)
    k_scale = RD ** -0.5
    a_scale = cfg.QK ** -0.5
    gb, gs = {}, {}

    hn1, r1 = _rms_fwd(x, sp['g_attn'], name="rms_attn")
    proj = _mm_nn(hn1, W['w_in'], name="in_proj")
    proj = proj.reshape(S, N_CHIP, cfg.INp)[:, :, :cfg.INs].reshape(S, cfg.IN)
    cuts = [RW, 2 * RW, 3 * RW, 4 * RW, 4 * RW + cfg.QL, 4 * RW + cfg.QL + cfg.KVL]
    rq, rk, rv, rg, cq, ckv, kr = jnp.split(proj, cuts, axis=1)

    rq_h, rk_h, rv_h, rg_h = (_head_major(t, H, RD) for t in (rq, rk, rv, rg))
    q_r = _rope(rq_h, rc, rsa, rsb, RD // 2, 1.0, BF16, name="ret_rope_q").reshape(H, S, RD)
    k_r = _rope(rk_h, rc, rsa, rsb, RD // 2, k_scale, BF16, name="ret_rope_k").reshape(H, S, RD)
    v_r = rv_h.astype(BF16).reshape(H, S, RD)
    o_h, states = _ret_fwd(q_r, k_r, v_r, dec, q_dec, k_dec, s_dec)
    o_h = o_h.reshape(H * S, RD)

    def gate_fn(ov, gv):
        oc = ov - _mean(ov)
        ron = oc * lax.rsqrt(_mean(oc * oc) + EPS)
        return [gv * _sigmoid(gv) * ron]
    ro_h = _rowwise(gate_fn, [(o_h, 'row'), (rg_h, 'row')], [(RD, BF16)], name="ret_gate")[0]
    ro = _seq_major(ro_h, H, RD)

    cqn, rcq = _rms_fwd(cq, sp['g_q_lora'], name="rms_q")
    ckvn, rckv = _rms_fwd(ckv, sp['g_kv_lora'], name="rms_kv")
    Q = _mm_nn(cqn, W['w_uq'], name="q_up")
    KV = _mm_nn(ckvn, W['w_ukv'], name="kv_up")
    pad = jnp.zeros((MH, S, MLA_QK_PAD - cfg.QK), F32)
    q_raw = jnp.concatenate([Q.reshape(S, MH, cfg.QK).transpose(1, 0, 2), pad], 2).reshape(MH * S, MLA_QK_PAD)
    KV3 = KV.reshape(S, MH, cfg.KVH).transpose(1, 0, 2)
    k_raw = jnp.concatenate([KV3[:, :, :MLA_NOPE], jnp.broadcast_to(kr[None], (MH, S, MLA_ROPE)), pad], 2)
    k_raw = k_raw.reshape(MH * S, MLA_QK_PAD)
    v_m = KV3[:, :, MLA_NOPE:].astype(BF16)
    q_m = _rope(q_raw, mc, msa, msb, MLA_ROPE // 2, 1.0, BF16, name="mla_rope_q").reshape(MH, S, MLA_QK_PAD)
    k_m = _rope(k_raw, mc, msa, msb, MLA_ROPE // 2, 1.0, BF16, name="mla_rope_k").reshape(MH, S, MLA_QK_PAD)
    mo_h, lse = _flash_fwd(q_m, k_m, v_m, a_scale)
    mo = _seq_major(mo_h.reshape(MH * S, MV), MH, MV)

    cat = jnp.concatenate([ro, mo.astype(BF16)], axis=1)
    h1 = _mm_nn(cat, W['w_o'], name="out_proj", res=x)

    hn2, r2 = _rms_fwd(h1, sp['g_ffn'], name="rms_ffn")
    G = _mm_nn(hn2, W['w_ffn_gate'], name="ffn_gate")
    U = _mm_nn(hn2, W['w_ffn_up'], name="ffn_up")
    a, act = _ffn_fwd(G, U, sp['conv_w'], sp['conv_b'])
    h2 = _mm_nn(act, W['w_ffn_down'], name="ffn_down", res=h1)

    hn3, r3 = _rms_fwd(h2, sp['g_ple'], name="rms_ple")
    Z = _mm_nn(hn3, W['w_ple_gate'], name="ple_gate")
    p_b = p.astype(BF16)
    PP = _mm_nn(p_b, W['w_ple_proj'], name="ple_proj")

    def head_fn(h2v, zv, ppv, tv, gv):
        gate = _sigmoid(zv)
        h3 = h2v + gate * ppv
        r4 = lax.rsqrt(_mean(h3 * h3) + EPS)
        n4 = h3 * r4
        e = n4 * gv - tv
        dy = e * (1.0 / D)
        dn = dy * gv
        dh3 = r4 * (dn - n4 * _mean(dn * n4))
        dpp = dh3 * gate
        dz = dh3 * ppv * gate * (1.0 - gate)
        loss = jnp.sum(0.5 * _mean(e * e), axis=0, keepdims=True)
        return [dh3, dz, dpp, _colsum(dy * n4), jnp.broadcast_to(loss, (1, LANE))]
    dh3, dZ, dPP, dgf, loss = _rowwise(
        head_fn, [(h2, 'row'), (Z, 'row'), (PP, 'row'), (tgt, 'row'), (sp['g_final'], 'bcast')],
        [(D, F32), (D, BF16), (D, BF16)], [D, LANE], name="ple_loss_head")
    gs['g_final'] = dgf
    loss = loss[0, 0]

    gb['w_ple_proj'] = _mm_tn(p_b, dPP, N_CHIP, name="ple_proj_dw")
    gb['w_ple_gate'] = _mm_tn(hn3, dZ, 1, name="ple_gate_dw")
    dhn3 = _mm_nt(dZ, W['w_ple_gate'], name="ple_gate_dx")
    dh2, dh2_b, gs['g_ple'] = _rms_bwd(h2, r3, dhn3, sp['g_ple'], dh3, name="rms_ple_bwd")

    dact = _mm_nt(dh2_b, W['w_ffn_down'], name="ffn_down_dx")
    gb['w_ffn_down'] = _mm_tn(act, dh2_b, 1, name="ffn_down_dw")
    da, dU, gs['conv_b'] = _ffn_bwd_act(a, U, dact)
    dG, gs['conv_w'] = _ffn_bwd_conv(da, G, sp['conv_w'])
    gb['w_ffn_gate'] = _mm_tn(hn2, dG, N_CHIP, name="ffn_gate_dw")
    gb['w_ffn_up'] = _mm_tn(hn2, dU, N_CHIP, name="ffn_up_dw")
    dhn2 = _mm_nt(dG, W['w_ffn_gate'], name="ffn_gate_dx")
    dhn2 = _mm_nt(dU, W['w_ffn_up'], name="ffn_up_dx", res=dhn2)
    dh1, dh1_b, gs['g_ffn'] = _rms_bwd(h1, r2, dhn2, sp['g_ffn'], dh2, name="rms_ffn_bwd")

    dcat = _mm_nt(dh1_b, W['w_o'], name="out_proj_dx")
    gb['w_o'] = _mm_tn(cat, dh1_b, 1, name="out_proj_dw")
    dro_h = _head_major(dcat[:, :RW], H, RD)

    def gate_bwd_fn(ov, gv, dv):
        oc = ov - _mean(ov)
        rs = lax.rsqrt(_mean(oc * oc) + EPS)
        ron = oc * rs
        sg = _sigmoid(gv)
        dron = dv * gv * sg
        drg = dv * ron * sg * (1.0 + gv * (1.0 - sg))
        do = rs * (dron - _mean(dron) - ron * _mean(dron * ron))
        return [do, drg]
    do_h, drg_h = _rowwise(gate_bwd_fn, [(o_h, 'row'), (rg_h, 'row'), (dro_h, 'row')],
                           [(RD, BF16), (RD, F32)], name="ret_gate_bwd")
    gq_r, gk_r, gv_r = _ret_bwd(q_r, k_r, v_r, do_h.reshape(H, S, RD), states, dec, q_dec, k_dec, s_dec)
    drq_h = _rope(gq_r.reshape(H * S, RD), rc, -rsa, -rsb, RD // 2, 1.0, F32, name="ret_rope_q_bwd")
    drk_h = _rope(gk_r.reshape(H * S, RD), rc, -rsa, -rsb, RD // 2, k_scale, F32, name="ret_rope_k_bwd")
    drq, drk, drv, drg = (_seq_major(t, H, RD) for t in (drq_h, drk_h, gv_r.reshape(H * S, RD), drg_h))

    dmo_h = _head_major(dcat[:, RW:], MH, MV)
    dlt = _rowwise(lambda ov, dv: [jnp.sum(ov * dv, axis=1, keepdims=True)],
                   [(mo_h.reshape(MH * S, MV), 'row'), (dmo_h, 'row')], [(1, F32)], name="mla_delta")[0]
    dmo_b = dmo_h.astype(BF16).reshape(MH, S, MV)
    dlt = dlt.reshape(MH, S, 1)
    gq_m = _flash_dq(q_m, k_m, v_m, dmo_b, lse, dlt, a_scale)
    gk_m, gv_m = _flash_dkv(q_m, k_m, v_m, dmo_b, lse, dlt, a_scale)
    dq_raw = _rope(gq_m.reshape(MH * S, MLA_QK_PAD), mc, -msa, -msb, MLA_ROPE // 2, 1.0, F32, name="mla_rope_q_bwd")
    dk_raw = _rope(gk_m.reshape(MH * S, MLA_QK_PAD), mc, -msa, -msb, MLA_ROPE // 2, 1.0, F32, name="mla_rope_k_bwd")
    dQ = dq_raw.reshape(MH, S, MLA_QK_PAD)[:, :, :cfg.QK].transpose(1, 0, 2).reshape(S, MH * cfg.QK).astype(BF16)
    dk3 = dk_raw.reshape(MH, S, MLA_QK_PAD)
    dKV = jnp.concatenate([dk3[:, :, :MLA_NOPE], gv_m], 2).transpose(1, 0, 2).reshape(S, MH * cfg.KVH).astype(BF16)
    dkr_heads = [(dk3[h, :, MLA_NOPE:cfg.QK], 'row') for h in range(MH)]
    dkr = _rowwise(lambda *v: [functools.reduce(lambda s, t: s + t, v)], dkr_heads, [(MLA_ROPE, F32)],
                   name="mla_rope_k_heads")[0]

    gb['w_uq'] = _mm_tn(cqn, dQ, N_CHIP, name="q_up_dw")
    dcqn = _mm_nt(dQ, W['w_uq'], name="q_up_dx")
    dcq, _, gs['g_q_lora'] = _rms_bwd(cq, rcq, dcqn, sp['g_q_lora'], None, name="rms_q_bwd")
    gb['w_ukv'] = _mm_tn(ckvn, dKV, N_CHIP, name="kv_up_dw")
    dckvn = _mm_nt(dKV, W['w_ukv'], name="kv_up_dx")
    dckv, _, gs['g_kv_lora'] = _rms_bwd(ckv, rckv, dckvn, sp['g_kv_lora'], None, name="rms_kv_bwd")

    dproj = jnp.concatenate([drq, drk, drv, drg, dcq, dckv, dkr], axis=1).astype(BF16)
    dproj = jnp.pad(dproj.reshape(S, N_CHIP, cfg.INs), ((0, 0), (0, 0), (0, cfg.INp - cfg.INs)))
    dproj = dproj.reshape(S, N_CHIP * cfg.INp)
    gb['w_in'] = _mm_tn(hn1, dproj, N_CHIP, name="in_proj_dw")
    dhn1 = _mm_nt(dproj, W['w_in'], name="in_proj_dx")
    dx, _, gs['g_attn'] = _rms_bwd(x, r1, dhn1, sp['g_attn'], dh1, name="rms_attn_bwd")
    return loss, dx, gb, gs


def _padded_shape(name, shape):
    K, n = shape
    if name in COL_SHARDED:
        return K, _round_up(n, LANE)
    return _round_up(K, LANE), n


def _pad_shard(name, w):
    K, n = _padded_shape(name, w.shape)
    return jnp.pad(w.astype(BF16), ((0, K - w.shape[0]), (0, n - w.shape[1])))


def _as_operand(name, gathered):
    if name in COL_SHARDED:
        return gathered
    return gathered.reshape(1, gathered.shape[0] * gathered.shape[1], gathered.shape[2])


def _grad_pieces(name, g):
    if name in COL_SHARDED:
        return g
    return g.reshape(N_CHIP, g.shape[1] // N_CHIP, g.shape[2])


def _channels_padded(v, cfg):
    r = v.shape[0]
    return jnp.pad(v.reshape(r, N_CHIP, cfg.Fs), ((0, 0), (0, 0), (0, cfg.Fp - cfg.Fs))).reshape(r, cfg.F4)


def _channels_unpadded(v, cfg):
    r = v.shape[0]
    return v.reshape(r, N_CHIP, cfg.Fp)[:, :, :cfg.Fs].reshape(r, cfg.F)


HBM = pl.BlockSpec(memory_space=pltpu.HBM)


def _mesh_pos():
    return lax.axis_index("x"), lax.axis_index("y"), lax.axis_index("c")


def _other_chips(x, y):
    return [(1 - x, y), (x, 1 - y), (1 - x, 1 - y)]


def _chip_id(cx, cy):
    return 2 * cx + cy


def _half(ref_rows, core):
    half = ref_rows // 2
    return pl.ds(core * half, half)


def _comm_call(body, name, ins, out_shape, sems):
    return pl.pallas_call(
        body, name=name, in_specs=[HBM] * len(ins), out_specs=[HBM] * len(out_shape), out_shape=out_shape,
        scratch_shapes=sems,
    )(*ins)


def _gather_weights(shards):
    n = len(shards)

    def body(*refs):
        srcs, outs = refs[:n], refs[n:2 * n]
        ici_send, ici_recv, d2d_send, d2d_recv = refs[2 * n:]
        x, y, c = _mesh_pos()
        mine = _chip_id(x, y)
        others = _other_chips(x, y)

        def over_ici(w, j, chip):
            rows = _half(srcs[w].shape[0], c)
            return pltpu.make_async_remote_copy(
                src_ref=srcs[w].at[rows], dst_ref=outs[w].at[chip, rows],
                send_sem=ici_send.at[w, j], recv_sem=ici_recv.at[w, j],
                device_id=(*others[j], c), device_id_type=MESH)

        def over_d2d(w, j, core):
            rows = _half(srcs[w].shape[0], core)
            slab = outs[w].at[_chip_id(*others[j]), rows]
            return pltpu.make_async_remote_copy(
                src_ref=slab, dst_ref=slab, send_sem=d2d_send.at[w, j], recv_sem=d2d_recv.at[w, j],
                device_id=(x, y, 1 - c), device_id_type=MESH)

        sends = [over_ici(w, j, mine) for w in range(n) for j in range(3)]
        for cp in sends:
            cp.start()
        for w in range(n):
            for j in range(3):
                over_ici(w, j, _chip_id(*others[j])).wait_recv()
                fwd = over_d2d(w, j, c)
                fwd.start()
                sends.append(fwd)
        for w in range(n):
            for j in range(3):
                over_d2d(w, j, 1 - c).wait_recv()
        for cp in sends:
            cp.wait_send()

    out_shape = [jax.ShapeDtypeStruct((N_CHIP,) + s.shape, s.dtype) for s in shards]
    others = _comm_call(body, "gather_weights", shards, out_shape, [pltpu.SemaphoreType.DMA((n, 3))] * 4)
    cx, cy, _ = _mesh_pos()
    return [lax.dynamic_update_index_in_dim(o, s, _chip_id(cx, cy), 0) for o, s in zip(others, shards)]


def _pair_exchange(grads):
    n = len(grads)

    def body(*refs):
        srcs, outs = refs[:n], refs[n:2 * n]
        send, recv = refs[2 * n:]
        x, y, c = _mesh_pos()
        copies = []
        for w in range(n):
            rows = _half(srcs[w].shape[1], 1 - c)
            copies.append(pltpu.make_async_remote_copy(
                src_ref=srcs[w].at[:, rows, :], dst_ref=outs[w], send_sem=send.at[w], recv_sem=recv.at[w],
                device_id=(x, y, 1 - c), device_id_type=MESH))
        for cp in copies:
            cp.start()
        for cp in copies:
            cp.wait()

    out_shape = [jax.ShapeDtypeStruct((g.shape[0], g.shape[1] // 2, g.shape[2]), g.dtype) for g in grads]
    return _comm_call(body, "grad_pair_exchange", grads, out_shape, [pltpu.SemaphoreType.DMA((n,))] * 2)


def _chip_exchange(parts):
    n = len(parts)

    def body(*refs):
        srcs, outs = refs[:n], refs[n:2 * n]
        send, recv = refs[2 * n:]
        x, y, c = _mesh_pos()
        others = _other_chips(x, y)
        copies = []
        for w in range(n):
            for j in range(3):
                copies.append(pltpu.make_async_remote_copy(
                    src_ref=srcs[w].at[_chip_id(*others[j])], dst_ref=outs[w].at[j],
                    send_sem=send.at[w, j], recv_sem=recv.at[w, j],
                    device_id=(*others[j], c), device_id_type=MESH))
        for cp in copies:
            cp.start()
        for cp in copies:
            cp.wait()

    out_shape = [jax.ShapeDtypeStruct((3,) + p.shape[1:], p.dtype) for p in parts]
    return _comm_call(body, "grad_chip_exchange", parts, out_shape, [pltpu.SemaphoreType.DMA((n, 3))] * 2)


def _sibling_share(halves):
    n = len(halves)

    def body(*refs):
        srcs, outs = refs[:n], refs[n:2 * n]
        send, recv = refs[2 * n:]
        x, y, c = _mesh_pos()
        remote = []
        for w in range(n):
            rows = _half(outs[w].shape[0], c)
            remote.append(pltpu.make_async_remote_copy(
                src_ref=srcs[w], dst_ref=outs[w].at[rows], send_sem=send.at[w], recv_sem=recv.at[w],
                device_id=(x, y, 1 - c), device_id_type=MESH))
        for cp in remote:
            cp.start()
        for w in range(n):
            theirs = outs[w].at[_half(outs[w].shape[0], 1 - c)]
            pltpu.make_async_remote_copy(
                src_ref=srcs[w], dst_ref=theirs, send_sem=send.at[w], recv_sem=recv.at[w],
                device_id=(x, y, 1 - c), device_id_type=MESH).wait_recv()
        for cp in remote:
            cp.wait_send()

    out_shape = [jax.ShapeDtypeStruct((2 * h.shape[0], h.shape[1]), h.dtype) for h in halves]
    theirs = _comm_call(body, "grad_sibling_share", halves, out_shape, [pltpu.SemaphoreType.DMA((n,))] * 2)
    cc = lax.axis_index("c")
    return [lax.dynamic_update_slice_in_dim(t, h, cc * h.shape[0], axis=0) for t, h in zip(theirs, halves)]


N_DEV = 8


def _gather_small(v):
    r, width = v.shape

    def body(v_ref, out_ref, send_sems, recv_sems, local_sem):
        x, y, c = _mesh_pos()
        me, sibling = (x, y, c), (x, y, 1 - c)
        chips = _other_chips(x, y)

        def rows(px, py, pc):
            return out_ref.at[pl.ds((4 * px + 2 * py + pc) * r, r), :]

        def copy(k, block, to, src=None):
            return pltpu.make_async_remote_copy(
                src_ref=rows(*block) if src is None else src, dst_ref=rows(*block),
                send_sem=send_sems.at[k], recv_sem=recv_sems.at[k], device_id=to, device_id_type=MESH)

        mine = pltpu.make_async_copy(v_ref, rows(*me), local_sem)
        mine.start()
        first = [copy(0, me, sibling, src=v_ref)]
        first += [copy(1 + j, me, (*chip, c), src=v_ref) for j, chip in enumerate(chips)]
        for cp in first:
            cp.start()
        passed = [copy(4 + j, (*chip, c), sibling) for j, chip in enumerate(chips)]
        for j, chip in enumerate(chips):
            copy(1 + j, (*chip, c), me).wait_recv()
            passed[j].start()
        copy(0, sibling, me).wait_recv()
        for j, chip in enumerate(chips):
            copy(4 + j, (*chip, 1 - c), me).wait_recv()
        for cp in first + passed:
            cp.wait_send()
        mine.wait()

    vmem = pl.BlockSpec(memory_space=pltpu.VMEM)
    return pl.pallas_call(
        body, name="gather_small", out_shape=jax.ShapeDtypeStruct((N_DEV * r, width), v.dtype),
        in_specs=[vmem], out_specs=vmem,
        scratch_shapes=[pltpu.SemaphoreType.DMA((7,)), pltpu.SemaphoreType.DMA((7,)), pltpu.SemaphoreType.DMA],
    )(v)


def _pack(arrays):
    flat = jnp.concatenate([a.reshape(-1) for a in arrays])
    size = _round_up(flat.shape[0], 8 * LANE)
    return jnp.pad(flat, (0, size - flat.shape[0])).reshape(size // LANE, LANE)


def _unpack(packed, shapes):
    flat = packed.reshape(-1)
    out, at = [], 0
    for s in shapes:
        size = math.prod(s)
        out.append(flat[at:at + size].reshape(s))
        at += size
    return out


def _sum_devices(gathered):
    r = gathered.shape[0] // N_DEV
    blocks = [(gathered[d * r:(d + 1) * r], 'row') for d in range(N_DEV)]
    return _rowwise(lambda *v: [functools.reduce(lambda s, t: s + t, v)], blocks, [(LANE, F32)],
                    name="small_grad_sum")[0]


def _reduce_tiles(half, n):
    return _tile(half, 128, 8)


def _pair_add(g32, r1):
    G, K, n = g32.shape
    half = K // 2
    tr = _reduce_tiles(half, n)
    nrt = half // tr

    def body(g_ref, r_ref, s32_ref, sb_ref):
        s = g_ref[...] + r_ref[...].astype(F32)
        s32_ref[...] = s
        sb_ref[...] = s.astype(BF16)

    blk = pl.BlockSpec((None, tr, n), lambda k, i: (k, i, 0))
    return pl.pallas_call(
        body, name="grad_pair_add", grid=(G, nrt),
        in_specs=[pl.BlockSpec((None, tr, n), lambda k, i: (k, lax.axis_index("c") * nrt + i, 0)), blk],
        out_specs=[blk, blk],
        out_shape=[jax.ShapeDtypeStruct((G, half, n), F32), jax.ShapeDtypeStruct((G, half, n), BF16)],
        compiler_params=_params(("parallel", "parallel")),
    )(g32, r1)


def _chip_add(s32, r2):
    _, half, n = s32.shape
    tr = _reduce_tiles(half, n)

    def body(s_ref, a_ref, b_ref, c_ref, o_ref):
        o_ref[...] = ((s_ref[...] + a_ref[...].astype(F32)) + b_ref[...].astype(F32)) + c_ref[...].astype(F32)

    def piece(j):
        return pl.BlockSpec((None, tr, n), lambda i, j=j: (j, i, 0))

    def mine(i):
        return _chip_id(lax.axis_index("x"), lax.axis_index("y")), i, 0

    return pl.pallas_call(
        body, name="grad_chip_add", grid=(half // tr,),
        in_specs=[pl.BlockSpec((None, tr, n), mine), piece(0), piece(1), piece(2)],
        out_specs=pl.BlockSpec((tr, n), lambda i: (i, 0)),
        out_shape=jax.ShapeDtypeStruct((half, n), F32),
        compiler_params=_params(("parallel",)),
    )(s32, r2, r2, r2)


def _adamw_math(wv, gv, mv, vv):
    m2 = ADAM_B1 * mv + (1.0 - ADAM_B1) * gv
    v2 = ADAM_B2 * vv + (1.0 - ADAM_B2) * (gv * gv)
    m_hat = m2 / (1.0 - ADAM_B1 ** ADAM_STEP)
    v_hat = v2 / (1.0 - ADAM_B2 ** ADAM_STEP)
    delta = -ADAM_LR * (m_hat / (jnp.sqrt(v_hat) + ADAM_EPS) + ADAM_WD * wv)
    return [delta, m2, v2]


def _adamw(w, g, m, v, *, name):
    width = w.shape[1]
    return _rowwise(_adamw_math, [(w, 'row'), (g, 'row'), (m, 'row'), (v, 'row')], [(width, F32)] * 3, name=name)


def _adamw_sharded(w, g_padded, m, v, *, name):
    _, K, n = w.shape
    n_pad = g_padded.shape[1]
    ts = 8
    while ts * 2 <= 256 and ts * 2 * 8 * n_pad * 4 <= ROWWISE_BLOCK_BYTES and K % (ts * 2) == 0:
        ts *= 2

    def body(w_ref, g_ref, m_ref, v_ref, go_ref, d_ref, mo_ref, vo_ref):
        gv = g_ref[:, :n] if n != n_pad else g_ref[...]
        go_ref[...] = gv
        d_ref[...], mo_ref[...], vo_ref[...] = _adamw_math(w_ref[...], gv, m_ref[...], v_ref[...])

    blk = pl.BlockSpec((None, ts, n), lambda i: (0, i, 0))
    out = jax.ShapeDtypeStruct((1, K, n), F32)
    return pl.pallas_call(
        body, name=name, grid=(K // ts,),
        in_specs=[blk, pl.BlockSpec((ts, n_pad), lambda i: (i, 0)), blk, blk],
        out_specs=[blk] * 4, out_shape=[out] * 4,
        compiler_params=_params(("parallel",)),
    )(w, g_padded, m, v)


def kernel(x, p, w_in, g_attn, g_q_lora, g_kv_lora, w_uq, w_ukv, w_o, g_ffn, w_ffn_gate, w_ffn_up, conv_w, conv_b, w_ffn_down, g_ple, w_ple_gate, w_ple_proj, g_final, loss_target, m_w_in, m_g_attn, m_g_q_lora, m_g_kv_lora, m_w_uq, m_w_ukv, m_w_o, m_g_ffn, m_w_ffn_gate, m_w_ffn_up, m_conv_w, m_conv_b, m_w_ffn_down, m_g_ple, m_w_ple_gate, m_w_ple_proj, m_g_final, v_w_in, v_g_attn, v_g_q_lora, v_g_kv_lora, v_w_uq, v_w_ukv, v_w_o, v_g_ffn, v_w_ffn_gate, v_w_ffn_up, v_conv_w, v_conv_b, v_w_ffn_down, v_g_ple, v_w_ple_gate, v_w_ple_proj, v_g_final):
    weights = dict(w_in=w_in, g_attn=g_attn, g_q_lora=g_q_lora, g_kv_lora=g_kv_lora, w_uq=w_uq, w_ukv=w_ukv, w_o=w_o,
                   g_ffn=g_ffn, w_ffn_gate=w_ffn_gate, w_ffn_up=w_ffn_up, conv_w=conv_w, conv_b=conv_b,
                   w_ffn_down=w_ffn_down, g_ple=g_ple, w_ple_gate=w_ple_gate, w_ple_proj=w_ple_proj, g_final=g_final)
    mom1 = dict(w_in=m_w_in, g_attn=m_g_attn, g_q_lora=m_g_q_lora, g_kv_lora=m_g_kv_lora, w_uq=m_w_uq, w_ukv=m_w_ukv,
                w_o=m_w_o, g_ffn=m_g_ffn, w_ffn_gate=m_w_ffn_gate, w_ffn_up=m_w_ffn_up, conv_w=m_conv_w,
                conv_b=m_conv_b, w_ffn_down=m_w_ffn_down, g_ple=m_g_ple, w_ple_gate=m_w_ple_gate,
                w_ple_proj=m_w_ple_proj, g_final=m_g_final)
    mom2 = dict(w_in=v_w_in, g_attn=v_g_attn, g_q_lora=v_g_q_lora, g_kv_lora=v_g_kv_lora, w_uq=v_w_uq, w_ukv=v_w_ukv,
                w_o=v_w_o, g_ffn=v_g_ffn, w_ffn_gate=v_w_ffn_gate, w_ffn_up=v_w_ffn_up, conv_w=v_conv_w,
                conv_b=v_conv_b, w_ffn_down=v_w_ffn_down, g_ple=v_g_ple, w_ple_gate=v_w_ple_gate,
                w_ple_proj=v_w_ple_proj, g_final=v_g_final)
    _, S, D = x.shape
    cfg = _Cfg(S, D)
    cx, cy, _ = _mesh_pos()

    shards = [_pad_shard(name, weights[name][0]) for name in BIG]
    gathered = _gather_weights(shards)
    W = {name: _as_operand(name, g) for name, g in zip(BIG, gathered)}
    cw_all = _gather_small(_pack([jnp.pad(conv_w[0], ((0, 0), (0, cfg.Fp - cfg.Fs)))]))
    r_cw = cw_all.shape[0] // N_DEV
    cw_chips = [_unpack(cw_all[2 * k * r_cw:(2 * k + 1) * r_cw], [(CONV_WIDTH, cfg.Fp)])[0] for k in range(N_CHIP)]
    sp = dict(g_attn=g_attn, g_q_lora=g_q_lora, g_kv_lora=g_kv_lora, g_ffn=g_ffn, g_ple=g_ple,
              g_final=g_final.reshape(1, D), conv_w=jnp.concatenate(cw_chips, axis=1),
              conv_b=_channels_padded(conv_b, cfg))

    loss, dx, gb, gs = _local_step(cfg, x[0], p[0, 0], loss_target[0], W, sp)
    loss = lax.psum(loss, ("x", "y", "c"))

    g32 = [_grad_pieces(name, gb[name][0]) for name in BIG]
    gbf = [_grad_pieces(name, gb[name][1]) for name in BIG]
    from_sibling = _pair_exchange(gbf)
    pair = [_pair_add(g, r) for g, r in zip(g32, from_sibling)]
    from_chips = _chip_exchange([s[1] for s in pair])
    halves = [_chip_add(s[0], r) for s, r in zip(pair, from_chips)]
    whole = dict(zip(BIG, _sibling_share(halves)))
    grads = {}

    small_names = ['g_attn', 'g_q_lora', 'g_kv_lora', 'g_ffn', 'g_ple', 'g_final', 'conv_b', 'conv_w']
    small_sum = _sum_devices(_gather_small(_pack([gs[name] for name in small_names])))
    for name, g in zip(small_names, _unpack(small_sum, [gs[name].shape for name in small_names])):
        grads[name] = g
    grads['g_final'] = grads['g_final'].reshape(D)
    grads['conv_b'] = _channels_unpadded(grads['conv_b'], cfg)
    mine = _chip_id(cx, cy)
    grads['conv_w'] = lax.dynamic_slice_in_dim(grads['conv_w'], mine * cfg.Fp, cfg.Fp, axis=1)[:, :cfg.Fs]
    grads['conv_w'] = grads['conv_w'].reshape(1, CONV_WIDTH, cfg.Fs)

    delta, new_m, new_v = {}, {}, {}
    for name in BIG:
        grads[name], delta[name], new_m[name], new_v[name] = _adamw_sharded(
            weights[name], whole[name], mom1[name], mom2[name], name="adamw_" + name)
    for name in WEIGHTS:
        if name in BIG:
            continue
        shape = weights[name].shape
        flat = (shape[-2], shape[-1]) if len(shape) == 3 else (1, shape[-1])
        d, m2, v2 = _adamw(weights[name].reshape(flat), grads[name].reshape(flat), mom1[name].reshape(flat),
                           mom2[name].reshape(flat), name="adamw_" + name)
        delta[name], new_m[name], new_v[name] = d.reshape(shape), m2.reshape(shape), v2.reshape(shape)

    return (loss, dx.reshape(1, S, D), *[grads[n] for n in WEIGHTS], *[delta[n] for n in WEIGHTS],
            *[new_m[n] for n in WEIGHTS], *[new_v[n] for n in WEIGHTS])
```

```python
import functools
import math

import jax
import jax.numpy as jnp
from jax import lax
from jax.experimental import pallas as pl
from jax.experimental.pallas import tpu as pltpu

F32 = jnp.float32
BF16 = jnp.bfloat16

LANE = 128
VMEM_LIMIT = 56 * 1024 * 1024
ROWWISE_BLOCK_BYTES = 5 * 1024 * 1024

N_CHIP = 4
MESH = pl.DeviceIdType.MESH

CHUNK = 64
RET_HEADS = 8
MLA_HEADS = 16
MLA_NOPE = 128
MLA_ROPE = 64
MLA_QK_PAD = 256
PLE_DIM = 256
CONV_WIDTH = 3
ROPE_BASE = 10000.0
EPS = 1e-6
RET_BLOCK = 256

ADAM_LR = 0.001
ADAM_B1 = 0.9
ADAM_B2 = 0.999
ADAM_EPS = 1e-08
ADAM_WD = 0.01
ADAM_STEP = 10

WEIGHTS = ['w_in', 'g_attn', 'g_q_lora', 'g_kv_lora', 'w_uq', 'w_ukv', 'w_o', 'g_ffn', 'w_ffn_gate', 'w_ffn_up',
           'conv_w', 'conv_b', 'w_ffn_down', 'g_ple', 'w_ple_gate', 'w_ple_proj', 'g_final']
COL_SHARDED = ('w_in', 'w_uq', 'w_ukv', 'w_ffn_gate', 'w_ffn_up', 'w_ple_proj')
ROW_SHARDED = ('w_o', 'w_ffn_down', 'w_ple_gate')
BIG = COL_SHARDED + ROW_SHARDED
SMALL_REPLICATED = ('g_attn', 'g_q_lora', 'g_kv_lora', 'g_ffn', 'conv_b', 'g_ple', 'g_final')


def _round_up(n, m):
    return (n + m - 1) // m * m


def _tile(dim, cap, align=LANE):
    if dim <= cap:
        return dim
    t = cap // align * align
    while t >= align:
        if dim % t == 0:
            return t
        t -= align
    return dim


def _params(sem):
    return pltpu.CompilerParams(dimension_semantics=sem, vmem_limit_bytes=VMEM_LIMIT)


HBM = pl.BlockSpec(memory_space=pltpu.HBM)


class _Side:
    def __init__(self, ins, out_shape, sems, start, finish):
        self.ins, self.out_shape, self.sems, self.start, self.finish = ins, out_shape, sems, start, finish


def _run_side(side, name):
    n_in, n_out = len(side.ins), len(side.out_shape)

    def body(*refs):
        ins, outs, sems = refs[:n_in], refs[n_in:n_in + n_out], refs[n_in + n_out:]
        side.start(ins, outs, sems)
        side.finish(ins, outs, sems)

    return pl.pallas_call(
        body, name=name, in_specs=[HBM] * n_in, out_specs=[HBM] * n_out, out_shape=side.out_shape,
        scratch_shapes=side.sems,
    )(*side.ins)


def _hosted_call(body, side, *, name, grid, in_specs, out_specs, out_shape, scratch_shapes, operands, semantics):
    if side is None:
        res = pl.pallas_call(
            body, name=name, grid=grid, in_specs=in_specs, out_specs=out_specs, out_shape=out_shape,
            scratch_shapes=scratch_shapes, compiler_params=_params(semantics))(*operands)
        return res, []
    n_in, n_out, n_scr = len(in_specs), len(out_specs), len(scratch_shapes)
    s_in, s_out = len(side.ins), len(side.out_shape)

    def hosted(*refs):
        refs = list(refs)
        ins, refs = refs[:n_in], refs[n_in:]
        side_ins, refs = refs[:s_in], refs[s_in:]
        outs, refs = refs[:n_out], refs[n_out:]
        side_outs, refs = refs[:s_out], refs[s_out:]
        scratch, sems = refs[:n_scr], refs[n_scr:]
        ids = [pl.program_id(a) for a in range(len(grid))]
        first = functools.reduce(jnp.logical_and, [i == 0 for i in ids])
        last = functools.reduce(jnp.logical_and, [i == g - 1 for i, g in zip(ids, grid)])

        @pl.when(first)
        def _():
            side.start(side_ins, side_outs, sems)

        body(*ins, *outs, *scratch)

        @pl.when(last)
        def _():
            side.finish(side_ins, side_outs, sems)

    res = pl.pallas_call(
        hosted, name=name, grid=grid, in_specs=list(in_specs) + [HBM] * s_in,
        out_specs=list(out_specs) + [HBM] * s_out, out_shape=list(out_shape) + list(side.out_shape),
        scratch_shapes=list(scratch_shapes) + list(side.sems),
        compiler_params=_params(("arbitrary",) * len(grid)))(*operands, *side.ins)
    return res[:n_out], res[n_out:]


def _mm_call(name, dims, grid, in_specs, out_specs, out_shape, acc_shape, operands, has_res, side=None):
    nsteps = grid[2]
    n_out = len(out_shape)

    def body(*refs):
        a_ref, b_ref = refs[0], refs[1]
        res_ref = refs[2] if has_res else None
        outs = refs[2 + has_res:2 + has_res + n_out]
        acc = refs[2 + has_res + n_out]
        k = pl.program_id(2)

        @pl.when(k == 0)
        def _():
            acc[...] = jnp.zeros_like(acc)

        acc[...] += lax.dot_general(a_ref[...], b_ref[...], (dims, ((), ())), preferred_element_type=F32)

        @pl.when(k == nsteps - 1)
        def _():
            r = acc[...]
            if has_res:
                r = r + res_ref[...]
            for o in outs:
                o[...] = r.astype(o.dtype)

    outs, side_outs = _hosted_call(
        body, side, name=name, grid=grid, in_specs=in_specs, out_specs=out_specs, out_shape=out_shape,
        scratch_shapes=[pltpu.VMEM(acc_shape, F32)], operands=operands,
        semantics=("parallel", "parallel", "arbitrary"))
    return (outs, side_outs) if side is not None else outs


def _mm_nn(a, w, *, name, res=None, out_dtype=F32, side=None):
    M, K = a.shape
    G, _, n = w.shape
    tm, tn, tk = _tile(M, 1024), _tile(n, 1408), _tile(K, 1024)
    npg = n // tn
    grid = (M // tm, G * npg, K // tk)
    in_specs = [pl.BlockSpec((tm, tk), lambda i, j, k: (i, k)),
                pl.BlockSpec((None, tk, tn), lambda i, j, k: (j // npg, k, j % npg))]
    operands = [a, w]
    if res is not None:
        in_specs.append(pl.BlockSpec((tm, tn), lambda i, j, k: (i, j)))
        operands.append(res)
    out_specs = [pl.BlockSpec((tm, tn), lambda i, j, k: (i, j))]
    out_shape = [jax.ShapeDtypeStruct((M, G * n), out_dtype)]
    got = _mm_call(name, ((1,), (0,)), grid, in_specs, out_specs, out_shape, (tm, tn), operands, res is not None, side)
    return (got[0][0], got[1]) if side is not None else got[0]


def _mm_nt(g, w, *, name, res=None, out_dtype=F32):
    M, _ = g.shape
    G, K, n = w.shape
    tm, tko, tn = _tile(M, 1024), _tile(K, 1024), _tile(n, 1408)
    npg = n // tn
    grid = (M // tm, K // tko, G * npg)
    in_specs = [pl.BlockSpec((tm, tn), lambda i, j, k: (i, k)),
                pl.BlockSpec((None, tko, tn), lambda i, j, k: (k // npg, j, k % npg))]
    operands = [g, w]
    if res is not None:
        in_specs.append(pl.BlockSpec((tm, tko), lambda i, j, k: (i, j)))
        operands.append(res)
    out_specs = [pl.BlockSpec((tm, tko), lambda i, j, k: (i, j))]
    out_shape = [jax.ShapeDtypeStruct((M, K), out_dtype)]
    return _mm_call(name, ((1,), (1,)), grid, in_specs, out_specs, out_shape, (tm, tko), operands, res is not None)[0]


def _mm_tn(a, g, groups, *, name):
    M, K = a.shape
    n = g.shape[1] // groups
    tm, tko, tn = _tile(M, 1024), _tile(K, 1024), _tile(n, 1408)
    npg = n // tn
    grid = (K // tko, groups * npg, M // tm)
    in_specs = [pl.BlockSpec((tm, tko), lambda i, j, k: (k, i)),
                pl.BlockSpec((tm, tn), lambda i, j, k: (k, j))]
    out_spec = pl.BlockSpec((None, tko, tn), lambda i, j, k: (j // npg, i, j % npg))
    out_shape = [jax.ShapeDtypeStruct((groups, K, n), F32), jax.ShapeDtypeStruct((groups, K, n), BF16)]
    return _mm_call(name, ((0,), (0,)), grid, in_specs, [out_spec, out_spec], out_shape, (tko, tn), [a, g], False)


def _rowwise(fn, ins, outs, accs=(), *, name):
    R = next(a.shape[0] for a, kind in ins if kind == 'row')
    row_bytes = sum(a.shape[1] * a.dtype.itemsize for a, kind in ins if kind != 'bcast')
    row_bytes += sum(w * jnp.dtype(dt).itemsize for w, dt in outs)
    ts = 8
    while ts * 2 <= 512 and ts * 2 * row_bytes <= ROWWISE_BLOCK_BYTES:
        ts *= 2
    ts = min(ts, R)
    for a, kind in ins:
        if kind == 'per':
            ts = math.gcd(ts, a.shape[0])
    while R % ts:
        ts //= 2
    in_specs = []
    for a, kind in ins:
        w = a.shape[1]
        if kind == 'row':
            in_specs.append(pl.BlockSpec((ts, w), lambda i: (i, 0)))
        elif kind == 'bcast':
            in_specs.append(pl.BlockSpec((1, w), lambda i: (0, 0)))
        else:
            nper = a.shape[0] // ts
            in_specs.append(pl.BlockSpec((ts, w), lambda i, nper=nper: (i % nper, 0)))
    out_specs = [pl.BlockSpec((ts, w), lambda i: (i, 0)) for w, _ in outs]
    out_specs += [pl.BlockSpec((1, w), lambda i: (0, 0)) for w in accs]
    out_shape = [jax.ShapeDtypeStruct((R, w), dt) for w, dt in outs]
    out_shape += [jax.ShapeDtypeStruct((1, w), F32) for w in accs]
    n_in, n_out = len(ins), len(outs)

    def body(*refs):
        vals = [r[...] for r in refs[:n_in]]
        res = fn(*vals)
        for r, v in zip(refs[n_in:n_in + n_out], res[:n_out]):
            r[...] = v.astype(r.dtype)
        if accs:
            first = pl.program_id(0) == 0
            for r, v in zip(refs[n_in + n_out:], res[n_out:]):
                @pl.when(first)
                def _(r=r, v=v):
                    r[...] = v

                @pl.when(jnp.logical_not(first))
                def _(r=r, v=v):
                    r[...] += v

    return pl.pallas_call(
        body, name=name, grid=(R // ts,), in_specs=in_specs, out_specs=out_specs, out_shape=out_shape,
        compiler_params=_params(("arbitrary",) if accs else ("parallel",)),
    )(*[a for a, _ in ins])


def _mean(v):
    return jnp.mean(v, axis=-1, keepdims=True)


def _colsum(v):
    return jnp.sum(v, axis=0, keepdims=True)


def _sigmoid(v):
    return 1.0 / (1.0 + jnp.exp(-v))


def _rms_fwd(x, g, *, name):
    def fn(xv, gv):
        r = lax.rsqrt(_mean(xv * xv) + EPS)
        return [xv * r * gv, r]
    w = x.shape[1]
    return _rowwise(fn, [(x, 'row'), (g, 'bcast')], [(w, BF16), (1, F32)], name=name)


def _rms_bwd(x, r, dhn, g, dres, *, name):
    def fn(xv, rv, dv, gv, *rest):
        n = xv * rv
        dn = dv * gv
        dx = rv * (dn - n * _mean(dn * n))
        if rest:
            dx = dx + rest[0]
        return [dx, dx, _colsum(dv * n)]
    w = x.shape[1]
    ins = [(x, 'row'), (r, 'row'), (dhn, 'row'), (g, 'bcast')]
    if dres is not None:
        ins.append((dres, 'row'))
    return _rowwise(fn, ins, [(w, F32), (w, BF16)], [w], name=name)


def _rope(x, cos, sin_a, sin_b, shift, scale, out_dtype, *, name):
    w = x.shape[1]
    wr = cos.shape[1]
    lo = w - wr

    def fn(xv, cv, sav, sbv):
        xr = xv[:, lo:] if lo else xv
        y = xr * cv + pltpu.roll(xr, wr - shift, 1) * sav + pltpu.roll(xr, shift, 1) * sbv
        if lo:
            y = jnp.concatenate([xv[:, :lo], y], axis=1)
        return [y * scale]
    return _rowwise(fn, [(x, 'row'), (cos, 'per'), (sin_a, 'per'), (sin_b, 'per')], [(w, out_dtype)], name=name)[0]


def _dot(a, b, ca, cb):
    return lax.dot_general(a, b, (((ca,), (cb,)), ((), ())), preferred_element_type=F32)


def _ret_fwd(q, k, v, dec, dq_, dk_, ds_):
    H, S, d = q.shape
    T = dec.shape[1]
    nC = S // T

    def body(q_ref, k_ref, v_ref, m_ref, qd_ref, kd_ref, sd_ref, o_ref, st_ref, state):
        @pl.when(pl.program_id(1) == 0)
        def _():
            state[...] = jnp.zeros_like(state)

        st = state[...]
        st_ref[...] = st
        qv, kv, vv = q_ref[...], k_ref[...], v_ref[...]
        p = (_dot(qv, kv, 1, 1) * m_ref[...]).astype(BF16)
        qs = (qv.astype(F32) * qd_ref[...]).astype(BF16)
        o_ref[...] = _dot(p, vv, 1, 0) + _dot(qs, st.astype(BF16), 1, 0)
        ks = (kv.astype(F32) * kd_ref[...]).astype(BF16)
        state[...] = st * sd_ref[...] + _dot(ks, vv, 0, 0)

    blk = pl.BlockSpec((None, T, d), lambda h, c: (h, c, 0))
    return pl.pallas_call(
        body, name="ret_fwd", grid=(H, nC),
        in_specs=[blk, blk, blk,
                  pl.BlockSpec((None, T, T), lambda h, c: (h, 0, 0)),
                  pl.BlockSpec((None, T, 1), lambda h, c: (h, 0, 0)),
                  pl.BlockSpec((None, T, 1), lambda h, c: (h, 0, 0)),
                  pl.BlockSpec((None, 1, 1), lambda h, c: (h, 0, 0))],
        out_specs=[blk, pl.BlockSpec((None, None, d, d), lambda h, c: (h, c, 0, 0))],
        out_shape=[jax.ShapeDtypeStruct((H, S, d), F32), jax.ShapeDtypeStruct((H, nC, d, d), F32)],
        scratch_shapes=[pltpu.VMEM((d, d), F32)],
        compiler_params=_params(("parallel", "arbitrary")),
    )(q, k, v, dec, dq_, dk_, ds_)


def _ret_bwd(q, k, v, do, states, dec, dq_, dk_, ds_):
    H, S, d = q.shape
    T = dec.shape[1]
    nC = S // T

    def body(q_ref, k_ref, v_ref, do_ref, st_ref, m_ref, qd_ref, kd_ref, sd_ref, gq_ref, gk_ref, gv_ref, dstate):
        @pl.when(pl.program_id(1) == 0)
        def _():
            dstate[...] = jnp.zeros_like(dstate)

        qv, kv, vv, dov = q_ref[...], k_ref[...], v_ref[...], do_ref[...]
        m = m_ref[...]
        qd, kd = qd_ref[...], kd_ref[...]
        ds = dstate[...]
        dsb = ds.astype(BF16)
        sb = st_ref[...].astype(BF16)
        p = (_dot(qv, kv, 1, 1) * m).astype(BF16)
        da = (_dot(dov, vv, 1, 1) * m).astype(BF16)
        qs = (qv.astype(F32) * qd).astype(BF16)
        ks = (kv.astype(F32) * kd).astype(BF16)
        gq_ref[...] = _dot(da, kv, 1, 0) + _dot(dov, sb, 1, 1) * qd
        gk_ref[...] = _dot(da, qv, 0, 0) + _dot(vv, dsb, 1, 1) * kd
        gv_ref[...] = _dot(p, dov, 0, 0) + _dot(ks, dsb, 1, 0)
        dstate[...] = ds * sd_ref[...] + _dot(qs, dov, 0, 0)

    blk = pl.BlockSpec((None, T, d), lambda h, c: (h, nC - 1 - c, 0))
    out = jax.ShapeDtypeStruct((H, S, d), F32)
    return pl.pallas_call(
        body, name="ret_bwd", grid=(H, nC),
        in_specs=[blk, blk, blk, blk,
                  pl.BlockSpec((None, None, d, d), lambda h, c: (h, nC - 1 - c, 0, 0)),
                  pl.BlockSpec((None, T, T), lambda h, c: (h, 0, 0)),
                  pl.BlockSpec((None, T, 1), lambda h, c: (h, 0, 0)),
                  pl.BlockSpec((None, T, 1), lambda h, c: (h, 0, 0)),
                  pl.BlockSpec((None, 1, 1), lambda h, c: (h, 0, 0))],
        out_specs=[blk, blk, blk], out_shape=[out, out, out],
        scratch_shapes=[pltpu.VMEM((d, d), F32)],
        compiler_params=_params(("parallel", "arbitrary")),
    )(q, k, v, do, states, dec, dq_, dk_, ds_)


def _ret_tables(T, d):
    h = jnp.arange(RET_HEADS, dtype=F32)
    log_g = jnp.log1p(-jnp.exp2(-5.0 - h))
    idx = jnp.arange(T, dtype=F32)
    diff = idx[:, None] - idx[None, :]
    same = (jnp.arange(T)[:, None] // CHUNK) == (jnp.arange(T)[None, :] // CHUNK)
    earlier = (jnp.arange(T)[None, :] // CHUNK) < (jnp.arange(T)[:, None] // CHUNK)
    expo = jnp.where(same, jnp.abs(diff), diff)
    dec = jnp.where(same | earlier, jnp.exp(log_g[:, None, None] * expo[None]), 0.0)
    q_dec = jnp.exp(log_g[:, None] * (idx + 1.0))[..., None]
    k_dec = jnp.exp(log_g[:, None] * (T - 1.0 - idx))[..., None]
    s_dec = jnp.exp(log_g * T)[:, None, None]
    return dec.astype(F32), q_dec, k_dec, s_dec


NEG = -1e30


ROW_GROUP = 512


def _scores(q, kv, r, diagonal):
    s = _dot(q, kv, 1, 1)
    if diagonal:
        rg, T = s.shape
        qc = (r * rg + lax.broadcasted_iota(jnp.int32, (rg, T), 0)) // CHUNK
        kc = lax.broadcasted_iota(jnp.int32, (rg, T), 1) // CHUNK
        s = jnp.where(kc <= qc, s, NEG)
    return s


def _flash_fwd(q, k, v, side=None):
    H, S, dk = q.shape
    dv = v.shape[2]
    T = _tile(S, 512)
    n = S // T
    rg = min(ROW_GROUP, T)

    def body(q_ref, k_ref, v_ref, o_ref, lse_ref, m_s, l_s, acc):
        qi, ki = pl.program_id(1), pl.program_id(2)

        @pl.when(ki == 0)
        def _():
            m_s[...] = jnp.full_like(m_s, NEG)
            l_s[...] = jnp.zeros_like(l_s)
            acc[...] = jnp.zeros_like(acc)

        def step(diagonal):
            kv, vv = k_ref[...], v_ref[...]
            for r in range(T // rg):
                rows = pl.ds(r * rg, rg)
                s = _scores(q_ref[rows, :], kv, r, diagonal)
                m_old = m_s[rows, :]
                m_new = jnp.maximum(m_old, jnp.max(s, axis=1, keepdims=True))
                p = jnp.exp(s - m_new)
                alpha = jnp.exp(m_old - m_new)
                l_s[rows, :] = alpha * l_s[rows, :] + jnp.sum(p, axis=1, keepdims=True)
                acc[rows, :] = alpha * acc[rows, :] + _dot(p.astype(BF16), vv, 1, 0)
                m_s[rows, :] = m_new

        @pl.when(ki < qi)
        def _():
            step(False)

        @pl.when(ki == qi)
        def _():
            step(True)
            o_ref[...] = acc[...] / l_s[...]
            lse_ref[...] = m_s[...] + jnp.log(l_s[...])

    kv_map = lambda h, i, j: (h, jnp.minimum(i, j), 0)
    return _hosted_call(
        body, side, name="mla_fwd", grid=(H, n, n),
        in_specs=[pl.BlockSpec((None, T, dk), lambda h, i, j: (h, i, 0)),
                  pl.BlockSpec((None, T, dk), kv_map),
                  pl.BlockSpec((None, T, dv), kv_map)],
        out_specs=[pl.BlockSpec((None, T, dv), lambda h, i, j: (h, i, 0)),
                   pl.BlockSpec((None, T, 1), lambda h, i, j: (h, i, 0))],
        out_shape=[jax.ShapeDtypeStruct((H, S, dv), F32), jax.ShapeDtypeStruct((H, S, 1), F32)],
        scratch_shapes=[pltpu.VMEM((T, 1), F32), pltpu.VMEM((T, 1), F32), pltpu.VMEM((T, dv), F32)],
        operands=[q, k, v], semantics=("parallel", "parallel", "arbitrary"))


def _flash_dq(q, k, v, do, lse, dlt, side=None):
    H, S, dk = q.shape
    dv = v.shape[2]
    T = _tile(S, 512)
    n = S // T
    rg = min(ROW_GROUP, T)

    def body(q_ref, k_ref, v_ref, do_ref, lse_ref, dlt_ref, dq_ref, acc):
        qi, ki = pl.program_id(1), pl.program_id(2)

        @pl.when(ki == 0)
        def _():
            acc[...] = jnp.zeros_like(acc)

        def step(diagonal):
            kv, vv = k_ref[...], v_ref[...]
            for r in range(T // rg):
                rows = pl.ds(r * rg, rg)
                p = jnp.exp(_scores(q_ref[rows, :], kv, r, diagonal) - lse_ref[rows, :])
                dp = _dot(do_ref[rows, :], vv, 1, 1)
                ds = (p * (dp - dlt_ref[rows, :])).astype(BF16)
                acc[rows, :] += _dot(ds, kv, 1, 0)

        @pl.when(ki < qi)
        def _():
            step(False)

        @pl.when(ki == qi)
        def _():
            step(True)
            dq_ref[...] = acc[...]

    kv_map = lambda h, i, j: (h, jnp.minimum(i, j), 0)
    q_map = lambda h, i, j: (h, i, 0)
    return _hosted_call(
        body, side, name="mla_dq", grid=(H, n, n),
        in_specs=[pl.BlockSpec((None, T, dk), q_map), pl.BlockSpec((None, T, dk), kv_map),
                  pl.BlockSpec((None, T, dv), kv_map), pl.BlockSpec((None, T, dv), q_map),
                  pl.BlockSpec((None, T, 1), q_map), pl.BlockSpec((None, T, 1), q_map)],
        out_specs=[pl.BlockSpec((None, T, dk), q_map)],
        out_shape=[jax.ShapeDtypeStruct((H, S, dk), F32)],
        scratch_shapes=[pltpu.VMEM((T, dk), F32)],
        operands=[q, k, v, do, lse, dlt], semantics=("parallel", "parallel", "arbitrary"))


def _flash_dkv(q, k, v, do, lse, dlt, side=None):
    H, S, dk = q.shape
    dv = v.shape[2]
    T = _tile(S, 512)
    n = S // T
    rg = min(ROW_GROUP, T)

    def body(q_ref, k_ref, v_ref, do_ref, lse_ref, dlt_ref, dk_ref, dv_ref, acc_k, acc_v):
        ki, qi = pl.program_id(1), pl.program_id(2)

        @pl.when(qi == 0)
        def _():
            acc_k[...] = jnp.zeros_like(acc_k)
            acc_v[...] = jnp.zeros_like(acc_v)

        def step(diagonal):
            kv, vv = k_ref[...], v_ref[...]
            for r in range(T // rg):
                rows = pl.ds(r * rg, rg)
                qv, dov = q_ref[rows, :], do_ref[rows, :]
                p = jnp.exp(_scores(qv, kv, r, diagonal) - lse_ref[rows, :])
                acc_v[...] += _dot(p.astype(BF16), dov, 0, 0)
                dp = _dot(dov, vv, 1, 1)
                ds = (p * (dp - dlt_ref[rows, :])).astype(BF16)
                acc_k[...] += _dot(ds, qv, 0, 0)

        @pl.when(qi > ki)
        def _():
            step(False)

        @pl.when(qi == ki)
        def _():
            step(True)

        @pl.when(qi == n - 1)
        def _():
            dk_ref[...] = acc_k[...]
            dv_ref[...] = acc_v[...]

    q_map = lambda h, j, i: (h, jnp.maximum(i, j), 0)
    kv_map = lambda h, j, i: (h, j, 0)
    return _hosted_call(
        body, side, name="mla_dkv", grid=(H, n, n),
        in_specs=[pl.BlockSpec((None, T, dk), q_map), pl.BlockSpec((None, T, dk), kv_map),
                  pl.BlockSpec((None, T, dv), kv_map), pl.BlockSpec((None, T, dv), q_map),
                  pl.BlockSpec((None, T, 1), q_map), pl.BlockSpec((None, T, 1), q_map)],
        out_specs=[pl.BlockSpec((None, T, dk), kv_map), pl.BlockSpec((None, T, dv), kv_map)],
        out_shape=[jax.ShapeDtypeStruct((H, S, dk), F32), jax.ShapeDtypeStruct((H, S, dv), F32)],
        scratch_shapes=[pltpu.VMEM((T, dk), F32), pltpu.VMEM((T, dv), F32)],
        operands=[q, k, v, do, lse, dlt], semantics=("parallel", "parallel", "arbitrary"))


def _shift_down(cur, prev, by):
    rows = lax.broadcasted_iota(jnp.int32, cur.shape, 0)
    return jnp.where(rows < by, pltpu.roll(prev, by, 0), pltpu.roll(cur, by, 0))


def _shift_up(cur, nxt, by):
    ts = cur.shape[0]
    rows = lax.broadcasted_iota(jnp.int32, cur.shape, 0)
    return jnp.where(rows >= ts - by, pltpu.roll(nxt, ts - by, 0), pltpu.roll(cur, ts - by, 0))


def _ffn_tiles(S, F):
    return _tile(S, 256, 8), _tile(F, 1024)


def _ffn_fwd(G, U, cw, cb):
    S, F = G.shape
    ts, tc = _ffn_tiles(S, F)

    def body(g_ref, gp_ref, u_ref, cw_ref, cb_ref, a_ref, act_ref):
        cur = g_ref[...]
        prev = gp_ref[...] * (pl.program_id(1) > 0).astype(F32)
        a = (cb_ref[...] + cw_ref[0:1, :] * _shift_down(cur, prev, 2) + cw_ref[1:2, :] * _shift_down(cur, prev, 1)
             + cw_ref[2:3, :] * cur)
        a_ref[...] = a
        act_ref[...] = (a * _sigmoid(a) * u_ref[...]).astype(BF16)

    cur_spec = pl.BlockSpec((ts, tc), lambda j, i: (i, j))
    return pl.pallas_call(
        body, name="ffn_act_fwd", grid=(F // tc, S // ts),
        in_specs=[cur_spec, pl.BlockSpec((ts, tc), lambda j, i: (jnp.maximum(i - 1, 0), j)), cur_spec,
                  pl.BlockSpec((CONV_WIDTH, tc), lambda j, i: (0, j)), pl.BlockSpec((1, tc), lambda j, i: (0, j))],
        out_specs=[cur_spec, cur_spec],
        out_shape=[jax.ShapeDtypeStruct((S, F), F32), jax.ShapeDtypeStruct((S, F), BF16)],
        compiler_params=_params(("parallel", "parallel")),
    )(G, G, U, cw, cb)


def _ffn_bwd_act(a, U, dact):
    S, F = a.shape
    ts, tc = _ffn_tiles(S, F)

    def body(a_ref, u_ref, d_ref, da_ref, du_ref, db_ref):
        av, dv = a_ref[...], d_ref[...]
        sg = _sigmoid(av)
        du_ref[...] = (dv * av * sg).astype(BF16)
        da = dv * u_ref[...] * sg * (1.0 + av * (1.0 - sg))
        da_ref[...] = da

        @pl.when(pl.program_id(1) == 0)
        def _():
            db_ref[...] = jnp.zeros_like(db_ref)

        db_ref[...] += _colsum(da)

    cur_spec = pl.BlockSpec((ts, tc), lambda j, i: (i, j))
    return pl.pallas_call(
        body, name="ffn_act_bwd", grid=(F // tc, S // ts),
        in_specs=[cur_spec, cur_spec, cur_spec],
        out_specs=[cur_spec, cur_spec, pl.BlockSpec((1, tc), lambda j, i: (0, j))],
        out_shape=[jax.ShapeDtypeStruct((S, F), F32), jax.ShapeDtypeStruct((S, F), BF16),
                   jax.ShapeDtypeStruct((1, F), F32)],
        compiler_params=_params(("parallel", "arbitrary")),
    )(a, U, dact)


def _ffn_bwd_conv(da, G, cw):
    S, F = da.shape
    ts, tc = _ffn_tiles(S, F)
    n = S // ts

    def body(d_ref, dn_ref, g_ref, gp_ref, cw_ref, dg_ref, dw_ref):
        i = pl.program_id(1)
        dcur = d_ref[...]
        dnxt = dn_ref[...] * (i < n - 1).astype(F32)
        cur = g_ref[...]
        prev = gp_ref[...] * (i > 0).astype(F32)
        dg = (cw_ref[2:3, :] * dcur + cw_ref[1:2, :] * _shift_up(dcur, dnxt, 1)
              + cw_ref[0:1, :] * _shift_up(dcur, dnxt, 2))
        dg_ref[...] = dg.astype(BF16)

        @pl.when(i == 0)
        def _():
            dw_ref[...] = jnp.zeros_like(dw_ref)

        dw_ref[0:1, :] += _colsum(dcur * _shift_down(cur, prev, 2))
        dw_ref[1:2, :] += _colsum(dcur * _shift_down(cur, prev, 1))
        dw_ref[2:3, :] += _colsum(dcur * cur)

    cur_spec = pl.BlockSpec((ts, tc), lambda j, i: (i, j))
    return pl.pallas_call(
        body, name="ffn_conv_bwd", grid=(F // tc, n),
        in_specs=[cur_spec, pl.BlockSpec((ts, tc), lambda j, i: (jnp.minimum(i + 1, n - 1), j)),
                  cur_spec, pl.BlockSpec((ts, tc), lambda j, i: (jnp.maximum(i - 1, 0), j)),
                  pl.BlockSpec((CONV_WIDTH, tc), lambda j, i: (0, j))],
        out_specs=[cur_spec, pl.BlockSpec((CONV_WIDTH, tc), lambda j, i: (0, j))],
        out_shape=[jax.ShapeDtypeStruct((S, F), BF16), jax.ShapeDtypeStruct((CONV_WIDTH, F), F32)],
        compiler_params=_params(("parallel", "arbitrary")),
    )(da, da, G, G, cw)


class _Cfg:
    def __init__(self, S, D):
        self.S, self.D = S, D
        self.RD = D // (2 * RET_HEADS)
        self.RW = RET_HEADS * self.RD
        self.MV = (D - self.RW) // MLA_HEADS
        self.QL, self.KVL = D // 4, D // 8
        self.F = ((8 * D // 3 + 255) // 256) * 256
        self.IN = 4 * self.RW + self.QL + self.KVL + MLA_ROPE
        self.INs = self.IN // N_CHIP
        self.INp = _round_up(self.INs, LANE)
        self.Fs = self.F // N_CHIP
        self.Fp = _round_up(self.Fs, LANE)
        self.F4 = N_CHIP * self.Fp
        self.QK = MLA_NOPE + MLA_ROPE
        self.KVH = MLA_NOPE + self.MV


def _head_major(t, H, d):
    S = t.shape[0]
    return t.reshape(S, H, d).transpose(1, 0, 2).reshape(H * S, d)


def _seq_major(t, H, d):
    S = t.shape[0] // H
    return t.reshape(H, S, d).transpose(1, 0, 2).reshape(S, H * d)


def _rope_tables(cfg):
    S = cfg.S

    def cs(dim):
        inv = 1.0 / (ROPE_BASE ** (jnp.arange(0, dim, 2, dtype=F32) / dim))
        ang = jnp.arange(S, dtype=F32)[:, None] * inv[None, :]
        return jnp.cos(ang), jnp.sin(ang)

    c, s = cs(cfg.RD)
    z = jnp.zeros_like(s)
    ret = (jnp.concatenate([c, c], 1), jnp.concatenate([-s, z], 1), jnp.concatenate([z, s], 1))
    c, s = cs(MLA_ROPE)
    tail1 = jnp.ones((S, MLA_QK_PAD - cfg.QK), F32)
    tail0 = jnp.zeros((S, MLA_QK_PAD - cfg.QK), F32)
    z = jnp.zeros_like(s)
    mla = (jnp.concatenate([c, c, tail1], 1), jnp.concatenate([-s, z, tail0], 1), jnp.concatenate([z, s, tail0], 1))
    return ret, mla


GATHER_BEHIND_IN_PROJ = ('w_ffn_gate',)
GATHER_BEHIND_MLA = ('w_ffn_up', 'w_ffn_down', 'w_ple_gate', 'w_ple_proj')
GATHER_FIRST = tuple(n for n in BIG if n not in GATHER_BEHIND_IN_PROJ + GATHER_BEHIND_MLA)
REDUCE_BEHIND_DQ = ('w_ffn_gate', 'w_ple_gate', 'w_o')
REDUCE_BEHIND_DKV = ('w_ffn_up', 'w_ffn_down', 'w_ple_proj')
REDUCE_LAST = tuple(n for n in BIG if n not in REDUCE_BEHIND_DQ + REDUCE_BEHIND_DKV)


def _local_step(cfg, x, p, tgt, W, late, sp):
    W = dict(W)
    S, D, RD, RW, MV = cfg.S, cfg.D, cfg.RD, cfg.RW, cfg.MV
    H, MH = RET_HEADS, MLA_HEADS
    (rc, rsa, rsb), (mc, msa, msb) = _rope_tables(cfg)
    dec, q_dec, k_dec, s_dec = _ret_tables(min(RET_BLOCK, S), RD)
    k_scale = RD ** -0.5
    a_scale = cfg.QK ** -0.5
    gb, gs = {}, {}

    hn1, r1 = _rms_fwd(x, sp['g_attn'], name="rms_attn")
    proj, got = _mm_nn(hn1, W['w_in'], name="in_proj",
                       side=_gather_side([late[n] for n in GATHER_BEHIND_IN_PROJ]))
    W.update({n: _as_operand(n, g) for n, g in zip(GATHER_BEHIND_IN_PROJ, got)})
    proj = proj.reshape(S, N_CHIP, cfg.INp)[:, :, :cfg.INs].reshape(S, cfg.IN)
    cuts = [RW, 2 * RW, 3 * RW, 4 * RW, 4 * RW + cfg.QL, 4 * RW + cfg.QL + cfg.KVL]
    rq, rk, rv, rg, cq, ckv, kr = jnp.split(proj, cuts, axis=1)

    rq_h, rk_h, rv_h, rg_h = (_head_major(t, H, RD) for t in (rq, rk, rv, rg))
    q_r = _rope(rq_h, rc, rsa, rsb, RD // 2, 1.0, BF16, name="ret_rope_q").reshape(H, S, RD)
    k_r = _rope(rk_h, rc, rsa, rsb, RD // 2, k_scale, BF16, name="ret_rope_k").reshape(H, S, RD)
    v_r = rv_h.astype(BF16).reshape(H, S, RD)
    o_h, states = _ret_fwd(q_r, k_r, v_r, dec, q_dec, k_dec, s_dec)
    o_h = o_h.reshape(H * S, RD)

    def gate_fn(ov, gv):
        oc = ov - _mean(ov)
        ron = oc * lax.rsqrt(_mean(oc * oc) + EPS)
        return [gv * _sigmoid(gv) * ron]
    ro_h = _rowwise(gate_fn, [(o_h, 'row'), (rg_h, 'row')], [(RD, BF16)], name="ret_gate")[0]
    ro = _seq_major(ro_h, H, RD)

    cqn, rcq = _rms_fwd(cq, sp['g_q_lora'], name="rms_q")
    ckvn, rckv = _rms_fwd(ckv, sp['g_kv_lora'], name="rms_kv")
    Q = _mm_nn(cqn, W['w_uq'], name="q_up")
    KV = _mm_nn(ckvn, W['w_ukv'], name="kv_up")
    pad = jnp.zeros((MH, S, MLA_QK_PAD - cfg.QK), F32)
    q_raw = jnp.concatenate([Q.reshape(S, MH, cfg.QK).transpose(1, 0, 2), pad], 2).reshape(MH * S, MLA_QK_PAD)
    KV3 = KV.reshape(S, MH, cfg.KVH).transpose(1, 0, 2)
    k_raw = jnp.concatenate([KV3[:, :, :MLA_NOPE], jnp.broadcast_to(kr[None], (MH, S, MLA_ROPE)), pad], 2)
    k_raw = k_raw.reshape(MH * S, MLA_QK_PAD)
    v_m = KV3[:, :, MLA_NOPE:].astype(BF16)
    q_m = _rope(q_raw, mc, msa, msb, MLA_ROPE // 2, a_scale, BF16, name="mla_rope_q").reshape(MH, S, MLA_QK_PAD)
    k_m = _rope(k_raw, mc, msa, msb, MLA_ROPE // 2, 1.0, BF16, name="mla_rope_k").reshape(MH, S, MLA_QK_PAD)
    (mo_h, lse), got = _flash_fwd(q_m, k_m, v_m, side=_gather_side([late[n] for n in GATHER_BEHIND_MLA]))
    W.update({n: _as_operand(n, g) for n, g in zip(GATHER_BEHIND_MLA, got)})
    mo = _seq_major(mo_h.reshape(MH * S, MV), MH, MV)

    cat = jnp.concatenate([ro, mo.astype(BF16)], axis=1)
    h1 = _mm_nn(cat, W['w_o'], name="out_proj", res=x)

    hn2, r2 = _rms_fwd(h1, sp['g_ffn'], name="rms_ffn")
    G = _mm_nn(hn2, W['w_ffn_gate'], name="ffn_gate")
    U = _mm_nn(hn2, W['w_ffn_up'], name="ffn_up")
    a, act = _ffn_fwd(G, U, sp['conv_w'], sp['conv_b'])
    h2 = _mm_nn(act, W['w_ffn_down'], name="ffn_down", res=h1)

    hn3, r3 = _rms_fwd(h2, sp['g_ple'], name="rms_ple")
    Z = _mm_nn(hn3, W['w_ple_gate'], name="ple_gate")
    p_b = p.astype(BF16)
    PP = _mm_nn(p_b, W['w_ple_proj'], name="ple_proj")

    def head_fn(h2v, zv, ppv, tv, gv):
        gate = _sigmoid(zv)
        h3 = h2v + gate * ppv
        r4 = lax.rsqrt(_mean(h3 * h3) + EPS)
        n4 = h3 * r4
        e = n4 * gv - tv
        dy = e * (1.0 / D)
        dn = dy * gv
        dh3 = r4 * (dn - n4 * _mean(dn * n4))
        dpp = dh3 * gate
        dz = dh3 * ppv * gate * (1.0 - gate)
        loss = jnp.sum(0.5 * _mean(e * e), axis=0, keepdims=True)
        return [dh3, dz, dpp, _colsum(dy * n4), jnp.broadcast_to(loss, (1, LANE))]
    dh3, dZ, dPP, dgf, loss = _rowwise(
        head_fn, [(h2, 'row'), (Z, 'row'), (PP, 'row'), (tgt, 'row'), (sp['g_final'], 'bcast')],
        [(D, F32), (D, BF16), (D, BF16)], [D, LANE], name="ple_loss_head")
    gs['g_final'] = dgf
    loss = loss[0, 0]

    gb['w_ple_proj'] = _mm_tn(p_b, dPP, N_CHIP, name="ple_proj_dw")
    gb['w_ple_gate'] = _mm_tn(hn3, dZ, 1, name="ple_gate_dw")
    dhn3 = _mm_nt(dZ, W['w_ple_gate'], name="ple_gate_dx")
    dh2, dh2_b, gs['g_ple'] = _rms_bwd(h2, r3, dhn3, sp['g_ple'], dh3, name="rms_ple_bwd")

    dact = _mm_nt(dh2_b, W['w_ffn_down'], name="ffn_down_dx")
    gb['w_ffn_down'] = _mm_tn(act, dh2_b, 1, name="ffn_down_dw")
    da, dU, gs['conv_b'] = _ffn_bwd_act(a, U, dact)
    dG, gs['conv_w'] = _ffn_bwd_conv(da, G, sp['conv_w'])
    gb['w_ffn_gate'] = _mm_tn(hn2, dG, N_CHIP, name="ffn_gate_dw")
    gb['w_ffn_up'] = _mm_tn(hn2, dU, N_CHIP, name="ffn_up_dw")
    dhn2 = _mm_nt(dG, W['w_ffn_gate'], name="ffn_gate_dx")
    dhn2 = _mm_nt(dU, W['w_ffn_up'], name="ffn_up_dx", res=dhn2)
    dh1, dh1_b, gs['g_ffn'] = _rms_bwd(h1, r2, dhn2, sp['g_ffn'], dh2, name="rms_ffn_bwd")

    dcat = _mm_nt(dh1_b, W['w_o'], name="out_proj_dx")
    gb['w_o'] = _mm_tn(cat, dh1_b, 1, name="out_proj_dw")

    def pair_sums(names):
        mine32 = [_grad_pieces(n, gb[n][0]) for n in names]
        from_sibling = _pair_exchange([_grad_pieces(n, gb[n][1]) for n in names])
        return {n: _pair_add(g, r) for n, g, r in zip(names, mine32, from_sibling)}

    pair = pair_sums(REDUCE_BEHIND_DQ + REDUCE_BEHIND_DKV)
    from_chips = {}
    dro_h = _head_major(dcat[:, :RW], H, RD)

    def gate_bwd_fn(ov, gv, dv):
        oc = ov - _mean(ov)
        rs = lax.rsqrt(_mean(oc * oc) + EPS)
        ron = oc * rs
        sg = _sigmoid(gv)
        dron = dv * gv * sg
        drg = dv * ron * sg * (1.0 + gv * (1.0 - sg))
        do = rs * (dron - _mean(dron) - ron * _mean(dron * ron))
        return [do, drg]
    do_h, drg_h = _rowwise(gate_bwd_fn, [(o_h, 'row'), (rg_h, 'row'), (dro_h, 'row')],
                           [(RD, BF16), (RD, F32)], name="ret_gate_bwd")
    gq_r, gk_r, gv_r = _ret_bwd(q_r, k_r, v_r, do_h.reshape(H, S, RD), states, dec, q_dec, k_dec, s_dec)
    drq_h = _rope(gq_r.reshape(H * S, RD), rc, -rsa, -rsb, RD // 2, 1.0, F32, name="ret_rope_q_bwd")
    drk_h = _rope(gk_r.reshape(H * S, RD), rc, -rsa, -rsb, RD // 2, k_scale, F32, name="ret_rope_k_bwd")
    drq, drk, drv, drg = (_seq_major(t, H, RD) for t in (drq_h, drk_h, gv_r.reshape(H * S, RD), drg_h))

    dmo_h = _head_major(dcat[:, RW:], MH, MV)
    dlt = _rowwise(lambda ov, dv: [jnp.sum(ov * dv, axis=1, keepdims=True)],
                   [(mo_h.reshape(MH * S, MV), 'row'), (dmo_h, 'row')], [(1, F32)], name="mla_delta")[0]
    dmo_b = dmo_h.astype(BF16).reshape(MH, S, MV)
    dlt = dlt.reshape(MH, S, 1)
    (gq_m,), got = _flash_dq(q_m, k_m, v_m, dmo_b, lse, dlt,
                             side=_exchange_side([pair[n][1] for n in REDUCE_BEHIND_DQ]))
    from_chips.update(zip(REDUCE_BEHIND_DQ, got))
    (gk_m, gv_m), got = _flash_dkv(q_m, k_m, v_m, dmo_b, lse, dlt,
                                   side=_exchange_side([pair[n][1] for n in REDUCE_BEHIND_DKV]))
    from_chips.update(zip(REDUCE_BEHIND_DKV, got))
    dq_raw = _rope(gq_m.reshape(MH * S, MLA_QK_PAD), mc, -msa, -msb, MLA_ROPE // 2, a_scale, F32,
                   name="mla_rope_q_bwd")
    dk_raw = _rope(gk_m.reshape(MH * S, MLA_QK_PAD), mc, -msa, -msb, MLA_ROPE // 2, 1.0, F32, name="mla_rope_k_bwd")
    dQ = dq_raw.reshape(MH, S, MLA_QK_PAD)[:, :, :cfg.QK].transpose(1, 0, 2).reshape(S, MH * cfg.QK).astype(BF16)
    dk3 = dk_raw.reshape(MH, S, MLA_QK_PAD)
    dKV = jnp.concatenate([dk3[:, :, :MLA_NOPE], gv_m], 2).transpose(1, 0, 2).reshape(S, MH * cfg.KVH).astype(BF16)
    dkr_heads = [(dk3[h, :, MLA_NOPE:cfg.QK], 'row') for h in range(MH)]
    dkr = _rowwise(lambda *v: [functools.reduce(lambda s, t: s + t, v)], dkr_heads, [(MLA_ROPE, F32)],
                   name="mla_rope_k_heads")[0]

    gb['w_uq'] = _mm_tn(cqn, dQ, N_CHIP, name="q_up_dw")
    dcqn = _mm_nt(dQ, W['w_uq'], name="q_up_dx")
    dcq, _, gs['g_q_lora'] = _rms_bwd(cq, rcq, dcqn, sp['g_q_lora'], None, name="rms_q_bwd")
    gb['w_ukv'] = _mm_tn(ckvn, dKV, N_CHIP, name="kv_up_dw")
    dckvn = _mm_nt(dKV, W['w_ukv'], name="kv_up_dx")
    dckv, _, gs['g_kv_lora'] = _rms_bwd(ckv, rckv, dckvn, sp['g_kv_lora'], None, name="rms_kv_bwd")

    dproj = jnp.concatenate([drq, drk, drv, drg, dcq, dckv, dkr], axis=1).astype(BF16)
    dproj = jnp.pad(dproj.reshape(S, N_CHIP, cfg.INs), ((0, 0), (0, 0), (0, cfg.INp - cfg.INs)))
    dproj = dproj.reshape(S, N_CHIP * cfg.INp)
    gb['w_in'] = _mm_tn(hn1, dproj, N_CHIP, name="in_proj_dw")
    dhn1 = _mm_nt(dproj, W['w_in'], name="in_proj_dx")
    dx, _, gs['g_attn'] = _rms_bwd(x, r1, dhn1, sp['g_attn'], dh1, name="rms_attn_bwd")
    pair.update(pair_sums(REDUCE_LAST))
    from_chips.update(zip(REDUCE_LAST, _chip_exchange([pair[n][1] for n in REDUCE_LAST])))
    return loss, dx, {n: (pair[n][0], from_chips[n]) for n in BIG}, gs


def _padded_shape(name, shape):
    K, n = shape
    if name in COL_SHARDED:
        return K, _round_up(n, LANE)
    return _round_up(K, LANE), n


def _pad_shard(name, w):
    K, n = _padded_shape(name, w.shape)
    return jnp.pad(w.astype(BF16), ((0, K - w.shape[0]), (0, n - w.shape[1])))


def _as_operand(name, gathered):
    if name in COL_SHARDED:
        return gathered
    return gathered.reshape(1, gathered.shape[0] * gathered.shape[1], gathered.shape[2])


def _grad_pieces(name, g):
    if name in COL_SHARDED:
        return g
    return g.reshape(N_CHIP, g.shape[1] // N_CHIP, g.shape[2])


def _channels_padded(v, cfg):
    r = v.shape[0]
    return jnp.pad(v.reshape(r, N_CHIP, cfg.Fs), ((0, 0), (0, 0), (0, cfg.Fp - cfg.Fs))).reshape(r, cfg.F4)


def _channels_unpadded(v, cfg):
    r = v.shape[0]
    return v.reshape(r, N_CHIP, cfg.Fp)[:, :, :cfg.Fs].reshape(r, cfg.F)


def _mesh_pos():
    return lax.axis_index("x"), lax.axis_index("y"), lax.axis_index("c")


def _other_chips(x, y):
    return [(1 - x, y), (x, 1 - y), (1 - x, 1 - y)]


def _chip_id(cx, cy):
    return 2 * cx + cy


def _half(ref_rows, core):
    half = ref_rows // 2
    return pl.ds(core * half, half)


def _comm_call(body, name, ins, out_shape, sems):
    return pl.pallas_call(
        body, name=name, in_specs=[HBM] * len(ins), out_specs=[HBM] * len(out_shape), out_shape=out_shape,
        scratch_shapes=sems,
    )(*ins)


def _gather_side(shards):
    n = len(shards)

    def copies(srcs, outs, sems):
        ici_send, ici_recv, d2d_send, d2d_recv, own_send, own_recv = sems
        x, y, c = _mesh_pos()
        mine = _chip_id(x, y)
        others = _other_chips(x, y)

        def over_ici(w, j, chip):
            rows = _half(srcs[w].shape[0], c)
            return pltpu.make_async_remote_copy(
                src_ref=srcs[w].at[rows], dst_ref=outs[w].at[chip, rows],
                send_sem=ici_send.at[w, j], recv_sem=ici_recv.at[w, j],
                device_id=(*others[j], c), device_id_type=MESH)

        def over_d2d(w, j, core):
            rows = _half(srcs[w].shape[0], core)
            slab = outs[w].at[_chip_id(*others[j]), rows]
            return pltpu.make_async_remote_copy(
                src_ref=slab, dst_ref=slab, send_sem=d2d_send.at[w, j], recv_sem=d2d_recv.at[w, j],
                device_id=(x, y, 1 - c), device_id_type=MESH)

        def own(w):
            return pltpu.make_async_remote_copy(
                src_ref=srcs[w], dst_ref=outs[w].at[mine], send_sem=own_send.at[w], recv_sem=own_recv.at[w],
                device_id=(x, y, 1 - c), device_id_type=MESH)

        return c, mine, others, over_ici, over_d2d, own

    def start(srcs, outs, sems):
        c, mine, others, over_ici, over_d2d, own = copies(srcs, outs, sems)
        for w in range(n):
            for j in range(3):
                over_ici(w, j, mine).start()
        for w in range(n):
            own(w).start()

    def finish(srcs, outs, sems):
        c, mine, others, over_ici, over_d2d, own = copies(srcs, outs, sems)
        for w in range(n):
            for j in range(3):
                over_ici(w, j, _chip_id(*others[j])).wait_recv()
                over_d2d(w, j, c).start()
        for w in range(n):
            for j in range(3):
                over_d2d(w, j, 1 - c).wait_recv()
        for w in range(n):
            own(w).wait()
            for j in range(3):
                over_ici(w, j, mine).wait_send()
                over_d2d(w, j, c).wait_send()

    out_shape = [jax.ShapeDtypeStruct((N_CHIP,) + s.shape, s.dtype) for s in shards]
    sems = [pltpu.SemaphoreType.DMA((n, 3))] * 4 + [pltpu.SemaphoreType.DMA((n,))] * 2
    return _Side(shards, out_shape, sems, start, finish)


def _gather_weights(shards):
    return _run_side(_gather_side(shards), "gather_weights")


def _pair_exchange(grads):
    n = len(grads)

    def body(*refs):
        srcs, outs = refs[:n], refs[n:2 * n]
        send, recv = refs[2 * n:]
        x, y, c = _mesh_pos()
        copies = []
        for w in range(n):
            rows = _half(srcs[w].shape[1], 1 - c)
            copies.append(pltpu.make_async_remote_copy(
                src_ref=srcs[w].at[:, rows, :], dst_ref=outs[w], send_sem=send.at[w], recv_sem=recv.at[w],
                device_id=(x, y, 1 - c), device_id_type=MESH))
        for cp in copies:
            cp.start()
        for cp in copies:
            cp.wait()

    out_shape = [jax.ShapeDtypeStruct((g.shape[0], g.shape[1] // 2, g.shape[2]), g.dtype) for g in grads]
    return _comm_call(body, "grad_pair_exchange", grads, out_shape, [pltpu.SemaphoreType.DMA((n,))] * 2)


def _exchange_side(parts):
    n = len(parts)

    def copies(srcs, outs, sems):
        send, recv = sems
        x, y, c = _mesh_pos()
        others = _other_chips(x, y)
        return [pltpu.make_async_remote_copy(
            src_ref=srcs[w].at[_chip_id(*others[j])], dst_ref=outs[w].at[j],
            send_sem=send.at[w, j], recv_sem=recv.at[w, j],
            device_id=(*others[j], c), device_id_type=MESH) for w in range(n) for j in range(3)]

    def start(srcs, outs, sems):
        for cp in copies(srcs, outs, sems):
            cp.start()

    def finish(srcs, outs, sems):
        for cp in copies(srcs, outs, sems):
            cp.wait()

    out_shape = [jax.ShapeDtypeStruct((3,) + p.shape[1:], p.dtype) for p in parts]
    return _Side(parts, out_shape, [pltpu.SemaphoreType.DMA((n, 3))] * 2, start, finish)


def _chip_exchange(parts):
    return _run_side(_exchange_side(parts), "grad_chip_exchange")


def _sibling_share(halves):
    n = len(halves)

    def body(*refs):
        srcs, outs = refs[:n], refs[n:2 * n]
        send, recv = refs[2 * n:]
        x, y, c = _mesh_pos()
        remote = []
        for w in range(n):
            rows = _half(outs[w].shape[0], c)
            remote.append(pltpu.make_async_remote_copy(
                src_ref=srcs[w], dst_ref=outs[w].at[rows], send_sem=send.at[w], recv_sem=recv.at[w],
                device_id=(x, y, 1 - c), device_id_type=MESH))
        for cp in remote:
            cp.start()
        for w in range(n):
            theirs = outs[w].at[_half(outs[w].shape[0], 1 - c)]
            pltpu.make_async_remote_copy(
                src_ref=srcs[w], dst_ref=theirs, send_sem=send.at[w], recv_sem=recv.at[w],
                device_id=(x, y, 1 - c), device_id_type=MESH).wait_recv()
        for cp in remote:
            cp.wait_send()

    out_shape = [jax.ShapeDtypeStruct((2 * h.shape[0], h.shape[1]), h.dtype) for h in halves]
    theirs = _comm_call(body, "grad_sibling_share", halves, out_shape, [pltpu.SemaphoreType.DMA((n,))] * 2)
    cc = lax.axis_index("c")
    return [lax.dynamic_update_slice_in_dim(t, h, cc * h.shape[0], axis=0) for t, h in zip(theirs, halves)]


N_DEV = 8


def _gather_small(v):
    r, width = v.shape

    def body(v_ref, out_ref, send_sems, recv_sems, local_sem):
        x, y, c = _mesh_pos()
        me, sibling = (x, y, c), (x, y, 1 - c)
        chips = _other_chips(x, y)

        def rows(px, py, pc):
            return out_ref.at[pl.ds((4 * px + 2 * py + pc) * r, r), :]

        def copy(k, block, to, src=None):
            return pltpu.make_async_remote_copy(
                src_ref=rows(*block) if src is None else src, dst_ref=rows(*block),
                send_sem=send_sems.at[k], recv_sem=recv_sems.at[k], device_id=to, device_id_type=MESH)

        mine = pltpu.make_async_copy(v_ref, rows(*me), local_sem)
        mine.start()
        first = [copy(0, me, sibling, src=v_ref)]
        first += [copy(1 + j, me, (*chip, c), src=v_ref) for j, chip in enumerate(chips)]
        for cp in first:
            cp.start()
        passed = [copy(4 + j, (*chip, c), sibling) for j, chip in enumerate(chips)]
        for j, chip in enumerate(chips):
            copy(1 + j, (*chip, c), me).wait_recv()
            passed[j].start()
        copy(0, sibling, me).wait_recv()
        for j, chip in enumerate(chips):
            copy(4 + j, (*chip, 1 - c), me).wait_recv()
        for cp in first + passed:
            cp.wait_send()
        mine.wait()

    vmem = pl.BlockSpec(memory_space=pltpu.VMEM)
    return pl.pallas_call(
        body, name="gather_small", out_shape=jax.ShapeDtypeStruct((N_DEV * r, width), v.dtype),
        in_specs=[vmem], out_specs=vmem,
        scratch_shapes=[pltpu.SemaphoreType.DMA((7,)), pltpu.SemaphoreType.DMA((7,)), pltpu.SemaphoreType.DMA],
    )(v)


def _pack(arrays):
    flat = jnp.concatenate([a.reshape(-1) for a in arrays])
    size = _round_up(flat.shape[0], 8 * LANE)
    return jnp.pad(flat, (0, size - flat.shape[0])).reshape(size // LANE, LANE)


def _unpack(packed, shapes):
    flat = packed.reshape(-1)
    out, at = [], 0
    for s in shapes:
        size = math.prod(s)
        out.append(flat[at:at + size].reshape(s))
        at += size
    return out


def _sum_devices(gathered):
    r = gathered.shape[0] // N_DEV
    blocks = [(gathered[d * r:(d + 1) * r], 'row') for d in range(N_DEV)]
    return _rowwise(lambda *v: [functools.reduce(lambda s, t: s + t, v)], blocks, [(LANE, F32)],
                    name="small_grad_sum")[0]


def _reduce_tiles(half, n):
    return _tile(half, 128, 8)


def _pair_add(g32, r1):
    G, K, n = g32.shape
    half = K // 2
    tr = _reduce_tiles(half, n)
    nrt = half // tr

    def body(g_ref, r_ref, s32_ref, sb_ref):
        s = g_ref[...] + r_ref[...].astype(F32)
        s32_ref[...] = s
        sb_ref[...] = s.astype(BF16)

    blk = pl.BlockSpec((None, tr, n), lambda k, i: (k, i, 0))
    return pl.pallas_call(
        body, name="grad_pair_add", grid=(G, nrt),
        in_specs=[pl.BlockSpec((None, tr, n), lambda k, i: (k, lax.axis_index("c") * nrt + i, 0)), blk],
        out_specs=[blk, blk],
        out_shape=[jax.ShapeDtypeStruct((G, half, n), F32), jax.ShapeDtypeStruct((G, half, n), BF16)],
        compiler_params=_params(("parallel", "parallel")),
    )(g32, r1)


def _chip_add(s32, r2):
    _, half, n = s32.shape
    tr = _reduce_tiles(half, n)

    def body(s_ref, a_ref, b_ref, c_ref, o_ref):
        o_ref[...] = ((s_ref[...] + a_ref[...].astype(F32)) + b_ref[...].astype(F32)) + c_ref[...].astype(F32)

    def piece(j):
        return pl.BlockSpec((None, tr, n), lambda i, j=j: (j, i, 0))

    def mine(i):
        return _chip_id(lax.axis_index("x"), lax.axis_index("y")), i, 0

    return pl.pallas_call(
        body, name="grad_chip_add", grid=(half // tr,),
        in_specs=[pl.BlockSpec((None, tr, n), mine), piece(0), piece(1), piece(2)],
        out_specs=pl.BlockSpec((tr, n), lambda i: (i, 0)),
        out_shape=jax.ShapeDtypeStruct((half, n), F32),
        compiler_params=_params(("parallel",)),
    )(s32, r2, r2, r2)


def _adamw_math(wv, gv, mv, vv):
    m2 = ADAM_B1 * mv + (1.0 - ADAM_B1) * gv
    v2 = ADAM_B2 * vv + (1.0 - ADAM_B2) * (gv * gv)
    m_hat = m2 / (1.0 - ADAM_B1 ** ADAM_STEP)
    v_hat = v2 / (1.0 - ADAM_B2 ** ADAM_STEP)
    delta = -ADAM_LR * (m_hat / (jnp.sqrt(v_hat) + ADAM_EPS) + ADAM_WD * wv)
    return [delta, m2, v2]


def _adamw(w, g, m, v, *, name):
    width = w.shape[1]
    return _rowwise(_adamw_math, [(w, 'row'), (g, 'row'), (m, 'row'), (v, 'row')], [(width, F32)] * 3, name=name)


def _adamw_sharded(w, g_padded, m, v, *, name):
    _, K, n = w.shape
    n_pad = g_padded.shape[1]
    ts = 8
    while ts * 2 <= 256 and ts * 2 * 8 * n_pad * 4 <= ROWWISE_BLOCK_BYTES and K % (ts * 2) == 0:
        ts *= 2

    def body(w_ref, g_ref, m_ref, v_ref, go_ref, d_ref, mo_ref, vo_ref):
        gv = g_ref[:, :n] if n != n_pad else g_ref[...]
        go_ref[...] = gv
        d_ref[...], mo_ref[...], vo_ref[...] = _adamw_math(w_ref[...], gv, m_ref[...], v_ref[...])

    blk = pl.BlockSpec((None, ts, n), lambda i: (0, i, 0))
    out = jax.ShapeDtypeStruct((1, K, n), F32)
    return pl.pallas_call(
        body, name=name, grid=(K // ts,),
        in_specs=[blk, pl.BlockSpec((ts, n_pad), lambda i: (i, 0)), blk, blk],
        out_specs=[blk] * 4, out_shape=[out] * 4,
        compiler_params=_params(("parallel",)),
    )(w, g_padded, m, v)


def kernel(x, p, w_in, g_attn, g_q_lora, g_kv_lora, w_uq, w_ukv, w_o, g_ffn, w_ffn_gate, w_ffn_up, conv_w, conv_b, w_ffn_down, g_ple, w_ple_gate, w_ple_proj, g_final, loss_target, m_w_in, m_g_attn, m_g_q_lora, m_g_kv_lora, m_w_uq, m_w_ukv, m_w_o, m_g_ffn, m_w_ffn_gate, m_w_ffn_up, m_conv_w, m_conv_b, m_w_ffn_down, m_g_ple, m_w_ple_gate, m_w_ple_proj, m_g_final, v_w_in, v_g_attn, v_g_q_lora, v_g_kv_lora, v_w_uq, v_w_ukv, v_w_o, v_g_ffn, v_w_ffn_gate, v_w_ffn_up, v_conv_w, v_conv_b, v_w_ffn_down, v_g_ple, v_w_ple_gate, v_w_ple_proj, v_g_final):
    weights = dict(w_in=w_in, g_attn=g_attn, g_q_lora=g_q_lora, g_kv_lora=g_kv_lora, w_uq=w_uq, w_ukv=w_ukv, w_o=w_o,
                   g_ffn=g_ffn, w_ffn_gate=w_ffn_gate, w_ffn_up=w_ffn_up, conv_w=conv_w, conv_b=conv_b,
                   w_ffn_down=w_ffn_down, g_ple=g_ple, w_ple_gate=w_ple_gate, w_ple_proj=w_ple_proj, g_final=g_final)
    mom1 = dict(w_in=m_w_in, g_attn=m_g_attn, g_q_lora=m_g_q_lora, g_kv_lora=m_g_kv_lora, w_uq=m_w_uq, w_ukv=m_w_ukv,
                w_o=m_w_o, g_ffn=m_g_ffn, w_ffn_gate=m_w_ffn_gate, w_ffn_up=m_w_ffn_up, conv_w=m_conv_w,
                conv_b=m_conv_b, w_ffn_down=m_w_ffn_down, g_ple=m_g_ple, w_ple_gate=m_w_ple_gate,
                w_ple_proj=m_w_ple_proj, g_final=m_g_final)
    mom2 = dict(w_in=v_w_in, g_attn=v_g_attn, g_q_lora=v_g_q_lora, g_kv_lora=v_g_kv_lora, w_uq=v_w_uq, w_ukv=v_w_ukv,
                w_o=v_w_o, g_ffn=v_g_ffn, w_ffn_gate=v_w_ffn_gate, w_ffn_up=v_w_ffn_up, conv_w=v_conv_w,
                conv_b=v_conv_b, w_ffn_down=v_w_ffn_down, g_ple=v_g_ple, w_ple_gate=v_w_ple_gate,
                w_ple_proj=v_w_ple_proj, g_final=v_g_final)
    _, S, D = x.shape
    cfg = _Cfg(S, D)
    cx, cy, _ = _mesh_pos()

    shards = {name: _pad_shard(name, weights[name][0]) for name in BIG}
    gathered = _gather_weights([shards[name] for name in GATHER_FIRST])
    W = {name: _as_operand(name, g) for name, g in zip(GATHER_FIRST, gathered)}
    cw_all = _gather_small(_pack([jnp.pad(conv_w[0], ((0, 0), (0, cfg.Fp - cfg.Fs)))]))
    r_cw = cw_all.shape[0] // N_DEV
    cw_chips = [_unpack(cw_all[2 * k * r_cw:(2 * k + 1) * r_cw], [(CONV_WIDTH, cfg.Fp)])[0] for k in range(N_CHIP)]
    sp = dict(g_attn=g_attn, g_q_lora=g_q_lora, g_kv_lora=g_kv_lora, g_ffn=g_ffn, g_ple=g_ple,
              g_final=g_final.reshape(1, D), conv_w=jnp.concatenate(cw_chips, axis=1),
              conv_b=_channels_padded(conv_b, cfg))

    loss, dx, parts, gs = _local_step(cfg, x[0], p[0, 0], loss_target[0], W, shards, sp)
    loss = lax.psum(loss, ("x", "y", "c"))

    halves = [_chip_add(*parts[name]) for name in BIG]
    whole = dict(zip(BIG, _sibling_share(halves)))
    grads = {}

    small_names = ['g_attn', 'g_q_lora', 'g_kv_lora', 'g_ffn', 'g_ple', 'g_final', 'conv_b', 'conv_w']
    small_sum = _sum_devices(_gather_small(_pack([gs[name] for name in small_names])))
    for name, g in zip(small_names, _unpack(small_sum, [gs[name].shape for name in small_names])):
        grads[name] = g
    grads['g_final'] = grads['g_final'].reshape(D)
    grads['conv_b'] = _channels_unpadded(grads['conv_b'], cfg)
    mine = _chip_id(cx, cy)
    grads['conv_w'] = lax.dynamic_slice_in_dim(grads['conv_w'], mine * cfg.Fp, cfg.Fp, axis=1)[:, :cfg.Fs]
    grads['conv_w'] = grads['conv_w'].reshape(1, CONV_WIDTH, cfg.Fs)

    delta, new_m, new_v = {}, {}, {}
    for name in BIG:
        grads[name], delta[name], new_m[name], new_v[name] = _adamw_sharded(
            weights[name], whole[name], mom1[name], mom2[name], name="adamw_" + name)
    for name in WEIGHTS:
        if name in BIG:
            continue
        shape = weights[name].shape
        flat = (shape[-2], shape[-1]) if len(shape) == 3 else (1, shape[-1])
        d, m2, v2 = _adamw(weights[name].reshape(flat), grads[name].reshape(flat), mom1[name].reshape(flat),
                           mom2[name].reshape(flat), name="adamw_" + name)
        delta[name], new_m[name], new_v[name] = d.reshape(shape), m2.reshape(shape), v2.reshape(shape)

    return (loss, dx.reshape(1, S, D), *[grads[n] for n in WEIGHTS], *[delta[n] for n in WEIGHTS],
            *[new_m[n] for n in WEIGHTS], *[new_v[n] for n in WEIGHTS])
```

```python
import functools
import math

import jax
import jax.numpy as jnp
from jax import lax
from jax.experimental import pallas as pl
from jax.experimental.pallas import tpu as pltpu

F32 = jnp.float32
BF16 = jnp.bfloat16

LANE = 128
VMEM_LIMIT = 56 * 1024 * 1024
ROWWISE_BLOCK_BYTES = 5 * 1024 * 1024

N_CHIP = 4
MESH = pl.DeviceIdType.MESH

CHUNK = 64
RET_HEADS = 8
MLA_HEADS = 16
MLA_NOPE = 128
MLA_ROPE = 64
MLA_QK_PAD = 256
PLE_DIM = 256
CONV_WIDTH = 3
ROPE_BASE = 10000.0
EPS = 1e-6
RET_BLOCK = 256

ADAM_LR = 0.001
ADAM_B1 = 0.9
ADAM_B2 = 0.999
ADAM_EPS = 1e-08
ADAM_WD = 0.01
ADAM_STEP = 10

WEIGHTS = ['w_in', 'g_attn', 'g_q_lora', 'g_kv_lora', 'w_uq', 'w_ukv', 'w_o', 'g_ffn', 'w_ffn_gate', 'w_ffn_up',
           'conv_w', 'conv_b', 'w_ffn_down', 'g_ple', 'w_ple_gate', 'w_ple_proj', 'g_final']
COL_SHARDED = ('w_in', 'w_uq', 'w_ukv', 'w_ffn_gate', 'w_ffn_up', 'w_ple_proj')
ROW_SHARDED = ('w_o', 'w_ffn_down', 'w_ple_gate')
BIG = COL_SHARDED + ROW_SHARDED
SMALL_REPLICATED = ('g_attn', 'g_q_lora', 'g_kv_lora', 'g_ffn', 'conv_b', 'g_ple', 'g_final')


def _round_up(n, m):
    return (n + m - 1) // m * m


def _tile(dim, cap, align=LANE):
    if dim <= cap:
        return dim
    t = cap // align * align
    while t >= align:
        if dim % t == 0:
            return t
        t -= align
    return dim


def _params(sem):
    return pltpu.CompilerParams(dimension_semantics=sem, vmem_limit_bytes=VMEM_LIMIT)


HBM = pl.BlockSpec(memory_space=pltpu.HBM)


class _Side:
    def __init__(self, ins, out_shape, sems, start, finish):
        self.ins, self.out_shape, self.sems, self.start, self.finish = ins, out_shape, sems, start, finish


def _run_side(side, name):
    n_in, n_out = len(side.ins), len(side.out_shape)

    def body(*refs):
        ins, outs, sems = refs[:n_in], refs[n_in:n_in + n_out], refs[n_in + n_out:]
        side.start(ins, outs, sems)
        side.finish(ins, outs, sems)

    return pl.pallas_call(
        body, name=name, in_specs=[HBM] * n_in, out_specs=[HBM] * n_out, out_shape=side.out_shape,
        scratch_shapes=side.sems,
    )(*side.ins)


def _hosted_call(body, side, *, name, grid, in_specs, out_specs, out_shape, scratch_shapes, operands, semantics):
    if side is None:
        res = pl.pallas_call(
            body, name=name, grid=grid, in_specs=in_specs, out_specs=out_specs, out_shape=out_shape,
            scratch_shapes=scratch_shapes, compiler_params=_params(semantics))(*operands)
        return res, []
    n_in, n_out, n_scr = len(in_specs), len(out_specs), len(scratch_shapes)
    s_in, s_out = len(side.ins), len(side.out_shape)

    def hosted(*refs):
        refs = list(refs)
        ins, refs = refs[:n_in], refs[n_in:]
        side_ins, refs = refs[:s_in], refs[s_in:]
        outs, refs = refs[:n_out], refs[n_out:]
        side_outs, refs = refs[:s_out], refs[s_out:]
        scratch, sems = refs[:n_scr], refs[n_scr:]
        ids = [pl.program_id(a) for a in range(len(grid))]
        first = functools.reduce(jnp.logical_and, [i == 0 for i in ids])
        last = functools.reduce(jnp.logical_and, [i == g - 1 for i, g in zip(ids, grid)])

        @pl.when(first)
        def _():
            side.start(side_ins, side_outs, sems)

        body(*ins, *outs, *scratch)

        @pl.when(last)
        def _():
            side.finish(side_ins, side_outs, sems)

    res = pl.pallas_call(
        hosted, name=name, grid=grid, in_specs=list(in_specs) + [HBM] * s_in,
        out_specs=list(out_specs) + [HBM] * s_out, out_shape=list(out_shape) + list(side.out_shape),
        scratch_shapes=list(scratch_shapes) + list(side.sems),
        compiler_params=_params(("arbitrary",) * len(grid)))(*operands, *side.ins)
    return res[:n_out], res[n_out:]


def _mm_call(name, dims, grid, in_specs, out_specs, out_shape, acc_shape, operands, has_res, side=None):
    nsteps = grid[2]
    n_out = len(out_shape)

    def body(*refs):
        a_ref, b_ref = refs[0], refs[1]
        res_ref = refs[2] if has_res else None
        outs = refs[2 + has_res:2 + has_res + n_out]
        acc = refs[2 + has_res + n_out]
        k = pl.program_id(2)

        @pl.when(k == 0)
        def _():
            acc[...] = jnp.zeros_like(acc)

        acc[...] += lax.dot_general(a_ref[...], b_ref[...], (dims, ((), ())), preferred_element_type=F32)

        @pl.when(k == nsteps - 1)
        def _():
            r = acc[...]
            if has_res:
                r = r + res_ref[...]
            for o in outs:
                o[...] = r.astype(o.dtype)

    outs, side_outs = _hosted_call(
        body, side, name=name, grid=grid, in_specs=in_specs, out_specs=out_specs, out_shape=out_shape,
        scratch_shapes=[pltpu.VMEM(acc_shape, F32)], operands=operands,
        semantics=("parallel", "parallel", "arbitrary"))
    return (outs, side_outs) if side is not None else outs


def _mm_nn(a, w, *, name, res=None, out_dtype=F32, side=None):
    M, K = a.shape
    G, _, n = w.shape
    tm, tn, tk = _tile(M, 1024), _tile(n, 1408), _tile(K, 1024)
    npg = n // tn
    grid = (M // tm, G * npg, K // tk)
    in_specs = [pl.BlockSpec((tm, tk), lambda i, j, k: (i, k)),
                pl.BlockSpec((None, tk, tn), lambda i, j, k: (j // npg, k, j % npg))]
    operands = [a, w]
    if res is not None:
        in_specs.append(pl.BlockSpec((tm, tn), lambda i, j, k: (i, j)))
        operands.append(res)
    out_specs = [pl.BlockSpec((tm, tn), lambda i, j, k: (i, j))]
    out_shape = [jax.ShapeDtypeStruct((M, G * n), out_dtype)]
    got = _mm_call(name, ((1,), (0,)), grid, in_specs, out_specs, out_shape, (tm, tn), operands, res is not None, side)
    return (got[0][0], got[1]) if side is not None else got[0]


def _mm_nt(g, w, *, name, res=None, out_dtype=F32, side=None):
    M, _ = g.shape
    G, K, n = w.shape
    tm, tko, tn = _tile(M, 1024), _tile(K, 1024), _tile(n, 1408)
    npg = n // tn
    grid = (M // tm, K // tko, G * npg)
    in_specs = [pl.BlockSpec((tm, tn), lambda i, j, k: (i, k)),
                pl.BlockSpec((None, tko, tn), lambda i, j, k: (k // npg, j, k % npg))]
    operands = [g, w]
    if res is not None:
        in_specs.append(pl.BlockSpec((tm, tko), lambda i, j, k: (i, j)))
        operands.append(res)
    out_specs = [pl.BlockSpec((tm, tko), lambda i, j, k: (i, j))]
    out_shape = [jax.ShapeDtypeStruct((M, K), out_dtype)]
    got = _mm_call(name, ((1,), (1,)), grid, in_specs, out_specs, out_shape, (tm, tko), operands, res is not None, side)
    return (got[0][0], got[1]) if side is not None else got[0]


def _mm_tn(a, g, groups, *, name):
    M, K = a.shape
    n = g.shape[1] // groups
    tm, tko, tn = _tile(M, 1024), _tile(K, 1024), _tile(n, 1408)
    npg = n // tn
    grid = (K // tko, groups * npg, M // tm)
    in_specs = [pl.BlockSpec((tm, tko), lambda i, j, k: (k, i)),
                pl.BlockSpec((tm, tn), lambda i, j, k: (k, j))]
    out_spec = pl.BlockSpec((None, tko, tn), lambda i, j, k: (j // npg, i, j % npg))
    out_shape = [jax.ShapeDtypeStruct((groups, K, n), F32), jax.ShapeDtypeStruct((groups, K, n), BF16)]
    return _mm_call(name, ((0,), (0,)), grid, in_specs, [out_spec, out_spec], out_shape, (tko, tn), [a, g], False)


def _rowwise(fn, ins, outs, accs=(), *, name):
    R = next(a.shape[0] for a, kind in ins if kind == 'row')
    row_bytes = sum(a.shape[1] * a.dtype.itemsize for a, kind in ins if kind != 'bcast')
    row_bytes += sum(w * jnp.dtype(dt).itemsize for w, dt in outs)
    ts = 8
    while ts * 2 <= 512 and ts * 2 * row_bytes <= ROWWISE_BLOCK_BYTES:
        ts *= 2
    ts = min(ts, R)
    for a, kind in ins:
        if kind == 'per':
            ts = math.gcd(ts, a.shape[0])
    while R % ts:
        ts //= 2
    in_specs = []
    for a, kind in ins:
        w = a.shape[1]
        if kind == 'row':
            in_specs.append(pl.BlockSpec((ts, w), lambda i: (i, 0)))
        elif kind == 'bcast':
            in_specs.append(pl.BlockSpec((1, w), lambda i: (0, 0)))
        else:
            nper = a.shape[0] // ts
            in_specs.append(pl.BlockSpec((ts, w), lambda i, nper=nper: (i % nper, 0)))
    out_specs = [pl.BlockSpec((ts, w), lambda i: (i, 0)) for w, _ in outs]
    out_specs += [pl.BlockSpec((1, w), lambda i: (0, 0)) for w in accs]
    out_shape = [jax.ShapeDtypeStruct((R, w), dt) for w, dt in outs]
    out_shape += [jax.ShapeDtypeStruct((1, w), F32) for w in accs]
    n_in, n_out = len(ins), len(outs)

    def body(*refs):
        vals = [r[...] for r in refs[:n_in]]
        res = fn(*vals)
        for r, v in zip(refs[n_in:n_in + n_out], res[:n_out]):
            r[...] = v.astype(r.dtype)
        if accs:
            first = pl.program_id(0) == 0
            for r, v in zip(refs[n_in + n_out:], res[n_out:]):
                @pl.when(first)
                def _(r=r, v=v):
                    r[...] = v

                @pl.when(jnp.logical_not(first))
                def _(r=r, v=v):
                    r[...] += v

    return pl.pallas_call(
        body, name=name, grid=(R // ts,), in_specs=in_specs, out_specs=out_specs, out_shape=out_shape,
        compiler_params=_params(("arbitrary",) if accs else ("parallel",)),
    )(*[a for a, _ in ins])


def _mean(v):
    return jnp.mean(v, axis=-1, keepdims=True)


def _colsum(v):
    return jnp.sum(v, axis=0, keepdims=True)


def _sigmoid(v):
    return 1.0 / (1.0 + jnp.exp(-v))


def _rms_fwd(x, g, *, name):
    def fn(xv, gv):
        r = lax.rsqrt(_mean(xv * xv) + EPS)
        return [xv * r * gv, r]
    w = x.shape[1]
    return _rowwise(fn, [(x, 'row'), (g, 'bcast')], [(w, BF16), (1, F32)], name=name)


def _rms_bwd(x, r, dhn, g, dres, *, name):
    def fn(xv, rv, dv, gv, *rest):
        n = xv * rv
        dn = dv * gv
        dx = rv * (dn - n * _mean(dn * n))
        if rest:
            dx = dx + rest[0]
        return [dx, dx, _colsum(dv * n)]
    w = x.shape[1]
    ins = [(x, 'row'), (r, 'row'), (dhn, 'row'), (g, 'bcast')]
    if dres is not None:
        ins.append((dres, 'row'))
    return _rowwise(fn, ins, [(w, F32), (w, BF16)], [w], name=name)


def _rope(x, cos, sin_a, sin_b, shift, scale, out_dtype, *, name):
    w = x.shape[1]
    wr = cos.shape[1]
    lo = w - wr

    def fn(xv, cv, sav, sbv):
        xr = xv[:, lo:] if lo else xv
        y = xr * cv + pltpu.roll(xr, wr - shift, 1) * sav + pltpu.roll(xr, shift, 1) * sbv
        if lo:
            y = jnp.concatenate([xv[:, :lo], y], axis=1)
        return [y * scale]
    return _rowwise(fn, [(x, 'row'), (cos, 'per'), (sin_a, 'per'), (sin_b, 'per')], [(w, out_dtype)], name=name)[0]


def _dot(a, b, ca, cb):
    return lax.dot_general(a, b, (((ca,), (cb,)), ((), ())), preferred_element_type=F32)


def _ret_fwd(q, k, v, dec, dq_, dk_, ds_):
    H, S, d = q.shape
    T = dec.shape[1]
    nC = S // T

    def body(q_ref, k_ref, v_ref, m_ref, qd_ref, kd_ref, sd_ref, o_ref, st_ref, state):
        @pl.when(pl.program_id(1) == 0)
        def _():
            state[...] = jnp.zeros_like(state)

        st = state[...]
        st_ref[...] = st
        qv, kv, vv = q_ref[...], k_ref[...], v_ref[...]
        p = (_dot(qv, kv, 1, 1) * m_ref[...]).astype(BF16)
        qs = (qv.astype(F32) * qd_ref[...]).astype(BF16)
        o_ref[...] = _dot(p, vv, 1, 0) + _dot(qs, st.astype(BF16), 1, 0)
        ks = (kv.astype(F32) * kd_ref[...]).astype(BF16)
        state[...] = st * sd_ref[...] + _dot(ks, vv, 0, 0)

    blk = pl.BlockSpec((None, T, d), lambda h, c: (h, c, 0))
    return pl.pallas_call(
        body, name="ret_fwd", grid=(H, nC),
        in_specs=[blk, blk, blk,
                  pl.BlockSpec((None, T, T), lambda h, c: (h, 0, 0)),
                  pl.BlockSpec((None, T, 1), lambda h, c: (h, 0, 0)),
                  pl.BlockSpec((None, T, 1), lambda h, c: (h, 0, 0)),
                  pl.BlockSpec((None, 1, 1), lambda h, c: (h, 0, 0))],
        out_specs=[blk, pl.BlockSpec((None, None, d, d), lambda h, c: (h, c, 0, 0))],
        out_shape=[jax.ShapeDtypeStruct((H, S, d), F32), jax.ShapeDtypeStruct((H, nC, d, d), F32)],
        scratch_shapes=[pltpu.VMEM((d, d), F32)],
        compiler_params=_params(("parallel", "arbitrary")),
    )(q, k, v, dec, dq_, dk_, ds_)


def _ret_bwd(q, k, v, do, states, dec, dq_, dk_, ds_):
    H, S, d = q.shape
    T = dec.shape[1]
    nC = S // T

    def body(q_ref, k_ref, v_ref, do_ref, st_ref, m_ref, qd_ref, kd_ref, sd_ref, gq_ref, gk_ref, gv_ref, dstate):
        @pl.when(pl.program_id(1) == 0)
        def _():
            dstate[...] = jnp.zeros_like(dstate)

        qv, kv, vv, dov = q_ref[...], k_ref[...], v_ref[...], do_ref[...]
        m = m_ref[...]
        qd, kd = qd_ref[...], kd_ref[...]
        ds = dstate[...]
        dsb = ds.astype(BF16)
        sb = st_ref[...].astype(BF16)
        p = (_dot(qv, kv, 1, 1) * m).astype(BF16)
        da = (_dot(dov, vv, 1, 1) * m).astype(BF16)
        qs = (qv.astype(F32) * qd).astype(BF16)
        ks = (kv.astype(F32) * kd).astype(BF16)
        gq_ref[...] = _dot(da, kv, 1, 0) + _dot(dov, sb, 1, 1) * qd
        gk_ref[...] = _dot(da, qv, 0, 0) + _dot(vv, dsb, 1, 1) * kd
        gv_ref[...] = _dot(p, dov, 0, 0) + _dot(ks, dsb, 1, 0)
        dstate[...] = ds * sd_ref[...] + _dot(qs, dov, 0, 0)

    blk = pl.BlockSpec((None, T, d), lambda h, c: (h, nC - 1 - c, 0))
    out = jax.ShapeDtypeStruct((H, S, d), F32)
    return pl.pallas_call(
        body, name="ret_bwd", grid=(H, nC),
        in_specs=[blk, blk, blk, blk,
                  pl.BlockSpec((None, None, d, d), lambda h, c: (h, nC - 1 - c, 0, 0)),
                  pl.BlockSpec((None, T, T), lambda h, c: (h, 0, 0)),
                  pl.BlockSpec((None, T, 1), lambda h, c: (h, 0, 0)),
                  pl.BlockSpec((None, T, 1), lambda h, c: (h, 0, 0)),
                  pl.BlockSpec((None, 1, 1), lambda h, c: (h, 0, 0))],
        out_specs=[blk, blk, blk], out_shape=[out, out, out],
        scratch_shapes=[pltpu.VMEM((d, d), F32)],
        compiler_params=_params(("parallel", "arbitrary")),
    )(q, k, v, do, states, dec, dq_, dk_, ds_)


def _ret_tables(T, d):
    h = jnp.arange(RET_HEADS, dtype=F32)
    log_g = jnp.log1p(-jnp.exp2(-5.0 - h))
    idx = jnp.arange(T, dtype=F32)
    diff = idx[:, None] - idx[None, :]
    same = (jnp.arange(T)[:, None] // CHUNK) == (jnp.arange(T)[None, :] // CHUNK)
    earlier = (jnp.arange(T)[None, :] // CHUNK) < (jnp.arange(T)[:, None] // CHUNK)
    expo = jnp.where(same, jnp.abs(diff), diff)
    dec = jnp.where(same | earlier, jnp.exp(log_g[:, None, None] * expo[None]), 0.0)
    q_dec = jnp.exp(log_g[:, None] * (idx + 1.0))[..., None]
    k_dec = jnp.exp(log_g[:, None] * (T - 1.0 - idx))[..., None]
    s_dec = jnp.exp(log_g * T)[:, None, None]
    return dec.astype(F32), q_dec, k_dec, s_dec


NEG = -1e30


ROW_GROUP = 512


def _scores(q, kv, r, diagonal):
    s = _dot(q, kv, 1, 1)
    if diagonal:
        rg, T = s.shape
        qc = (r * rg + lax.broadcasted_iota(jnp.int32, (rg, T), 0)) // CHUNK
        kc = lax.broadcasted_iota(jnp.int32, (rg, T), 1) // CHUNK
        s = jnp.where(kc <= qc, s, NEG)
    return s


def _by_query_block(p, n):
    i = sum((p >= t * (t + 1) // 2).astype(jnp.int32) for t in range(1, n))
    return i, p - i * (i + 1) // 2


def _by_key_block(p, n):
    j = sum((p >= t * n - t * (t - 1) // 2).astype(jnp.int32) for t in range(1, n))
    return j, j + p - (j * n - j * (j - 1) // 2)


def _flash_fwd(q, k, v, side=None):
    H, S, dk = q.shape
    dv = v.shape[2]
    T = _tile(S, 512)
    n = S // T
    rg = min(ROW_GROUP, T)

    def body(q_ref, k_ref, v_ref, o_ref, lse_ref, m_s, l_s, acc):
        qi, ki = _by_query_block(pl.program_id(1), n)

        @pl.when(ki == 0)
        def _():
            m_s[...] = jnp.full_like(m_s, NEG)
            l_s[...] = jnp.zeros_like(l_s)
            acc[...] = jnp.zeros_like(acc)

        def step(diagonal):
            kv, vv = k_ref[...], v_ref[...]
            for r in range(T // rg):
                rows = pl.ds(r * rg, rg)
                s = _scores(q_ref[rows, :], kv, r, diagonal)
                m_old = m_s[rows, :]
                m_new = jnp.maximum(m_old, jnp.max(s, axis=1, keepdims=True))
                p = jnp.exp(s - m_new)
                alpha = jnp.exp(m_old - m_new)
                l_s[rows, :] = alpha * l_s[rows, :] + jnp.sum(p, axis=1, keepdims=True)
                acc[rows, :] = alpha * acc[rows, :] + _dot(p.astype(BF16), vv, 1, 0)
                m_s[rows, :] = m_new

        @pl.when(ki < qi)
        def _():
            step(False)

        @pl.when(ki == qi)
        def _():
            step(True)
            o_ref[...] = acc[...] / l_s[...]
            lse_ref[...] = m_s[...] + jnp.log(l_s[...])

    q_map = lambda h, p: (h, _by_query_block(p, n)[0], 0)
    kv_map = lambda h, p: (h, _by_query_block(p, n)[1], 0)
    return _hosted_call(
        body, side, name="mla_fwd", grid=(H, n * (n + 1) // 2),
        in_specs=[pl.BlockSpec((None, T, dk), q_map), pl.BlockSpec((None, T, dk), kv_map),
                  pl.BlockSpec((None, T, dv), kv_map)],
        out_specs=[pl.BlockSpec((None, T, dv), q_map), pl.BlockSpec((None, T, 1), q_map)],
        out_shape=[jax.ShapeDtypeStruct((H, S, dv), F32), jax.ShapeDtypeStruct((H, S, 1), F32)],
        scratch_shapes=[pltpu.VMEM((T, 1), F32), pltpu.VMEM((T, 1), F32), pltpu.VMEM((T, dv), F32)],
        operands=[q, k, v], semantics=("parallel", "arbitrary"))


def _flash_dq(q, k, v, do, lse, dlt, side=None):
    H, S, dk = q.shape
    dv = v.shape[2]
    T = _tile(S, 512)
    n = S // T
    rg = min(ROW_GROUP, T)

    def body(q_ref, k_ref, v_ref, do_ref, lse_ref, dlt_ref, dq_ref, acc):
        qi, ki = _by_query_block(pl.program_id(1), n)

        @pl.when(ki == 0)
        def _():
            acc[...] = jnp.zeros_like(acc)

        def step(diagonal):
            kv, vv = k_ref[...], v_ref[...]
            for r in range(T // rg):
                rows = pl.ds(r * rg, rg)
                p = jnp.exp(_scores(q_ref[rows, :], kv, r, diagonal) - lse_ref[rows, :])
                dp = _dot(do_ref[rows, :], vv, 1, 1)
                ds = (p * (dp - dlt_ref[rows, :])).astype(BF16)
                acc[rows, :] += _dot(ds, kv, 1, 0)

        @pl.when(ki < qi)
        def _():
            step(False)

        @pl.when(ki == qi)
        def _():
            step(True)
            dq_ref[...] = acc[...]

    q_map = lambda h, p: (h, _by_query_block(p, n)[0], 0)
    kv_map = lambda h, p: (h, _by_query_block(p, n)[1], 0)
    return _hosted_call(
        body, side, name="mla_dq", grid=(H, n * (n + 1) // 2),
        in_specs=[pl.BlockSpec((None, T, dk), q_map), pl.BlockSpec((None, T, dk), kv_map),
                  pl.BlockSpec((None, T, dv), kv_map), pl.BlockSpec((None, T, dv), q_map),
                  pl.BlockSpec((None, T, 1), q_map), pl.BlockSpec((None, T, 1), q_map)],
        out_specs=[pl.BlockSpec((None, T, dk), q_map)],
        out_shape=[jax.ShapeDtypeStruct((H, S, dk), F32)],
        scratch_shapes=[pltpu.VMEM((T, dk), F32)],
        operands=[q, k, v, do, lse, dlt], semantics=("parallel", "arbitrary"))


def _flash_dkv(q, k, v, do, lse, dlt, side=None):
    H, S, dk = q.shape
    dv = v.shape[2]
    T = _tile(S, 512)
    n = S // T
    rg = min(ROW_GROUP, T)

    def body(q_ref, k_ref, v_ref, do_ref, lse_ref, dlt_ref, dk_ref, dv_ref, acc_k, acc_v):
        ki, qi = _by_key_block(pl.program_id(1), n)

        @pl.when(qi == ki)
        def _():
            acc_k[...] = jnp.zeros_like(acc_k)
            acc_v[...] = jnp.zeros_like(acc_v)

        def step(diagonal):
            kv, vv = k_ref[...], v_ref[...]
            for r in range(T // rg):
                rows = pl.ds(r * rg, rg)
                qv, dov = q_ref[rows, :], do_ref[rows, :]
                p = jnp.exp(_scores(qv, kv, r, diagonal) - lse_ref[rows, :])
                acc_v[...] += _dot(p.astype(BF16), dov, 0, 0)
                dp = _dot(dov, vv, 1, 1)
                ds = (p * (dp - dlt_ref[rows, :])).astype(BF16)
                acc_k[...] += _dot(ds, qv, 0, 0)

        @pl.when(qi > ki)
        def _():
            step(False)

        @pl.when(qi == ki)
        def _():
            step(True)

        @pl.when(qi == n - 1)
        def _():
            dk_ref[...] = acc_k[...]
            dv_ref[...] = acc_v[...]

    q_map = lambda h, p: (h, _by_key_block(p, n)[1], 0)
    kv_map = lambda h, p: (h, _by_key_block(p, n)[0], 0)
    return _hosted_call(
        body, side, name="mla_dkv", grid=(H, n * (n + 1) // 2),
        in_specs=[pl.BlockSpec((None, T, dk), q_map), pl.BlockSpec((None, T, dk), kv_map),
                  pl.BlockSpec((None, T, dv), kv_map), pl.BlockSpec((None, T, dv), q_map),
                  pl.BlockSpec((None, T, 1), q_map), pl.BlockSpec((None, T, 1), q_map)],
        out_specs=[pl.BlockSpec((None, T, dk), kv_map), pl.BlockSpec((None, T, dv), kv_map)],
        out_shape=[jax.ShapeDtypeStruct((H, S, dk), F32), jax.ShapeDtypeStruct((H, S, dv), F32)],
        scratch_shapes=[pltpu.VMEM((T, dk), F32), pltpu.VMEM((T, dv), F32)],
        operands=[q, k, v, do, lse, dlt], semantics=("parallel", "arbitrary"))


def _shift_down(cur, prev, by):
    rows = lax.broadcasted_iota(jnp.int32, cur.shape, 0)
    return jnp.where(rows < by, pltpu.roll(prev, by, 0), pltpu.roll(cur, by, 0))


def _shift_up(cur, nxt, by):
    ts = cur.shape[0]
    rows = lax.broadcasted_iota(jnp.int32, cur.shape, 0)
    return jnp.where(rows >= ts - by, pltpu.roll(nxt, ts - by, 0), pltpu.roll(cur, ts - by, 0))


def _ffn_tiles(S, F):
    return _tile(S, 256, 8), _tile(F, 1024)


def _ffn_fwd(G, U, cw, cb):
    S, F = G.shape
    ts, tc = _ffn_tiles(S, F)

    def body(g_ref, gp_ref, u_ref, cw_ref, cb_ref, a_ref, act_ref):
        cur = g_ref[...]
        prev = gp_ref[...] * (pl.program_id(1) > 0).astype(F32)
        a = (cb_ref[...] + cw_ref[0:1, :] * _shift_down(cur, prev, 2) + cw_ref[1:2, :] * _shift_down(cur, prev, 1)
             + cw_ref[2:3, :] * cur)
        a_ref[...] = a
        act_ref[...] = (a * _sigmoid(a) * u_ref[...]).astype(BF16)

    cur_spec = pl.BlockSpec((ts, tc), lambda j, i: (i, j))
    return pl.pallas_call(
        body, name="ffn_act_fwd", grid=(F // tc, S // ts),
        in_specs=[cur_spec, pl.BlockSpec((ts, tc), lambda j, i: (jnp.maximum(i - 1, 0), j)), cur_spec,
                  pl.BlockSpec((CONV_WIDTH, tc), lambda j, i: (0, j)), pl.BlockSpec((1, tc), lambda j, i: (0, j))],
        out_specs=[cur_spec, cur_spec],
        out_shape=[jax.ShapeDtypeStruct((S, F), F32), jax.ShapeDtypeStruct((S, F), BF16)],
        compiler_params=_params(("parallel", "parallel")),
    )(G, G, U, cw, cb)


def _ffn_bwd_act(a, U, dact):
    S, F = a.shape
    ts, tc = _ffn_tiles(S, F)

    def body(a_ref, u_ref, d_ref, da_ref, du_ref, db_ref):
        av, dv = a_ref[...], d_ref[...]
        sg = _sigmoid(av)
        du_ref[...] = (dv * av * sg).astype(BF16)
        da = dv * u_ref[...] * sg * (1.0 + av * (1.0 - sg))
        da_ref[...] = da

        @pl.when(pl.program_id(1) == 0)
        def _():
            db_ref[...] = jnp.zeros_like(db_ref)

        db_ref[...] += _colsum(da)

    cur_spec = pl.BlockSpec((ts, tc), lambda j, i: (i, j))
    return pl.pallas_call(
        body, name="ffn_act_bwd", grid=(F // tc, S // ts),
        in_specs=[cur_spec, cur_spec, cur_spec],
        out_specs=[cur_spec, cur_spec, pl.BlockSpec((1, tc), lambda j, i: (0, j))],
        out_shape=[jax.ShapeDtypeStruct((S, F), F32), jax.ShapeDtypeStruct((S, F), BF16),
                   jax.ShapeDtypeStruct((1, F), F32)],
        compiler_params=_params(("parallel", "arbitrary")),
    )(a, U, dact)


def _ffn_bwd_conv(da, G, cw):
    S, F = da.shape
    ts, tc = _ffn_tiles(S, F)
    n = S // ts

    def body(d_ref, dn_ref, g_ref, gp_ref, cw_ref, dg_ref, dw_ref):
        i = pl.program_id(1)
        dcur = d_ref[...]
        dnxt = dn_ref[...] * (i < n - 1).astype(F32)
        cur = g_ref[...]
        prev = gp_ref[...] * (i > 0).astype(F32)
        dg = (cw_ref[2:3, :] * dcur + cw_ref[1:2, :] * _shift_up(dcur, dnxt, 1)
              + cw_ref[0:1, :] * _shift_up(dcur, dnxt, 2))
        dg_ref[...] = dg.astype(BF16)

        @pl.when(i == 0)
        def _():
            dw_ref[...] = jnp.zeros_like(dw_ref)

        dw_ref[0:1, :] += _colsum(dcur * _shift_down(cur, prev, 2))
        dw_ref[1:2, :] += _colsum(dcur * _shift_down(cur, prev, 1))
        dw_ref[2:3, :] += _colsum(dcur * cur)

    cur_spec = pl.BlockSpec((ts, tc), lambda j, i: (i, j))
    return pl.pallas_call(
        body, name="ffn_conv_bwd", grid=(F // tc, n),
        in_specs=[cur_spec, pl.BlockSpec((ts, tc), lambda j, i: (jnp.minimum(i + 1, n - 1), j)),
                  cur_spec, pl.BlockSpec((ts, tc), lambda j, i: (jnp.maximum(i - 1, 0), j)),
                  pl.BlockSpec((CONV_WIDTH, tc), lambda j, i: (0, j))],
        out_specs=[cur_spec, pl.BlockSpec((CONV_WIDTH, tc), lambda j, i: (0, j))],
        out_shape=[jax.ShapeDtypeStruct((S, F), BF16), jax.ShapeDtypeStruct((CONV_WIDTH, F), F32)],
        compiler_params=_params(("parallel", "arbitrary")),
    )(da, da, G, G, cw)


class _Cfg:
    def __init__(self, S, D):
        self.S, self.D = S, D
        self.RD = D // (2 * RET_HEADS)
        self.RW = RET_HEADS * self.RD
        self.MV = (D - self.RW) // MLA_HEADS
        self.QL, self.KVL = D // 4, D // 8
        self.F = ((8 * D // 3 + 255) // 256) * 256
        self.IN = 4 * self.RW + self.QL + self.KVL + MLA_ROPE
        self.INs = self.IN // N_CHIP
        self.INp = _round_up(self.INs, LANE)
        self.Fs = self.F // N_CHIP
        self.Fp = _round_up(self.Fs, LANE)
        self.F4 = N_CHIP * self.Fp
        self.QK = MLA_NOPE + MLA_ROPE
        self.KVH = MLA_NOPE + self.MV


def _head_major(t, H, d):
    S = t.shape[0]
    return t.reshape(S, H, d).transpose(1, 0, 2).reshape(H * S, d)


def _seq_major(t, H, d):
    S = t.shape[0] // H
    return t.reshape(H, S, d).transpose(1, 0, 2).reshape(S, H * d)


def _rope_tables(cfg):
    S = cfg.S

    def cs(dim):
        inv = 1.0 / (ROPE_BASE ** (jnp.arange(0, dim, 2, dtype=F32) / dim))
        ang = jnp.arange(S, dtype=F32)[:, None] * inv[None, :]
        return jnp.cos(ang), jnp.sin(ang)

    c, s = cs(cfg.RD)
    z = jnp.zeros_like(s)
    ret = (jnp.concatenate([c, c], 1), jnp.concatenate([-s, z], 1), jnp.concatenate([z, s], 1))
    c, s = cs(MLA_ROPE)
    tail1 = jnp.ones((S, MLA_QK_PAD - cfg.QK), F32)
    tail0 = jnp.zeros((S, MLA_QK_PAD - cfg.QK), F32)
    z = jnp.zeros_like(s)
    mla = (jnp.concatenate([c, c, tail1], 1), jnp.concatenate([-s, z, tail0], 1), jnp.concatenate([z, s, tail0], 1))
    return ret, mla


GATHER_FIRST = ('w_in',)
GATHER_BEHIND_IN_PROJ = ('w_uq', 'w_ukv', 'w_o')
GATHER_BEHIND_MLA = ('w_ffn_gate', 'w_ffn_up')
GATHER_BEHIND_FFN_GATE = ('w_ffn_down',)
GATHER_BEHIND_FFN_UP = ('w_ple_gate', 'w_ple_proj')
PAIR_BEHIND_FFN_DX = ('w_ple_proj', 'w_ple_gate', 'w_ffn_down', 'w_ffn_gate', 'w_ffn_up')
PAIR_AFTER_OUT_PROJ = ('w_o',)
REDUCE_BEHIND_DQ = ('w_ffn_gate', 'w_ple_gate', 'w_o')
REDUCE_BEHIND_DKV = ('w_ffn_up', 'w_ffn_down', 'w_ple_proj')
REDUCE_LAST = ('w_uq', 'w_ukv', 'w_in')


def _local_step(cfg, x, p, tgt, W, late, sp):
    W = dict(W)
    S, D, RD, RW, MV = cfg.S, cfg.D, cfg.RD, cfg.RW, cfg.MV
    H, MH = RET_HEADS, MLA_HEADS
    (rc, rsa, rsb), (mc, msa, msb) = _rope_tables(cfg)
    dec, q_dec, k_dec, s_dec = _ret_tables(min(RET_BLOCK, S), RD)
    k_scale = RD ** -0.5
    a_scale = cfg.QK ** -0.5
    gb, gs = {}, {}

    hn1, r1 = _rms_fwd(x, sp['g_attn'], name="rms_attn")
    proj, got = _mm_nn(hn1, W['w_in'], name="in_proj",
                       side=_gather_side([late[n] for n in GATHER_BEHIND_IN_PROJ]))
    W.update({n: _as_operand(n, g) for n, g in zip(GATHER_BEHIND_IN_PROJ, got)})
    proj = proj.reshape(S, N_CHIP, cfg.INp)[:, :, :cfg.INs].reshape(S, cfg.IN)
    cuts = [RW, 2 * RW, 3 * RW, 4 * RW, 4 * RW + cfg.QL, 4 * RW + cfg.QL + cfg.KVL]
    rq, rk, rv, rg, cq, ckv, kr = jnp.split(proj, cuts, axis=1)

    rq_h, rk_h, rv_h, rg_h = (_head_major(t, H, RD) for t in (rq, rk, rv, rg))
    q_r = _rope(rq_h, rc, rsa, rsb, RD // 2, 1.0, BF16, name="ret_rope_q").reshape(H, S, RD)
    k_r = _rope(rk_h, rc, rsa, rsb, RD // 2, k_scale, BF16, name="ret_rope_k").reshape(H, S, RD)
    v_r = rv_h.astype(BF16).reshape(H, S, RD)
    o_h, states = _ret_fwd(q_r, k_r, v_r, dec, q_dec, k_dec, s_dec)
    o_h = o_h.reshape(H * S, RD)

    def gate_fn(ov, gv):
        oc = ov - _mean(ov)
        ron = oc * lax.rsqrt(_mean(oc * oc) + EPS)
        return [gv * _sigmoid(gv) * ron]
    ro_h = _rowwise(gate_fn, [(o_h, 'row'), (rg_h, 'row')], [(RD, BF16)], name="ret_gate")[0]
    ro = _seq_major(ro_h, H, RD)

    cqn, rcq = _rms_fwd(cq, sp['g_q_lora'], name="rms_q")
    ckvn, rckv = _rms_fwd(ckv, sp['g_kv_lora'], name="rms_kv")
    Q = _mm_nn(cqn, W['w_uq'], name="q_up")
    KV = _mm_nn(ckvn, W['w_ukv'], name="kv_up")
    pad = jnp.zeros((MH, S, MLA_QK_PAD - cfg.QK), F32)
    q_raw = jnp.concatenate([Q.reshape(S, MH, cfg.QK).transpose(1, 0, 2), pad], 2).reshape(MH * S, MLA_QK_PAD)
    KV3 = KV.reshape(S, MH, cfg.KVH).transpose(1, 0, 2)
    k_raw = jnp.concatenate([KV3[:, :, :MLA_NOPE], jnp.broadcast_to(kr[None], (MH, S, MLA_ROPE)), pad], 2)
    k_raw = k_raw.reshape(MH * S, MLA_QK_PAD)
    v_m = KV3[:, :, MLA_NOPE:].astype(BF16)
    q_m = _rope(q_raw, mc, msa, msb, MLA_ROPE // 2, a_scale, BF16, name="mla_rope_q").reshape(MH, S, MLA_QK_PAD)
    k_m = _rope(k_raw, mc, msa, msb, MLA_ROPE // 2, 1.0, BF16, name="mla_rope_k").reshape(MH, S, MLA_QK_PAD)
    (mo_h, lse), got = _flash_fwd(q_m, k_m, v_m, side=_gather_side([late[n] for n in GATHER_BEHIND_MLA]))
    W.update({n: _as_operand(n, g) for n, g in zip(GATHER_BEHIND_MLA, got)})
    mo = _seq_major(mo_h.reshape(MH * S, MV), MH, MV)

    cat = jnp.concatenate([ro, mo.astype(BF16)], axis=1)
    h1 = _mm_nn(cat, W['w_o'], name="out_proj", res=x)

    hn2, r2 = _rms_fwd(h1, sp['g_ffn'], name="rms_ffn")
    G, got = _mm_nn(hn2, W['w_ffn_gate'], name="ffn_gate",
                    side=_gather_side([late[n] for n in GATHER_BEHIND_FFN_GATE]))
    W.update({n: _as_operand(n, g) for n, g in zip(GATHER_BEHIND_FFN_GATE, got)})
    U, got = _mm_nn(hn2, W['w_ffn_up'], name="ffn_up", side=_gather_side([late[n] for n in GATHER_BEHIND_FFN_UP]))
    W.update({n: _as_operand(n, g) for n, g in zip(GATHER_BEHIND_FFN_UP, got)})
    a, act = _ffn_fwd(G, U, sp['conv_w'], sp['conv_b'])
    h2 = _mm_nn(act, W['w_ffn_down'], name="ffn_down", res=h1)

    hn3, r3 = _rms_fwd(h2, sp['g_ple'], name="rms_ple")
    Z = _mm_nn(hn3, W['w_ple_gate'], name="ple_gate")
    p_b = p.astype(BF16)
    PP = _mm_nn(p_b, W['w_ple_proj'], name="ple_proj")

    def head_fn(h2v, zv, ppv, tv, gv):
        gate = _sigmoid(zv)
        h3 = h2v + gate * ppv
        r4 = lax.rsqrt(_mean(h3 * h3) + EPS)
        n4 = h3 * r4
        e = n4 * gv - tv
        dy = e * (1.0 / D)
        dn = dy * gv
        dh3 = r4 * (dn - n4 * _mean(dn * n4))
        dpp = dh3 * gate
        dz = dh3 * ppv * gate * (1.0 - gate)
        loss = jnp.sum(0.5 * _mean(e * e), axis=0, keepdims=True)
        return [dh3, dz, dpp, _colsum(dy * n4), jnp.broadcast_to(loss, (1, LANE))]
    dh3, dZ, dPP, dgf, loss = _rowwise(
        head_fn, [(h2, 'row'), (Z, 'row'), (PP, 'row'), (tgt, 'row'), (sp['g_final'], 'bcast')],
        [(D, F32), (D, BF16), (D, BF16)], [D, LANE], name="ple_loss_head")
    gs['g_final'] = dgf
    loss = loss[0, 0]

    gb['w_ple_proj'] = _mm_tn(p_b, dPP, N_CHIP, name="ple_proj_dw")
    gb['w_ple_gate'] = _mm_tn(hn3, dZ, 1, name="ple_gate_dw")
    dhn3 = _mm_nt(dZ, W['w_ple_gate'], name="ple_gate_dx")
    dh2, dh2_b, gs['g_ple'] = _rms_bwd(h2, r3, dhn3, sp['g_ple'], dh3, name="rms_ple_bwd")

    dact = _mm_nt(dh2_b, W['w_ffn_down'], name="ffn_down_dx")
    gb['w_ffn_down'] = _mm_tn(act, dh2_b, 1, name="ffn_down_dw")
    da, dU, gs['conv_b'] = _ffn_bwd_act(a, U, dact)
    dG, gs['conv_w'] = _ffn_bwd_conv(da, G, sp['conv_w'])
    gb['w_ffn_gate'] = _mm_tn(hn2, dG, N_CHIP, name="ffn_gate_dw")
    gb['w_ffn_up'] = _mm_tn(hn2, dU, N_CHIP, name="ffn_up_dw")
    def pair_sums(names, from_sibling):
        return {n: _pair_add(_grad_pieces(n, gb[n][0]), r) for n, r in zip(names, from_sibling)}

    def wire(names):
        return [_grad_pieces(n, gb[n][1]) for n in names]

    dhn2, got = _mm_nt(dG, W['w_ffn_gate'], name="ffn_gate_dx", side=_pair_side(wire(PAIR_BEHIND_FFN_DX)))
    pair = pair_sums(PAIR_BEHIND_FFN_DX, got)
    dhn2 = _mm_nt(dU, W['w_ffn_up'], name="ffn_up_dx", res=dhn2)
    dh1, dh1_b, gs['g_ffn'] = _rms_bwd(h1, r2, dhn2, sp['g_ffn'], dh2, name="rms_ffn_bwd")

    dcat = _mm_nt(dh1_b, W['w_o'], name="out_proj_dx")
    gb['w_o'] = _mm_tn(cat, dh1_b, 1, name="out_proj_dw")
    pair.update(pair_sums(PAIR_AFTER_OUT_PROJ, _pair_exchange(wire(PAIR_AFTER_OUT_PROJ))))
    from_chips = {}
    dro_h = _head_major(dcat[:, :RW], H, RD)

    def gate_bwd_fn(ov, gv, dv):
        oc = ov - _mean(ov)
        rs = lax.rsqrt(_mean(oc * oc) + EPS)
        ron = oc * rs
        sg = _sigmoid(gv)
        dron = dv * gv * sg
        drg = dv * ron * sg * (1.0 + gv * (1.0 - sg))
        do = rs * (dron - _mean(dron) - ron * _mean(dron * ron))
        return [do, drg]
    do_h, drg_h = _rowwise(gate_bwd_fn, [(o_h, 'row'), (rg_h, 'row'), (dro_h, 'row')],
                           [(RD, BF16), (RD, F32)], name="ret_gate_bwd")
    gq_r, gk_r, gv_r = _ret_bwd(q_r, k_r, v_r, do_h.reshape(H, S, RD), states, dec, q_dec, k_dec, s_dec)
    drq_h = _rope(gq_r.reshape(H * S, RD), rc, -rsa, -rsb, RD // 2, 1.0, F32, name="ret_rope_q_bwd")
    drk_h = _rope(gk_r.reshape(H * S, RD), rc, -rsa, -rsb, RD // 2, k_scale, F32, name="ret_rope_k_bwd")
    drq, drk, drv, drg = (_seq_major(t, H, RD) for t in (drq_h, drk_h, gv_r.reshape(H * S, RD), drg_h))

    dmo_h = _head_major(dcat[:, RW:], MH, MV)
    dlt = _rowwise(lambda ov, dv: [jnp.sum(ov * dv, axis=1, keepdims=True)],
                   [(mo_h.reshape(MH * S, MV), 'row'), (dmo_h, 'row')], [(1, F32)], name="mla_delta")[0]
    dmo_b = dmo_h.astype(BF16).reshape(MH, S, MV)
    dlt = dlt.reshape(MH, S, 1)
    (gq_m,), got = _flash_dq(q_m, k_m, v_m, dmo_b, lse, dlt,
                             side=_exchange_side([pair[n][1] for n in REDUCE_BEHIND_DQ]))
    from_chips.update(zip(REDUCE_BEHIND_DQ, got))
    (gk_m, gv_m), got = _flash_dkv(q_m, k_m, v_m, dmo_b, lse, dlt,
                                   side=_exchange_side([pair[n][1] for n in REDUCE_BEHIND_DKV]))
    from_chips.update(zip(REDUCE_BEHIND_DKV, got))
    dq_raw = _rope(gq_m.reshape(MH * S, MLA_QK_PAD), mc, -msa, -msb, MLA_ROPE // 2, a_scale, F32,
                   name="mla_rope_q_bwd")
    dk_raw = _rope(gk_m.reshape(MH * S, MLA_QK_PAD), mc, -msa, -msb, MLA_ROPE // 2, 1.0, F32, name="mla_rope_k_bwd")
    dQ = dq_raw.reshape(MH, S, MLA_QK_PAD)[:, :, :cfg.QK].transpose(1, 0, 2).reshape(S, MH * cfg.QK).astype(BF16)
    dk3 = dk_raw.reshape(MH, S, MLA_QK_PAD)
    dKV = jnp.concatenate([dk3[:, :, :MLA_NOPE], gv_m], 2).transpose(1, 0, 2).reshape(S, MH * cfg.KVH).astype(BF16)
    dkr_heads = [(dk3[h, :, MLA_NOPE:cfg.QK], 'row') for h in range(MH)]
    dkr = _rowwise(lambda *v: [functools.reduce(lambda s, t: s + t, v)], dkr_heads, [(MLA_ROPE, F32)],
                   name="mla_rope_k_heads")[0]

    gb['w_uq'] = _mm_tn(cqn, dQ, N_CHIP, name="q_up_dw")
    dcqn = _mm_nt(dQ, W['w_uq'], name="q_up_dx")
    dcq, _, gs['g_q_lora'] = _rms_bwd(cq, rcq, dcqn, sp['g_q_lora'], None, name="rms_q_bwd")
    gb['w_ukv'] = _mm_tn(ckvn, dKV, N_CHIP, name="kv_up_dw")
    dckvn = _mm_nt(dKV, W['w_ukv'], name="kv_up_dx")
    dckv, _, gs['g_kv_lora'] = _rms_bwd(ckv, rckv, dckvn, sp['g_kv_lora'], None, name="rms_kv_bwd")

    dproj = jnp.concatenate([drq, drk, drv, drg, dcq, dckv, dkr], axis=1).astype(BF16)
    dproj = jnp.pad(dproj.reshape(S, N_CHIP, cfg.INs), ((0, 0), (0, 0), (0, cfg.INp - cfg.INs)))
    dproj = dproj.reshape(S, N_CHIP * cfg.INp)
    gb['w_in'] = _mm_tn(hn1, dproj, N_CHIP, name="in_proj_dw")
    pair.update(pair_sums(REDUCE_LAST, _pair_exchange(wire(REDUCE_LAST))))
    dhn1, got = _mm_nt(dproj, W['w_in'], name="in_proj_dx", side=_exchange_side([pair[n][1] for n in REDUCE_LAST]))
    from_chips.update(zip(REDUCE_LAST, got))
    dx, _, gs['g_attn'] = _rms_bwd(x, r1, dhn1, sp['g_attn'], dh1, name="rms_attn_bwd")
    return loss, dx, {n: (pair[n][0], from_chips[n]) for n in BIG}, gs


def _padded_shape(name, shape):
    K, n = shape
    if name in COL_SHARDED:
        return K, _round_up(n, LANE)
    return _round_up(K, LANE), n


def _pad_shard(name, w):
    K, n = _padded_shape(name, w.shape)
    return jnp.pad(w.astype(BF16), ((0, K - w.shape[0]), (0, n - w.shape[1])))


def _as_operand(name, gathered):
    if name in COL_SHARDED:
        return gathered
    return gathered.reshape(1, gathered.shape[0] * gathered.shape[1], gathered.shape[2])


def _grad_pieces(name, g):
    if name in COL_SHARDED:
        return g
    return g.reshape(N_CHIP, g.shape[1] // N_CHIP, g.shape[2])


def _channels_padded(v, cfg):
    r = v.shape[0]
    return jnp.pad(v.reshape(r, N_CHIP, cfg.Fs), ((0, 0), (0, 0), (0, cfg.Fp - cfg.Fs))).reshape(r, cfg.F4)


def _channels_unpadded(v, cfg):
    r = v.shape[0]
    return v.reshape(r, N_CHIP, cfg.Fp)[:, :, :cfg.Fs].reshape(r, cfg.F)


def _mesh_pos():
    return lax.axis_index("x"), lax.axis_index("y"), lax.axis_index("c")


def _other_chips(x, y):
    return [(1 - x, y), (x, 1 - y), (1 - x, 1 - y)]


def _chip_id(cx, cy):
    return 2 * cx + cy


def _half(ref_rows, core):
    half = ref_rows // 2
    return pl.ds(core * half, half)


def _gather_side(shards):
    n = len(shards)

    def copies(srcs, outs, sems):
        ici_send, ici_recv, d2d_send, d2d_recv, own_send, own_recv = sems
        x, y, c = _mesh_pos()
        mine = _chip_id(x, y)
        others = _other_chips(x, y)

        def over_ici(w, j, chip):
            rows = _half(srcs[w].shape[0], c)
            return pltpu.make_async_remote_copy(
                src_ref=srcs[w].at[rows], dst_ref=outs[w].at[chip, rows],
                send_sem=ici_send.at[w, j], recv_sem=ici_recv.at[w, j],
                device_id=(*others[j], c), device_id_type=MESH)

        def over_d2d(w, j, core):
            rows = _half(srcs[w].shape[0], core)
            slab = outs[w].at[_chip_id(*others[j]), rows]
            return pltpu.make_async_remote_copy(
                src_ref=slab, dst_ref=slab, send_sem=d2d_send.at[w, j], recv_sem=d2d_recv.at[w, j],
                device_id=(x, y, 1 - c), device_id_type=MESH)

        def own(w):
            return pltpu.make_async_remote_copy(
                src_ref=srcs[w], dst_ref=outs[w].at[mine], send_sem=own_send.at[w], recv_sem=own_recv.at[w],
                device_id=(x, y, 1 - c), device_id_type=MESH)

        return c, mine, others, over_ici, over_d2d, own

    def start(srcs, outs, sems):
        c, mine, others, over_ici, over_d2d, own = copies(srcs, outs, sems)
        for w in range(n):
            for j in range(3):
                over_ici(w, j, mine).start()
        for w in range(n):
            own(w).start()

    def finish(srcs, outs, sems):
        c, mine, others, over_ici, over_d2d, own = copies(srcs, outs, sems)
        for w in range(n):
            for j in range(3):
                over_ici(w, j, _chip_id(*others[j])).wait_recv()
                over_d2d(w, j, c).start()
        for w in range(n):
            for j in range(3):
                over_d2d(w, j, 1 - c).wait_recv()
        for w in range(n):
            own(w).wait()
            for j in range(3):
                over_ici(w, j, mine).wait_send()
                over_d2d(w, j, c).wait_send()

    out_shape = [jax.ShapeDtypeStruct((N_CHIP,) + s.shape, s.dtype) for s in shards]
    sems = [pltpu.SemaphoreType.DMA((n, 3))] * 4 + [pltpu.SemaphoreType.DMA((n,))] * 2
    return _Side(shards, out_shape, sems, start, finish)


def _gather_weights(shards):
    return _run_side(_gather_side(shards), "gather_weights")


def _pair_side(grads):
    n = len(grads)

    def copies(srcs, outs, sems):
        send, recv = sems
        x, y, c = _mesh_pos()
        return [pltpu.make_async_remote_copy(
            src_ref=srcs[w].at[:, _half(srcs[w].shape[1], 1 - c), :], dst_ref=outs[w],
            send_sem=send.at[w], recv_sem=recv.at[w],
            device_id=(x, y, 1 - c), device_id_type=MESH) for w in range(n)]

    def start(srcs, outs, sems):
        for cp in copies(srcs, outs, sems):
            cp.start()

    def finish(srcs, outs, sems):
        for cp in copies(srcs, outs, sems):
            cp.wait()

    out_shape = [jax.ShapeDtypeStruct((g.shape[0], g.shape[1] // 2, g.shape[2]), g.dtype) for g in grads]
    return _Side(grads, out_shape, [pltpu.SemaphoreType.DMA((n,))] * 2, start, finish)


def _pair_exchange(grads):
    return _run_side(_pair_side(grads), "grad_pair_exchange")


def _exchange_side(parts):
    n = len(parts)

    def copies(srcs, outs, sems):
        send, recv = sems
        x, y, c = _mesh_pos()
        others = _other_chips(x, y)
        return [pltpu.make_async_remote_copy(
            src_ref=srcs[w].at[_chip_id(*others[j])], dst_ref=outs[w].at[j],
            send_sem=send.at[w, j], recv_sem=recv.at[w, j],
            device_id=(*others[j], c), device_id_type=MESH) for w in range(n) for j in range(3)]

    def start(srcs, outs, sems):
        for cp in copies(srcs, outs, sems):
            cp.start()

    def finish(srcs, outs, sems):
        for cp in copies(srcs, outs, sems):
            cp.wait()

    out_shape = [jax.ShapeDtypeStruct((3,) + p.shape[1:], p.dtype) for p in parts]
    return _Side(parts, out_shape, [pltpu.SemaphoreType.DMA((n, 3))] * 2, start, finish)


def _sibling_share(shards):
    n = len(shards)

    def body(*refs):
        outs = refs[n:2 * n]
        send, recv = refs[2 * n:]
        x, y, c = _mesh_pos()

        def half_of(w, core):
            rows = outs[w].at[_half(outs[w].shape[0], core)]
            return pltpu.make_async_remote_copy(
                src_ref=rows, dst_ref=rows, send_sem=send.at[w], recv_sem=recv.at[w],
                device_id=(x, y, 1 - c), device_id_type=MESH)

        for w in range(n):
            half_of(w, c).start()
        for w in range(n):
            half_of(w, 1 - c).wait_recv()
        for w in range(n):
            half_of(w, c).wait_send()

    return pl.pallas_call(
        body, name="grad_sibling_share", in_specs=[HBM] * n, out_specs=[HBM] * n,
        out_shape=[jax.ShapeDtypeStruct(s.shape, s.dtype) for s in shards],
        scratch_shapes=[pltpu.SemaphoreType.DMA((n,))] * 2, input_output_aliases={w: w for w in range(n)},
    )(*shards)


N_DEV = 8


def _gather_small(v):
    r, width = v.shape

    def body(v_ref, out_ref, send_sems, recv_sems, local_sem):
        x, y, c = _mesh_pos()
        me, sibling = (x, y, c), (x, y, 1 - c)
        chips = _other_chips(x, y)

        def rows(px, py, pc):
            return out_ref.at[pl.ds((4 * px + 2 * py + pc) * r, r), :]

        def copy(k, block, to, src=None):
            return pltpu.make_async_remote_copy(
                src_ref=rows(*block) if src is None else src, dst_ref=rows(*block),
                send_sem=send_sems.at[k], recv_sem=recv_sems.at[k], device_id=to, device_id_type=MESH)

        mine = pltpu.make_async_copy(v_ref, rows(*me), local_sem)
        mine.start()
        first = [copy(0, me, sibling, src=v_ref)]
        first += [copy(1 + j, me, (*chip, c), src=v_ref) for j, chip in enumerate(chips)]
        for cp in first:
            cp.start()
        passed = [copy(4 + j, (*chip, c), sibling) for j, chip in enumerate(chips)]
        for j, chip in enumerate(chips):
            copy(1 + j, (*chip, c), me).wait_recv()
            passed[j].start()
        copy(0, sibling, me).wait_recv()
        for j, chip in enumerate(chips):
            copy(4 + j, (*chip, 1 - c), me).wait_recv()
        for cp in first + passed:
            cp.wait_send()
        mine.wait()

    vmem = pl.BlockSpec(memory_space=pltpu.VMEM)
    return pl.pallas_call(
        body, name="gather_small", out_shape=jax.ShapeDtypeStruct((N_DEV * r, width), v.dtype),
        in_specs=[vmem], out_specs=vmem,
        scratch_shapes=[pltpu.SemaphoreType.DMA((7,)), pltpu.SemaphoreType.DMA((7,)), pltpu.SemaphoreType.DMA],
    )(v)


def _pack(arrays):
    flat = jnp.concatenate([a.reshape(-1) for a in arrays])
    size = _round_up(flat.shape[0], 8 * LANE)
    return jnp.pad(flat, (0, size - flat.shape[0])).reshape(size // LANE, LANE)


def _unpack(packed, shapes):
    flat = packed.reshape(-1)
    out, at = [], 0
    for s in shapes:
        size = math.prod(s)
        out.append(flat[at:at + size].reshape(s))
        at += size
    return out


def _sum_devices(gathered):
    r = gathered.shape[0] // N_DEV
    blocks = [(gathered[d * r:(d + 1) * r], 'row') for d in range(N_DEV)]
    return _rowwise(lambda *v: [functools.reduce(lambda s, t: s + t, v)], blocks, [(LANE, F32)],
                    name="small_grad_sum")[0]


def _reduce_tiles(half, n):
    return _tile(half, 128, 8)


def _pair_add(g32, r1):
    G, K, n = g32.shape
    half = K // 2
    tr = _reduce_tiles(half, n)
    nrt = half // tr

    def body(g_ref, r_ref, s32_ref, sb_ref):
        s = g_ref[...] + r_ref[...].astype(F32)
        s32_ref[...] = s
        sb_ref[...] = s.astype(BF16)

    blk = pl.BlockSpec((None, tr, n), lambda k, i: (k, i, 0))
    return pl.pallas_call(
        body, name="grad_pair_add", grid=(G, nrt),
        in_specs=[pl.BlockSpec((None, tr, n), lambda k, i: (k, lax.axis_index("c") * nrt + i, 0)), blk],
        out_specs=[blk, blk],
        out_shape=[jax.ShapeDtypeStruct((G, half, n), F32), jax.ShapeDtypeStruct((G, half, n), BF16)],
        compiler_params=_params(("parallel", "parallel")),
    )(g32, r1)


def _chip_add(s32, r2):
    _, half, n = s32.shape
    tr = _reduce_tiles(half, n)
    nrt = half // tr

    def body(s_ref, a_ref, b_ref, c_ref, o_ref):
        o_ref[...] = ((s_ref[...] + a_ref[...].astype(F32)) + b_ref[...].astype(F32)) + c_ref[...].astype(F32)

    def piece(j):
        return pl.BlockSpec((None, tr, n), lambda i, j=j: (j, i, 0))

    def mine(i):
        return _chip_id(lax.axis_index("x"), lax.axis_index("y")), i, 0

    return pl.pallas_call(
        body, name="grad_chip_add", grid=(nrt,),
        in_specs=[pl.BlockSpec((None, tr, n), mine), piece(0), piece(1), piece(2)],
        out_specs=pl.BlockSpec((tr, n), lambda i: (lax.axis_index("c") * nrt + i, 0)),
        out_shape=jax.ShapeDtypeStruct((2 * half, n), F32),
        compiler_params=_params(("parallel",)),
    )(s32, r2, r2, r2)


def _adamw_math(wv, gv, mv, vv):
    m2 = ADAM_B1 * mv + (1.0 - ADAM_B1) * gv
    v2 = ADAM_B2 * vv + (1.0 - ADAM_B2) * (gv * gv)
    m_hat = m2 / (1.0 - ADAM_B1 ** ADAM_STEP)
    v_hat = v2 / (1.0 - ADAM_B2 ** ADAM_STEP)
    delta = -ADAM_LR * (m_hat / (jnp.sqrt(v_hat) + ADAM_EPS) + ADAM_WD * wv)
    return [delta, m2, v2]


def _adamw(w, g, m, v, *, name):
    width = w.shape[1]
    return _rowwise(_adamw_math, [(w, 'row'), (g, 'row'), (m, 'row'), (v, 'row')], [(width, F32)] * 3, name=name)


def _adamw_sharded(w, g_padded, m, v, *, name):
    _, K, n = w.shape
    n_pad = g_padded.shape[1]
    ts = 8
    while ts * 2 <= 256 and ts * 2 * 8 * n_pad * 4 <= ROWWISE_BLOCK_BYTES and K % (ts * 2) == 0:
        ts *= 2

    def body(w_ref, g_ref, m_ref, v_ref, go_ref, d_ref, mo_ref, vo_ref):
        gv = g_ref[:, :n] if n != n_pad else g_ref[...]
        go_ref[...] = gv
        d_ref[...], mo_ref[...], vo_ref[...] = _adamw_math(w_ref[...], gv, m_ref[...], v_ref[...])

    blk = pl.BlockSpec((None, ts, n), lambda i: (0, i, 0))
    out = jax.ShapeDtypeStruct((1, K, n), F32)
    return pl.pallas_call(
        body, name=name, grid=(K // ts,),
        in_specs=[blk, pl.BlockSpec((ts, n_pad), lambda i: (i, 0)), blk, blk],
        out_specs=[blk] * 4, out_shape=[out] * 4,
        compiler_params=_params(("parallel",)),
    )(w, g_padded, m, v)


def kernel(x, p, w_in, g_attn, g_q_lora, g_kv_lora, w_uq, w_ukv, w_o, g_ffn, w_ffn_gate, w_ffn_up, conv_w, conv_b, w_ffn_down, g_ple, w_ple_gate, w_ple_proj, g_final, loss_target, m_w_in, m_g_attn, m_g_q_lora, m_g_kv_lora, m_w_uq, m_w_ukv, m_w_o, m_g_ffn, m_w_ffn_gate, m_w_ffn_up, m_conv_w, m_conv_b, m_w_ffn_down, m_g_ple, m_w_ple_gate, m_w_ple_proj, m_g_final, v_w_in, v_g_attn, v_g_q_lora, v_g_kv_lora, v_w_uq, v_w_ukv, v_w_o, v_g_ffn, v_w_ffn_gate, v_w_ffn_up, v_conv_w, v_conv_b, v_w_ffn_down, v_g_ple, v_w_ple_gate, v_w_ple_proj, v_g_final):
    weights = dict(w_in=w_in, g_attn=g_attn, g_q_lora=g_q_lora, g_kv_lora=g_kv_lora, w_uq=w_uq, w_ukv=w_ukv, w_o=w_o,
                   g_ffn=g_ffn, w_ffn_gate=w_ffn_gate, w_ffn_up=w_ffn_up, conv_w=conv_w, conv_b=conv_b,
                   w_ffn_down=w_ffn_down, g_ple=g_ple, w_ple_gate=w_ple_gate, w_ple_proj=w_ple_proj, g_final=g_final)
    mom1 = dict(w_in=m_w_in, g_attn=m_g_attn, g_q_lora=m_g_q_lora, g_kv_lora=m_g_kv_lora, w_uq=m_w_uq, w_ukv=m_w_ukv,
                w_o=m_w_o, g_ffn=m_g_ffn, w_ffn_gate=m_w_ffn_gate, w_ffn_up=m_w_ffn_up, conv_w=m_conv_w,
                conv_b=m_conv_b, w_ffn_down=m_w_ffn_down, g_ple=m_g_ple, w_ple_gate=m_w_ple_gate,
                w_ple_proj=m_w_ple_proj, g_final=m_g_final)
    mom2 = dict(w_in=v_w_in, g_attn=v_g_attn, g_q_lora=v_g_q_lora, g_kv_lora=v_g_kv_lora, w_uq=v_w_uq, w_ukv=v_w_ukv,
                w_o=v_w_o, g_ffn=v_g_ffn, w_ffn_gate=v_w_ffn_gate, w_ffn_up=v_w_ffn_up, conv_w=v_conv_w,
                conv_b=v_conv_b, w_ffn_down=v_w_ffn_down, g_ple=v_g_ple, w_ple_gate=v_w_ple_gate,
                w_ple_proj=v_w_ple_proj, g_final=v_g_final)
    _, S, D = x.shape
    cfg = _Cfg(S, D)
    cx, cy, _ = _mesh_pos()

    shards = {name: _pad_shard(name, weights[name][0]) for name in BIG}
    gathered = _gather_weights([shards[name] for name in GATHER_FIRST])
    W = {name: _as_operand(name, g) for name, g in zip(GATHER_FIRST, gathered)}
    cw_all = _gather_small(_pack([jnp.pad(conv_w[0], ((0, 0), (0, cfg.Fp - cfg.Fs)))]))
    r_cw = cw_all.shape[0] // N_DEV
    cw_chips = [_unpack(cw_all[2 * k * r_cw:(2 * k + 1) * r_cw], [(CONV_WIDTH, cfg.Fp)])[0] for k in range(N_CHIP)]
    sp = dict(g_attn=g_attn, g_q_lora=g_q_lora, g_kv_lora=g_kv_lora, g_ffn=g_ffn, g_ple=g_ple,
              g_final=g_final.reshape(1, D), conv_w=jnp.concatenate(cw_chips, axis=1),
              conv_b=_channels_padded(conv_b, cfg))

    loss, dx, parts, gs = _local_step(cfg, x[0], p[0, 0], loss_target[0], W, shards, sp)
    loss = lax.psum(loss, ("x", "y", "c"))

    halves = [_chip_add(*parts[name]) for name in BIG]
    whole = dict(zip(BIG, _sibling_share(halves)))
    grads = {}

    small_names = ['g_attn', 'g_q_lora', 'g_kv_lora', 'g_ffn', 'g_ple', 'g_final', 'conv_b', 'conv_w']
    small_sum = _sum_devices(_gather_small(_pack([gs[name] for name in small_names])))
    for name, g in zip(small_names, _unpack(small_sum, [gs[name].shape for name in small_names])):
        grads[name] = g
    grads['g_final'] = grads['g_final'].reshape(D)
    grads['conv_b'] = _channels_unpadded(grads['conv_b'], cfg)
    mine = _chip_id(cx, cy)
    grads['conv_w'] = lax.dynamic_slice_in_dim(grads['conv_w'], mine * cfg.Fp, cfg.Fp, axis=1)[:, :cfg.Fs]
    grads['conv_w'] = grads['conv_w'].reshape(1, CONV_WIDTH, cfg.Fs)

    delta, new_m, new_v = {}, {}, {}
    for name in BIG:
        grads[name], delta[name], new_m[name], new_v[name] = _adamw_sharded(
            weights[name], whole[name], mom1[name], mom2[name], name="adamw_" + name)
    for name in WEIGHTS:
        if name in BIG:
            continue
        shape = weights[name].shape
        flat = (shape[-2], shape[-1]) if len(shape) == 3 else (1, shape[-1])
        d, m2, v2 = _adamw(weights[name].reshape(flat), grads[name].reshape(flat), mom1[name].reshape(flat),
                           mom2[name].reshape(flat), name="adamw_" + name)
        delta[name], new_m[name], new_v[name] = d.reshape(shape), m2.reshape(shape), v2.reshape(shape)

    return (loss, dx.reshape(1, S, D), *[grads[n] for n in WEIGHTS], *[delta[n] for n in WEIGHTS],
            *[new_m[n] for n in WEIGHTS], *[new_v[n] for n in WEIGHTS])
```

```python
import functools
import math

import jax
import jax.numpy as jnp
from jax import lax
from jax.experimental import pallas as pl
from jax.experimental.pallas import tpu as pltpu

F32 = jnp.float32
BF16 = jnp.bfloat16

LANE = 128
VMEM_LIMIT = 56 * 1024 * 1024
ROWWISE_BLOCK_BYTES = 5 * 1024 * 1024
MM_TILE = 1024
MM_TILE_N = 1408
MM_TILE_CONTRACT = 2048
MM_TILE_CONTRACT_N = 2816

N_CHIP = 4
MESH = pl.DeviceIdType.MESH

CHUNK = 64
RET_HEADS = 8
MLA_HEADS = 16
MLA_NOPE = 128
MLA_ROPE = 64
MLA_QK_PAD = 256
PLE_DIM = 256
CONV_WIDTH = 3
ROPE_BASE = 10000.0
EPS = 1e-6
RET_BLOCK = 256

ADAM_LR = 0.001
ADAM_B1 = 0.9
ADAM_B2 = 0.999
ADAM_EPS = 1e-08
ADAM_WD = 0.01
ADAM_STEP = 10

WEIGHTS = ['w_in', 'g_attn', 'g_q_lora', 'g_kv_lora', 'w_uq', 'w_ukv', 'w_o', 'g_ffn', 'w_ffn_gate', 'w_ffn_up',
           'conv_w', 'conv_b', 'w_ffn_down', 'g_ple', 'w_ple_gate', 'w_ple_proj', 'g_final']
COL_SHARDED = ('w_in', 'w_uq', 'w_ukv', 'w_ffn_gate', 'w_ffn_up', 'w_ple_proj')
ROW_SHARDED = ('w_o', 'w_ffn_down', 'w_ple_gate')
BIG = COL_SHARDED + ROW_SHARDED
SMALL_REPLICATED = ('g_attn', 'g_q_lora', 'g_kv_lora', 'g_ffn', 'conv_b', 'g_ple', 'g_final')


def _round_up(n, m):
    return (n + m - 1) // m * m


def _tile(dim, cap, align=LANE):
    if dim <= cap:
        return dim
    t = cap // align * align
    while t >= align:
        if dim % t == 0:
            return t
        t -= align
    return dim


def _params(sem):
    return pltpu.CompilerParams(dimension_semantics=sem, vmem_limit_bytes=VMEM_LIMIT)


HBM = pl.BlockSpec(memory_space=pltpu.HBM)


class _Side:
    def __init__(self, ins, out_shape, sems, start, finish):
        self.ins, self.out_shape, self.sems, self.start, self.finish = ins, out_shape, sems, start, finish


def _run_side(side, name):
    n_in, n_out = len(side.ins), len(side.out_shape)

    def body(*refs):
        ins, outs, sems = refs[:n_in], refs[n_in:n_in + n_out], refs[n_in + n_out:]
        side.start(ins, outs, sems)
        side.finish(ins, outs, sems)

    return pl.pallas_call(
        body, name=name, in_specs=[HBM] * n_in, out_specs=[HBM] * n_out, out_shape=side.out_shape,
        scratch_shapes=side.sems,
    )(*side.ins)


def _hosted_call(body, side, *, name, grid, in_specs, out_specs, out_shape, scratch_shapes, operands, semantics):
    if side is None:
        res = pl.pallas_call(
            body, name=name, grid=grid, in_specs=in_specs, out_specs=out_specs, out_shape=out_shape,
            scratch_shapes=scratch_shapes, compiler_params=_params(semantics))(*operands)
        return res, []
    n_in, n_out, n_scr = len(in_specs), len(out_specs), len(scratch_shapes)
    s_in, s_out = len(side.ins), len(side.out_shape)

    def hosted(*refs):
        refs = list(refs)
        ins, refs = refs[:n_in], refs[n_in:]
        side_ins, refs = refs[:s_in], refs[s_in:]
        outs, refs = refs[:n_out], refs[n_out:]
        side_outs, refs = refs[:s_out], refs[s_out:]
        scratch, sems = refs[:n_scr], refs[n_scr:]
        ids = [pl.program_id(a) for a in range(len(grid))]
        first = functools.reduce(jnp.logical_and, [i == 0 for i in ids])
        last = functools.reduce(jnp.logical_and, [i == g - 1 for i, g in zip(ids, grid)])

        @pl.when(first)
        def _():
            side.start(side_ins, side_outs, sems)

        body(*ins, *outs, *scratch)

        @pl.when(last)
        def _():
            side.finish(side_ins, side_outs, sems)

    res = pl.pallas_call(
        hosted, name=name, grid=grid, in_specs=list(in_specs) + [HBM] * s_in,
        out_specs=list(out_specs) + [HBM] * s_out, out_shape=list(out_shape) + list(side.out_shape),
        scratch_shapes=list(scratch_shapes) + list(side.sems),
        compiler_params=_params(("arbitrary",) * len(grid)))(*operands, *side.ins)
    return res[:n_out], res[n_out:]


def _mm_call(name, dims, grid, in_specs, out_specs, out_shape, acc_shape, operands, has_res, side=None):
    nsteps = grid[2]
    n_out = len(out_shape)

    def body(*refs):
        a_ref, b_ref = refs[0], refs[1]
        res_ref = refs[2] if has_res else None
        outs = refs[2 + has_res:2 + has_res + n_out]
        acc = refs[2 + has_res + n_out]
        k = pl.program_id(2)

        @pl.when(k == 0)
        def _():
            acc[...] = jnp.zeros_like(acc)

        acc[...] += lax.dot_general(a_ref[...], b_ref[...], (dims, ((), ())), preferred_element_type=F32)

        @pl.when(k == nsteps - 1)
        def _():
            r = acc[...]
            if has_res:
                r = r + res_ref[...]
            for o in outs:
                o[...] = r.astype(o.dtype)

    outs, side_outs = _hosted_call(
        body, side, name=name, grid=grid, in_specs=in_specs, out_specs=out_specs, out_shape=out_shape,
        scratch_shapes=[pltpu.VMEM(acc_shape, F32)], operands=operands,
        semantics=("parallel", "parallel", "arbitrary"))
    return (outs, side_outs) if side is not None else outs


def _mm_nn(a, w, *, name, res=None, out_dtype=F32, side=None):
    M, K = a.shape
    G, _, n = w.shape
    tm, tn, tk = _tile(M, MM_TILE), _tile(n, MM_TILE_N), _tile(K, MM_TILE_CONTRACT)
    npg = n // tn
    grid = (M // tm, G * npg, K // tk)
    in_specs = [pl.BlockSpec((tm, tk), lambda i, j, k: (i, k)),
                pl.BlockSpec((None, tk, tn), lambda i, j, k: (j // npg, k, j % npg))]
    operands = [a, w]
    if res is not None:
        in_specs.append(pl.BlockSpec((tm, tn), lambda i, j, k: (i, j)))
        operands.append(res)
    out_specs = [pl.BlockSpec((tm, tn), lambda i, j, k: (i, j))]
    out_shape = [jax.ShapeDtypeStruct((M, G * n), out_dtype)]
    got = _mm_call(name, ((1,), (0,)), grid, in_specs, out_specs, out_shape, (tm, tn), operands, res is not None, side)
    return (got[0][0], got[1]) if side is not None else got[0]


def _mm_nt(g, w, *, name, res=None, out_dtype=F32, side=None):
    M, _ = g.shape
    G, K, n = w.shape
    tm, tko, tn = _tile(M, MM_TILE), _tile(K, MM_TILE), _tile(n, MM_TILE_CONTRACT_N)
    npg = n // tn
    grid = (M // tm, K // tko, G * npg)
    in_specs = [pl.BlockSpec((tm, tn), lambda i, j, k: (i, k)),
                pl.BlockSpec((None, tko, tn), lambda i, j, k: (k // npg, j, k % npg))]
    operands = [g, w]
    if res is not None:
        in_specs.append(pl.BlockSpec((tm, tko), lambda i, j, k: (i, j)))
        operands.append(res)
    out_specs = [pl.BlockSpec((tm, tko), lambda i, j, k: (i, j))]
    out_shape = [jax.ShapeDtypeStruct((M, K), out_dtype)]
    got = _mm_call(name, ((1,), (1,)), grid, in_specs, out_specs, out_shape, (tm, tko), operands, res is not None, side)
    return (got[0][0], got[1]) if side is not None else got[0]


def _mm_tn(a, g, groups, *, name):
    M, K = a.shape
    n = g.shape[1] // groups
    tm, tko, tn = _tile(M, MM_TILE_CONTRACT), _tile(K, MM_TILE), _tile(n, MM_TILE_N)
    npg = n // tn
    grid = (K // tko, groups * npg, M // tm)
    in_specs = [pl.BlockSpec((tm, tko), lambda i, j, k: (k, i)),
                pl.BlockSpec((tm, tn), lambda i, j, k: (k, j))]
    out_spec = pl.BlockSpec((None, tko, tn), lambda i, j, k: (j // npg, i, j % npg))
    out_shape = [jax.ShapeDtypeStruct((groups, K, n), F32), jax.ShapeDtypeStruct((groups, K, n), BF16)]
    return _mm_call(name, ((0,), (0,)), grid, in_specs, [out_spec, out_spec], out_shape, (tko, tn), [a, g], False)


def _rowwise(fn, ins, outs, accs=(), *, name):
    R = next(a.shape[0] for a, kind in ins if kind == 'row')
    row_bytes = sum(a.shape[1] * a.dtype.itemsize for a, kind in ins if kind != 'bcast')
    row_bytes += sum(w * jnp.dtype(dt).itemsize for w, dt in outs)
    ts = 8
    while ts * 2 <= 512 and ts * 2 * row_bytes <= ROWWISE_BLOCK_BYTES:
        ts *= 2
    ts = min(ts, R)
    for a, kind in ins:
        if kind == 'per':
            ts = math.gcd(ts, a.shape[0])
    while R % ts:
        ts //= 2
    in_specs = []
    for a, kind in ins:
        w = a.shape[1]
        if kind == 'row':
            in_specs.append(pl.BlockSpec((ts, w), lambda i: (i, 0)))
        elif kind == 'bcast':
            in_specs.append(pl.BlockSpec((1, w), lambda i: (0, 0)))
        else:
            nper = a.shape[0] // ts
            in_specs.append(pl.BlockSpec((ts, w), lambda i, nper=nper: (i % nper, 0)))
    out_specs = [pl.BlockSpec((ts, w), lambda i: (i, 0)) for w, _ in outs]
    out_specs += [pl.BlockSpec((1, w), lambda i: (0, 0)) for w in accs]
    out_shape = [jax.ShapeDtypeStruct((R, w), dt) for w, dt in outs]
    out_shape += [jax.ShapeDtypeStruct((1, w), F32) for w in accs]
    n_in, n_out = len(ins), len(outs)

    def body(*refs):
        vals = [r[...] for r in refs[:n_in]]
        res = fn(*vals)
        for r, v in zip(refs[n_in:n_in + n_out], res[:n_out]):
            r[...] = v.astype(r.dtype)
        if accs:
            first = pl.program_id(0) == 0
            for r, v in zip(refs[n_in + n_out:], res[n_out:]):
                @pl.when(first)
                def _(r=r, v=v):
                    r[...] = v

                @pl.when(jnp.logical_not(first))
                def _(r=r, v=v):
                    r[...] += v

    return pl.pallas_call(
        body, name=name, grid=(R // ts,), in_specs=in_specs, out_specs=out_specs, out_shape=out_shape,
        compiler_params=_params(("arbitrary",) if accs else ("parallel",)),
    )(*[a for a, _ in ins])


def _mean(v):
    return jnp.mean(v, axis=-1, keepdims=True)


def _colsum(v):
    return jnp.sum(v, axis=0, keepdims=True)


def _sigmoid(v):
    return 1.0 / (1.0 + jnp.exp(-v))


def _rms_fwd(x, g, *, name):
    def fn(xv, gv):
        r = lax.rsqrt(_mean(xv * xv) + EPS)
        return [xv * r * gv, r]
    w = x.shape[1]
    return _rowwise(fn, [(x, 'row'), (g, 'bcast')], [(w, BF16), (1, F32)], name=name)


def _rms_bwd(x, r, dhn, g, dres, *, name):
    def fn(xv, rv, dv, gv, *rest):
        n = xv * rv
        dn = dv * gv
        dx = rv * (dn - n * _mean(dn * n))
        if rest:
            dx = dx + rest[0]
        return [dx, dx, _colsum(dv * n)]
    w = x.shape[1]
    ins = [(x, 'row'), (r, 'row'), (dhn, 'row'), (g, 'bcast')]
    if dres is not None:
        ins.append((dres, 'row'))
    return _rowwise(fn, ins, [(w, F32), (w, BF16)], [w], name=name)


def _rope(x, cos, sin_a, sin_b, shift, scale, out_dtype, *, name):
    w = x.shape[1]
    wr = cos.shape[1]
    lo = w - wr

    def fn(xv, cv, sav, sbv):
        xr = xv[:, lo:] if lo else xv
        y = xr * cv + pltpu.roll(xr, wr - shift, 1) * sav + pltpu.roll(xr, shift, 1) * sbv
        if lo:
            y = jnp.concatenate([xv[:, :lo], y], axis=1)
        return [y * scale]
    return _rowwise(fn, [(x, 'row'), (cos, 'per'), (sin_a, 'per'), (sin_b, 'per')], [(w, out_dtype)], name=name)[0]


def _dot(a, b, ca, cb):
    return lax.dot_general(a, b, (((ca,), (cb,)), ((), ())), preferred_element_type=F32)


def _ret_fwd(q, k, v, dec, dq_, dk_, ds_):
    H, S, d = q.shape
    T = dec.shape[1]
    nC = S // T

    def body(q_ref, k_ref, v_ref, m_ref, qd_ref, kd_ref, sd_ref, o_ref, st_ref, state):
        @pl.when(pl.program_id(1) == 0)
        def _():
            state[...] = jnp.zeros_like(state)

        st = state[...]
        st_ref[...] = st
        qv, kv, vv = q_ref[...], k_ref[...], v_ref[...]
        p = (_dot(qv, kv, 1, 1) * m_ref[...]).astype(BF16)
        qs = (qv.astype(F32) * qd_ref[...]).astype(BF16)
        o_ref[...] = _dot(p, vv, 1, 0) + _dot(qs, st.astype(BF16), 1, 0)
        ks = (kv.astype(F32) * kd_ref[...]).astype(BF16)
        state[...] = st * sd_ref[...] + _dot(ks, vv, 0, 0)

    blk = pl.BlockSpec((None, T, d), lambda h, c: (h, c, 0))
    return pl.pallas_call(
        body, name="ret_fwd", grid=(H, nC),
        in_specs=[blk, blk, blk,
                  pl.BlockSpec((None, T, T), lambda h, c: (h, 0, 0)),
                  pl.BlockSpec((None, T, 1), lambda h, c: (h, 0, 0)),
                  pl.BlockSpec((None, T, 1), lambda h, c: (h, 0, 0)),
                  pl.BlockSpec((None, 1, 1), lambda h, c: (h, 0, 0))],
        out_specs=[blk, pl.BlockSpec((None, None, d, d), lambda h, c: (h, c, 0, 0))],
        out_shape=[jax.ShapeDtypeStruct((H, S, d), F32), jax.ShapeDtypeStruct((H, nC, d, d), F32)],
        scratch_shapes=[pltpu.VMEM((d, d), F32)],
        compiler_params=_params(("parallel", "arbitrary")),
    )(q, k, v, dec, dq_, dk_, ds_)


def _ret_bwd(q, k, v, do, states, dec, dq_, dk_, ds_):
    H, S, d = q.shape
    T = dec.shape[1]
    nC = S // T

    def body(q_ref, k_ref, v_ref, do_ref, st_ref, m_ref, qd_ref, kd_ref, sd_ref, gq_ref, gk_ref, gv_ref, dstate):
        @pl.when(pl.program_id(1) == 0)
        def _():
            dstate[...] = jnp.zeros_like(dstate)

        qv, kv, vv, dov = q_ref[...], k_ref[...], v_ref[...], do_ref[...]
        m = m_ref[...]
        qd, kd = qd_ref[...], kd_ref[...]
        ds = dstate[...]
        dsb = ds.astype(BF16)
        sb = st_ref[...].astype(BF16)
        p = (_dot(qv, kv, 1, 1) * m).astype(BF16)
        da = (_dot(dov, vv, 1, 1) * m).astype(BF16)
        qs = (qv.astype(F32) * qd).astype(BF16)
        ks = (kv.astype(F32) * kd).astype(BF16)
        gq_ref[...] = _dot(da, kv, 1, 0) + _dot(dov, sb, 1, 1) * qd
        gk_ref[...] = _dot(da, qv, 0, 0) + _dot(vv, dsb, 1, 1) * kd
        gv_ref[...] = _dot(p, dov, 0, 0) + _dot(ks, dsb, 1, 0)
        dstate[...] = ds * sd_ref[...] + _dot(qs, dov, 0, 0)

    blk = pl.BlockSpec((None, T, d), lambda h, c: (h, nC - 1 - c, 0))
    out = jax.ShapeDtypeStruct((H, S, d), F32)
    return pl.pallas_call(
        body, name="ret_bwd", grid=(H, nC),
        in_specs=[blk, blk, blk, blk,
                  pl.BlockSpec((None, None, d, d), lambda h, c: (h, nC - 1 - c, 0, 0)),
                  pl.BlockSpec((None, T, T), lambda h, c: (h, 0, 0)),
                  pl.BlockSpec((None, T, 1), lambda h, c: (h, 0, 0)),
                  pl.BlockSpec((None, T, 1), lambda h, c: (h, 0, 0)),
                  pl.BlockSpec((None, 1, 1), lambda h, c: (h, 0, 0))],
        out_specs=[blk, blk, blk], out_shape=[out, out, out],
        scratch_shapes=[pltpu.VMEM((d, d), F32)],
        compiler_params=_params(("parallel", "arbitrary")),
    )(q, k, v, do, states, dec, dq_, dk_, ds_)


def _ret_tables(T, d):
    h = jnp.arange(RET_HEADS, dtype=F32)
    log_g = jnp.log1p(-jnp.exp2(-5.0 - h))
    idx = jnp.arange(T, dtype=F32)
    diff = idx[:, None] - idx[None, :]
    same = (jnp.arange(T)[:, None] // CHUNK) == (jnp.arange(T)[None, :] // CHUNK)
    earlier = (jnp.arange(T)[None, :] // CHUNK) < (jnp.arange(T)[:, None] // CHUNK)
    expo = jnp.where(same, jnp.abs(diff), diff)
    dec = jnp.where(same | earlier, jnp.exp(log_g[:, None, None] * expo[None]), 0.0)
    q_dec = jnp.exp(log_g[:, None] * (idx + 1.0))[..., None]
    k_dec = jnp.exp(log_g[:, None] * (T - 1.0 - idx))[..., None]
    s_dec = jnp.exp(log_g * T)[:, None, None]
    return dec.astype(F32), q_dec, k_dec, s_dec


NEG = -1e30


ROW_GROUP = 512


def _scores(q, kv, r, diagonal):
    s = _dot(q, kv, 1, 1)
    if diagonal:
        rg, T = s.shape
        qc = (r * rg + lax.broadcasted_iota(jnp.int32, (rg, T), 0)) // CHUNK
        kc = lax.broadcasted_iota(jnp.int32, (rg, T), 1) // CHUNK
        s = jnp.where(kc <= qc, s, NEG)
    return s


def _by_query_block(p, n):
    i = sum((p >= t * (t + 1) // 2).astype(jnp.int32) for t in range(1, n))
    return i, p - i * (i + 1) // 2


def _by_key_block(p, n):
    j = sum((p >= t * n - t * (t - 1) // 2).astype(jnp.int32) for t in range(1, n))
    return j, j + p - (j * n - j * (j - 1) // 2)


def _flash_fwd(q, k, v, side=None):
    H, S, dk = q.shape
    dv = v.shape[2]
    T = _tile(S, 512)
    n = S // T
    rg = min(ROW_GROUP, T)

    def body(q_ref, k_ref, v_ref, o_ref, lse_ref, m_s, l_s, acc):
        qi, ki = _by_query_block(pl.program_id(1), n)

        @pl.when(ki == 0)
        def _():
            m_s[...] = jnp.full_like(m_s, NEG)
            l_s[...] = jnp.zeros_like(l_s)
            acc[...] = jnp.zeros_like(acc)

        def step(diagonal):
            kv, vv = k_ref[...], v_ref[...]
            for r in range(T // rg):
                rows = pl.ds(r * rg, rg)
                s = _scores(q_ref[rows, :], kv, r, diagonal)
                m_old = m_s[rows, :]
                m_new = jnp.maximum(m_old, jnp.max(s, axis=1, keepdims=True))
                p = jnp.exp(s - m_new)
                alpha = jnp.exp(m_old - m_new)
                l_s[rows, :] = alpha * l_s[rows, :] + jnp.sum(p, axis=1, keepdims=True)
                acc[rows, :] = alpha * acc[rows, :] + _dot(p.astype(BF16), vv, 1, 0)
                m_s[rows, :] = m_new

        @pl.when(ki < qi)
        def _():
            step(False)

        @pl.when(ki == qi)
        def _():
            step(True)
            o_ref[...] = acc[...] / l_s[...]
            lse_ref[...] = m_s[...] + jnp.log(l_s[...])

    q_map = lambda h, p: (h, _by_query_block(p, n)[0], 0)
    kv_map = lambda h, p: (h, _by_query_block(p, n)[1], 0)
    return _hosted_call(
        body, side, name="mla_fwd", grid=(H, n * (n + 1) // 2),
        in_specs=[pl.BlockSpec((None, T, dk), q_map), pl.BlockSpec((None, T, dk), kv_map),
                  pl.BlockSpec((None, T, dv), kv_map)],
        out_specs=[pl.BlockSpec((None, T, dv), q_map), pl.BlockSpec((None, T, 1), q_map)],
        out_shape=[jax.ShapeDtypeStruct((H, S, dv), F32), jax.ShapeDtypeStruct((H, S, 1), F32)],
        scratch_shapes=[pltpu.VMEM((T, 1), F32), pltpu.VMEM((T, 1), F32), pltpu.VMEM((T, dv), F32)],
        operands=[q, k, v], semantics=("parallel", "arbitrary"))


def _flash_dq(q, k, v, do, lse, dlt, side=None):
    H, S, dk = q.shape
    dv = v.shape[2]
    T = _tile(S, 512)
    n = S // T
    rg = min(ROW_GROUP, T)

    def body(q_ref, k_ref, v_ref, do_ref, lse_ref, dlt_ref, dq_ref, acc):
        qi, ki = _by_query_block(pl.program_id(1), n)

        @pl.when(ki == 0)
        def _():
            acc[...] = jnp.zeros_like(acc)

        def step(diagonal):
            kv, vv = k_ref[...], v_ref[...]
            for r in range(T // rg):
                rows = pl.ds(r * rg, rg)
                p = jnp.exp(_scores(q_ref[rows, :], kv, r, diagonal) - lse_ref[rows, :])
                dp = _dot(do_ref[rows, :], vv, 1, 1)
                ds = (p * (dp - dlt_ref[rows, :])).astype(BF16)
                acc[rows, :] += _dot(ds, kv, 1, 0)

        @pl.when(ki < qi)
        def _():
            step(False)

        @pl.when(ki == qi)
        def _():
            step(True)
            dq_ref[...] = acc[...]

    q_map = lambda h, p: (h, _by_query_block(p, n)[0], 0)
    kv_map = lambda h, p: (h, _by_query_block(p, n)[1], 0)
    return _hosted_call(
        body, side, name="mla_dq", grid=(H, n * (n + 1) // 2),
        in_specs=[pl.BlockSpec((None, T, dk), q_map), pl.BlockSpec((None, T, dk), kv_map),
                  pl.BlockSpec((None, T, dv), kv_map), pl.BlockSpec((None, T, dv), q_map),
                  pl.BlockSpec((None, T, 1), q_map), pl.BlockSpec((None, T, 1), q_map)],
        out_specs=[pl.BlockSpec((None, T, dk), q_map)],
        out_shape=[jax.ShapeDtypeStruct((H, S, dk), F32)],
        scratch_shapes=[pltpu.VMEM((T, dk), F32)],
        operands=[q, k, v, do, lse, dlt], semantics=("parallel", "arbitrary"))


def _flash_dkv(q, k, v, do, lse, dlt, side=None):
    H, S, dk = q.shape
    dv = v.shape[2]
    T = _tile(S, 512)
    n = S // T
    rg = min(ROW_GROUP, T)

    def body(q_ref, k_ref, v_ref, do_ref, lse_ref, dlt_ref, dk_ref, dv_ref, acc_k, acc_v):
        ki, qi = _by_key_block(pl.program_id(1), n)

        @pl.when(qi == ki)
        def _():
            acc_k[...] = jnp.zeros_like(acc_k)
            acc_v[...] = jnp.zeros_like(acc_v)

        def step(diagonal):
            kv, vv = k_ref[...], v_ref[...]
            for r in range(T // rg):
                rows = pl.ds(r * rg, rg)
                qv, dov = q_ref[rows, :], do_ref[rows, :]
                p = jnp.exp(_scores(qv, kv, r, diagonal) - lse_ref[rows, :])
                acc_v[...] += _dot(p.astype(BF16), dov, 0, 0)
                dp = _dot(dov, vv, 1, 1)
                ds = (p * (dp - dlt_ref[rows, :])).astype(BF16)
                acc_k[...] += _dot(ds, qv, 0, 0)

        @pl.when(qi > ki)
        def _():
            step(False)

        @pl.when(qi == ki)
        def _():
            step(True)

        @pl.when(qi == n - 1)
        def _():
            dk_ref[...] = acc_k[...]
            dv_ref[...] = acc_v[...]

    q_map = lambda h, p: (h, _by_key_block(p, n)[1], 0)
    kv_map = lambda h, p: (h, _by_key_block(p, n)[0], 0)
    return _hosted_call(
        body, side, name="mla_dkv", grid=(H, n * (n + 1) // 2),
        in_specs=[pl.BlockSpec((None, T, dk), q_map), pl.BlockSpec((None, T, dk), kv_map),
                  pl.BlockSpec((None, T, dv), kv_map), pl.BlockSpec((None, T, dv), q_map),
                  pl.BlockSpec((None, T, 1), q_map), pl.BlockSpec((None, T, 1), q_map)],
        out_specs=[pl.BlockSpec((None, T, dk), kv_map), pl.BlockSpec((None, T, dv), kv_map)],
        out_shape=[jax.ShapeDtypeStruct((H, S, dk), F32), jax.ShapeDtypeStruct((H, S, dv), F32)],
        scratch_shapes=[pltpu.VMEM((T, dk), F32), pltpu.VMEM((T, dv), F32)],
        operands=[q, k, v, do, lse, dlt], semantics=("parallel", "arbitrary"))


def _shift_down(cur, prev, by):
    rows = lax.broadcasted_iota(jnp.int32, cur.shape, 0)
    return jnp.where(rows < by, pltpu.roll(prev, by, 0), pltpu.roll(cur, by, 0))


def _shift_up(cur, nxt, by):
    ts = cur.shape[0]
    rows = lax.broadcasted_iota(jnp.int32, cur.shape, 0)
    return jnp.where(rows >= ts - by, pltpu.roll(nxt, ts - by, 0), pltpu.roll(cur, ts - by, 0))


def _ffn_tiles(S, F):
    return _tile(S, 256, 8), _tile(F, 1024)


def _ffn_fwd(G, U, cw, cb):
    S, F = G.shape
    ts, tc = _ffn_tiles(S, F)

    def body(g_ref, gp_ref, u_ref, cw_ref, cb_ref, a_ref, act_ref):
        cur = g_ref[...]
        prev = gp_ref[...] * (pl.program_id(1) > 0).astype(F32)
        a = (cb_ref[...] + cw_ref[0:1, :] * _shift_down(cur, prev, 2) + cw_ref[1:2, :] * _shift_down(cur, prev, 1)
             + cw_ref[2:3, :] * cur)
        a_ref[...] = a
        act_ref[...] = (a * _sigmoid(a) * u_ref[...]).astype(BF16)

    cur_spec = pl.BlockSpec((ts, tc), lambda j, i: (i, j))
    return pl.pallas_call(
        body, name="ffn_act_fwd", grid=(F // tc, S // ts),
        in_specs=[cur_spec, pl.BlockSpec((ts, tc), lambda j, i: (jnp.maximum(i - 1, 0), j)), cur_spec,
                  pl.BlockSpec((CONV_WIDTH, tc), lambda j, i: (0, j)), pl.BlockSpec((1, tc), lambda j, i: (0, j))],
        out_specs=[cur_spec, cur_spec],
        out_shape=[jax.ShapeDtypeStruct((S, F), F32), jax.ShapeDtypeStruct((S, F), BF16)],
        compiler_params=_params(("parallel", "parallel")),
    )(G, G, U, cw, cb)


def _ffn_bwd_act(a, U, dact):
    S, F = a.shape
    ts, tc = _ffn_tiles(S, F)

    def body(a_ref, u_ref, d_ref, da_ref, du_ref, db_ref):
        av, dv = a_ref[...], d_ref[...]
        sg = _sigmoid(av)
        du_ref[...] = (dv * av * sg).astype(BF16)
        da = dv * u_ref[...] * sg * (1.0 + av * (1.0 - sg))
        da_ref[...] = da

        @pl.when(pl.program_id(1) == 0)
        def _():
            db_ref[...] = jnp.zeros_like(db_ref)

        db_ref[...] += _colsum(da)

    cur_spec = pl.BlockSpec((ts, tc), lambda j, i: (i, j))
    return pl.pallas_call(
        body, name="ffn_act_bwd", grid=(F // tc, S // ts),
        in_specs=[cur_spec, cur_spec, cur_spec],
        out_specs=[cur_spec, cur_spec, pl.BlockSpec((1, tc), lambda j, i: (0, j))],
        out_shape=[jax.ShapeDtypeStruct((S, F), F32), jax.ShapeDtypeStruct((S, F), BF16),
                   jax.ShapeDtypeStruct((1, F), F32)],
        compiler_params=_params(("parallel", "arbitrary")),
    )(a, U, dact)


def _ffn_bwd_conv(da, G, cw):
    S, F = da.shape
    ts, tc = _ffn_tiles(S, F)
    n = S // ts

    def body(d_ref, dn_ref, g_ref, gp_ref, cw_ref, dg_ref, dw_ref):
        i = pl.program_id(1)
        dcur = d_ref[...]
        dnxt = dn_ref[...] * (i < n - 1).astype(F32)
        cur = g_ref[...]
        prev = gp_ref[...] * (i > 0).astype(F32)
        dg = (cw_ref[2:3, :] * dcur + cw_ref[1:2, :] * _shift_up(dcur, dnxt, 1)
              + cw_ref[0:1, :] * _shift_up(dcur, dnxt, 2))
        dg_ref[...] = dg.astype(BF16)

        @pl.when(i == 0)
        def _():
            dw_ref[...] = jnp.zeros_like(dw_ref)

        dw_ref[0:1, :] += _colsum(dcur * _shift_down(cur, prev, 2))
        dw_ref[1:2, :] += _colsum(dcur * _shift_down(cur, prev, 1))
        dw_ref[2:3, :] += _colsum(dcur * cur)

    cur_spec = pl.BlockSpec((ts, tc), lambda j, i: (i, j))
    return pl.pallas_call(
        body, name="ffn_conv_bwd", grid=(F // tc, n),
        in_specs=[cur_spec, pl.BlockSpec((ts, tc), lambda j, i: (jnp.minimum(i + 1, n - 1), j)),
                  cur_spec, pl.BlockSpec((ts, tc), lambda j, i: (jnp.maximum(i - 1, 0), j)),
                  pl.BlockSpec((CONV_WIDTH, tc), lambda j, i: (0, j))],
        out_specs=[cur_spec, pl.BlockSpec((CONV_WIDTH, tc), lambda j, i: (0, j))],
        out_shape=[jax.ShapeDtypeStruct((S, F), BF16), jax.ShapeDtypeStruct((CONV_WIDTH, F), F32)],
        compiler_params=_params(("parallel", "arbitrary")),
    )(da, da, G, G, cw)


class _Cfg:
    def __init__(self, S, D):
        self.S, self.D = S, D
        self.RD = D // (2 * RET_HEADS)
        self.RW = RET_HEADS * self.RD
        self.MV = (D - self.RW) // MLA_HEADS
        self.QL, self.KVL = D // 4, D // 8
        self.F = ((8 * D // 3 + 255) // 256) * 256
        self.IN = 4 * self.RW + self.QL + self.KVL + MLA_ROPE
        self.INs = self.IN // N_CHIP
        self.INp = _round_up(self.INs, LANE)
        self.Fs = self.F // N_CHIP
        self.Fp = _round_up(self.Fs, LANE)
        self.F4 = N_CHIP * self.Fp
        self.QK = MLA_NOPE + MLA_ROPE
        self.KVH = MLA_NOPE + self.MV


def _head_major(t, H, d):
    S = t.shape[0]
    return t.reshape(S, H, d).transpose(1, 0, 2).reshape(H * S, d)


def _seq_major(t, H, d):
    S = t.shape[0] // H
    return t.reshape(H, S, d).transpose(1, 0, 2).reshape(S, H * d)


def _rope_tables(cfg):
    S = cfg.S

    def cs(dim):
        inv = 1.0 / (ROPE_BASE ** (jnp.arange(0, dim, 2, dtype=F32) / dim))
        ang = jnp.arange(S, dtype=F32)[:, None] * inv[None, :]
        return jnp.cos(ang), jnp.sin(ang)

    c, s = cs(cfg.RD)
    z = jnp.zeros_like(s)
    ret = (jnp.concatenate([c, c], 1), jnp.concatenate([-s, z], 1), jnp.concatenate([z, s], 1))
    c, s = cs(MLA_ROPE)
    tail1 = jnp.ones((S, MLA_QK_PAD - cfg.QK), F32)
    tail0 = jnp.zeros((S, MLA_QK_PAD - cfg.QK), F32)
    z = jnp.zeros_like(s)
    mla = (jnp.concatenate([c, c, tail1], 1), jnp.concatenate([-s, z, tail0], 1), jnp.concatenate([z, s, tail0], 1))
    return ret, mla


GATHER_FIRST = ('w_in',)
GATHER_BEHIND_IN_PROJ = ('w_uq', 'w_ukv', 'w_o')
GATHER_BEHIND_MLA = ('w_ffn_gate', 'w_ffn_up')
GATHER_BEHIND_FFN_GATE = ('w_ffn_down',)
GATHER_BEHIND_FFN_UP = ('w_ple_gate', 'w_ple_proj')
PAIR_BEHIND_FFN_DX = ('w_ple_proj', 'w_ple_gate', 'w_ffn_down', 'w_ffn_gate', 'w_ffn_up')
PAIR_AFTER_OUT_PROJ = ('w_o',)
REDUCE_BEHIND_DQ = ('w_ffn_gate', 'w_ple_gate', 'w_o')
REDUCE_BEHIND_DKV = ('w_ffn_up', 'w_ffn_down', 'w_ple_proj')
REDUCE_LAST = ('w_uq', 'w_ukv', 'w_in')


def _local_step(cfg, x, p, tgt, W, late, sp):
    W = dict(W)
    S, D, RD, RW, MV = cfg.S, cfg.D, cfg.RD, cfg.RW, cfg.MV
    H, MH = RET_HEADS, MLA_HEADS
    (rc, rsa, rsb), (mc, msa, msb) = _rope_tables(cfg)
    dec, q_dec, k_dec, s_dec = _ret_tables(min(RET_BLOCK, S), RD)
    k_scale = RD ** -0.5
    a_scale = cfg.QK ** -0.5
    gb, gs = {}, {}

    hn1, r1 = _rms_fwd(x, sp['g_attn'], name="rms_attn")
    proj, got = _mm_nn(hn1, W['w_in'], name="in_proj",
                       side=_gather_side([late[n] for n in GATHER_BEHIND_IN_PROJ]))
    W.update({n: _as_operand(n, g) for n, g in zip(GATHER_BEHIND_IN_PROJ, got)})
    proj = proj.reshape(S, N_CHIP, cfg.INp)[:, :, :cfg.INs].reshape(S, cfg.IN)
    cuts = [RW, 2 * RW, 3 * RW, 4 * RW, 4 * RW + cfg.QL, 4 * RW + cfg.QL + cfg.KVL]
    rq, rk, rv, rg, cq, ckv, kr = jnp.split(proj, cuts, axis=1)

    rq_h, rk_h, rv_h, rg_h = (_head_major(t, H, RD) for t in (rq, rk, rv, rg))
    q_r = _rope(rq_h, rc, rsa, rsb, RD // 2, 1.0, BF16, name="ret_rope_q").reshape(H, S, RD)
    k_r = _rope(rk_h, rc, rsa, rsb, RD // 2, k_scale, BF16, name="ret_rope_k").reshape(H, S, RD)
    v_r = rv_h.astype(BF16).reshape(H, S, RD)
    o_h, states = _ret_fwd(q_r, k_r, v_r, dec, q_dec, k_dec, s_dec)
    o_h = o_h.reshape(H * S, RD)

    def gate_fn(ov, gv):
        oc = ov - _mean(ov)
        ron = oc * lax.rsqrt(_mean(oc * oc) + EPS)
        return [gv * _sigmoid(gv) * ron]
    ro_h = _rowwise(gate_fn, [(o_h, 'row'), (rg_h, 'row')], [(RD, BF16)], name="ret_gate")[0]
    ro = _seq_major(ro_h, H, RD)

    cqn, rcq = _rms_fwd(cq, sp['g_q_lora'], name="rms_q")
    ckvn, rckv = _rms_fwd(ckv, sp['g_kv_lora'], name="rms_kv")
    Q = _mm_nn(cqn, W['w_uq'], name="q_up")
    KV = _mm_nn(ckvn, W['w_ukv'], name="kv_up")
    pad = jnp.zeros((MH, S, MLA_QK_PAD - cfg.QK), F32)
    q_raw = jnp.concatenate([Q.reshape(S, MH, cfg.QK).transpose(1, 0, 2), pad], 2).reshape(MH * S, MLA_QK_PAD)
    KV3 = KV.reshape(S, MH, cfg.KVH).transpose(1, 0, 2)
    k_raw = jnp.concatenate([KV3[:, :, :MLA_NOPE], jnp.broadcast_to(kr[None], (MH, S, MLA_ROPE)), pad], 2)
    k_raw = k_raw.reshape(MH * S, MLA_QK_PAD)
    v_m = KV3[:, :, MLA_NOPE:].astype(BF16)
    q_m = _rope(q_raw, mc, msa, msb, MLA_ROPE // 2, a_scale, BF16, name="mla_rope_q").reshape(MH, S, MLA_QK_PAD)
    k_m = _rope(k_raw, mc, msa, msb, MLA_ROPE // 2, 1.0, BF16, name="mla_rope_k").reshape(MH, S, MLA_QK_PAD)
    (mo_h, lse), got = _flash_fwd(q_m, k_m, v_m, side=_gather_side([late[n] for n in GATHER_BEHIND_MLA]))
    W.update({n: _as_operand(n, g) for n, g in zip(GATHER_BEHIND_MLA, got)})
    mo = _seq_major(mo_h.reshape(MH * S, MV), MH, MV)

    cat = jnp.concatenate([ro, mo.astype(BF16)], axis=1)
    h1 = _mm_nn(cat, W['w_o'], name="out_proj", res=x)

    hn2, r2 = _rms_fwd(h1, sp['g_ffn'], name="rms_ffn")
    G, got = _mm_nn(hn2, W['w_ffn_gate'], name="ffn_gate",
                    side=_gather_side([late[n] for n in GATHER_BEHIND_FFN_GATE]))
    W.update({n: _as_operand(n, g) for n, g in zip(GATHER_BEHIND_FFN_GATE, got)})
    U, got = _mm_nn(hn2, W['w_ffn_up'], name="ffn_up", side=_gather_side([late[n] for n in GATHER_BEHIND_FFN_UP]))
    W.update({n: _as_operand(n, g) for n, g in zip(GATHER_BEHIND_FFN_UP, got)})
    a, act = _ffn_fwd(G, U, sp['conv_w'], sp['conv_b'])
    h2 = _mm_nn(act, W['w_ffn_down'], name="ffn_down", res=h1)

    hn3, r3 = _rms_fwd(h2, sp['g_ple'], name="rms_ple")
    Z = _mm_nn(hn3, W['w_ple_gate'], name="ple_gate")
    p_b = p.astype(BF16)
    PP = _mm_nn(p_b, W['w_ple_proj'], name="ple_proj")

    def head_fn(h2v, zv, ppv, tv, gv):
        gate = _sigmoid(zv)
        h3 = h2v + gate * ppv
        r4 = lax.rsqrt(_mean(h3 * h3) + EPS)
        n4 = h3 * r4
        e = n4 * gv - tv
        dy = e * (1.0 / D)
        dn = dy * gv
        dh3 = r4 * (dn - n4 * _mean(dn * n4))
        dpp = dh3 * gate
        dz = dh3 * ppv * gate * (1.0 - gate)
        loss = jnp.sum(0.5 * _mean(e * e), axis=0, keepdims=True)
        return [dh3, dz, dpp, _colsum(dy * n4), jnp.broadcast_to(loss, (1, LANE))]
    dh3, dZ, dPP, dgf, loss = _rowwise(
        head_fn, [(h2, 'row'), (Z, 'row'), (PP, 'row'), (tgt, 'row'), (sp['g_final'], 'bcast')],
        [(D, F32), (D, BF16), (D, BF16)], [D, LANE], name="ple_loss_head")
    gs['g_final'] = dgf
    loss = loss[0, 0]

    gb['w_ple_proj'] = _mm_tn(p_b, dPP, N_CHIP, name="ple_proj_dw")
    gb['w_ple_gate'] = _mm_tn(hn3, dZ, 1, name="ple_gate_dw")
    dhn3 = _mm_nt(dZ, W['w_ple_gate'], name="ple_gate_dx")
    dh2, dh2_b, gs['g_ple'] = _rms_bwd(h2, r3, dhn3, sp['g_ple'], dh3, name="rms_ple_bwd")

    dact = _mm_nt(dh2_b, W['w_ffn_down'], name="ffn_down_dx")
    gb['w_ffn_down'] = _mm_tn(act, dh2_b, 1, name="ffn_down_dw")
    da, dU, gs['conv_b'] = _ffn_bwd_act(a, U, dact)
    dG, gs['conv_w'] = _ffn_bwd_conv(da, G, sp['conv_w'])
    gb['w_ffn_gate'] = _mm_tn(hn2, dG, N_CHIP, name="ffn_gate_dw")
    gb['w_ffn_up'] = _mm_tn(hn2, dU, N_CHIP, name="ffn_up_dw")
    def pair_sums(names, from_sibling):
        return {n: _pair_add(_grad_pieces(n, gb[n][0]), r) for n, r in zip(names, from_sibling)}

    def wire(names):
        return [_grad_pieces(n, gb[n][1]) for n in names]

    dhn2, got = _mm_nt(dG, W['w_ffn_gate'], name="ffn_gate_dx", side=_pair_side(wire(PAIR_BEHIND_FFN_DX)))
    pair = pair_sums(PAIR_BEHIND_FFN_DX, got)
    dhn2 = _mm_nt(dU, W['w_ffn_up'], name="ffn_up_dx", res=dhn2)
    dh1, dh1_b, gs['g_ffn'] = _rms_bwd(h1, r2, dhn2, sp['g_ffn'], dh2, name="rms_ffn_bwd")

    dcat = _mm_nt(dh1_b, W['w_o'], name="out_proj_dx")
    gb['w_o'] = _mm_tn(cat, dh1_b, 1, name="out_proj_dw")
    pair.update(pair_sums(PAIR_AFTER_OUT_PROJ, _pair_exchange(wire(PAIR_AFTER_OUT_PROJ))))
    from_chips = {}
    dro_h = _head_major(dcat[:, :RW], H, RD)

    def gate_bwd_fn(ov, gv, dv):
        oc = ov - _mean(ov)
        rs = lax.rsqrt(_mean(oc * oc) + EPS)
        ron = oc * rs
        sg = _sigmoid(gv)
        dron = dv * gv * sg
        drg = dv * ron * sg * (1.0 + gv * (1.0 - sg))
        do = rs * (dron - _mean(dron) - ron * _mean(dron * ron))
        return [do, drg]
    do_h, drg_h = _rowwise(gate_bwd_fn, [(o_h, 'row'), (rg_h, 'row'), (dro_h, 'row')],
                           [(RD, BF16), (RD, F32)], name="ret_gate_bwd")
    gq_r, gk_r, gv_r = _ret_bwd(q_r, k_r, v_r, do_h.reshape(H, S, RD), states, dec, q_dec, k_dec, s_dec)
    drq_h = _rope(gq_r.reshape(H * S, RD), rc, -rsa, -rsb, RD // 2, 1.0, F32, name="ret_rope_q_bwd")
    drk_h = _rope(gk_r.reshape(H * S, RD), rc, -rsa, -rsb, RD // 2, k_scale, F32, name="ret_rope_k_bwd")
    drq, drk, drv, drg = (_seq_major(t, H, RD) for t in (drq_h, drk_h, gv_r.reshape(H * S, RD), drg_h))

    dmo_h = _head_major(dcat[:, RW:], MH, MV)
    dlt = _rowwise(lambda ov, dv: [jnp.sum(ov * dv, axis=1, keepdims=True)],
                   [(mo_h.reshape(MH * S, MV), 'row'), (dmo_h, 'row')], [(1, F32)], name="mla_delta")[0]
    dmo_b = dmo_h.astype(BF16).reshape(MH, S, MV)
    dlt = dlt.reshape(MH, S, 1)
    (gq_m,), got = _flash_dq(q_m, k_m, v_m, dmo_b, lse, dlt,
                             side=_exchange_side([pair[n][1] for n in REDUCE_BEHIND_DQ]))
    from_chips.update(zip(REDUCE_BEHIND_DQ, got))
    (gk_m, gv_m), got = _flash_dkv(q_m, k_m, v_m, dmo_b, lse, dlt,
                                   side=_exchange_side([pair[n][1] for n in REDUCE_BEHIND_DKV]))
    from_chips.update(zip(REDUCE_BEHIND_DKV, got))
    dq_raw = _rope(gq_m.reshape(MH * S, MLA_QK_PAD), mc, -msa, -msb, MLA_ROPE // 2, a_scale, F32,
                   name="mla_rope_q_bwd")
    dk_raw = _rope(gk_m.reshape(MH * S, MLA_QK_PAD), mc, -msa, -msb, MLA_ROPE // 2, 1.0, F32, name="mla_rope_k_bwd")
    dQ = dq_raw.reshape(MH, S, MLA_QK_PAD)[:, :, :cfg.QK].transpose(1, 0, 2).reshape(S, MH * cfg.QK).astype(BF16)
    dk3 = dk_raw.reshape(MH, S, MLA_QK_PAD)
    dKV = jnp.concatenate([dk3[:, :, :MLA_NOPE], gv_m], 2).transpose(1, 0, 2).reshape(S, MH * cfg.KVH).astype(BF16)
    dkr_heads = [(dk3[h, :, MLA_NOPE:cfg.QK], 'row') for h in range(MH)]
    dkr = _rowwise(lambda *v: [functools.reduce(lambda s, t: s + t, v)], dkr_heads, [(MLA_ROPE, F32)],
                   name="mla_rope_k_heads")[0]

    gb['w_uq'] = _mm_tn(cqn, dQ, N_CHIP, name="q_up_dw")
    dcqn = _mm_nt(dQ, W['w_uq'], name="q_up_dx")
    dcq, _, gs['g_q_lora'] = _rms_bwd(cq, rcq, dcqn, sp['g_q_lora'], None, name="rms_q_bwd")
    gb['w_ukv'] = _mm_tn(ckvn, dKV, N_CHIP, name="kv_up_dw")
    dckvn = _mm_nt(dKV, W['w_ukv'], name="kv_up_dx")
    dckv, _, gs['g_kv_lora'] = _rms_bwd(ckv, rckv, dckvn, sp['g_kv_lora'], None, name="rms_kv_bwd")

    dproj = jnp.concatenate([drq, drk, drv, drg, dcq, dckv, dkr], axis=1).astype(BF16)
    dproj = jnp.pad(dproj.reshape(S, N_CHIP, cfg.INs), ((0, 0), (0, 0), (0, cfg.INp - cfg.INs)))
    dproj = dproj.reshape(S, N_CHIP * cfg.INp)
    gb['w_in'] = _mm_tn(hn1, dproj, N_CHIP, name="in_proj_dw")
    pair.update(pair_sums(REDUCE_LAST, _pair_exchange(wire(REDUCE_LAST))))
    dhn1, got = _mm_nt(dproj, W['w_in'], name="in_proj_dx", side=_exchange_side([pair[n][1] for n in REDUCE_LAST]))
    from_chips.update(zip(REDUCE_LAST, got))
    dx, _, gs['g_attn'] = _rms_bwd(x, r1, dhn1, sp['g_attn'], dh1, name="rms_attn_bwd")
    return loss, dx, {n: (pair[n][0], from_chips[n]) for n in BIG}, gs


def _padded_shape(name, shape):
    K, n = shape
    if name in COL_SHARDED:
        return K, _round_up(n, LANE)
    return _round_up(K, LANE), n


def _pad_shard(name, w):
    K, n = _padded_shape(name, w.shape)
    return jnp.pad(w.astype(BF16), ((0, K - w.shape[0]), (0, n - w.shape[1])))


def _as_operand(name, gathered):
    if name in COL_SHARDED:
        return gathered
    return gathered.reshape(1, gathered.shape[0] * gathered.shape[1], gathered.shape[2])


def _grad_pieces(name, g):
    if name in COL_SHARDED:
        return g
    return g.reshape(N_CHIP, g.shape[1] // N_CHIP, g.shape[2])


def _channels_padded(v, cfg):
    r = v.shape[0]
    return jnp.pad(v.reshape(r, N_CHIP, cfg.Fs), ((0, 0), (0, 0), (0, cfg.Fp - cfg.Fs))).reshape(r, cfg.F4)


def _channels_unpadded(v, cfg):
    r = v.shape[0]
    return v.reshape(r, N_CHIP, cfg.Fp)[:, :, :cfg.Fs].reshape(r, cfg.F)


def _mesh_pos():
    return lax.axis_index("x"), lax.axis_index("y"), lax.axis_index("c")


def _other_chips(x, y):
    return [(1 - x, y), (x, 1 - y), (1 - x, 1 - y)]


def _chip_id(cx, cy):
    return 2 * cx + cy


def _half(ref_rows, core):
    half = ref_rows // 2
    return pl.ds(core * half, half)


def _gather_side(shards):
    n = len(shards)

    def copies(srcs, outs, sems):
        ici_send, ici_recv, d2d_send, d2d_recv, own_send, own_recv = sems
        x, y, c = _mesh_pos()
        mine = _chip_id(x, y)
        others = _other_chips(x, y)

        def over_ici(w, j, chip):
            rows = _half(srcs[w].shape[0], c)
            return pltpu.make_async_remote_copy(
                src_ref=srcs[w].at[rows], dst_ref=outs[w].at[chip, rows],
                send_sem=ici_send.at[w, j], recv_sem=ici_recv.at[w, j],
                device_id=(*others[j], c), device_id_type=MESH)

        def over_d2d(w, j, core):
            rows = _half(srcs[w].shape[0], core)
            slab = outs[w].at[_chip_id(*others[j]), rows]
            return pltpu.make_async_remote_copy(
                src_ref=slab, dst_ref=slab, send_sem=d2d_send.at[w, j], recv_sem=d2d_recv.at[w, j],
                device_id=(x, y, 1 - c), device_id_type=MESH)

        def own(w):
            return pltpu.make_async_remote_copy(
                src_ref=srcs[w], dst_ref=outs[w].at[mine], send_sem=own_send.at[w], recv_sem=own_recv.at[w],
                device_id=(x, y, 1 - c), device_id_type=MESH)

        return c, mine, others, over_ici, over_d2d, own

    def start(srcs, outs, sems):
        c, mine, others, over_ici, over_d2d, own = copies(srcs, outs, sems)
        for w in range(n):
            for j in range(3):
                over_ici(w, j, mine).start()
        for w in range(n):
            own(w).start()

    def finish(srcs, outs, sems):
        c, mine, others, over_ici, over_d2d, own = copies(srcs, outs, sems)
        for w in range(n):
            for j in range(3):
                over_ici(w, j, _chip_id(*others[j])).wait_recv()
                over_d2d(w, j, c).start()
        for w in range(n):
            for j in range(3):
                over_d2d(w, j, 1 - c).wait_recv()
        for w in range(n):
            own(w).wait()
            for j in range(3):
                over_ici(w, j, mine).wait_send()
                over_d2d(w, j, c).wait_send()

    out_shape = [jax.ShapeDtypeStruct((N_CHIP,) + s.shape, s.dtype) for s in shards]
    sems = [pltpu.SemaphoreType.DMA((n, 3))] * 4 + [pltpu.SemaphoreType.DMA((n,))] * 2
    return _Side(shards, out_shape, sems, start, finish)


def _gather_weights(shards):
    return _run_side(_gather_side(shards), "gather_weights")


def _pair_side(grads):
    n = len(grads)

    def copies(srcs, outs, sems):
        send, recv = sems
        x, y, c = _mesh_pos()
        return [pltpu.make_async_remote_copy(
            src_ref=srcs[w].at[:, _half(srcs[w].shape[1], 1 - c), :], dst_ref=outs[w],
            send_sem=send.at[w], recv_sem=recv.at[w],
            device_id=(x, y, 1 - c), device_id_type=MESH) for w in range(n)]

    def start(srcs, outs, sems):
        for cp in copies(srcs, outs, sems):
            cp.start()

    def finish(srcs, outs, sems):
        for cp in copies(srcs, outs, sems):
            cp.wait()

    out_shape = [jax.ShapeDtypeStruct((g.shape[0], g.shape[1] // 2, g.shape[2]), g.dtype) for g in grads]
    return _Side(grads, out_shape, [pltpu.SemaphoreType.DMA((n,))] * 2, start, finish)


def _pair_exchange(grads):
    return _run_side(_pair_side(grads), "grad_pair_exchange")


def _exchange_side(parts):
    n = len(parts)

    def copies(srcs, outs, sems):
        send, recv = sems
        x, y, c = _mesh_pos()
        others = _other_chips(x, y)
        return [pltpu.make_async_remote_copy(
            src_ref=srcs[w].at[_chip_id(*others[j])], dst_ref=outs[w].at[j],
            send_sem=send.at[w, j], recv_sem=recv.at[w, j],
            device_id=(*others[j], c), device_id_type=MESH) for w in range(n) for j in range(3)]

    def start(srcs, outs, sems):
        for cp in copies(srcs, outs, sems):
            cp.start()

    def finish(srcs, outs, sems):
        for cp in copies(srcs, outs, sems):
            cp.wait()

    out_shape = [jax.ShapeDtypeStruct((3,) + p.shape[1:], p.dtype) for p in parts]
    return _Side(parts, out_shape, [pltpu.SemaphoreType.DMA((n, 3))] * 2, start, finish)


def _sibling_share(shards):
    n = len(shards)

    def body(*refs):
        outs = refs[n:2 * n]
        send, recv = refs[2 * n:]
        x, y, c = _mesh_pos()

        def half_of(w, core):
            rows = outs[w].at[_half(outs[w].shape[0], core)]
            return pltpu.make_async_remote_copy(
                src_ref=rows, dst_ref=rows, send_sem=send.at[w], recv_sem=recv.at[w],
                device_id=(x, y, 1 - c), device_id_type=MESH)

        for w in range(n):
            half_of(w, c).start()
        for w in range(n):
            half_of(w, 1 - c).wait_recv()
        for w in range(n):
            half_of(w, c).wait_send()

    return pl.pallas_call(
        body, name="grad_sibling_share", in_specs=[HBM] * n, out_specs=[HBM] * n,
        out_shape=[jax.ShapeDtypeStruct(s.shape, s.dtype) for s in shards],
        scratch_shapes=[pltpu.SemaphoreType.DMA((n,))] * 2, input_output_aliases={w: w for w in range(n)},
    )(*shards)


N_DEV = 8


def _gather_small(v):
    r, width = v.shape

    def body(v_ref, out_ref, send_sems, recv_sems, local_sem):
        x, y, c = _mesh_pos()
        me, sibling = (x, y, c), (x, y, 1 - c)
        chips = _other_chips(x, y)

        def rows(px, py, pc):
            return out_ref.at[pl.ds((4 * px + 2 * py + pc) * r, r), :]

        def copy(k, block, to, src=None):
            return pltpu.make_async_remote_copy(
                src_ref=rows(*block) if src is None else src, dst_ref=rows(*block),
                send_sem=send_sems.at[k], recv_sem=recv_sems.at[k], device_id=to, device_id_type=MESH)

        mine = pltpu.make_async_copy(v_ref, rows(*me), local_sem)
        mine.start()
        first = [copy(0, me, sibling, src=v_ref)]
        first += [copy(1 + j, me, (*chip, c), src=v_ref) for j, chip in enumerate(chips)]
        for cp in first:
            cp.start()
        passed = [copy(4 + j, (*chip, c), sibling) for j, chip in enumerate(chips)]
        for j, chip in enumerate(chips):
            copy(1 + j, (*chip, c), me).wait_recv()
            passed[j].start()
        copy(0, sibling, me).wait_recv()
        for j, chip in enumerate(chips):
            copy(4 + j, (*chip, 1 - c), me).wait_recv()
        for cp in first + passed:
            cp.wait_send()
        mine.wait()

    vmem = pl.BlockSpec(memory_space=pltpu.VMEM)
    return pl.pallas_call(
        body, name="gather_small", out_shape=jax.ShapeDtypeStruct((N_DEV * r, width), v.dtype),
        in_specs=[vmem], out_specs=vmem,
        scratch_shapes=[pltpu.SemaphoreType.DMA((7,)), pltpu.SemaphoreType.DMA((7,)), pltpu.SemaphoreType.DMA],
    )(v)


def _pack(arrays):
    flat = jnp.concatenate([a.reshape(-1) for a in arrays])
    size = _round_up(flat.shape[0], 8 * LANE)
    return jnp.pad(flat, (0, size - flat.shape[0])).reshape(size // LANE, LANE)


def _unpack(packed, shapes):
    flat = packed.reshape(-1)
    out, at = [], 0
    for s in shapes:
        size = math.prod(s)
        out.append(flat[at:at + size].reshape(s))
        at += size
    return out


def _sum_devices(gathered):
    r = gathered.shape[0] // N_DEV
    blocks = [(gathered[d * r:(d + 1) * r], 'row') for d in range(N_DEV)]
    return _rowwise(lambda *v: [functools.reduce(lambda s, t: s + t, v)], blocks, [(LANE, F32)],
                    name="small_grad_sum")[0]


def _reduce_tiles(half, n):
    return _tile(half, 128, 8)


def _pair_add(g32, r1):
    G, K, n = g32.shape
    half = K // 2
    tr = _reduce_tiles(half, n)
    nrt = half // tr

    def body(g_ref, r_ref, s32_ref, sb_ref):
        s = g_ref[...] + r_ref[...].astype(F32)
        s32_ref[...] = s
        sb_ref[...] = s.astype(BF16)

    blk = pl.BlockSpec((None, tr, n), lambda k, i: (k, i, 0))
    return pl.pallas_call(
        body, name="grad_pair_add", grid=(G, nrt),
        in_specs=[pl.BlockSpec((None, tr, n), lambda k, i: (k, lax.axis_index("c") * nrt + i, 0)), blk],
        out_specs=[blk, blk],
        out_shape=[jax.ShapeDtypeStruct((G, half, n), F32), jax.ShapeDtypeStruct((G, half, n), BF16)],
        compiler_params=_params(("parallel", "parallel")),
    )(g32, r1)


def _chip_add(s32, r2):
    _, half, n = s32.shape
    tr = _reduce_tiles(half, n)
    nrt = half // tr

    def body(s_ref, a_ref, b_ref, c_ref, o_ref):
        o_ref[...] = ((s_ref[...] + a_ref[...].astype(F32)) + b_ref[...].astype(F32)) + c_ref[...].astype(F32)

    def piece(j):
        return pl.BlockSpec((None, tr, n), lambda i, j=j: (j, i, 0))

    def mine(i):
        return _chip_id(lax.axis_index("x"), lax.axis_index("y")), i, 0

    return pl.pallas_call(
        body, name="grad_chip_add", grid=(nrt,),
        in_specs=[pl.BlockSpec((None, tr, n), mine), piece(0), piece(1), piece(2)],
        out_specs=pl.BlockSpec((tr, n), lambda i: (lax.axis_index("c") * nrt + i, 0)),
        out_shape=jax.ShapeDtypeStruct((2 * half, n), F32),
        compiler_params=_params(("parallel",)),
    )(s32, r2, r2, r2)


def _adamw_math(wv, gv, mv, vv):
    m2 = ADAM_B1 * mv + (1.0 - ADAM_B1) * gv
    v2 = ADAM_B2 * vv + (1.0 - ADAM_B2) * (gv * gv)
    m_hat = m2 / (1.0 - ADAM_B1 ** ADAM_STEP)
    v_hat = v2 / (1.0 - ADAM_B2 ** ADAM_STEP)
    delta = -ADAM_LR * (m_hat / (jnp.sqrt(v_hat) + ADAM_EPS) + ADAM_WD * wv)
    return [delta, m2, v2]


def _adamw(w, g, m, v, *, name):
    width = w.shape[1]
    return _rowwise(_adamw_math, [(w, 'row'), (g, 'row'), (m, 'row'), (v, 'row')], [(width, F32)] * 3, name=name)


def _adamw_sharded(w, g_padded, m, v, *, name):
    _, K, n = w.shape
    n_pad = g_padded.shape[1]
    ts = 8
    while ts * 2 <= 256 and ts * 2 * 8 * n_pad * 4 <= ROWWISE_BLOCK_BYTES and K % (ts * 2) == 0:
        ts *= 2

    def body(w_ref, g_ref, m_ref, v_ref, go_ref, d_ref, mo_ref, vo_ref):
        gv = g_ref[:, :n] if n != n_pad else g_ref[...]
        go_ref[...] = gv
        d_ref[...], mo_ref[...], vo_ref[...] = _adamw_math(w_ref[...], gv, m_ref[...], v_ref[...])

    blk = pl.BlockSpec((None, ts, n), lambda i: (0, i, 0))
    out = jax.ShapeDtypeStruct((1, K, n), F32)
    return pl.pallas_call(
        body, name=name, grid=(K // ts,),
        in_specs=[blk, pl.BlockSpec((ts, n_pad), lambda i: (i, 0)), blk, blk],
        out_specs=[blk] * 4, out_shape=[out] * 4,
        compiler_params=_params(("parallel",)),
    )(w, g_padded, m, v)


def kernel(x, p, w_in, g_attn, g_q_lora, g_kv_lora, w_uq, w_ukv, w_o, g_ffn, w_ffn_gate, w_ffn_up, conv_w, conv_b, w_ffn_down, g_ple, w_ple_gate, w_ple_proj, g_final, loss_target, m_w_in, m_g_attn, m_g_q_lora, m_g_kv_lora, m_w_uq, m_w_ukv, m_w_o, m_g_ffn, m_w_ffn_gate, m_w_ffn_up, m_conv_w, m_conv_b, m_w_ffn_down, m_g_ple, m_w_ple_gate, m_w_ple_proj, m_g_final, v_w_in, v_g_attn, v_g_q_lora, v_g_kv_lora, v_w_uq, v_w_ukv, v_w_o, v_g_ffn, v_w_ffn_gate, v_w_ffn_up, v_conv_w, v_conv_b, v_w_ffn_down, v_g_ple, v_w_ple_gate, v_w_ple_proj, v_g_final):
    weights = dict(w_in=w_in, g_attn=g_attn, g_q_lora=g_q_lora, g_kv_lora=g_kv_lora, w_uq=w_uq, w_ukv=w_ukv, w_o=w_o,
                   g_ffn=g_ffn, w_ffn_gate=w_ffn_gate, w_ffn_up=w_ffn_up, conv_w=conv_w, conv_b=conv_b,
                   w_ffn_down=w_ffn_down, g_ple=g_ple, w_ple_gate=w_ple_gate, w_ple_proj=w_ple_proj, g_final=g_final)
    mom1 = dict(w_in=m_w_in, g_attn=m_g_attn, g_q_lora=m_g_q_lora, g_kv_lora=m_g_kv_lora, w_uq=m_w_uq, w_ukv=m_w_ukv,
                w_o=m_w_o, g_ffn=m_g_ffn, w_ffn_gate=m_w_ffn_gate, w_ffn_up=m_w_ffn_up, conv_w=m_conv_w,
                conv_b=m_conv_b, w_ffn_down=m_w_ffn_down, g_ple=m_g_ple, w_ple_gate=m_w_ple_gate,
                w_ple_proj=m_w_ple_proj, g_final=m_g_final)
    mom2 = dict(w_in=v_w_in, g_attn=v_g_attn, g_q_lora=v_g_q_lora, g_kv_lora=v_g_kv_lora, w_uq=v_w_uq, w_ukv=v_w_ukv,
                w_o=v_w_o, g_ffn=v_g_ffn, w_ffn_gate=v_w_ffn_gate, w_ffn_up=v_w_ffn_up, conv_w=v_conv_w,
                conv_b=v_conv_b, w_ffn_down=v_w_ffn_down, g_ple=v_g_ple, w_ple_gate=v_w_ple_gate,
                w_ple_proj=v_w_ple_proj, g_final=v_g_final)
    _, S, D = x.shape
    cfg = _Cfg(S, D)
    cx, cy, _ = _mesh_pos()

    shards = {name: _pad_shard(name, weights[name][0]) for name in BIG}
    gathered = _gather_weights([shards[name] for name in GATHER_FIRST])
    W = {name: _as_operand(name, g) for name, g in zip(GATHER_FIRST, gathered)}
    cw_all = _gather_small(_pack([jnp.pad(conv_w[0], ((0, 0), (0, cfg.Fp - cfg.Fs)))]))
    r_cw = cw_all.shape[0] // N_DEV
    cw_chips = [_unpack(cw_all[2 * k * r_cw:(2 * k + 1) * r_cw], [(CONV_WIDTH, cfg.Fp)])[0] for k in range(N_CHIP)]
    sp = dict(g_attn=g_attn, g_q_lora=g_q_lora, g_kv_lora=g_kv_lora, g_ffn=g_ffn, g_ple=g_ple,
              g_final=g_final.reshape(1, D), conv_w=jnp.concatenate(cw_chips, axis=1),
              conv_b=_channels_padded(conv_b, cfg))

    loss, dx, parts, gs = _local_step(cfg, x[0], p[0, 0], loss_target[0], W, shards, sp)
    loss = lax.psum(loss, ("x", "y", "c"))

    halves = [_chip_add(*parts[name]) for name in BIG]
    whole = dict(zip(BIG, _sibling_share(halves)))
    grads = {}

    small_names = ['g_attn', 'g_q_lora', 'g_kv_lora', 'g_ffn', 'g_ple', 'g_final', 'conv_b', 'conv_w']
    small_sum = _sum_devices(_gather_small(_pack([gs[name] for name in small_names])))
    for name, g in zip(small_names, _unpack(small_sum, [gs[name].shape for name in small_names])):
        grads[name] = g
    grads['g_final'] = grads['g_final'].reshape(D)
    grads['conv_b'] = _channels_unpadded(grads['conv_b'], cfg)
    mine = _chip_id(cx, cy)
    grads['conv_w'] = lax.dynamic_slice_in_dim(grads['conv_w'], mine * cfg.Fp, cfg.Fp, axis=1)[:, :cfg.Fs]
    grads['conv_w'] = grads['conv_w'].reshape(1, CONV_WIDTH, cfg.Fs)

    delta, new_m, new_v = {}, {}, {}
    for name in BIG:
        operands = (weights[name], whole[name], mom1[name], mom2[name])
        flip = weights[name].shape[2] % LANE != 0
        if flip:
            operands = tuple(a.T if a.ndim == 2 else a.transpose(0, 2, 1) for a in operands)
        outs = _adamw_sharded(*operands, name="adamw_" + name)
        if flip:
            outs = [o.transpose(0, 2, 1) for o in outs]
        grads[name], delta[name], new_m[name], new_v[name] = outs
    for name in WEIGHTS:
        if name in BIG:
            continue
        shape = weights[name].shape
        flat = (shape[-2], shape[-1]) if len(shape) == 3 else (1, shape[-1])
        d, m2, v2 = _adamw(weights[name].reshape(flat), grads[name].reshape(flat), mom1[name].reshape(flat),
                           mom2[name].reshape(flat), name="adamw_" + name)
        delta[name], new_m[name], new_v[name] = d.reshape(shape), m2.reshape(shape), v2.reshape(shape)

    return (loss, dx.reshape(1, S, D), *[grads[n] for n in WEIGHTS], *[delta[n] for n in WEIGHTS],
            *[new_m[n] for n in WEIGHTS], *[new_v[n] for n in WEIGHTS])
```

```python
import functools
import math

import jax
import jax.numpy as jnp
from jax import lax
from jax.experimental import pallas as pl
from jax.experimental.pallas import tpu as pltpu

F32 = jnp.float32
BF16 = jnp.bfloat16

LANE = 128
VMEM_LIMIT = 56 * 1024 * 1024
ROWWISE_BLOCK_BYTES = 5 * 1024 * 1024
MM_TILE = 1024
MM_TILE_N = 1408
MM_TILE_CONTRACT = 2816
MM_TILE_CONTRACT_N = 2816

N_CHIP = 4
MESH = pl.DeviceIdType.MESH

CHUNK = 64
RET_HEADS = 8
MLA_HEADS = 16
MLA_NOPE = 128
MLA_ROPE = 64
MLA_QK_PAD = 256
PLE_DIM = 256
CONV_WIDTH = 3
ROPE_BASE = 10000.0
EPS = 1e-6
RET_BLOCK = 256

ADAM_LR = 0.001
ADAM_B1 = 0.9
ADAM_B2 = 0.999
ADAM_EPS = 1e-08
ADAM_WD = 0.01
ADAM_STEP = 10

WEIGHTS = ['w_in', 'g_attn', 'g_q_lora', 'g_kv_lora', 'w_uq', 'w_ukv', 'w_o', 'g_ffn', 'w_ffn_gate', 'w_ffn_up',
           'conv_w', 'conv_b', 'w_ffn_down', 'g_ple', 'w_ple_gate', 'w_ple_proj', 'g_final']
COL_SHARDED = ('w_in', 'w_uq', 'w_ukv', 'w_ffn_gate', 'w_ffn_up', 'w_ple_proj')
ROW_SHARDED = ('w_o', 'w_ffn_down', 'w_ple_gate')
BIG = COL_SHARDED + ROW_SHARDED
SMALL_REPLICATED = ('g_attn', 'g_q_lora', 'g_kv_lora', 'g_ffn', 'conv_b', 'g_ple', 'g_final')


def _round_up(n, m):
    return (n + m - 1) // m * m


def _tile(dim, cap, align=LANE):
    if dim <= cap:
        return dim
    t = cap // align * align
    while t >= align:
        if dim % t == 0:
            return t
        t -= align
    return dim


def _params(sem):
    return pltpu.CompilerParams(dimension_semantics=sem, vmem_limit_bytes=VMEM_LIMIT)


HBM = pl.BlockSpec(memory_space=pltpu.HBM)


class _Side:
    def __init__(self, ins, out_shape, sems, start, finish):
        self.ins, self.out_shape, self.sems, self.start, self.finish = ins, out_shape, sems, start, finish


def _run_side(side, name):
    n_in, n_out = len(side.ins), len(side.out_shape)

    def body(*refs):
        ins, outs, sems = refs[:n_in], refs[n_in:n_in + n_out], refs[n_in + n_out:]
        side.start(ins, outs, sems)
        side.finish(ins, outs, sems)

    return pl.pallas_call(
        body, name=name, in_specs=[HBM] * n_in, out_specs=[HBM] * n_out, out_shape=side.out_shape,
        scratch_shapes=side.sems,
    )(*side.ins)


def _hosted_call(body, side, *, name, grid, in_specs, out_specs, out_shape, scratch_shapes, operands, semantics):
    if side is None:
        res = pl.pallas_call(
            body, name=name, grid=grid, in_specs=in_specs, out_specs=out_specs, out_shape=out_shape,
            scratch_shapes=scratch_shapes, compiler_params=_params(semantics))(*operands)
        return res, []
    n_in, n_out, n_scr = len(in_specs), len(out_specs), len(scratch_shapes)
    s_in, s_out = len(side.ins), len(side.out_shape)

    def hosted(*refs):
        refs = list(refs)
        ins, refs = refs[:n_in], refs[n_in:]
        side_ins, refs = refs[:s_in], refs[s_in:]
        outs, refs = refs[:n_out], refs[n_out:]
        side_outs, refs = refs[:s_out], refs[s_out:]
        scratch, sems = refs[:n_scr], refs[n_scr:]
        ids = [pl.program_id(a) for a in range(len(grid))]
        first = functools.reduce(jnp.logical_and, [i == 0 for i in ids])
        last = functools.reduce(jnp.logical_and, [i == g - 1 for i, g in zip(ids, grid)])

        @pl.when(first)
        def _():
            side.start(side_ins, side_outs, sems)

        body(*ins, *outs, *scratch)

        @pl.when(last)
        def _():
            side.finish(side_ins, side_outs, sems)

    res = pl.pallas_call(
        hosted, name=name, grid=grid, in_specs=list(in_specs) + [HBM] * s_in,
        out_specs=list(out_specs) + [HBM] * s_out, out_shape=list(out_shape) + list(side.out_shape),
        scratch_shapes=list(scratch_shapes) + list(side.sems),
        compiler_params=_params(("arbitrary",) * len(grid)))(*operands, *side.ins)
    return res[:n_out], res[n_out:]


def _mm_call(name, dims, grid, in_specs, out_specs, out_shape, acc_shape, operands, has_res, side=None):
    nsteps = grid[2]
    n_out = len(out_shape)

    def body(*refs):
        a_ref, b_ref = refs[0], refs[1]
        res_ref = refs[2] if has_res else None
        outs = refs[2 + has_res:2 + has_res + n_out]
        acc = refs[2 + has_res + n_out]
        k = pl.program_id(2)

        @pl.when(k == 0)
        def _():
            acc[...] = jnp.zeros_like(acc)

        acc[...] += lax.dot_general(a_ref[...], b_ref[...], (dims, ((), ())), preferred_element_type=F32)

        @pl.when(k == nsteps - 1)
        def _():
            r = acc[...]
            if has_res:
                r = r + res_ref[...]
            for o in outs:
                o[...] = r.astype(o.dtype)

    outs, side_outs = _hosted_call(
        body, side, name=name, grid=grid, in_specs=in_specs, out_specs=out_specs, out_shape=out_shape,
        scratch_shapes=[pltpu.VMEM(acc_shape, F32)], operands=operands,
        semantics=("parallel", "parallel", "arbitrary"))
    return (outs, side_outs) if side is not None else outs


def _mm_nn(a, w, *, name, res=None, out_dtype=F32, side=None):
    M, K = a.shape
    G, _, n = w.shape
    tm, tn, tk = _tile(M, MM_TILE), _tile(n, MM_TILE_N), _tile(K, MM_TILE_CONTRACT)
    npg = n // tn
    grid = (M // tm, G * npg, K // tk)
    in_specs = [pl.BlockSpec((tm, tk), lambda i, j, k: (i, k)),
                pl.BlockSpec((None, tk, tn), lambda i, j, k: (j // npg, k, j % npg))]
    operands = [a, w]
    if res is not None:
        in_specs.append(pl.BlockSpec((tm, tn), lambda i, j, k: (i, j)))
        operands.append(res)
    out_specs = [pl.BlockSpec((tm, tn), lambda i, j, k: (i, j))]
    out_shape = [jax.ShapeDtypeStruct((M, G * n), out_dtype)]
    got = _mm_call(name, ((1,), (0,)), grid, in_specs, out_specs, out_shape, (tm, tn), operands, res is not None, side)
    return (got[0][0], got[1]) if side is not None else got[0]


def _mm_nt(g, w, *, name, res=None, out_dtype=F32, side=None):
    M, _ = g.shape
    G, K, n = w.shape
    tm, tko, tn = _tile(M, MM_TILE), _tile(K, MM_TILE), _tile(n, MM_TILE_CONTRACT_N)
    npg = n // tn
    grid = (M // tm, K // tko, G * npg)
    in_specs = [pl.BlockSpec((tm, tn), lambda i, j, k: (i, k)),
                pl.BlockSpec((None, tko, tn), lambda i, j, k: (k // npg, j, k % npg))]
    operands = [g, w]
    if res is not None:
        in_specs.append(pl.BlockSpec((tm, tko), lambda i, j, k: (i, j)))
        operands.append(res)
    out_specs = [pl.BlockSpec((tm, tko), lambda i, j, k: (i, j))]
    out_shape = [jax.ShapeDtypeStruct((M, K), out_dtype)]
    got = _mm_call(name, ((1,), (1,)), grid, in_specs, out_specs, out_shape, (tm, tko), operands, res is not None, side)
    return (got[0][0], got[1]) if side is not None else got[0]


def _mm_tn(a, g, groups, *, name):
    M, K = a.shape
    n = g.shape[1] // groups
    tm, tko, tn = _tile(M, MM_TILE_CONTRACT), _tile(K, MM_TILE), _tile(n, MM_TILE_N)
    npg = n // tn
    grid = (K // tko, groups * npg, M // tm)
    in_specs = [pl.BlockSpec((tm, tko), lambda i, j, k: (k, i)),
                pl.BlockSpec((tm, tn), lambda i, j, k: (k, j))]
    out_spec = pl.BlockSpec((None, tko, tn), lambda i, j, k: (j // npg, i, j % npg))
    out_shape = [jax.ShapeDtypeStruct((groups, K, n), F32), jax.ShapeDtypeStruct((groups, K, n), BF16)]
    return _mm_call(name, ((0,), (0,)), grid, in_specs, [out_spec, out_spec], out_shape, (tko, tn), [a, g], False)


def _rowwise(fn, ins, outs, accs=(), *, name):
    R = next(a.shape[0] for a, kind in ins if kind == 'row')
    row_bytes = sum(a.shape[1] * a.dtype.itemsize for a, kind in ins if kind != 'bcast')
    row_bytes += sum(w * jnp.dtype(dt).itemsize for w, dt in outs)
    ts = 8
    while ts * 2 <= 512 and ts * 2 * row_bytes <= ROWWISE_BLOCK_BYTES:
        ts *= 2
    ts = min(ts, R)
    for a, kind in ins:
        if kind == 'per':
            ts = math.gcd(ts, a.shape[0])
    while R % ts:
        ts //= 2
    in_specs = []
    for a, kind in ins:
        w = a.shape[1]
        if kind == 'row':
            in_specs.append(pl.BlockSpec((ts, w), lambda i: (i, 0)))
        elif kind == 'bcast':
            in_specs.append(pl.BlockSpec((1, w), lambda i: (0, 0)))
        else:
            nper = a.shape[0] // ts
            in_specs.append(pl.BlockSpec((ts, w), lambda i, nper=nper: (i % nper, 0)))
    out_specs = [pl.BlockSpec((ts, w), lambda i: (i, 0)) for w, _ in outs]
    out_specs += [pl.BlockSpec((1, w), lambda i: (0, 0)) for w in accs]
    out_shape = [jax.ShapeDtypeStruct((R, w), dt) for w, dt in outs]
    out_shape += [jax.ShapeDtypeStruct((1, w), F32) for w in accs]
    n_in, n_out = len(ins), len(outs)

    def body(*refs):
        vals = [r[...] for r in refs[:n_in]]
        res = fn(*vals)
        for r, v in zip(refs[n_in:n_in + n_out], res[:n_out]):
            r[...] = v.astype(r.dtype)
        if accs:
            first = pl.program_id(0) == 0
            for r, v in zip(refs[n_in + n_out:], res[n_out:]):
                @pl.when(first)
                def _(r=r, v=v):
                    r[...] = v

                @pl.when(jnp.logical_not(first))
                def _(r=r, v=v):
                    r[...] += v

    return pl.pallas_call(
        body, name=name, grid=(R // ts,), in_specs=in_specs, out_specs=out_specs, out_shape=out_shape,
        compiler_params=_params(("arbitrary",) if accs else ("parallel",)),
    )(*[a for a, _ in ins])


def _mean(v):
    return jnp.mean(v, axis=-1, keepdims=True)


def _colsum(v):
    return jnp.sum(v, axis=0, keepdims=True)


def _sigmoid(v):
    return 1.0 / (1.0 + jnp.exp(-v))


def _rms_fwd(x, g, *, name):
    def fn(xv, gv):
        r = lax.rsqrt(_mean(xv * xv) + EPS)
        return [xv * r * gv, r]
    w = x.shape[1]
    return _rowwise(fn, [(x, 'row'), (g, 'bcast')], [(w, BF16), (1, F32)], name=name)


def _rms_bwd(x, r, dhn, g, dres, *, name):
    def fn(xv, rv, dv, gv, *rest):
        n = xv * rv
        dn = dv * gv
        dx = rv * (dn - n * _mean(dn * n))
        if rest:
            dx = dx + rest[0]
        return [dx, dx, _colsum(dv * n)]
    w = x.shape[1]
    ins = [(x, 'row'), (r, 'row'), (dhn, 'row'), (g, 'bcast')]
    if dres is not None:
        ins.append((dres, 'row'))
    return _rowwise(fn, ins, [(w, F32), (w, BF16)], [w], name=name)


def _rope(x, cos, sin_a, sin_b, shift, scale, out_dtype, *, name):
    w = x.shape[1]
    wr = cos.shape[1]
    lo = w - wr

    def fn(xv, cv, sav, sbv):
        xr = xv[:, lo:] if lo else xv
        y = xr * cv + pltpu.roll(xr, wr - shift, 1) * sav + pltpu.roll(xr, shift, 1) * sbv
        if lo:
            y = jnp.concatenate([xv[:, :lo], y], axis=1)
        return [y * scale]
    return _rowwise(fn, [(x, 'row'), (cos, 'per'), (sin_a, 'per'), (sin_b, 'per')], [(w, out_dtype)], name=name)[0]


def _dot(a, b, ca, cb):
    return lax.dot_general(a, b, (((ca,), (cb,)), ((), ())), preferred_element_type=F32)


def _ret_fwd(q, k, v, dec, dq_, dk_, ds_):
    H, S, d = q.shape
    T = dec.shape[1]
    nC = S // T

    def body(q_ref, k_ref, v_ref, m_ref, qd_ref, kd_ref, sd_ref, o_ref, st_ref, state):
        @pl.when(pl.program_id(1) == 0)
        def _():
            state[...] = jnp.zeros_like(state)

        st = state[...]
        st_ref[...] = st
        qv, kv, vv = q_ref[...], k_ref[...], v_ref[...]
        p = (_dot(qv, kv, 1, 1) * m_ref[...]).astype(BF16)
        qs = (qv.astype(F32) * qd_ref[...]).astype(BF16)
        o_ref[...] = _dot(p, vv, 1, 0) + _dot(qs, st.astype(BF16), 1, 0)
        ks = (kv.astype(F32) * kd_ref[...]).astype(BF16)
        state[...] = st * sd_ref[...] + _dot(ks, vv, 0, 0)

    blk = pl.BlockSpec((None, T, d), lambda h, c: (h, c, 0))
    return pl.pallas_call(
        body, name="ret_fwd", grid=(H, nC),
        in_specs=[blk, blk, blk,
                  pl.BlockSpec((None, T, T), lambda h, c: (h, 0, 0)),
                  pl.BlockSpec((None, T, 1), lambda h, c: (h, 0, 0)),
                  pl.BlockSpec((None, T, 1), lambda h, c: (h, 0, 0)),
                  pl.BlockSpec((None, 1, 1), lambda h, c: (h, 0, 0))],
        out_specs=[blk, pl.BlockSpec((None, None, d, d), lambda h, c: (h, c, 0, 0))],
        out_shape=[jax.ShapeDtypeStruct((H, S, d), F32), jax.ShapeDtypeStruct((H, nC, d, d), F32)],
        scratch_shapes=[pltpu.VMEM((d, d), F32)],
        compiler_params=_params(("parallel", "arbitrary")),
    )(q, k, v, dec, dq_, dk_, ds_)


def _ret_bwd(q, k, v, do, states, dec, dq_, dk_, ds_):
    H, S, d = q.shape
    T = dec.shape[1]
    nC = S // T

    def body(q_ref, k_ref, v_ref, do_ref, st_ref, m_ref, qd_ref, kd_ref, sd_ref, gq_ref, gk_ref, gv_ref, dstate):
        @pl.when(pl.program_id(1) == 0)
        def _():
            dstate[...] = jnp.zeros_like(dstate)

        qv, kv, vv, dov = q_ref[...], k_ref[...], v_ref[...], do_ref[...]
        m = m_ref[...]
        qd, kd = qd_ref[...], kd_ref[...]
        ds = dstate[...]
        dsb = ds.astype(BF16)
        sb = st_ref[...].astype(BF16)
        p = (_dot(qv, kv, 1, 1) * m).astype(BF16)
        da = (_dot(dov, vv, 1, 1) * m).astype(BF16)
        qs = (qv.astype(F32) * qd).astype(BF16)
        ks = (kv.astype(F32) * kd).astype(BF16)
        gq_ref[...] = _dot(da, kv, 1, 0) + _dot(dov, sb, 1, 1) * qd
        gk_ref[...] = _dot(da, qv, 0, 0) + _dot(vv, dsb, 1, 1) * kd
        gv_ref[...] = _dot(p, dov, 0, 0) + _dot(ks, dsb, 1, 0)
        dstate[...] = ds * sd_ref[...] + _dot(qs, dov, 0, 0)

    blk = pl.BlockSpec((None, T, d), lambda h, c: (h, nC - 1 - c, 0))
    out = jax.ShapeDtypeStruct((H, S, d), F32)
    return pl.pallas_call(
        body, name="ret_bwd", grid=(H, nC),
        in_specs=[blk, blk, blk, blk,
                  pl.BlockSpec((None, None, d, d), lambda h, c: (h, nC - 1 - c, 0, 0)),
                  pl.BlockSpec((None, T, T), lambda h, c: (h, 0, 0)),
                  pl.BlockSpec((None, T, 1), lambda h, c: (h, 0, 0)),
                  pl.BlockSpec((None, T, 1), lambda h, c: (h, 0, 0)),
                  pl.BlockSpec((None, 1, 1), lambda h, c: (h, 0, 0))],
        out_specs=[blk, blk, blk], out_shape=[out, out, out],
        scratch_shapes=[pltpu.VMEM((d, d), F32)],
        compiler_params=_params(("parallel", "arbitrary")),
    )(q, k, v, do, states, dec, dq_, dk_, ds_)


def _ret_tables(T, d):
    h = jnp.arange(RET_HEADS, dtype=F32)
    log_g = jnp.log1p(-jnp.exp2(-5.0 - h))
    idx = jnp.arange(T, dtype=F32)
    diff = idx[:, None] - idx[None, :]
    same = (jnp.arange(T)[:, None] // CHUNK) == (jnp.arange(T)[None, :] // CHUNK)
    earlier = (jnp.arange(T)[None, :] // CHUNK) < (jnp.arange(T)[:, None] // CHUNK)
    expo = jnp.where(same, jnp.abs(diff), diff)
    dec = jnp.where(same | earlier, jnp.exp(log_g[:, None, None] * expo[None]), 0.0)
    q_dec = jnp.exp(log_g[:, None] * (idx + 1.0))[..., None]
    k_dec = jnp.exp(log_g[:, None] * (T - 1.0 - idx))[..., None]
    s_dec = jnp.exp(log_g * T)[:, None, None]
    return dec.astype(F32), q_dec, k_dec, s_dec


NEG = -1e30


ROW_GROUP = 512


def _scores(q, kv, r, diagonal):
    s = _dot(q, kv, 1, 1)
    if diagonal:
        rg, T = s.shape
        qc = (r * rg + lax.broadcasted_iota(jnp.int32, (rg, T), 0)) // CHUNK
        kc = lax.broadcasted_iota(jnp.int32, (rg, T), 1) // CHUNK
        s = jnp.where(kc <= qc, s, NEG)
    return s


def _by_query_block(p, n):
    i = sum((p >= t * (t + 1) // 2).astype(jnp.int32) for t in range(1, n))
    return i, p - i * (i + 1) // 2


def _by_key_block(p, n):
    j = sum((p >= t * n - t * (t - 1) // 2).astype(jnp.int32) for t in range(1, n))
    return j, j + p - (j * n - j * (j - 1) // 2)


def _flash_fwd(q, k, v, side=None):
    H, S, dk = q.shape
    dv = v.shape[2]
    T = _tile(S, 512)
    n = S // T
    rg = min(ROW_GROUP, T)

    def body(q_ref, k_ref, v_ref, o_ref, lse_ref, m_s, l_s, acc):
        qi, ki = _by_query_block(pl.program_id(1), n)

        @pl.when(ki == 0)
        def _():
            m_s[...] = jnp.full_like(m_s, NEG)
            l_s[...] = jnp.zeros_like(l_s)
            acc[...] = jnp.zeros_like(acc)

        def step(diagonal):
            kv, vv = k_ref[...], v_ref[...]
            for r in range(T // rg):
                rows = pl.ds(r * rg, rg)
                s = _scores(q_ref[rows, :], kv, r, diagonal)
                m_old = m_s[rows, :]
                m_new = jnp.maximum(m_old, jnp.max(s, axis=1, keepdims=True))
                p = jnp.exp(s - m_new)
                alpha = jnp.exp(m_old - m_new)
                l_s[rows, :] = alpha * l_s[rows, :] + jnp.sum(p, axis=1, keepdims=True)
                acc[rows, :] = alpha * acc[rows, :] + _dot(p.astype(BF16), vv, 1, 0)
                m_s[rows, :] = m_new

        @pl.when(ki < qi)
        def _():
            step(False)

        @pl.when(ki == qi)
        def _():
            step(True)
            o_ref[...] = acc[...] / l_s[...]
            lse_ref[...] = m_s[...] + jnp.log(l_s[...])

    q_map = lambda h, p: (h, _by_query_block(p, n)[0], 0)
    kv_map = lambda h, p: (h, _by_query_block(p, n)[1], 0)
    return _hosted_call(
        body, side, name="mla_fwd", grid=(H, n * (n + 1) // 2),
        in_specs=[pl.BlockSpec((None, T, dk), q_map), pl.BlockSpec((None, T, dk), kv_map),
                  pl.BlockSpec((None, T, dv), kv_map)],
        out_specs=[pl.BlockSpec((None, T, dv), q_map), pl.BlockSpec((None, T, 1), q_map)],
        out_shape=[jax.ShapeDtypeStruct((H, S, dv), F32), jax.ShapeDtypeStruct((H, S, 1), F32)],
        scratch_shapes=[pltpu.VMEM((T, 1), F32), pltpu.VMEM((T, 1), F32), pltpu.VMEM((T, dv), F32)],
        operands=[q, k, v], semantics=("parallel", "arbitrary"))


def _flash_dq(q, k, v, do, lse, dlt, side=None):
    H, S, dk = q.shape
    dv = v.shape[2]
    T = _tile(S, 512)
    n = S // T
    rg = min(ROW_GROUP, T)

    def body(q_ref, k_ref, v_ref, do_ref, lse_ref, dlt_ref, dq_ref, acc):
        qi, ki = _by_query_block(pl.program_id(1), n)

        @pl.when(ki == 0)
        def _():
            acc[...] = jnp.zeros_like(acc)

        def step(diagonal):
            kv, vv = k_ref[...], v_ref[...]
            for r in range(T // rg):
                rows = pl.ds(r * rg, rg)
                p = jnp.exp(_scores(q_ref[rows, :], kv, r, diagonal) - lse_ref[rows, :])
                dp = _dot(do_ref[rows, :], vv, 1, 1)
                ds = (p * (dp - dlt_ref[rows, :])).astype(BF16)
                acc[rows, :] += _dot(ds, kv, 1, 0)

        @pl.when(ki < qi)
        def _():
            step(False)

        @pl.when(ki == qi)
        def _():
            step(True)
            dq_ref[...] = acc[...]

    q_map = lambda h, p: (h, _by_query_block(p, n)[0], 0)
    kv_map = lambda h, p: (h, _by_query_block(p, n)[1], 0)
    return _hosted_call(
        body, side, name="mla_dq", grid=(H, n * (n + 1) // 2),
        in_specs=[pl.BlockSpec((None, T, dk), q_map), pl.BlockSpec((None, T, dk), kv_map),
                  pl.BlockSpec((None, T, dv), kv_map), pl.BlockSpec((None, T, dv), q_map),
                  pl.BlockSpec((None, T, 1), q_map), pl.BlockSpec((None, T, 1), q_map)],
        out_specs=[pl.BlockSpec((None, T, dk), q_map)],
        out_shape=[jax.ShapeDtypeStruct((H, S, dk), F32)],
        scratch_shapes=[pltpu.VMEM((T, dk), F32)],
        operands=[q, k, v, do, lse, dlt], semantics=("parallel", "arbitrary"))


def _flash_dkv(q, k, v, do, lse, dlt, side=None):
    H, S, dk = q.shape
    dv = v.shape[2]
    T = _tile(S, 512)
    n = S // T
    rg = min(ROW_GROUP, T)

    def body(q_ref, k_ref, v_ref, do_ref, lse_ref, dlt_ref, dk_ref, dv_ref, acc_k, acc_v):
        ki, qi = _by_key_block(pl.program_id(1), n)

        @pl.when(qi == ki)
        def _():
            acc_k[...] = jnp.zeros_like(acc_k)
            acc_v[...] = jnp.zeros_like(acc_v)

        def step(diagonal):
            kv, vv = k_ref[...], v_ref[...]
            for r in range(T // rg):
                rows = pl.ds(r * rg, rg)
                qv, dov = q_ref[rows, :], do_ref[rows, :]
                p = jnp.exp(_scores(qv, kv, r, diagonal) - lse_ref[rows, :])
                acc_v[...] += _dot(p.astype(BF16), dov, 0, 0)
                dp = _dot(dov, vv, 1, 1)
                ds = (p * (dp - dlt_ref[rows, :])).astype(BF16)
                acc_k[...] += _dot(ds, qv, 0, 0)

        @pl.when(qi > ki)
        def _():
            step(False)

        @pl.when(qi == ki)
        def _():
            step(True)

        @pl.when(qi == n - 1)
        def _():
            dk_ref[...] = acc_k[...]
            dv_ref[...] = acc_v[...]

    q_map = lambda h, p: (h, _by_key_block(p, n)[1], 0)
    kv_map = lambda h, p: (h, _by_key_block(p, n)[0], 0)
    return _hosted_call(
        body, side, name="mla_dkv", grid=(H, n * (n + 1) // 2),
        in_specs=[pl.BlockSpec((None, T, dk), q_map), pl.BlockSpec((None, T, dk), kv_map),
                  pl.BlockSpec((None, T, dv), kv_map), pl.BlockSpec((None, T, dv), q_map),
                  pl.BlockSpec((None, T, 1), q_map), pl.BlockSpec((None, T, 1), q_map)],
        out_specs=[pl.BlockSpec((None, T, dk), kv_map), pl.BlockSpec((None, T, dv), kv_map)],
        out_shape=[jax.ShapeDtypeStruct((H, S, dk), F32), jax.ShapeDtypeStruct((H, S, dv), F32)],
        scratch_shapes=[pltpu.VMEM((T, dk), F32), pltpu.VMEM((T, dv), F32)],
        operands=[q, k, v, do, lse, dlt], semantics=("parallel", "arbitrary"))


EDGE = 8


def _shift_down(cur, prev, by):
    ts, tc = cur.shape
    rows = lax.broadcasted_iota(jnp.int32, cur.shape, 0)
    head = jnp.concatenate([pltpu.roll(prev, by, 0), jnp.zeros((ts - EDGE, tc), cur.dtype)], axis=0)
    return jnp.where(rows < by, head, pltpu.roll(cur, by, 0))


def _shift_up(cur, nxt, by):
    ts, tc = cur.shape
    rows = lax.broadcasted_iota(jnp.int32, cur.shape, 0)
    tail = jnp.concatenate([jnp.zeros((ts - EDGE, tc), cur.dtype), pltpu.roll(nxt, EDGE - by, 0)], axis=0)
    return jnp.where(rows >= ts - by, tail, pltpu.roll(cur, ts - by, 0))


def _ffn_tiles(S, F):
    return _tile(S, 256, 8), _tile(F, 1024)


def _ffn_fwd(G, U, cw, cb):
    S, F = G.shape
    ts, tc = _ffn_tiles(S, F)

    def body(g_ref, gp_ref, u_ref, cw_ref, cb_ref, a_ref, act_ref):
        cur = g_ref[...]
        prev = gp_ref[...] * (pl.program_id(1) > 0).astype(F32)
        a = (cb_ref[...] + cw_ref[0:1, :] * _shift_down(cur, prev, 2) + cw_ref[1:2, :] * _shift_down(cur, prev, 1)
             + cw_ref[2:3, :] * cur)
        a_ref[...] = a
        act_ref[...] = (a * _sigmoid(a) * u_ref[...]).astype(BF16)

    cur_spec = pl.BlockSpec((ts, tc), lambda j, i: (i, j))
    return pl.pallas_call(
        body, name="ffn_act_fwd", grid=(F // tc, S // ts),
        in_specs=[cur_spec, pl.BlockSpec((EDGE, tc), lambda j, i: (jnp.maximum(i * (ts // EDGE) - 1, 0), j)),
                  cur_spec,
                  pl.BlockSpec((CONV_WIDTH, tc), lambda j, i: (0, j)), pl.BlockSpec((1, tc), lambda j, i: (0, j))],
        out_specs=[cur_spec, cur_spec],
        out_shape=[jax.ShapeDtypeStruct((S, F), F32), jax.ShapeDtypeStruct((S, F), BF16)],
        compiler_params=_params(("parallel", "parallel")),
    )(G, G, U, cw, cb)


def _ffn_bwd_act(a, U, dact):
    S, F = a.shape
    ts, tc = _ffn_tiles(S, F)

    def body(a_ref, u_ref, d_ref, da_ref, du_ref, db_ref):
        av, dv = a_ref[...], d_ref[...]
        sg = _sigmoid(av)
        du_ref[...] = (dv * av * sg).astype(BF16)
        da = dv * u_ref[...] * sg * (1.0 + av * (1.0 - sg))
        da_ref[...] = da

        @pl.when(pl.program_id(1) == 0)
        def _():
            db_ref[...] = jnp.zeros_like(db_ref)

        db_ref[...] += _colsum(da)

    cur_spec = pl.BlockSpec((ts, tc), lambda j, i: (i, j))
    return pl.pallas_call(
        body, name="ffn_act_bwd", grid=(F // tc, S // ts),
        in_specs=[cur_spec, cur_spec, cur_spec],
        out_specs=[cur_spec, cur_spec, pl.BlockSpec((1, tc), lambda j, i: (0, j))],
        out_shape=[jax.ShapeDtypeStruct((S, F), F32), jax.ShapeDtypeStruct((S, F), BF16),
                   jax.ShapeDtypeStruct((1, F), F32)],
        compiler_params=_params(("parallel", "arbitrary")),
    )(a, U, dact)


def _ffn_bwd_conv(da, G, cw):
    S, F = da.shape
    ts, tc = _ffn_tiles(S, F)
    n = S // ts

    def body(d_ref, dn_ref, g_ref, gp_ref, cw_ref, dg_ref, dw_ref):
        i = pl.program_id(1)
        dcur = d_ref[...]
        dnxt = dn_ref[...] * (i < n - 1).astype(F32)
        cur = g_ref[...]
        prev = gp_ref[...] * (i > 0).astype(F32)
        dg = (cw_ref[2:3, :] * dcur + cw_ref[1:2, :] * _shift_up(dcur, dnxt, 1)
              + cw_ref[0:1, :] * _shift_up(dcur, dnxt, 2))
        dg_ref[...] = dg.astype(BF16)

        @pl.when(i == 0)
        def _():
            dw_ref[...] = jnp.zeros_like(dw_ref)

        dw_ref[0:1, :] += _colsum(dcur * _shift_down(cur, prev, 2))
        dw_ref[1:2, :] += _colsum(dcur * _shift_down(cur, prev, 1))
        dw_ref[2:3, :] += _colsum(dcur * cur)

    cur_spec = pl.BlockSpec((ts, tc), lambda j, i: (i, j))
    return pl.pallas_call(
        body, name="ffn_conv_bwd", grid=(F // tc, n),
        in_specs=[cur_spec,
                  pl.BlockSpec((EDGE, tc), lambda j, i: (jnp.minimum((i + 1) * (ts // EDGE), S // EDGE - 1), j)),
                  cur_spec, pl.BlockSpec((EDGE, tc), lambda j, i: (jnp.maximum(i * (ts // EDGE) - 1, 0), j)),
                  pl.BlockSpec((CONV_WIDTH, tc), lambda j, i: (0, j))],
        out_specs=[cur_spec, pl.BlockSpec((CONV_WIDTH, tc), lambda j, i: (0, j))],
        out_shape=[jax.ShapeDtypeStruct((S, F), BF16), jax.ShapeDtypeStruct((CONV_WIDTH, F), F32)],
        compiler_params=_params(("parallel", "arbitrary")),
    )(da, da, G, G, cw)


class _Cfg:
    def __init__(self, S, D):
        self.S, self.D = S, D
        self.RD = D // (2 * RET_HEADS)
        self.RW = RET_HEADS * self.RD
        self.MV = (D - self.RW) // MLA_HEADS
        self.QL, self.KVL = D // 4, D // 8
        self.F = ((8 * D // 3 + 255) // 256) * 256
        self.IN = 4 * self.RW + self.QL + self.KVL + MLA_ROPE
        self.INs = self.IN // N_CHIP
        self.INp = _round_up(self.INs, LANE)
        self.Fs = self.F // N_CHIP
        self.Fp = _round_up(self.Fs, LANE)
        self.F4 = N_CHIP * self.Fp
        self.QK = MLA_NOPE + MLA_ROPE
        self.KVH = MLA_NOPE + self.MV


def _head_major(t, H, d):
    S = t.shape[0]
    return t.reshape(S, H, d).transpose(1, 0, 2).reshape(H * S, d)


def _seq_major(t, H, d):
    S = t.shape[0] // H
    return t.reshape(H, S, d).transpose(1, 0, 2).reshape(S, H * d)


def _rope_tables(cfg):
    S = cfg.S

    def cs(dim):
        inv = 1.0 / (ROPE_BASE ** (jnp.arange(0, dim, 2, dtype=F32) / dim))
        ang = jnp.arange(S, dtype=F32)[:, None] * inv[None, :]
        return jnp.cos(ang), jnp.sin(ang)

    c, s = cs(cfg.RD)
    z = jnp.zeros_like(s)
    ret = (jnp.concatenate([c, c], 1), jnp.concatenate([-s, z], 1), jnp.concatenate([z, s], 1))
    c, s = cs(MLA_ROPE)
    tail1 = jnp.ones((S, MLA_QK_PAD - cfg.QK), F32)
    tail0 = jnp.zeros((S, MLA_QK_PAD - cfg.QK), F32)
    z = jnp.zeros_like(s)
    mla = (jnp.concatenate([c, c, tail1], 1), jnp.concatenate([-s, z, tail0], 1), jnp.concatenate([z, s, tail0], 1))
    return ret, mla


GATHER_FIRST = ('w_in',)
GATHER_BEHIND_IN_PROJ = ('w_uq', 'w_ukv', 'w_o')
GATHER_BEHIND_MLA = ('w_ffn_gate', 'w_ffn_up')
GATHER_BEHIND_FFN_GATE = ('w_ffn_down',)
GATHER_BEHIND_FFN_UP = ('w_ple_gate', 'w_ple_proj')
PAIR_BEHIND_FFN_DX = ('w_ple_proj', 'w_ple_gate', 'w_ffn_down', 'w_ffn_gate', 'w_ffn_up')
PAIR_AFTER_OUT_PROJ = ('w_o',)
REDUCE_BEHIND_DQ = ('w_ffn_gate', 'w_ple_gate', 'w_o')
REDUCE_BEHIND_DKV = ('w_ffn_up', 'w_ffn_down', 'w_ple_proj')
REDUCE_LAST = ('w_uq', 'w_ukv', 'w_in')


def _local_step(cfg, x, p, tgt, W, late, sp):
    W = dict(W)
    S, D, RD, RW, MV = cfg.S, cfg.D, cfg.RD, cfg.RW, cfg.MV
    H, MH = RET_HEADS, MLA_HEADS
    (rc, rsa, rsb), (mc, msa, msb) = _rope_tables(cfg)
    dec, q_dec, k_dec, s_dec = _ret_tables(min(RET_BLOCK, S), RD)
    k_scale = RD ** -0.5
    a_scale = cfg.QK ** -0.5
    gb, gs = {}, {}

    hn1, r1 = _rms_fwd(x, sp['g_attn'], name="rms_attn")
    in_cols = _round_up(cfg.IN, LANE)
    w_in = W['w_in'][:, :, :cfg.INs].transpose(1, 0, 2).reshape(1, D, cfg.IN)
    w_in = jnp.pad(w_in, ((0, 0), (0, 0), (0, in_cols - cfg.IN)))
    proj, got = _mm_nn(hn1, w_in, name="in_proj", side=_gather_side([late[n] for n in GATHER_BEHIND_IN_PROJ]))
    W.update({n: _as_operand(n, g) for n, g in zip(GATHER_BEHIND_IN_PROJ, got)})
    cuts = [RW, 2 * RW, 3 * RW, 4 * RW, 4 * RW + cfg.QL, 4 * RW + cfg.QL + cfg.KVL, cfg.IN]
    rq, rk, rv, rg, cq, ckv, kr, _ = jnp.split(proj, cuts, axis=1)

    rq_h, rk_h, rv_h, rg_h = (_head_major(t, H, RD) for t in (rq, rk, rv, rg))
    q_r = _rope(rq_h, rc, rsa, rsb, RD // 2, 1.0, BF16, name="ret_rope_q").reshape(H, S, RD)
    k_r = _rope(rk_h, rc, rsa, rsb, RD // 2, k_scale, BF16, name="ret_rope_k").reshape(H, S, RD)
    v_r = rv_h.astype(BF16).reshape(H, S, RD)
    o_h, states = _ret_fwd(q_r, k_r, v_r, dec, q_dec, k_dec, s_dec)
    o_h = o_h.reshape(H * S, RD)

    def gate_fn(ov, gv):
        oc = ov - _mean(ov)
        ron = oc * lax.rsqrt(_mean(oc * oc) + EPS)
        return [gv * _sigmoid(gv) * ron]
    ro_h = _rowwise(gate_fn, [(o_h, 'row'), (rg_h, 'row')], [(RD, BF16)], name="ret_gate")[0]
    ro = _seq_major(ro_h, H, RD)

    cqn, rcq = _rms_fwd(cq, sp['g_q_lora'], name="rms_q")
    ckvn, rckv = _rms_fwd(ckv, sp['g_kv_lora'], name="rms_kv")
    Q = _mm_nn(cqn, W['w_uq'], name="q_up")
    KV = _mm_nn(ckvn, W['w_ukv'], name="kv_up")
    pad = jnp.zeros((MH, S, MLA_QK_PAD - cfg.QK), F32)
    q_raw = jnp.concatenate([Q.reshape(S, MH, cfg.QK).transpose(1, 0, 2), pad], 2).reshape(MH * S, MLA_QK_PAD)
    KV3 = KV.reshape(S, MH, cfg.KVH).transpose(1, 0, 2)
    k_raw = jnp.concatenate([KV3[:, :, :MLA_NOPE], jnp.broadcast_to(kr[None], (MH, S, MLA_ROPE)), pad], 2)
    k_raw = k_raw.reshape(MH * S, MLA_QK_PAD)
    v_m = KV3[:, :, MLA_NOPE:].astype(BF16)
    q_m = _rope(q_raw, mc, msa, msb, MLA_ROPE // 2, a_scale, BF16, name="mla_rope_q").reshape(MH, S, MLA_QK_PAD)
    k_m = _rope(k_raw, mc, msa, msb, MLA_ROPE // 2, 1.0, BF16, name="mla_rope_k").reshape(MH, S, MLA_QK_PAD)
    (mo_h, lse), got = _flash_fwd(q_m, k_m, v_m, side=_gather_side([late[n] for n in GATHER_BEHIND_MLA]))
    W.update({n: _as_operand(n, g) for n, g in zip(GATHER_BEHIND_MLA, got)})
    mo = _seq_major(mo_h.reshape(MH * S, MV), MH, MV)

    cat = jnp.concatenate([ro, mo.astype(BF16)], axis=1)
    h1 = _mm_nn(cat, W['w_o'], name="out_proj", res=x)

    hn2, r2 = _rms_fwd(h1, sp['g_ffn'], name="rms_ffn")
    G, got = _mm_nn(hn2, W['w_ffn_gate'], name="ffn_gate",
                    side=_gather_side([late[n] for n in GATHER_BEHIND_FFN_GATE]))
    W.update({n: _as_operand(n, g) for n, g in zip(GATHER_BEHIND_FFN_GATE, got)})
    U, got = _mm_nn(hn2, W['w_ffn_up'], name="ffn_up", side=_gather_side([late[n] for n in GATHER_BEHIND_FFN_UP]))
    W.update({n: _as_operand(n, g) for n, g in zip(GATHER_BEHIND_FFN_UP, got)})
    a, act = _ffn_fwd(G, U, sp['conv_w'], sp['conv_b'])
    h2 = _mm_nn(act, W['w_ffn_down'], name="ffn_down", res=h1)

    hn3, r3 = _rms_fwd(h2, sp['g_ple'], name="rms_ple")
    Z = _mm_nn(hn3, W['w_ple_gate'], name="ple_gate")
    p_b = p.astype(BF16)
    PP = _mm_nn(p_b, W['w_ple_proj'], name="ple_proj")

    def head_fn(h2v, zv, ppv, tv, gv):
        gate = _sigmoid(zv)
        h3 = h2v + gate * ppv
        r4 = lax.rsqrt(_mean(h3 * h3) + EPS)
        n4 = h3 * r4
        e = n4 * gv - tv
        dy = e * (1.0 / D)
        dn = dy * gv
        dh3 = r4 * (dn - n4 * _mean(dn * n4))
        dpp = dh3 * gate
        dz = dh3 * ppv * gate * (1.0 - gate)
        loss = jnp.sum(0.5 * _mean(e * e), axis=0, keepdims=True)
        return [dh3, dz, dpp, _colsum(dy * n4), jnp.broadcast_to(loss, (1, LANE))]
    dh3, dZ, dPP, dgf, loss = _rowwise(
        head_fn, [(h2, 'row'), (Z, 'row'), (PP, 'row'), (tgt, 'row'), (sp['g_final'], 'bcast')],
        [(D, F32), (D, BF16), (D, BF16)], [D, LANE], name="ple_loss_head")
    gs['g_final'] = dgf
    loss = loss[0, 0]

    gb['w_ple_proj'] = _mm_tn(p_b, dPP, N_CHIP, name="ple_proj_dw")
    gb['w_ple_gate'] = _mm_tn(hn3, dZ, 1, name="ple_gate_dw")
    dhn3 = _mm_nt(dZ, W['w_ple_gate'], name="ple_gate_dx")
    dh2, dh2_b, gs['g_ple'] = _rms_bwd(h2, r3, dhn3, sp['g_ple'], dh3, name="rms_ple_bwd")

    dact = _mm_nt(dh2_b, W['w_ffn_down'], name="ffn_down_dx")
    gb['w_ffn_down'] = _mm_tn(act, dh2_b, 1, name="ffn_down_dw")
    da, dU, gs['conv_b'] = _ffn_bwd_act(a, U, dact)
    dG, gs['conv_w'] = _ffn_bwd_conv(da, G, sp['conv_w'])
    gb['w_ffn_gate'] = _mm_tn(hn2, dG, N_CHIP, name="ffn_gate_dw")
    gb['w_ffn_up'] = _mm_tn(hn2, dU, N_CHIP, name="ffn_up_dw")
    def pair_sums(names, from_sibling):
        return {n: _pair_add(_grad_pieces(n, gb[n][0]), r) for n, r in zip(names, from_sibling)}

    def wire(names):
        return [_grad_pieces(n, gb[n][1]) for n in names]

    dhn2, got = _mm_nt(dG, W['w_ffn_gate'], name="ffn_gate_dx", side=_pair_side(wire(PAIR_BEHIND_FFN_DX)))
    pair = pair_sums(PAIR_BEHIND_FFN_DX, got)
    dhn2 = _mm_nt(dU, W['w_ffn_up'], name="ffn_up_dx", res=dhn2)
    dh1, dh1_b, gs['g_ffn'] = _rms_bwd(h1, r2, dhn2, sp['g_ffn'], dh2, name="rms_ffn_bwd")

    dcat = _mm_nt(dh1_b, W['w_o'], name="out_proj_dx")
    gb['w_o'] = _mm_tn(cat, dh1_b, 1, name="out_proj_dw")
    pair.update(pair_sums(PAIR_AFTER_OUT_PROJ, _pair_exchange(wire(PAIR_AFTER_OUT_PROJ))))
    from_chips = {}
    dro_h = _head_major(dcat[:, :RW], H, RD)

    def gate_bwd_fn(ov, gv, dv):
        oc = ov - _mean(ov)
        rs = lax.rsqrt(_mean(oc * oc) + EPS)
        ron = oc * rs
        sg = _sigmoid(gv)
        dron = dv * gv * sg
        drg = dv * ron * sg * (1.0 + gv * (1.0 - sg))
        do = rs * (dron - _mean(dron) - ron * _mean(dron * ron))
        return [do, drg]
    do_h, drg_h = _rowwise(gate_bwd_fn, [(o_h, 'row'), (rg_h, 'row'), (dro_h, 'row')],
                           [(RD, BF16), (RD, F32)], name="ret_gate_bwd")
    gq_r, gk_r, gv_r = _ret_bwd(q_r, k_r, v_r, do_h.reshape(H, S, RD), states, dec, q_dec, k_dec, s_dec)
    drq_h = _rope(gq_r.reshape(H * S, RD), rc, -rsa, -rsb, RD // 2, 1.0, F32, name="ret_rope_q_bwd")
    drk_h = _rope(gk_r.reshape(H * S, RD), rc, -rsa, -rsb, RD // 2, k_scale, F32, name="ret_rope_k_bwd")
    drq, drk, drv, drg = (_seq_major(t, H, RD) for t in (drq_h, drk_h, gv_r.reshape(H * S, RD), drg_h))

    dmo_h = _head_major(dcat[:, RW:], MH, MV)
    dlt = _rowwise(lambda ov, dv: [jnp.sum(ov * dv, axis=1, keepdims=True)],
                   [(mo_h.reshape(MH * S, MV), 'row'), (dmo_h, 'row')], [(1, F32)], name="mla_delta")[0]
    dmo_b = dmo_h.astype(BF16).reshape(MH, S, MV)
    dlt = dlt.reshape(MH, S, 1)
    (gq_m,), got = _flash_dq(q_m, k_m, v_m, dmo_b, lse, dlt,
                             side=_exchange_side([pair[n][1] for n in REDUCE_BEHIND_DQ]))
    from_chips.update(zip(REDUCE_BEHIND_DQ, got))
    (gk_m, gv_m), got = _flash_dkv(q_m, k_m, v_m, dmo_b, lse, dlt,
                                   side=_exchange_side([pair[n][1] for n in REDUCE_BEHIND_DKV]))
    from_chips.update(zip(REDUCE_BEHIND_DKV, got))
    dq_raw = _rope(gq_m.reshape(MH * S, MLA_QK_PAD), mc, -msa, -msb, MLA_ROPE // 2, a_scale, F32,
                   name="mla_rope_q_bwd")
    dk_raw = _rope(gk_m.reshape(MH * S, MLA_QK_PAD), mc, -msa, -msb, MLA_ROPE // 2, 1.0, F32, name="mla_rope_k_bwd")
    dQ = dq_raw.reshape(MH, S, MLA_QK_PAD)[:, :, :cfg.QK].transpose(1, 0, 2).reshape(S, MH * cfg.QK).astype(BF16)
    dk3 = dk_raw.reshape(MH, S, MLA_QK_PAD)
    dKV = jnp.concatenate([dk3[:, :, :MLA_NOPE], gv_m], 2).transpose(1, 0, 2).reshape(S, MH * cfg.KVH).astype(BF16)
    dkr_heads = [(dk3[h, :, MLA_NOPE:cfg.QK], 'row') for h in range(MH)]
    dkr = _rowwise(lambda *v: [functools.reduce(lambda s, t: s + t, v)], dkr_heads, [(MLA_ROPE, F32)],
                   name="mla_rope_k_heads")[0]

    gb['w_uq'] = _mm_tn(cqn, dQ, N_CHIP, name="q_up_dw")
    dcqn = _mm_nt(dQ, W['w_uq'], name="q_up_dx")
    dcq, _, gs['g_q_lora'] = _rms_bwd(cq, rcq, dcqn, sp['g_q_lora'], None, name="rms_q_bwd")
    gb['w_ukv'] = _mm_tn(ckvn, dKV, N_CHIP, name="kv_up_dw")
    dckvn = _mm_nt(dKV, W['w_ukv'], name="kv_up_dx")
    dckv, _, gs['g_kv_lora'] = _rms_bwd(ckv, rckv, dckvn, sp['g_kv_lora'], None, name="rms_kv_bwd")

    tail = jnp.zeros((S, in_cols - cfg.IN), F32)
    dproj = jnp.concatenate([drq, drk, drv, drg, dcq, dckv, dkr, tail], axis=1).astype(BF16)

    def per_chip(g):
        g = g[0, :, :cfg.IN].reshape(D, N_CHIP, cfg.INs).transpose(1, 0, 2)
        return jnp.pad(g, ((0, 0), (0, 0), (0, cfg.INp - cfg.INs)))

    gb['w_in'] = [per_chip(g) for g in _mm_tn(hn1, dproj, 1, name="in_proj_dw")]
    pair.update(pair_sums(REDUCE_LAST, _pair_exchange(wire(REDUCE_LAST))))
    dhn1, got = _mm_nt(dproj, w_in, name="in_proj_dx", side=_exchange_side([pair[n][1] for n in REDUCE_LAST]))
    from_chips.update(zip(REDUCE_LAST, got))
    dx, _, gs['g_attn'] = _rms_bwd(x, r1, dhn1, sp['g_attn'], dh1, name="rms_attn_bwd")
    return loss, dx, {n: (pair[n][0], from_chips[n]) for n in BIG}, gs


def _padded_shape(name, shape):
    K, n = shape
    if name in COL_SHARDED:
        return K, _round_up(n, LANE)
    return _round_up(K, LANE), n


def _pad_shard(name, w):
    K, n = _padded_shape(name, w.shape)
    return jnp.pad(w.astype(BF16), ((0, K - w.shape[0]), (0, n - w.shape[1])))


def _as_operand(name, gathered):
    if name in COL_SHARDED:
        return gathered
    return gathered.reshape(1, gathered.shape[0] * gathered.shape[1], gathered.shape[2])


def _grad_pieces(name, g):
    if name in COL_SHARDED:
        return g
    return g.reshape(N_CHIP, g.shape[1] // N_CHIP, g.shape[2])


def _channels_padded(v, cfg):
    r = v.shape[0]
    return jnp.pad(v.reshape(r, N_CHIP, cfg.Fs), ((0, 0), (0, 0), (0, cfg.Fp - cfg.Fs))).reshape(r, cfg.F4)


def _channels_unpadded(v, cfg):
    r = v.shape[0]
    return v.reshape(r, N_CHIP, cfg.Fp)[:, :, :cfg.Fs].reshape(r, cfg.F)


def _mesh_pos():
    return lax.axis_index("x"), lax.axis_index("y"), lax.axis_index("c")


def _other_chips(x, y):
    return [(1 - x, y), (x, 1 - y), (1 - x, 1 - y)]


def _chip_id(cx, cy):
    return 2 * cx + cy


def _half(ref_rows, core):
    half = ref_rows // 2
    return pl.ds(core * half, half)


def _gather_side(shards):
    n = len(shards)

    def copies(srcs, outs, sems):
        ici_send, ici_recv, d2d_send, d2d_recv, own_send, own_recv = sems
        x, y, c = _mesh_pos()
        mine = _chip_id(x, y)
        others = _other_chips(x, y)

        def over_ici(w, j, chip):
            rows = _half(srcs[w].shape[0], c)
            return pltpu.make_async_remote_copy(
                src_ref=srcs[w].at[rows], dst_ref=outs[w].at[chip, rows],
                send_sem=ici_send.at[w, j], recv_sem=ici_recv.at[w, j],
                device_id=(*others[j], c), device_id_type=MESH)

        def over_d2d(w, j, core):
            rows = _half(srcs[w].shape[0], core)
            slab = outs[w].at[_chip_id(*others[j]), rows]
            return pltpu.make_async_remote_copy(
                src_ref=slab, dst_ref=slab, send_sem=d2d_send.at[w, j], recv_sem=d2d_recv.at[w, j],
                device_id=(x, y, 1 - c), device_id_type=MESH)

        def own(w):
            return pltpu.make_async_remote_copy(
                src_ref=srcs[w], dst_ref=outs[w].at[mine], send_sem=own_send.at[w], recv_sem=own_recv.at[w],
                device_id=(x, y, 1 - c), device_id_type=MESH)

        return c, mine, others, over_ici, over_d2d, own

    def start(srcs, outs, sems):
        c, mine, others, over_ici, over_d2d, own = copies(srcs, outs, sems)
        for w in range(n):
            for j in range(3):
                over_ici(w, j, mine).start()
        for w in range(n):
            own(w).start()

    def finish(srcs, outs, sems):
        c, mine, others, over_ici, over_d2d, own = copies(srcs, outs, sems)
        for w in range(n):
            for j in range(3):
                over_ici(w, j, _chip_id(*others[j])).wait_recv()
                over_d2d(w, j, c).start()
        for w in range(n):
            for j in range(3):
                over_d2d(w, j, 1 - c).wait_recv()
        for w in range(n):
            own(w).wait()
            for j in range(3):
                over_ici(w, j, mine).wait_send()
                over_d2d(w, j, c).wait_send()

    out_shape = [jax.ShapeDtypeStruct((N_CHIP,) + s.shape, s.dtype) for s in shards]
    sems = [pltpu.SemaphoreType.DMA((n, 3))] * 4 + [pltpu.SemaphoreType.DMA((n,))] * 2
    return _Side(shards, out_shape, sems, start, finish)


def _gather_weights(shards):
    return _run_side(_gather_side(shards), "gather_weights")


def _pair_side(grads):
    n = len(grads)

    def copies(srcs, outs, sems):
        send, recv = sems
        x, y, c = _mesh_pos()
        return [pltpu.make_async_remote_copy(
            src_ref=srcs[w].at[:, _half(srcs[w].shape[1], 1 - c), :], dst_ref=outs[w],
            send_sem=send.at[w], recv_sem=recv.at[w],
            device_id=(x, y, 1 - c), device_id_type=MESH) for w in range(n)]

    def start(srcs, outs, sems):
        for cp in copies(srcs, outs, sems):
            cp.start()

    def finish(srcs, outs, sems):
        for cp in copies(srcs, outs, sems):
            cp.wait()

    out_shape = [jax.ShapeDtypeStruct((g.shape[0], g.shape[1] // 2, g.shape[2]), g.dtype) for g in grads]
    return _Side(grads, out_shape, [pltpu.SemaphoreType.DMA((n,))] * 2, start, finish)


def _pair_exchange(grads):
    return _run_side(_pair_side(grads), "grad_pair_exchange")


def _exchange_side(parts):
    n = len(parts)

    def copies(srcs, outs, sems):
        send, recv = sems
        x, y, c = _mesh_pos()
        others = _other_chips(x, y)
        return [pltpu.make_async_remote_copy(
            src_ref=srcs[w].at[_chip_id(*others[j])], dst_ref=outs[w].at[j],
            send_sem=send.at[w, j], recv_sem=recv.at[w, j],
            device_id=(*others[j], c), device_id_type=MESH) for w in range(n) for j in range(3)]

    def start(srcs, outs, sems):
        for cp in copies(srcs, outs, sems):
            cp.start()

    def finish(srcs, outs, sems):
        for cp in copies(srcs, outs, sems):
            cp.wait()

    out_shape = [jax.ShapeDtypeStruct((3,) + p.shape[1:], p.dtype) for p in parts]
    return _Side(parts, out_shape, [pltpu.SemaphoreType.DMA((n, 3))] * 2, start, finish)


def _sibling_share(shards):
    n = len(shards)

    def body(*refs):
        outs = refs[n:2 * n]
        send, recv = refs[2 * n:]
        x, y, c = _mesh_pos()

        def half_of(w, core):
            rows = outs[w].at[_half(outs[w].shape[0], core)]
            return pltpu.make_async_remote_copy(
                src_ref=rows, dst_ref=rows, send_sem=send.at[w], recv_sem=recv.at[w],
                device_id=(x, y, 1 - c), device_id_type=MESH)

        for w in range(n):
            half_of(w, c).start()
        for w in range(n):
            half_of(w, 1 - c).wait_recv()
        for w in range(n):
            half_of(w, c).wait_send()

    return pl.pallas_call(
        body, name="grad_sibling_share", in_specs=[HBM] * n, out_specs=[HBM] * n,
        out_shape=[jax.ShapeDtypeStruct(s.shape, s.dtype) for s in shards],
        scratch_shapes=[pltpu.SemaphoreType.DMA((n,))] * 2, input_output_aliases={w: w for w in range(n)},
    )(*shards)


N_DEV = 8


def _gather_small(v):
    r, width = v.shape

    def body(v_ref, out_ref, send_sems, recv_sems, local_sem):
        x, y, c = _mesh_pos()
        me, sibling = (x, y, c), (x, y, 1 - c)
        chips = _other_chips(x, y)

        def rows(px, py, pc):
            return out_ref.at[pl.ds((4 * px + 2 * py + pc) * r, r), :]

        def copy(k, block, to, src=None):
            return pltpu.make_async_remote_copy(
                src_ref=rows(*block) if src is None else src, dst_ref=rows(*block),
                send_sem=send_sems.at[k], recv_sem=recv_sems.at[k], device_id=to, device_id_type=MESH)

        mine = pltpu.make_async_copy(v_ref, rows(*me), local_sem)
        mine.start()
        first = [copy(0, me, sibling, src=v_ref)]
        first += [copy(1 + j, me, (*chip, c), src=v_ref) for j, chip in enumerate(chips)]
        for cp in first:
            cp.start()
        passed = [copy(4 + j, (*chip, c), sibling) for j, chip in enumerate(chips)]
        for j, chip in enumerate(chips):
            copy(1 + j, (*chip, c), me).wait_recv()
            passed[j].start()
        copy(0, sibling, me).wait_recv()
        for j, chip in enumerate(chips):
            copy(4 + j, (*chip, 1 - c), me).wait_recv()
        for cp in first + passed:
            cp.wait_send()
        mine.wait()

    vmem = pl.BlockSpec(memory_space=pltpu.VMEM)
    return pl.pallas_call(
        body, name="gather_small", out_shape=jax.ShapeDtypeStruct((N_DEV * r, width), v.dtype),
        in_specs=[vmem], out_specs=vmem,
        scratch_shapes=[pltpu.SemaphoreType.DMA((7,)), pltpu.SemaphoreType.DMA((7,)), pltpu.SemaphoreType.DMA],
    )(v)


def _pack(arrays):
    flat = jnp.concatenate([a.reshape(-1) for a in arrays])
    size = _round_up(flat.shape[0], 8 * LANE)
    return jnp.pad(flat, (0, size - flat.shape[0])).reshape(size // LANE, LANE)


def _unpack(packed, shapes):
    flat = packed.reshape(-1)
    out, at = [], 0
    for s in shapes:
        size = math.prod(s)
        out.append(flat[at:at + size].reshape(s))
        at += size
    return out


def _sum_devices(gathered):
    r = gathered.shape[0] // N_DEV
    blocks = [(gathered[d * r:(d + 1) * r], 'row') for d in range(N_DEV)]
    return _rowwise(lambda *v: [functools.reduce(lambda s, t: s + t, v)], blocks, [(LANE, F32)],
                    name="small_grad_sum")[0]


def _reduce_tiles(half, n):
    return _tile(half, 128, 8)


def _pair_add(g32, r1):
    G, K, n = g32.shape
    half = K // 2
    tr = _reduce_tiles(half, n)
    nrt = half // tr

    def body(g_ref, r_ref, s32_ref, sb_ref):
        s = g_ref[...] + r_ref[...].astype(F32)
        s32_ref[...] = s
        sb_ref[...] = s.astype(BF16)

    blk = pl.BlockSpec((None, tr, n), lambda k, i: (k, i, 0))
    return pl.pallas_call(
        body, name="grad_pair_add", grid=(G, nrt),
        in_specs=[pl.BlockSpec((None, tr, n), lambda k, i: (k, lax.axis_index("c") * nrt + i, 0)), blk],
        out_specs=[blk, blk],
        out_shape=[jax.ShapeDtypeStruct((G, half, n), F32), jax.ShapeDtypeStruct((G, half, n), BF16)],
        compiler_params=_params(("parallel", "parallel")),
    )(g32, r1)


def _chip_add(s32, r2):
    _, half, n = s32.shape
    tr = _reduce_tiles(half, n)
    nrt = half // tr

    def body(s_ref, a_ref, b_ref, c_ref, o_ref):
        o_ref[...] = ((s_ref[...] + a_ref[...].astype(F32)) + b_ref[...].astype(F32)) + c_ref[...].astype(F32)

    def piece(j):
        return pl.BlockSpec((None, tr, n), lambda i, j=j: (j, i, 0))

    def mine(i):
        return _chip_id(lax.axis_index("x"), lax.axis_index("y")), i, 0

    return pl.pallas_call(
        body, name="grad_chip_add", grid=(nrt,),
        in_specs=[pl.BlockSpec((None, tr, n), mine), piece(0), piece(1), piece(2)],
        out_specs=pl.BlockSpec((tr, n), lambda i: (lax.axis_index("c") * nrt + i, 0)),
        out_shape=jax.ShapeDtypeStruct((2 * half, n), F32),
        compiler_params=_params(("parallel",)),
    )(s32, r2, r2, r2)


def _adamw_math(wv, gv, mv, vv):
    m2 = ADAM_B1 * mv + (1.0 - ADAM_B1) * gv
    v2 = ADAM_B2 * vv + (1.0 - ADAM_B2) * (gv * gv)
    m_hat = m2 / (1.0 - ADAM_B1 ** ADAM_STEP)
    v_hat = v2 / (1.0 - ADAM_B2 ** ADAM_STEP)
    delta = -ADAM_LR * (m_hat / (jnp.sqrt(v_hat) + ADAM_EPS) + ADAM_WD * wv)
    return [delta, m2, v2]


def _adamw(w, g, m, v, *, name):
    width = w.shape[1]
    return _rowwise(_adamw_math, [(w, 'row'), (g, 'row'), (m, 'row'), (v, 'row')], [(width, F32)] * 3, name=name)


def _adamw_sharded(w, g_padded, m, v, *, name):
    _, K, n = w.shape
    n_pad = g_padded.shape[1]
    ts = 8
    while ts * 2 <= 256 and ts * 2 * 8 * n_pad * 4 <= ROWWISE_BLOCK_BYTES and K % (ts * 2) == 0:
        ts *= 2

    def body(w_ref, g_ref, m_ref, v_ref, go_ref, d_ref, mo_ref, vo_ref):
        gv = g_ref[:, :n] if n != n_pad else g_ref[...]
        go_ref[...] = gv
        d_ref[...], mo_ref[...], vo_ref[...] = _adamw_math(w_ref[...], gv, m_ref[...], v_ref[...])

    blk = pl.BlockSpec((None, ts, n), lambda i: (0, i, 0))
    out = jax.ShapeDtypeStruct((1, K, n), F32)
    return pl.pallas_call(
        body, name=name, grid=(K // ts,),
        in_specs=[blk, pl.BlockSpec((ts, n_pad), lambda i: (i, 0)), blk, blk],
        out_specs=[blk] * 4, out_shape=[out] * 4,
        compiler_params=_params(("parallel",)),
    )(w, g_padded, m, v)


def kernel(x, p, w_in, g_attn, g_q_lora, g_kv_lora, w_uq, w_ukv, w_o, g_ffn, w_ffn_gate, w_ffn_up, conv_w, conv_b, w_ffn_down, g_ple, w_ple_gate, w_ple_proj, g_final, loss_target, m_w_in, m_g_attn, m_g_q_lora, m_g_kv_lora, m_w_uq, m_w_ukv, m_w_o, m_g_ffn, m_w_ffn_gate, m_w_ffn_up, m_conv_w, m_conv_b, m_w_ffn_down, m_g_ple, m_w_ple_gate, m_w_ple_proj, m_g_final, v_w_in, v_g_attn, v_g_q_lora, v_g_kv_lora, v_w_uq, v_w_ukv, v_w_o, v_g_ffn, v_w_ffn_gate, v_w_ffn_up, v_conv_w, v_conv_b, v_w_ffn_down, v_g_ple, v_w_ple_gate, v_w_ple_proj, v_g_final):
    weights = dict(w_in=w_in, g_attn=g_attn, g_q_lora=g_q_lora, g_kv_lora=g_kv_lora, w_uq=w_uq, w_ukv=w_ukv, w_o=w_o,
                   g_ffn=g_ffn, w_ffn_gate=w_ffn_gate, w_ffn_up=w_ffn_up, conv_w=conv_w, conv_b=conv_b,
                   w_ffn_down=w_ffn_down, g_ple=g_ple, w_ple_gate=w_ple_gate, w_ple_proj=w_ple_proj, g_final=g_final)
    mom1 = dict(w_in=m_w_in, g_attn=m_g_attn, g_q_lora=m_g_q_lora, g_kv_lora=m_g_kv_lora, w_uq=m_w_uq, w_ukv=m_w_ukv,
                w_o=m_w_o, g_ffn=m_g_ffn, w_ffn_gate=m_w_ffn_gate, w_ffn_up=m_w_ffn_up, conv_w=m_conv_w,
                conv_b=m_conv_b, w_ffn_down=m_w_ffn_down, g_ple=m_g_ple, w_ple_gate=m_w_ple_gate,
                w_ple_proj=m_w_ple_proj, g_final=m_g_final)
    mom2 = dict(w_in=v_w_in, g_attn=v_g_attn, g_q_lora=v_g_q_lora, g_kv_lora=v_g_kv_lora, w_uq=v_w_uq, w_ukv=v_w_ukv,
                w_o=v_w_o, g_ffn=v_g_ffn, w_ffn_gate=v_w_ffn_gate, w_ffn_up=v_w_ffn_up, conv_w=v_conv_w,
                conv_b=v_conv_b, w_ffn_down=v_w_ffn_down, g_ple=v_g_ple, w_ple_gate=v_w_ple_gate,
                w_ple_proj=v_w_ple_proj, g_final=v_g_final)
    _, S, D = x.shape
    cfg = _Cfg(S, D)
    cx, cy, _ = _mesh_pos()

    shards = {name: _pad_shard(name, weights[name][0]) for name in BIG}
    gathered = _gather_weights([shards[name] for name in GATHER_FIRST])
    W = {name: _as_operand(name, g) for name, g in zip(GATHER_FIRST, gathered)}
    cw_all = _gather_small(_pack([jnp.pad(conv_w[0], ((0, 0), (0, cfg.Fp - cfg.Fs)))]))
    r_cw = cw_all.shape[0] // N_DEV
    cw_chips = [_unpack(cw_all[2 * k * r_cw:(2 * k + 1) * r_cw], [(CONV_WIDTH, cfg.Fp)])[0] for k in range(N_CHIP)]
    sp = dict(g_attn=g_attn, g_q_lora=g_q_lora, g_kv_lora=g_kv_lora, g_ffn=g_ffn, g_ple=g_ple,
              g_final=g_final.reshape(1, D), conv_w=jnp.concatenate(cw_chips, axis=1),
              conv_b=_channels_padded(conv_b, cfg))

    loss, dx, parts, gs = _local_step(cfg, x[0], p[0, 0], loss_target[0], W, shards, sp)
    loss = lax.psum(loss, ("x", "y", "c"))

    halves = [_chip_add(*parts[name]) for name in BIG]
    whole = dict(zip(BIG, _sibling_share(halves)))
    grads = {}

    small_names = ['g_attn', 'g_q_lora', 'g_kv_lora', 'g_ffn', 'g_ple', 'g_final', 'conv_b', 'conv_w']
    small_sum = _sum_devices(_gather_small(_pack([gs[name] for name in small_names])))
    for name, g in zip(small_names, _unpack(small_sum, [gs[name].shape for name in small_names])):
        grads[name] = g
    grads['g_final'] = grads['g_final'].reshape(D)
    grads['conv_b'] = _channels_unpadded(grads['conv_b'], cfg)
    mine = _chip_id(cx, cy)
    grads['conv_w'] = lax.dynamic_slice_in_dim(grads['conv_w'], mine * cfg.Fp, cfg.Fp, axis=1)[:, :cfg.Fs]
    grads['conv_w'] = grads['conv_w'].reshape(1, CONV_WIDTH, cfg.Fs)

    delta, new_m, new_v = {}, {}, {}
    for name in BIG:
        operands = (weights[name], whole[name], mom1[name], mom2[name])
        flip = weights[name].shape[2] % LANE != 0
        if flip:
            operands = tuple(a.T if a.ndim == 2 else a.transpose(0, 2, 1) for a in operands)
        outs = _adamw_sharded(*operands, name="adamw_" + name)
        if flip:
            outs = [o.transpose(0, 2, 1) for o in outs]
        grads[name], delta[name], new_m[name], new_v[name] = outs
    for name in WEIGHTS:
        if name in BIG:
            continue
        shape = weights[name].shape
        flat = (shape[-2], shape[-1]) if len(shape) == 3 else (1, shape[-1])
        d, m2, v2 = _adamw(weights[name].reshape(flat), grads[name].reshape(flat), mom1[name].reshape(flat),
                           mom2[name].reshape(flat), name="adamw_" + name)
        delta[name], new_m[name], new_v[name] = d.reshape(shape), m2.reshape(shape), v2.reshape(shape)

    return (loss, dx.reshape(1, S, D), *[grads[n] for n in WEIGHTS], *[delta[n] for n in WEIGHTS],
            *[new_m[n] for n in WEIGHTS], *[new_v[n] for n in WEIGHTS])
```

```python
import functools
import math

import jax
import jax.numpy as jnp
from jax import lax
from jax.experimental import pallas as pl
from jax.experimental.pallas import tpu as pltpu

F32 = jnp.float32
BF16 = jnp.bfloat16

LANE = 128
VMEM_LIMIT = 56 * 1024 * 1024
ROWWISE_BLOCK_BYTES = 5 * 1024 * 1024
ROWWISE_MAX_ROWS = 1024
MM_TILE = 1024
MM_TILE_N = 1408
MM_TILE_CONTRACT = 2816
MM_TILE_CONTRACT_N = 2816

N_CHIP = 4
MESH = pl.DeviceIdType.MESH

CHUNK = 64
RET_HEADS = 8
MLA_HEADS = 16
MLA_NOPE = 128
MLA_ROPE = 64
MLA_QK_PAD = 256
PLE_DIM = 256
CONV_WIDTH = 3
ROPE_BASE = 10000.0
EPS = 1e-6
RET_BLOCK = 256

ADAM_LR = 0.001
ADAM_B1 = 0.9
ADAM_B2 = 0.999
ADAM_EPS = 1e-08
ADAM_WD = 0.01
ADAM_STEP = 10

WEIGHTS = ['w_in', 'g_attn', 'g_q_lora', 'g_kv_lora', 'w_uq', 'w_ukv', 'w_o', 'g_ffn', 'w_ffn_gate', 'w_ffn_up',
           'conv_w', 'conv_b', 'w_ffn_down', 'g_ple', 'w_ple_gate', 'w_ple_proj', 'g_final']
COL_SHARDED = ('w_in', 'w_uq', 'w_ukv', 'w_ffn_gate', 'w_ffn_up', 'w_ple_proj')
ROW_SHARDED = ('w_o', 'w_ffn_down', 'w_ple_gate')
BIG = COL_SHARDED + ROW_SHARDED
SMALL_REPLICATED = ('g_attn', 'g_q_lora', 'g_kv_lora', 'g_ffn', 'conv_b', 'g_ple', 'g_final')


def _round_up(n, m):
    return (n + m - 1) // m * m


def _tile(dim, cap, align=LANE):
    if dim <= cap:
        return dim
    t = cap // align * align
    while t >= align:
        if dim % t == 0:
            return t
        t -= align
    return dim


def _params(sem):
    return pltpu.CompilerParams(dimension_semantics=sem, vmem_limit_bytes=VMEM_LIMIT)


HBM = pl.BlockSpec(memory_space=pltpu.HBM)


class _Side:
    def __init__(self, ins, out_shape, sems, start, finish):
        self.ins, self.out_shape, self.sems, self.start, self.finish = ins, out_shape, sems, start, finish


def _run_side(side, name):
    n_in, n_out = len(side.ins), len(side.out_shape)

    def body(*refs):
        ins, outs, sems = refs[:n_in], refs[n_in:n_in + n_out], refs[n_in + n_out:]
        side.start(ins, outs, sems)
        side.finish(ins, outs, sems)

    return pl.pallas_call(
        body, name=name, in_specs=[HBM] * n_in, out_specs=[HBM] * n_out, out_shape=side.out_shape,
        scratch_shapes=side.sems,
    )(*side.ins)


def _hosted_call(body, side, *, name, grid, in_specs, out_specs, out_shape, scratch_shapes, operands, semantics):
    if side is None:
        res = pl.pallas_call(
            body, name=name, grid=grid, in_specs=in_specs, out_specs=out_specs, out_shape=out_shape,
            scratch_shapes=scratch_shapes, compiler_params=_params(semantics))(*operands)
        return res, []
    n_in, n_out, n_scr = len(in_specs), len(out_specs), len(scratch_shapes)
    s_in, s_out = len(side.ins), len(side.out_shape)

    def hosted(*refs):
        refs = list(refs)
        ins, refs = refs[:n_in], refs[n_in:]
        side_ins, refs = refs[:s_in], refs[s_in:]
        outs, refs = refs[:n_out], refs[n_out:]
        side_outs, refs = refs[:s_out], refs[s_out:]
        scratch, sems = refs[:n_scr], refs[n_scr:]
        ids = [pl.program_id(a) for a in range(len(grid))]
        first = functools.reduce(jnp.logical_and, [i == 0 for i in ids])
        last = functools.reduce(jnp.logical_and, [i == g - 1 for i, g in zip(ids, grid)])

        @pl.when(first)
        def _():
            side.start(side_ins, side_outs, sems)

        body(*ins, *outs, *scratch)

        @pl.when(last)
        def _():
            side.finish(side_ins, side_outs, sems)

    res = pl.pallas_call(
        hosted, name=name, grid=grid, in_specs=list(in_specs) + [HBM] * s_in,
        out_specs=list(out_specs) + [HBM] * s_out, out_shape=list(out_shape) + list(side.out_shape),
        scratch_shapes=list(scratch_shapes) + list(side.sems),
        compiler_params=_params(("arbitrary",) * len(grid)))(*operands, *side.ins)
    return res[:n_out], res[n_out:]


def _mm_call(name, dims, grid, in_specs, out_specs, out_shape, acc_shape, operands, has_res, side=None):
    nsteps = grid[2]
    n_out = len(out_shape)

    def body(*refs):
        a_ref, b_ref = refs[0], refs[1]
        res_ref = refs[2] if has_res else None
        outs = refs[2 + has_res:2 + has_res + n_out]
        acc = refs[2 + has_res + n_out]
        k = pl.program_id(2)

        @pl.when(k == 0)
        def _():
            acc[...] = jnp.zeros_like(acc)

        acc[...] += lax.dot_general(a_ref[...], b_ref[...], (dims, ((), ())), preferred_element_type=F32)

        @pl.when(k == nsteps - 1)
        def _():
            r = acc[...]
            if has_res:
                r = r + res_ref[...]
            for o in outs:
                o[...] = r.astype(o.dtype)

    outs, side_outs = _hosted_call(
        body, side, name=name, grid=grid, in_specs=in_specs, out_specs=out_specs, out_shape=out_shape,
        scratch_shapes=[pltpu.VMEM(acc_shape, F32)], operands=operands,
        semantics=("parallel", "parallel", "arbitrary"))
    return (outs, side_outs) if side is not None else outs


def _mm_nn(a, w, *, name, res=None, out_dtype=F32, side=None):
    M, K = a.shape
    G, _, n = w.shape
    tm, tn, tk = _tile(M, MM_TILE), _tile(n, MM_TILE_N), _tile(K, MM_TILE_CONTRACT)
    npg = n // tn
    grid = (M // tm, G * npg, K // tk)
    in_specs = [pl.BlockSpec((tm, tk), lambda i, j, k: (i, k)),
                pl.BlockSpec((None, tk, tn), lambda i, j, k: (j // npg, k, j % npg))]
    operands = [a, w]
    if res is not None:
        in_specs.append(pl.BlockSpec((tm, tn), lambda i, j, k: (i, j)))
        operands.append(res)
    out_specs = [pl.BlockSpec((tm, tn), lambda i, j, k: (i, j))]
    out_shape = [jax.ShapeDtypeStruct((M, G * n), out_dtype)]
    got = _mm_call(name, ((1,), (0,)), grid, in_specs, out_specs, out_shape, (tm, tn), operands, res is not None, side)
    return (got[0][0], got[1]) if side is not None else got[0]


def _mm_nt(g, w, *, name, res=None, out_dtype=F32, side=None):
    M, _ = g.shape
    G, K, n = w.shape
    tm, tko, tn = _tile(M, MM_TILE), _tile(K, MM_TILE), _tile(n, MM_TILE_CONTRACT_N)
    npg = n // tn
    grid = (M // tm, K // tko, G * npg)
    in_specs = [pl.BlockSpec((tm, tn), lambda i, j, k: (i, k)),
                pl.BlockSpec((None, tko, tn), lambda i, j, k: (k // npg, j, k % npg))]
    operands = [g, w]
    if res is not None:
        in_specs.append(pl.BlockSpec((tm, tko), lambda i, j, k: (i, j)))
        operands.append(res)
    out_specs = [pl.BlockSpec((tm, tko), lambda i, j, k: (i, j))]
    out_shape = [jax.ShapeDtypeStruct((M, K), out_dtype)]
    got = _mm_call(name, ((1,), (1,)), grid, in_specs, out_specs, out_shape, (tm, tko), operands, res is not None, side)
    return (got[0][0], got[1]) if side is not None else got[0]


def _mm_tn(a, g, groups, *, name):
    M, K = a.shape
    n = g.shape[1] // groups
    tm, tko, tn = _tile(M, MM_TILE_CONTRACT), _tile(K, MM_TILE), _tile(n, MM_TILE_N)
    npg = n // tn
    grid = (K // tko, groups * npg, M // tm)
    in_specs = [pl.BlockSpec((tm, tko), lambda i, j, k: (k, i)),
                pl.BlockSpec((tm, tn), lambda i, j, k: (k, j))]
    out_spec = pl.BlockSpec((None, tko, tn), lambda i, j, k: (j // npg, i, j % npg))
    out_shape = [jax.ShapeDtypeStruct((groups, K, n), F32), jax.ShapeDtypeStruct((groups, K, n), BF16)]
    return _mm_call(name, ((0,), (0,)), grid, in_specs, [out_spec, out_spec], out_shape, (tko, tn), [a, g], False)


def _rowwise(fn, ins, outs, accs=(), *, name):
    R = next(a.shape[0] for a, kind in ins if kind == 'row')
    row_bytes = sum(a.shape[1] * a.dtype.itemsize for a, kind in ins if kind != 'bcast')
    row_bytes += sum(w * jnp.dtype(dt).itemsize for w, dt in outs)
    ts = 8
    while ts * 2 <= ROWWISE_MAX_ROWS and ts * 2 * row_bytes <= ROWWISE_BLOCK_BYTES:
        ts *= 2
    ts = min(ts, R)
    for a, kind in ins:
        if kind == 'per':
            ts = math.gcd(ts, a.shape[0])
    while R % ts:
        ts //= 2
    in_specs = []
    for a, kind in ins:
        w = a.shape[1]
        if kind == 'row':
            in_specs.append(pl.BlockSpec((ts, w), lambda i: (i, 0)))
        elif kind == 'bcast':
            in_specs.append(pl.BlockSpec((1, w), lambda i: (0, 0)))
        else:
            nper = a.shape[0] // ts
            in_specs.append(pl.BlockSpec((ts, w), lambda i, nper=nper: (i % nper, 0)))
    out_specs = [pl.BlockSpec((ts, w), lambda i: (i, 0)) for w, _ in outs]
    out_specs += [pl.BlockSpec((1, w), lambda i: (0, 0)) for w in accs]
    out_shape = [jax.ShapeDtypeStruct((R, w), dt) for w, dt in outs]
    out_shape += [jax.ShapeDtypeStruct((1, w), F32) for w in accs]
    n_in, n_out = len(ins), len(outs)

    def body(*refs):
        vals = [r[...] for r in refs[:n_in]]
        res = fn(*vals)
        for r, v in zip(refs[n_in:n_in + n_out], res[:n_out]):
            r[...] = v.astype(r.dtype)
        if accs:
            first = pl.program_id(0) == 0
            for r, v in zip(refs[n_in + n_out:], res[n_out:]):
                @pl.when(first)
                def _(r=r, v=v):
                    r[...] = v

                @pl.when(jnp.logical_not(first))
                def _(r=r, v=v):
                    r[...] += v

    return pl.pallas_call(
        body, name=name, grid=(R // ts,), in_specs=in_specs, out_specs=out_specs, out_shape=out_shape,
        compiler_params=_params(("arbitrary",) if accs else ("parallel",)),
    )(*[a for a, _ in ins])


def _mean(v):
    return jnp.mean(v, axis=-1, keepdims=True)


def _colsum(v):
    return jnp.sum(v, axis=0, keepdims=True)


def _sigmoid(v):
    return 1.0 / (1.0 + jnp.exp(-v))


def _rms_fwd(x, g, *, name):
    def fn(xv, gv):
        r = lax.rsqrt(_mean(xv * xv) + EPS)
        return [xv * r * gv, r]
    w = x.shape[1]
    return _rowwise(fn, [(x, 'row'), (g, 'bcast')], [(w, BF16), (1, F32)], name=name)


def _rms_bwd(x, r, dhn, g, dres, *, name):
    def fn(xv, rv, dv, gv, *rest):
        n = xv * rv
        dn = dv * gv
        dx = rv * (dn - n * _mean(dn * n))
        if rest:
            dx = dx + rest[0]
        return [dx, dx, _colsum(dv * n)]
    w = x.shape[1]
    ins = [(x, 'row'), (r, 'row'), (dhn, 'row'), (g, 'bcast')]
    if dres is not None:
        ins.append((dres, 'row'))
    return _rowwise(fn, ins, [(w, F32), (w, BF16)], [w], name=name)


def _rope(x, cos, sin_a, sin_b, shift, scale, out_dtype, *, name):
    w = x.shape[1]
    wr = cos.shape[1]
    lo = w - wr

    def fn(xv, cv, sav, sbv):
        xr = xv[:, lo:] if lo else xv
        y = xr * cv + pltpu.roll(xr, wr - shift, 1) * sav + pltpu.roll(xr, shift, 1) * sbv
        if lo:
            y = jnp.concatenate([xv[:, :lo], y], axis=1)
        return [y * scale]
    return _rowwise(fn, [(x, 'row'), (cos, 'per'), (sin_a, 'per'), (sin_b, 'per')], [(w, out_dtype)], name=name)[0]


def _dot(a, b, ca, cb):
    return lax.dot_general(a, b, (((ca,), (cb,)), ((), ())), preferred_element_type=F32)


def _ret_fwd(q, k, v, dec, dq_, dk_, ds_):
    H, S, d = q.shape
    T = dec.shape[1]
    nC = S // T

    def body(q_ref, k_ref, v_ref, m_ref, qd_ref, kd_ref, sd_ref, o_ref, st_ref, state):
        @pl.when(pl.program_id(1) == 0)
        def _():
            state[...] = jnp.zeros_like(state)

        st = state[...]
        st_ref[...] = st
        qv, kv, vv = q_ref[...], k_ref[...], v_ref[...]
        p = (_dot(qv, kv, 1, 1) * m_ref[...]).astype(BF16)
        qs = (qv.astype(F32) * qd_ref[...]).astype(BF16)
        o_ref[...] = _dot(p, vv, 1, 0) + _dot(qs, st.astype(BF16), 1, 0)
        ks = (kv.astype(F32) * kd_ref[...]).astype(BF16)
        state[...] = st * sd_ref[...] + _dot(ks, vv, 0, 0)

    blk = pl.BlockSpec((None, T, d), lambda h, c: (h, c, 0))
    return pl.pallas_call(
        body, name="ret_fwd", grid=(H, nC),
        in_specs=[blk, blk, blk,
                  pl.BlockSpec((None, T, T), lambda h, c: (h, 0, 0)),
                  pl.BlockSpec((None, T, 1), lambda h, c: (h, 0, 0)),
                  pl.BlockSpec((None, T, 1), lambda h, c: (h, 0, 0)),
                  pl.BlockSpec((None, 1, 1), lambda h, c: (h, 0, 0))],
        out_specs=[blk, pl.BlockSpec((None, None, d, d), lambda h, c: (h, c, 0, 0))],
        out_shape=[jax.ShapeDtypeStruct((H, S, d), F32), jax.ShapeDtypeStruct((H, nC, d, d), F32)],
        scratch_shapes=[pltpu.VMEM((d, d), F32)],
        compiler_params=_params(("parallel", "arbitrary")),
    )(q, k, v, dec, dq_, dk_, ds_)


def _ret_bwd(q, k, v, do, states, dec, dq_, dk_, ds_):
    H, S, d = q.shape
    T = dec.shape[1]
    nC = S // T

    def body(q_ref, k_ref, v_ref, do_ref, st_ref, m_ref, qd_ref, kd_ref, sd_ref, gq_ref, gk_ref, gv_ref, dstate):
        @pl.when(pl.program_id(1) == 0)
        def _():
            dstate[...] = jnp.zeros_like(dstate)

        qv, kv, vv, dov = q_ref[...], k_ref[...], v_ref[...], do_ref[...]
        m = m_ref[...]
        qd, kd = qd_ref[...], kd_ref[...]
        ds = dstate[...]
        dsb = ds.astype(BF16)
        sb = st_ref[...].astype(BF16)
        p = (_dot(qv, kv, 1, 1) * m).astype(BF16)
        da = (_dot(dov, vv, 1, 1) * m).astype(BF16)
        qs = (qv.astype(F32) * qd).astype(BF16)
        ks = (kv.astype(F32) * kd).astype(BF16)
        gq_ref[...] = _dot(da, kv, 1, 0) + _dot(dov, sb, 1, 1) * qd
        gk_ref[...] = _dot(da, qv, 0, 0) + _dot(vv, dsb, 1, 1) * kd
        gv_ref[...] = _dot(p, dov, 0, 0) + _dot(ks, dsb, 1, 0)
        dstate[...] = ds * sd_ref[...] + _dot(qs, dov, 0, 0)

    blk = pl.BlockSpec((None, T, d), lambda h, c: (h, nC - 1 - c, 0))
    out = jax.ShapeDtypeStruct((H, S, d), F32)
    return pl.pallas_call(
        body, name="ret_bwd", grid=(H, nC),
        in_specs=[blk, blk, blk, blk,
                  pl.BlockSpec((None, None, d, d), lambda h, c: (h, nC - 1 - c, 0, 0)),
                  pl.BlockSpec((None, T, T), lambda h, c: (h, 0, 0)),
                  pl.BlockSpec((None, T, 1), lambda h, c: (h, 0, 0)),
                  pl.BlockSpec((None, T, 1), lambda h, c: (h, 0, 0)),
                  pl.BlockSpec((None, 1, 1), lambda h, c: (h, 0, 0))],
        out_specs=[blk, blk, blk], out_shape=[out, out, out],
        scratch_shapes=[pltpu.VMEM((d, d), F32)],
        compiler_params=_params(("parallel", "arbitrary")),
    )(q, k, v, do, states, dec, dq_, dk_, ds_)


def _ret_tables(T, d):
    h = jnp.arange(RET_HEADS, dtype=F32)
    log_g = jnp.log1p(-jnp.exp2(-5.0 - h))
    idx = jnp.arange(T, dtype=F32)
    diff = idx[:, None] - idx[None, :]
    same = (jnp.arange(T)[:, None] // CHUNK) == (jnp.arange(T)[None, :] // CHUNK)
    earlier = (jnp.arange(T)[None, :] // CHUNK) < (jnp.arange(T)[:, None] // CHUNK)
    expo = jnp.where(same, jnp.abs(diff), diff)
    dec = jnp.where(same | earlier, jnp.exp(log_g[:, None, None] * expo[None]), 0.0)
    q_dec = jnp.exp(log_g[:, None] * (idx + 1.0))[..., None]
    k_dec = jnp.exp(log_g[:, None] * (T - 1.0 - idx))[..., None]
    s_dec = jnp.exp(log_g * T)[:, None, None]
    return dec.astype(F32), q_dec, k_dec, s_dec


NEG = -1e30


ROW_GROUP = 512


def _scores(q, kv, r, diagonal):
    s = _dot(q, kv, 1, 1)
    if diagonal:
        rg, T = s.shape
        qc = (r * rg + lax.broadcasted_iota(jnp.int32, (rg, T), 0)) // CHUNK
        kc = lax.broadcasted_iota(jnp.int32, (rg, T), 1) // CHUNK
        s = jnp.where(kc <= qc, s, NEG)
    return s


def _by_query_block(p, n):
    i = sum((p >= t * (t + 1) // 2).astype(jnp.int32) for t in range(1, n))
    return i, p - i * (i + 1) // 2


def _by_key_block(p, n):
    j = sum((p >= t * n - t * (t - 1) // 2).astype(jnp.int32) for t in range(1, n))
    return j, j + p - (j * n - j * (j - 1) // 2)


def _flash_fwd(q, k, v, side=None):
    H, S, dk = q.shape
    dv = v.shape[2]
    T = _tile(S, 512)
    n = S // T
    rg = min(ROW_GROUP, T)

    def body(q_ref, k_ref, v_ref, o_ref, lse_ref, m_s, l_s, acc):
        qi, ki = _by_query_block(pl.program_id(1), n)

        @pl.when(ki == 0)
        def _():
            m_s[...] = jnp.full_like(m_s, NEG)
            l_s[...] = jnp.zeros_like(l_s)
            acc[...] = jnp.zeros_like(acc)

        def step(diagonal):
            kv, vv = k_ref[...], v_ref[...]
            for r in range(T // rg):
                rows = pl.ds(r * rg, rg)
                s = _scores(q_ref[rows, :], kv, r, diagonal)
                m_old = m_s[rows, :]
                m_new = jnp.maximum(m_old, jnp.max(s, axis=1, keepdims=True))
                p = jnp.exp(s - m_new)
                alpha = jnp.exp(m_old - m_new)
                l_s[rows, :] = alpha * l_s[rows, :] + jnp.sum(p, axis=1, keepdims=True)
                acc[rows, :] = alpha * acc[rows, :] + _dot(p.astype(BF16), vv, 1, 0)
                m_s[rows, :] = m_new

        @pl.when(ki < qi)
        def _():
            step(False)

        @pl.when(ki == qi)
        def _():
            step(True)
            o_ref[...] = acc[...] / l_s[...]
            lse_ref[...] = m_s[...] + jnp.log(l_s[...])

    q_map = lambda h, p: (h, _by_query_block(p, n)[0], 0)
    kv_map = lambda h, p: (h, _by_query_block(p, n)[1], 0)
    return _hosted_call(
        body, side, name="mla_fwd", grid=(H, n * (n + 1) // 2),
        in_specs=[pl.BlockSpec((None, T, dk), q_map), pl.BlockSpec((None, T, dk), kv_map),
                  pl.BlockSpec((None, T, dv), kv_map)],
        out_specs=[pl.BlockSpec((None, T, dv), q_map), pl.BlockSpec((None, T, 1), q_map)],
        out_shape=[jax.ShapeDtypeStruct((H, S, dv), F32), jax.ShapeDtypeStruct((H, S, 1), F32)],
        scratch_shapes=[pltpu.VMEM((T, 1), F32), pltpu.VMEM((T, 1), F32), pltpu.VMEM((T, dv), F32)],
        operands=[q, k, v], semantics=("parallel", "arbitrary"))


def _flash_dq(q, k, v, do, lse, dlt, side=None):
    H, S, dk = q.shape
    dv = v.shape[2]
    T = _tile(S, 512)
    n = S // T
    rg = min(ROW_GROUP, T)

    def body(q_ref, k_ref, v_ref, do_ref, lse_ref, dlt_ref, dq_ref, acc):
        qi, ki = _by_query_block(pl.program_id(1), n)

        @pl.when(ki == 0)
        def _():
            acc[...] = jnp.zeros_like(acc)

        def step(diagonal):
            kv, vv = k_ref[...], v_ref[...]
            for r in range(T // rg):
                rows = pl.ds(r * rg, rg)
                p = jnp.exp(_scores(q_ref[rows, :], kv, r, diagonal) - lse_ref[rows, :])
                dp = _dot(do_ref[rows, :], vv, 1, 1)
                ds = (p * (dp - dlt_ref[rows, :])).astype(BF16)
                acc[rows, :] += _dot(ds, kv, 1, 0)

        @pl.when(ki < qi)
        def _():
            step(False)

        @pl.when(ki == qi)
        def _():
            step(True)
            dq_ref[...] = acc[...]

    q_map = lambda h, p: (h, _by_query_block(p, n)[0], 0)
    kv_map = lambda h, p: (h, _by_query_block(p, n)[1], 0)
    return _hosted_call(
        body, side, name="mla_dq", grid=(H, n * (n + 1) // 2),
        in_specs=[pl.BlockSpec((None, T, dk), q_map), pl.BlockSpec((None, T, dk), kv_map),
                  pl.BlockSpec((None, T, dv), kv_map), pl.BlockSpec((None, T, dv), q_map),
                  pl.BlockSpec((None, T, 1), q_map), pl.BlockSpec((None, T, 1), q_map)],
        out_specs=[pl.BlockSpec((None, T, dk), q_map)],
        out_shape=[jax.ShapeDtypeStruct((H, S, dk), F32)],
        scratch_shapes=[pltpu.VMEM((T, dk), F32)],
        operands=[q, k, v, do, lse, dlt], semantics=("parallel", "arbitrary"))


def _flash_dkv(q, k, v, do, lse, dlt, side=None):
    H, S, dk = q.shape
    dv = v.shape[2]
    T = _tile(S, 512)
    n = S // T
    rg = min(ROW_GROUP, T)

    def body(q_ref, k_ref, v_ref, do_ref, lse_ref, dlt_ref, dk_ref, dv_ref, acc_k, acc_v):
        ki, qi = _by_key_block(pl.program_id(1), n)

        @pl.when(qi == ki)
        def _():
            acc_k[...] = jnp.zeros_like(acc_k)
            acc_v[...] = jnp.zeros_like(acc_v)

        def step(diagonal):
            kv, vv = k_ref[...], v_ref[...]
            for r in range(T // rg):
                rows = pl.ds(r * rg, rg)
                qv, dov = q_ref[rows, :], do_ref[rows, :]
                p = jnp.exp(_scores(qv, kv, r, diagonal) - lse_ref[rows, :])
                acc_v[...] += _dot(p.astype(BF16), dov, 0, 0)
                dp = _dot(dov, vv, 1, 1)
                ds = (p * (dp - dlt_ref[rows, :])).astype(BF16)
                acc_k[...] += _dot(ds, qv, 0, 0)

        @pl.when(qi > ki)
        def _():
            step(False)

        @pl.when(qi == ki)
        def _():
            step(True)

        @pl.when(qi == n - 1)
        def _():
            dk_ref[...] = acc_k[...]
            dv_ref[...] = acc_v[...]

    q_map = lambda h, p: (h, _by_key_block(p, n)[1], 0)
    kv_map = lambda h, p: (h, _by_key_block(p, n)[0], 0)
    return _hosted_call(
        body, side, name="mla_dkv", grid=(H, n * (n + 1) // 2),
        in_specs=[pl.BlockSpec((None, T, dk), q_map), pl.BlockSpec((None, T, dk), kv_map),
                  pl.BlockSpec((None, T, dv), kv_map), pl.BlockSpec((None, T, dv), q_map),
                  pl.BlockSpec((None, T, 1), q_map), pl.BlockSpec((None, T, 1), q_map)],
        out_specs=[pl.BlockSpec((None, T, dk), kv_map), pl.BlockSpec((None, T, dv), kv_map)],
        out_shape=[jax.ShapeDtypeStruct((H, S, dk), F32), jax.ShapeDtypeStruct((H, S, dv), F32)],
        scratch_shapes=[pltpu.VMEM((T, dk), F32), pltpu.VMEM((T, dv), F32)],
        operands=[q, k, v, do, lse, dlt], semantics=("parallel", "arbitrary"))


EDGE = 8


def _shift_down(cur, prev, by):
    ts, tc = cur.shape
    rows = lax.broadcasted_iota(jnp.int32, cur.shape, 0)
    head = jnp.concatenate([pltpu.roll(prev, by, 0), jnp.zeros((ts - EDGE, tc), cur.dtype)], axis=0)
    return jnp.where(rows < by, head, pltpu.roll(cur, by, 0))


def _shift_up(cur, nxt, by):
    ts, tc = cur.shape
    rows = lax.broadcasted_iota(jnp.int32, cur.shape, 0)
    tail = jnp.concatenate([jnp.zeros((ts - EDGE, tc), cur.dtype), pltpu.roll(nxt, EDGE - by, 0)], axis=0)
    return jnp.where(rows >= ts - by, tail, pltpu.roll(cur, ts - by, 0))


def _ffn_tiles(S, F):
    return _tile(S, 256, 8), _tile(F, 1024)


def _ffn_fwd(G, U, cw, cb):
    S, F = G.shape
    ts, tc = _ffn_tiles(S, F)

    def body(g_ref, gp_ref, u_ref, cw_ref, cb_ref, a_ref, act_ref):
        cur = g_ref[...]
        prev = gp_ref[...] * (pl.program_id(1) > 0).astype(F32)
        a = (cb_ref[...] + cw_ref[0:1, :] * _shift_down(cur, prev, 2) + cw_ref[1:2, :] * _shift_down(cur, prev, 1)
             + cw_ref[2:3, :] * cur)
        a_ref[...] = a
        act_ref[...] = (a * _sigmoid(a) * u_ref[...]).astype(BF16)

    cur_spec = pl.BlockSpec((ts, tc), lambda j, i: (i, j))
    return pl.pallas_call(
        body, name="ffn_act_fwd", grid=(F // tc, S // ts),
        in_specs=[cur_spec, pl.BlockSpec((EDGE, tc), lambda j, i: (jnp.maximum(i * (ts // EDGE) - 1, 0), j)),
                  cur_spec,
                  pl.BlockSpec((CONV_WIDTH, tc), lambda j, i: (0, j)), pl.BlockSpec((1, tc), lambda j, i: (0, j))],
        out_specs=[cur_spec, cur_spec],
        out_shape=[jax.ShapeDtypeStruct((S, F), F32), jax.ShapeDtypeStruct((S, F), BF16)],
        compiler_params=_params(("parallel", "parallel")),
    )(G, G, U, cw, cb)


def _ffn_bwd_act(a, U, dact):
    S, F = a.shape
    ts, tc = _ffn_tiles(S, F)

    def body(a_ref, u_ref, d_ref, da_ref, du_ref, db_ref):
        av, dv = a_ref[...], d_ref[...]
        sg = _sigmoid(av)
        du_ref[...] = (dv * av * sg).astype(BF16)
        da = dv * u_ref[...] * sg * (1.0 + av * (1.0 - sg))
        da_ref[...] = da

        @pl.when(pl.program_id(1) == 0)
        def _():
            db_ref[...] = jnp.zeros_like(db_ref)

        db_ref[...] += _colsum(da)

    cur_spec = pl.BlockSpec((ts, tc), lambda j, i: (i, j))
    return pl.pallas_call(
        body, name="ffn_act_bwd", grid=(F // tc, S // ts),
        in_specs=[cur_spec, cur_spec, cur_spec],
        out_specs=[cur_spec, cur_spec, pl.BlockSpec((1, tc), lambda j, i: (0, j))],
        out_shape=[jax.ShapeDtypeStruct((S, F), F32), jax.ShapeDtypeStruct((S, F), BF16),
                   jax.ShapeDtypeStruct((1, F), F32)],
        compiler_params=_params(("parallel", "arbitrary")),
    )(a, U, dact)


def _ffn_bwd_conv(da, G, cw):
    S, F = da.shape
    ts, tc = _ffn_tiles(S, F)
    n = S // ts

    def body(d_ref, dn_ref, g_ref, gp_ref, cw_ref, dg_ref, dw_ref):
        i = pl.program_id(1)
        dcur = d_ref[...]
        dnxt = dn_ref[...] * (i < n - 1).astype(F32)
        cur = g_ref[...]
        prev = gp_ref[...] * (i > 0).astype(F32)
        dg = (cw_ref[2:3, :] * dcur + cw_ref[1:2, :] * _shift_up(dcur, dnxt, 1)
              + cw_ref[0:1, :] * _shift_up(dcur, dnxt, 2))
        dg_ref[...] = dg.astype(BF16)

        @pl.when(i == 0)
        def _():
            dw_ref[...] = jnp.zeros_like(dw_ref)

        dw_ref[0:1, :] += _colsum(dcur * _shift_down(cur, prev, 2))
        dw_ref[1:2, :] += _colsum(dcur * _shift_down(cur, prev, 1))
        dw_ref[2:3, :] += _colsum(dcur * cur)

    cur_spec = pl.BlockSpec((ts, tc), lambda j, i: (i, j))
    return pl.pallas_call(
        body, name="ffn_conv_bwd", grid=(F // tc, n),
        in_specs=[cur_spec,
                  pl.BlockSpec((EDGE, tc), lambda j, i: (jnp.minimum((i + 1) * (ts // EDGE), S // EDGE - 1), j)),
                  cur_spec, pl.BlockSpec((EDGE, tc), lambda j, i: (jnp.maximum(i * (ts // EDGE) - 1, 0), j)),
                  pl.BlockSpec((CONV_WIDTH, tc), lambda j, i: (0, j))],
        out_specs=[cur_spec, pl.BlockSpec((CONV_WIDTH, tc), lambda j, i: (0, j))],
        out_shape=[jax.ShapeDtypeStruct((S, F), BF16), jax.ShapeDtypeStruct((CONV_WIDTH, F), F32)],
        compiler_params=_params(("parallel", "arbitrary")),
    )(da, da, G, G, cw)


class _Cfg:
    def __init__(self, S, D):
        self.S, self.D = S, D
        self.RD = D // (2 * RET_HEADS)
        self.RW = RET_HEADS * self.RD
        self.MV = (D - self.RW) // MLA_HEADS
        self.QL, self.KVL = D // 4, D // 8
        self.F = ((8 * D // 3 + 255) // 256) * 256
        self.IN = 4 * self.RW + self.QL + self.KVL + MLA_ROPE
        self.INs = self.IN // N_CHIP
        self.INp = _round_up(self.INs, LANE)
        self.Fs = self.F // N_CHIP
        self.Fp = _round_up(self.Fs, LANE)
        self.F4 = N_CHIP * self.Fp
        self.QK = MLA_NOPE + MLA_ROPE
        self.KVH = MLA_NOPE + self.MV


def _head_major(t, H, d):
    S = t.shape[0]
    return t.reshape(S, H, d).transpose(1, 0, 2).reshape(H * S, d)


def _seq_major(t, H, d):
    S = t.shape[0] // H
    return t.reshape(H, S, d).transpose(1, 0, 2).reshape(S, H * d)


def _rope_tables(cfg):
    S = cfg.S

    def cs(dim):
        inv = 1.0 / (ROPE_BASE ** (jnp.arange(0, dim, 2, dtype=F32) / dim))
        ang = jnp.arange(S, dtype=F32)[:, None] * inv[None, :]
        return jnp.cos(ang), jnp.sin(ang)

    c, s = cs(cfg.RD)
    z = jnp.zeros_like(s)
    ret = (jnp.concatenate([c, c], 1), jnp.concatenate([-s, z], 1), jnp.concatenate([z, s], 1))
    c, s = cs(MLA_ROPE)
    tail1 = jnp.ones((S, MLA_QK_PAD - cfg.QK), F32)
    tail0 = jnp.zeros((S, MLA_QK_PAD - cfg.QK), F32)
    z = jnp.zeros_like(s)
    mla = (jnp.concatenate([c, c, tail1], 1), jnp.concatenate([-s, z, tail0], 1), jnp.concatenate([z, s, tail0], 1))
    return ret, mla


GATHER_FIRST = ('w_in',)
GATHER_BEHIND_IN_PROJ = ('w_uq', 'w_ukv', 'w_o')
GATHER_BEHIND_MLA = ('w_ffn_gate', 'w_ffn_up')
GATHER_BEHIND_FFN_GATE = ('w_ffn_down',)
GATHER_BEHIND_FFN_UP = ('w_ple_gate', 'w_ple_proj')
PAIR_BEHIND_FFN_DX = ('w_ple_proj', 'w_ple_gate', 'w_ffn_down', 'w_ffn_gate', 'w_ffn_up')
PAIR_AFTER_OUT_PROJ = ('w_o',)
REDUCE_BEHIND_DQ = ('w_ffn_gate', 'w_ple_gate', 'w_o')
REDUCE_BEHIND_DKV = ('w_ffn_up', 'w_ffn_down', 'w_ple_proj')
REDUCE_LAST = ('w_uq', 'w_ukv', 'w_in')


def _local_step(cfg, x, p, tgt, W, late, sp):
    W = dict(W)
    S, D, RD, RW, MV = cfg.S, cfg.D, cfg.RD, cfg.RW, cfg.MV
    H, MH = RET_HEADS, MLA_HEADS
    (rc, rsa, rsb), (mc, msa, msb) = _rope_tables(cfg)
    dec, q_dec, k_dec, s_dec = _ret_tables(min(RET_BLOCK, S), RD)
    k_scale = RD ** -0.5
    a_scale = cfg.QK ** -0.5
    gb, gs = {}, {}

    hn1, r1 = _rms_fwd(x, sp['g_attn'], name="rms_attn")
    proj, got = _mm_nn(hn1, W['w_in'], name="in_proj",
                       side=_gather_side([late[n] for n in GATHER_BEHIND_IN_PROJ]))
    W.update({n: _as_operand(n, g) for n, g in zip(GATHER_BEHIND_IN_PROJ, got)})
    proj = proj.reshape(S, N_CHIP, cfg.INp)[:, :, :cfg.INs].reshape(S, cfg.IN)
    cuts = [RW, 2 * RW, 3 * RW, 4 * RW, 4 * RW + cfg.QL, 4 * RW + cfg.QL + cfg.KVL]
    rq, rk, rv, rg, cq, ckv, kr = jnp.split(proj, cuts, axis=1)

    rq_h, rk_h, rv_h, rg_h = (_head_major(t, H, RD) for t in (rq, rk, rv, rg))
    q_r = _rope(rq_h, rc, rsa, rsb, RD // 2, 1.0, BF16, name="ret_rope_q").reshape(H, S, RD)
    k_r = _rope(rk_h, rc, rsa, rsb, RD // 2, k_scale, BF16, name="ret_rope_k").reshape(H, S, RD)
    v_r = rv_h.astype(BF16).reshape(H, S, RD)
    o_h, states = _ret_fwd(q_r, k_r, v_r, dec, q_dec, k_dec, s_dec)
    o_h = o_h.reshape(H * S, RD)

    def gate_fn(ov, gv):
        oc = ov - _mean(ov)
        ron = oc * lax.rsqrt(_mean(oc * oc) + EPS)
        return [gv * _sigmoid(gv) * ron]
    ro_h = _rowwise(gate_fn, [(o_h, 'row'), (rg_h, 'row')], [(RD, BF16)], name="ret_gate")[0]
    ro = _seq_major(ro_h, H, RD)

    cqn, rcq = _rms_fwd(cq, sp['g_q_lora'], name="rms_q")
    ckvn, rckv = _rms_fwd(ckv, sp['g_kv_lora'], name="rms_kv")
    Q = _mm_nn(cqn, W['w_uq'], name="q_up")
    KV = _mm_nn(ckvn, W['w_ukv'], name="kv_up")
    pad = jnp.zeros((MH, S, MLA_QK_PAD - cfg.QK), F32)
    q_raw = jnp.concatenate([Q.reshape(S, MH, cfg.QK).transpose(1, 0, 2), pad], 2).reshape(MH * S, MLA_QK_PAD)
    KV3 = KV.reshape(S, MH, cfg.KVH).transpose(1, 0, 2)
    k_raw = jnp.concatenate([KV3[:, :, :MLA_NOPE], jnp.broadcast_to(kr[None], (MH, S, MLA_ROPE)), pad], 2)
    k_raw = k_raw.reshape(MH * S, MLA_QK_PAD)
    v_m = KV3[:, :, MLA_NOPE:].astype(BF16)
    q_m = _rope(q_raw, mc, msa, msb, MLA_ROPE // 2, a_scale, BF16, name="mla_rope_q").reshape(MH, S, MLA_QK_PAD)
    k_m = _rope(k_raw, mc, msa, msb, MLA_ROPE // 2, 1.0, BF16, name="mla_rope_k").reshape(MH, S, MLA_QK_PAD)
    (mo_h, lse), got = _flash_fwd(q_m, k_m, v_m, side=_gather_side([late[n] for n in GATHER_BEHIND_MLA]))
    W.update({n: _as_operand(n, g) for n, g in zip(GATHER_BEHIND_MLA, got)})
    mo = _seq_major(mo_h.reshape(MH * S, MV), MH, MV)

    cat = jnp.concatenate([ro, mo.astype(BF16)], axis=1)
    h1 = _mm_nn(cat, W['w_o'], name="out_proj", res=x)

    hn2, r2 = _rms_fwd(h1, sp['g_ffn'], name="rms_ffn")
    G, got = _mm_nn(hn2, W['w_ffn_gate'], name="ffn_gate",
                    side=_gather_side([late[n] for n in GATHER_BEHIND_FFN_GATE]))
    W.update({n: _as_operand(n, g) for n, g in zip(GATHER_BEHIND_FFN_GATE, got)})
    U, got = _mm_nn(hn2, W['w_ffn_up'], name="ffn_up", side=_gather_side([late[n] for n in GATHER_BEHIND_FFN_UP]))
    W.update({n: _as_operand(n, g) for n, g in zip(GATHER_BEHIND_FFN_UP, got)})
    a, act = _ffn_fwd(G, U, sp['conv_w'], sp['conv_b'])
    h2 = _mm_nn(act, W['w_ffn_down'], name="ffn_down", res=h1)

    hn3, r3 = _rms_fwd(h2, sp['g_ple'], name="rms_ple")
    Z = _mm_nn(hn3, W['w_ple_gate'], name="ple_gate")
    p_b = p.astype(BF16)
    PP = _mm_nn(p_b, W['w_ple_proj'], name="ple_proj")

    def head_fn(h2v, zv, ppv, tv, gv):
        gate = _sigmoid(zv)
        h3 = h2v + gate * ppv
        r4 = lax.rsqrt(_mean(h3 * h3) + EPS)
        n4 = h3 * r4
        e = n4 * gv - tv
        dy = e * (1.0 / D)
        dn = dy * gv
        dh3 = r4 * (dn - n4 * _mean(dn * n4))
        dpp = dh3 * gate
        dz = dh3 * ppv * gate * (1.0 - gate)
        loss = jnp.sum(0.5 * _mean(e * e), axis=0, keepdims=True)
        return [dh3, dz, dpp, _colsum(dy * n4), jnp.broadcast_to(loss, (1, LANE))]
    dh3, dZ, dPP, dgf, loss = _rowwise(
        head_fn, [(h2, 'row'), (Z, 'row'), (PP, 'row'), (tgt, 'row'), (sp['g_final'], 'bcast')],
        [(D, F32), (D, BF16), (D, BF16)], [D, LANE], name="ple_loss_head")
    gs['g_final'] = dgf
    loss = loss[0, 0]

    gb['w_ple_proj'] = _mm_tn(p_b, dPP, N_CHIP, name="ple_proj_dw")
    gb['w_ple_gate'] = _mm_tn(hn3, dZ, 1, name="ple_gate_dw")
    dhn3 = _mm_nt(dZ, W['w_ple_gate'], name="ple_gate_dx")
    dh2, dh2_b, gs['g_ple'] = _rms_bwd(h2, r3, dhn3, sp['g_ple'], dh3, name="rms_ple_bwd")

    dact = _mm_nt(dh2_b, W['w_ffn_down'], name="ffn_down_dx")
    gb['w_ffn_down'] = _mm_tn(act, dh2_b, 1, name="ffn_down_dw")
    da, dU, gs['conv_b'] = _ffn_bwd_act(a, U, dact)
    dG, gs['conv_w'] = _ffn_bwd_conv(da, G, sp['conv_w'])
    gb['w_ffn_gate'] = _mm_tn(hn2, dG, N_CHIP, name="ffn_gate_dw")
    gb['w_ffn_up'] = _mm_tn(hn2, dU, N_CHIP, name="ffn_up_dw")
    def pair_sums(names, from_sibling):
        return {n: _pair_add(_grad_pieces(n, gb[n][0]), r) for n, r in zip(names, from_sibling)}

    def wire(names):
        return [_grad_pieces(n, gb[n][1]) for n in names]

    dhn2, got = _mm_nt(dG, W['w_ffn_gate'], name="ffn_gate_dx", side=_pair_side(wire(PAIR_BEHIND_FFN_DX)))
    pair = pair_sums(PAIR_BEHIND_FFN_DX, got)
    dhn2 = _mm_nt(dU, W['w_ffn_up'], name="ffn_up_dx", res=dhn2)
    dh1, dh1_b, gs['g_ffn'] = _rms_bwd(h1, r2, dhn2, sp['g_ffn'], dh2, name="rms_ffn_bwd")

    dcat = _mm_nt(dh1_b, W['w_o'], name="out_proj_dx")
    gb['w_o'] = _mm_tn(cat, dh1_b, 1, name="out_proj_dw")
    pair.update(pair_sums(PAIR_AFTER_OUT_PROJ, _pair_exchange(wire(PAIR_AFTER_OUT_PROJ))))
    from_chips = {}
    dro_h = _head_major(dcat[:, :RW], H, RD)

    def gate_bwd_fn(ov, gv, dv):
        oc = ov - _mean(ov)
        rs = lax.rsqrt(_mean(oc * oc) + EPS)
        ron = oc * rs
        sg = _sigmoid(gv)
        dron = dv * gv * sg
        drg = dv * ron * sg * (1.0 + gv * (1.0 - sg))
        do = rs * (dron - _mean(dron) - ron * _mean(dron * ron))
        return [do, drg]
    do_h, drg_h = _rowwise(gate_bwd_fn, [(o_h, 'row'), (rg_h, 'row'), (dro_h, 'row')],
                           [(RD, BF16), (RD, F32)], name="ret_gate_bwd")
    gq_r, gk_r, gv_r = _ret_bwd(q_r, k_r, v_r, do_h.reshape(H, S, RD), states, dec, q_dec, k_dec, s_dec)
    drq_h = _rope(gq_r.reshape(H * S, RD), rc, -rsa, -rsb, RD // 2, 1.0, F32, name="ret_rope_q_bwd")
    drk_h = _rope(gk_r.reshape(H * S, RD), rc, -rsa, -rsb, RD // 2, k_scale, F32, name="ret_rope_k_bwd")
    drq, drk, drv, drg = (_seq_major(t, H, RD) for t in (drq_h, drk_h, gv_r.reshape(H * S, RD), drg_h))

    dmo_h = _head_major(dcat[:, RW:], MH, MV)
    dlt = _rowwise(lambda ov, dv: [jnp.sum(ov * dv, axis=1, keepdims=True)],
                   [(mo_h.reshape(MH * S, MV), 'row'), (dmo_h, 'row')], [(1, F32)], name="mla_delta")[0]
    dmo_b = dmo_h.astype(BF16).reshape(MH, S, MV)
    dlt = dlt.reshape(MH, S, 1)
    (gq_m,), got = _flash_dq(q_m, k_m, v_m, dmo_b, lse, dlt,
                             side=_exchange_side([pair[n][1] for n in REDUCE_BEHIND_DQ]))
    from_chips.update(zip(REDUCE_BEHIND_DQ, got))
    (gk_m, gv_m), got = _flash_dkv(q_m, k_m, v_m, dmo_b, lse, dlt,
                                   side=_exchange_side([pair[n][1] for n in REDUCE_BEHIND_DKV]))
    from_chips.update(zip(REDUCE_BEHIND_DKV, got))
    dq_raw = _rope(gq_m.reshape(MH * S, MLA_QK_PAD), mc, -msa, -msb, MLA_ROPE // 2, a_scale, F32,
                   name="mla_rope_q_bwd")
    dk_raw = _rope(gk_m.reshape(MH * S, MLA_QK_PAD), mc, -msa, -msb, MLA_ROPE // 2, 1.0, F32, name="mla_rope_k_bwd")
    dQ = dq_raw.reshape(MH, S, MLA_QK_PAD)[:, :, :cfg.QK].transpose(1, 0, 2).reshape(S, MH * cfg.QK).astype(BF16)
    dk3 = dk_raw.reshape(MH, S, MLA_QK_PAD)
    dKV = jnp.concatenate([dk3[:, :, :MLA_NOPE], gv_m], 2).transpose(1, 0, 2).reshape(S, MH * cfg.KVH).astype(BF16)
    dkr_heads = [(dk3[h, :, MLA_NOPE:cfg.QK], 'row') for h in range(MH)]
    dkr = _rowwise(lambda *v: [functools.reduce(lambda s, t: s + t, v)], dkr_heads, [(MLA_ROPE, F32)],
                   name="mla_rope_k_heads")[0]

    gb['w_uq'] = _mm_tn(cqn, dQ, N_CHIP, name="q_up_dw")
    dcqn = _mm_nt(dQ, W['w_uq'], name="q_up_dx")
    dcq, _, gs['g_q_lora'] = _rms_bwd(cq, rcq, dcqn, sp['g_q_lora'], None, name="rms_q_bwd")
    gb['w_ukv'] = _mm_tn(ckvn, dKV, N_CHIP, name="kv_up_dw")
    dckvn = _mm_nt(dKV, W['w_ukv'], name="kv_up_dx")
    dckv, _, gs['g_kv_lora'] = _rms_bwd(ckv, rckv, dckvn, sp['g_kv_lora'], None, name="rms_kv_bwd")

    dproj = jnp.concatenate([drq, drk, drv, drg, dcq, dckv, dkr], axis=1).astype(BF16)
    dproj = jnp.pad(dproj.reshape(S, N_CHIP, cfg.INs), ((0, 0), (0, 0), (0, cfg.INp - cfg.INs)))
    dproj = dproj.reshape(S, N_CHIP * cfg.INp)
    gb['w_in'] = _mm_tn(hn1, dproj, N_CHIP, name="in_proj_dw")
    pair.update(pair_sums(REDUCE_LAST, _pair_exchange(wire(REDUCE_LAST))))
    dhn1, got = _mm_nt(dproj, W['w_in'], name="in_proj_dx", side=_exchange_side([pair[n][1] for n in REDUCE_LAST]))
    from_chips.update(zip(REDUCE_LAST, got))
    dx, _, gs['g_attn'] = _rms_bwd(x, r1, dhn1, sp['g_attn'], dh1, name="rms_attn_bwd")
    return loss, dx, {n: (pair[n][0], from_chips[n]) for n in BIG}, gs


def _padded_shape(name, shape):
    K, n = shape
    if name in COL_SHARDED:
        return K, _round_up(n, LANE)
    return _round_up(K, LANE), n


def _pad_shard(name, w):
    K, n = _padded_shape(name, w.shape)
    return jnp.pad(w.astype(BF16), ((0, K - w.shape[0]), (0, n - w.shape[1])))


def _as_operand(name, gathered):
    if name in COL_SHARDED:
        return gathered
    return gathered.reshape(1, gathered.shape[0] * gathered.shape[1], gathered.shape[2])


def _grad_pieces(name, g):
    if name in COL_SHARDED:
        return g
    return g.reshape(N_CHIP, g.shape[1] // N_CHIP, g.shape[2])


def _channels_padded(v, cfg):
    r = v.shape[0]
    return jnp.pad(v.reshape(r, N_CHIP, cfg.Fs), ((0, 0), (0, 0), (0, cfg.Fp - cfg.Fs))).reshape(r, cfg.F4)


def _channels_unpadded(v, cfg):
    r = v.shape[0]
    return v.reshape(r, N_CHIP, cfg.Fp)[:, :, :cfg.Fs].reshape(r, cfg.F)


def _mesh_pos():
    return lax.axis_index("x"), lax.axis_index("y"), lax.axis_index("c")


def _other_chips(x, y):
    return [(1 - x, y), (x, 1 - y), (1 - x, 1 - y)]


def _chip_id(cx, cy):
    return 2 * cx + cy


def _half(ref_rows, core):
    half = ref_rows // 2
    return pl.ds(core * half, half)


def _gather_side(shards):
    n = len(shards)

    def copies(srcs, outs, sems):
        ici_send, ici_recv, d2d_send, d2d_recv, own_send, own_recv = sems
        x, y, c = _mesh_pos()
        mine = _chip_id(x, y)
        others = _other_chips(x, y)

        def over_ici(w, j, chip):
            rows = _half(srcs[w].shape[0], c)
            return pltpu.make_async_remote_copy(
                src_ref=srcs[w].at[rows], dst_ref=outs[w].at[chip, rows],
                send_sem=ici_send.at[w, j], recv_sem=ici_recv.at[w, j],
                device_id=(*others[j], c), device_id_type=MESH)

        def over_d2d(w, j, core):
            rows = _half(srcs[w].shape[0], core)
            slab = outs[w].at[_chip_id(*others[j]), rows]
            return pltpu.make_async_remote_copy(
                src_ref=slab, dst_ref=slab, send_sem=d2d_send.at[w, j], recv_sem=d2d_recv.at[w, j],
                device_id=(x, y, 1 - c), device_id_type=MESH)

        def own(w):
            return pltpu.make_async_remote_copy(
                src_ref=srcs[w], dst_ref=outs[w].at[mine], send_sem=own_send.at[w], recv_sem=own_recv.at[w],
                device_id=(x, y, 1 - c), device_id_type=MESH)

        return c, mine, others, over_ici, over_d2d, own

    def start(srcs, outs, sems):
        c, mine, others, over_ici, over_d2d, own = copies(srcs, outs, sems)
        for w in range(n):
            for j in range(3):
                over_ici(w, j, mine).start()
        for w in range(n):
            own(w).start()

    def finish(srcs, outs, sems):
        c, mine, others, over_ici, over_d2d, own = copies(srcs, outs, sems)
        for w in range(n):
            for j in range(3):
                over_ici(w, j, _chip_id(*others[j])).wait_recv()
                over_d2d(w, j, c).start()
        for w in range(n):
            for j in range(3):
                over_d2d(w, j, 1 - c).wait_recv()
        for w in range(n):
            own(w).wait()
            for j in range(3):
                over_ici(w, j, mine).wait_send()
                over_d2d(w, j, c).wait_send()

    out_shape = [jax.ShapeDtypeStruct((N_CHIP,) + s.shape, s.dtype) for s in shards]
    sems = [pltpu.SemaphoreType.DMA((n, 3))] * 4 + [pltpu.SemaphoreType.DMA((n,))] * 2
    return _Side(shards, out_shape, sems, start, finish)


def _gather_weights(shards):
    return _run_side(_gather_side(shards), "gather_weights")


def _pair_side(grads):
    n = len(grads)

    def copies(srcs, outs, sems):
        send, recv = sems
        x, y, c = _mesh_pos()
        return [pltpu.make_async_remote_copy(
            src_ref=srcs[w].at[:, _half(srcs[w].shape[1], 1 - c), :], dst_ref=outs[w],
            send_sem=send.at[w], recv_sem=recv.at[w],
            device_id=(x, y, 1 - c), device_id_type=MESH) for w in range(n)]

    def start(srcs, outs, sems):
        for cp in copies(srcs, outs, sems):
            cp.start()

    def finish(srcs, outs, sems):
        for cp in copies(srcs, outs, sems):
            cp.wait()

    out_shape = [jax.ShapeDtypeStruct((g.shape[0], g.shape[1] // 2, g.shape[2]), g.dtype) for g in grads]
    return _Side(grads, out_shape, [pltpu.SemaphoreType.DMA((n,))] * 2, start, finish)


def _pair_exchange(grads):
    return _run_side(_pair_side(grads), "grad_pair_exchange")


def _exchange_side(parts):
    n = len(parts)

    def copies(srcs, outs, sems):
        send, recv = sems
        x, y, c = _mesh_pos()
        others = _other_chips(x, y)
        return [pltpu.make_async_remote_copy(
            src_ref=srcs[w].at[_chip_id(*others[j])], dst_ref=outs[w].at[j],
            send_sem=send.at[w, j], recv_sem=recv.at[w, j],
            device_id=(*others[j], c), device_id_type=MESH) for w in range(n) for j in range(3)]

    def start(srcs, outs, sems):
        for cp in copies(srcs, outs, sems):
            cp.start()

    def finish(srcs, outs, sems):
        for cp in copies(srcs, outs, sems):
            cp.wait()

    out_shape = [jax.ShapeDtypeStruct((3,) + p.shape[1:], p.dtype) for p in parts]
    return _Side(parts, out_shape, [pltpu.SemaphoreType.DMA((n, 3))] * 2, start, finish)


def _sibling_share(shards):
    n = len(shards)

    def body(*refs):
        outs = refs[n:2 * n]
        send, recv = refs[2 * n:]
        x, y, c = _mesh_pos()

        def half_of(w, core):
            rows = outs[w].at[_half(outs[w].shape[0], core)]
            return pltpu.make_async_remote_copy(
                src_ref=rows, dst_ref=rows, send_sem=send.at[w], recv_sem=recv.at[w],
                device_id=(x, y, 1 - c), device_id_type=MESH)

        for w in range(n):
            half_of(w, c).start()
        for w in range(n):
            half_of(w, 1 - c).wait_recv()
        for w in range(n):
            half_of(w, c).wait_send()

    return pl.pallas_call(
        body, name="grad_sibling_share", in_specs=[HBM] * n, out_specs=[HBM] * n,
        out_shape=[jax.ShapeDtypeStruct(s.shape, s.dtype) for s in shards],
        scratch_shapes=[pltpu.SemaphoreType.DMA((n,))] * 2, input_output_aliases={w: w for w in range(n)},
    )(*shards)


N_DEV = 8


def _gather_small(v):
    r, width = v.shape

    def body(v_ref, out_ref, send_sems, recv_sems, local_sem):
        x, y, c = _mesh_pos()
        me, sibling = (x, y, c), (x, y, 1 - c)
        chips = _other_chips(x, y)

        def rows(px, py, pc):
            return out_ref.at[pl.ds((4 * px + 2 * py + pc) * r, r), :]

        def copy(k, block, to, src=None):
            return pltpu.make_async_remote_copy(
                src_ref=rows(*block) if src is None else src, dst_ref=rows(*block),
                send_sem=send_sems.at[k], recv_sem=recv_sems.at[k], device_id=to, device_id_type=MESH)

        mine = pltpu.make_async_copy(v_ref, rows(*me), local_sem)
        mine.start()
        first = [copy(0, me, sibling, src=v_ref)]
        first += [copy(1 + j, me, (*chip, c), src=v_ref) for j, chip in enumerate(chips)]
        for cp in first:
            cp.start()
        passed = [copy(4 + j, (*chip, c), sibling) for j, chip in enumerate(chips)]
        for j, chip in enumerate(chips):
            copy(1 + j, (*chip, c), me).wait_recv()
            passed[j].start()
        copy(0, sibling, me).wait_recv()
        for j, chip in enumerate(chips):
            copy(4 + j, (*chip, 1 - c), me).wait_recv()
        for cp in first + passed:
            cp.wait_send()
        mine.wait()

    vmem = pl.BlockSpec(memory_space=pltpu.VMEM)
    return pl.pallas_call(
        body, name="gather_small", out_shape=jax.ShapeDtypeStruct((N_DEV * r, width), v.dtype),
        in_specs=[vmem], out_specs=vmem,
        scratch_shapes=[pltpu.SemaphoreType.DMA((7,)), pltpu.SemaphoreType.DMA((7,)), pltpu.SemaphoreType.DMA],
    )(v)


def _pack(arrays):
    flat = jnp.concatenate([a.reshape(-1) for a in arrays])
    size = _round_up(flat.shape[0], 8 * LANE)
    return jnp.pad(flat, (0, size - flat.shape[0])).reshape(size // LANE, LANE)


def _unpack(packed, shapes):
    flat = packed.reshape(-1)
    out, at = [], 0
    for s in shapes:
        size = math.prod(s)
        out.append(flat[at:at + size].reshape(s))
        at += size
    return out


def _sum_devices(gathered):
    r = gathered.shape[0] // N_DEV
    blocks = [(gathered[d * r:(d + 1) * r], 'row') for d in range(N_DEV)]
    return _rowwise(lambda *v: [functools.reduce(lambda s, t: s + t, v)], blocks, [(LANE, F32)],
                    name="small_grad_sum")[0]


def _reduce_tiles(half, n):
    return _tile(half, 128, 8)


def _pair_add(g32, r1):
    G, K, n = g32.shape
    half = K // 2
    tr = _reduce_tiles(half, n)
    nrt = half // tr

    def body(g_ref, r_ref, s32_ref, sb_ref):
        s = g_ref[...] + r_ref[...].astype(F32)
        s32_ref[...] = s
        sb_ref[...] = s.astype(BF16)

    blk = pl.BlockSpec((None, tr, n), lambda k, i: (k, i, 0))
    return pl.pallas_call(
        body, name="grad_pair_add", grid=(G, nrt),
        in_specs=[pl.BlockSpec((None, tr, n), lambda k, i: (k, lax.axis_index("c") * nrt + i, 0)), blk],
        out_specs=[blk, blk],
        out_shape=[jax.ShapeDtypeStruct((G, half, n), F32), jax.ShapeDtypeStruct((G, half, n), BF16)],
        compiler_params=_params(("parallel", "parallel")),
    )(g32, r1)


def _chip_add(s32, r2):
    _, half, n = s32.shape
    tr = _reduce_tiles(half, n)
    nrt = half // tr

    def body(s_ref, a_ref, b_ref, c_ref, o_ref):
        o_ref[...] = ((s_ref[...] + a_ref[...].astype(F32)) + b_ref[...].astype(F32)) + c_ref[...].astype(F32)

    def piece(j):
        return pl.BlockSpec((None, tr, n), lambda i, j=j: (j, i, 0))

    def mine(i):
        return _chip_id(lax.axis_index("x"), lax.axis_index("y")), i, 0

    return pl.pallas_call(
        body, name="grad_chip_add", grid=(nrt,),
        in_specs=[pl.BlockSpec((None, tr, n), mine), piece(0), piece(1), piece(2)],
        out_specs=pl.BlockSpec((tr, n), lambda i: (lax.axis_index("c") * nrt + i, 0)),
        out_shape=jax.ShapeDtypeStruct((2 * half, n), F32),
        compiler_params=_params(("parallel",)),
    )(s32, r2, r2, r2)


def _adamw_math(wv, gv, mv, vv):
    m2 = ADAM_B1 * mv + (1.0 - ADAM_B1) * gv
    v2 = ADAM_B2 * vv + (1.0 - ADAM_B2) * (gv * gv)
    m_hat = m2 / (1.0 - ADAM_B1 ** ADAM_STEP)
    v_hat = v2 / (1.0 - ADAM_B2 ** ADAM_STEP)
    delta = -ADAM_LR * (m_hat / (jnp.sqrt(v_hat) + ADAM_EPS) + ADAM_WD * wv)
    return [delta, m2, v2]


def _adamw(w, g, m, v, *, name):
    width = w.shape[1]
    return _rowwise(_adamw_math, [(w, 'row'), (g, 'row'), (m, 'row'), (v, 'row')], [(width, F32)] * 3, name=name)


def _adamw_sharded(w, g_padded, m, v, *, name):
    _, K, n = w.shape
    n_pad = g_padded.shape[1]
    ts = 8
    while ts * 2 <= 256 and ts * 2 * 8 * n_pad * 4 <= ROWWISE_BLOCK_BYTES and K % (ts * 2) == 0:
        ts *= 2

    def body(w_ref, g_ref, m_ref, v_ref, go_ref, d_ref, mo_ref, vo_ref):
        gv = g_ref[:, :n] if n != n_pad else g_ref[...]
        go_ref[...] = gv
        d_ref[...], mo_ref[...], vo_ref[...] = _adamw_math(w_ref[...], gv, m_ref[...], v_ref[...])

    blk = pl.BlockSpec((None, ts, n), lambda i: (0, i, 0))
    out = jax.ShapeDtypeStruct((1, K, n), F32)
    return pl.pallas_call(
        body, name=name, grid=(K // ts,),
        in_specs=[blk, pl.BlockSpec((ts, n_pad), lambda i: (i, 0)), blk, blk],
        out_specs=[blk] * 4, out_shape=[out] * 4,
        compiler_params=_params(("parallel",)),
    )(w, g_padded, m, v)


def kernel(x, p, w_in, g_attn, g_q_lora, g_kv_lora, w_uq, w_ukv, w_o, g_ffn, w_ffn_gate, w_ffn_up, conv_w, conv_b, w_ffn_down, g_ple, w_ple_gate, w_ple_proj, g_final, loss_target, m_w_in, m_g_attn, m_g_q_lora, m_g_kv_lora, m_w_uq, m_w_ukv, m_w_o, m_g_ffn, m_w_ffn_gate, m_w_ffn_up, m_conv_w, m_conv_b, m_w_ffn_down, m_g_ple, m_w_ple_gate, m_w_ple_proj, m_g_final, v_w_in, v_g_attn, v_g_q_lora, v_g_kv_lora, v_w_uq, v_w_ukv, v_w_o, v_g_ffn, v_w_ffn_gate, v_w_ffn_up, v_conv_w, v_conv_b, v_w_ffn_down, v_g_ple, v_w_ple_gate, v_w_ple_proj, v_g_final):
    weights = dict(w_in=w_in, g_attn=g_attn, g_q_lora=g_q_lora, g_kv_lora=g_kv_lora, w_uq=w_uq, w_ukv=w_ukv, w_o=w_o,
                   g_ffn=g_ffn, w_ffn_gate=w_ffn_gate, w_ffn_up=w_ffn_up, conv_w=conv_w, conv_b=conv_b,
                   w_ffn_down=w_ffn_down, g_ple=g_ple, w_ple_gate=w_ple_gate, w_ple_proj=w_ple_proj, g_final=g_final)
    mom1 = dict(w_in=m_w_in, g_attn=m_g_attn, g_q_lora=m_g_q_lora, g_kv_lora=m_g_kv_lora, w_uq=m_w_uq, w_ukv=m_w_ukv,
                w_o=m_w_o, g_ffn=m_g_ffn, w_ffn_gate=m_w_ffn_gate, w_ffn_up=m_w_ffn_up, conv_w=m_conv_w,
                conv_b=m_conv_b, w_ffn_down=m_w_ffn_down, g_ple=m_g_ple, w_ple_gate=m_w_ple_gate,
                w_ple_proj=m_w_ple_proj, g_final=m_g_final)
    mom2 = dict(w_in=v_w_in, g_attn=v_g_attn, g_q_lora=v_g_q_lora, g_kv_lora=v_g_kv_lora, w_uq=v_w_uq, w_ukv=v_w_ukv,
                w_o=v_w_o, g_ffn=v_g_ffn, w_ffn_gate=v_w_ffn_gate, w_ffn_up=v_w_ffn_up, conv_w=v_conv_w,
                conv_b=v_conv_b, w_ffn_down=v_w_ffn_down, g_ple=v_g_ple, w_ple_gate=v_w_ple_gate,
                w_ple_proj=v_w_ple_proj, g_final=v_g_final)
    _, S, D = x.shape
    cfg = _Cfg(S, D)
    cx, cy, _ = _mesh_pos()

    shards = {name: _pad_shard(name, weights[name][0]) for name in BIG}
    gathered = _gather_weights([shards[name] for name in GATHER_FIRST])
    W = {name: _as_operand(name, g) for name, g in zip(GATHER_FIRST, gathered)}
    cw_all = _gather_small(_pack([jnp.pad(conv_w[0], ((0, 0), (0, cfg.Fp - cfg.Fs)))]))
    r_cw = cw_all.shape[0] // N_DEV
    cw_chips = [_unpack(cw_all[2 * k * r_cw:(2 * k + 1) * r_cw], [(CONV_WIDTH, cfg.Fp)])[0] for k in range(N_CHIP)]
    sp = dict(g_attn=g_attn, g_q_lora=g_q_lora, g_kv_lora=g_kv_lora, g_ffn=g_ffn, g_ple=g_ple,
              g_final=g_final.reshape(1, D), conv_w=jnp.concatenate(cw_chips, axis=1),
              conv_b=_channels_padded(conv_b, cfg))

    loss, dx, parts, gs = _local_step(cfg, x[0], p[0, 0], loss_target[0], W, shards, sp)
    loss = lax.psum(loss, ("x", "y", "c"))

    halves = [_chip_add(*parts[name]) for name in BIG]
    whole = dict(zip(BIG, _sibling_share(halves)))
    grads = {}

    small_names = ['g_attn', 'g_q_lora', 'g_kv_lora', 'g_ffn', 'g_ple', 'g_final', 'conv_b', 'conv_w']
    small_sum = _sum_devices(_gather_small(_pack([gs[name] for name in small_names])))
    for name, g in zip(small_names, _unpack(small_sum, [gs[name].shape for name in small_names])):
        grads[name] = g
    grads['g_final'] = grads['g_final'].reshape(D)
    grads['conv_b'] = _channels_unpadded(grads['conv_b'], cfg)
    mine = _chip_id(cx, cy)
    grads['conv_w'] = lax.dynamic_slice_in_dim(grads['conv_w'], mine * cfg.Fp, cfg.Fp, axis=1)[:, :cfg.Fs]
    grads['conv_w'] = grads['conv_w'].reshape(1, CONV_WIDTH, cfg.Fs)

    delta, new_m, new_v = {}, {}, {}
    for name in BIG:
        operands = (weights[name], whole[name], mom1[name], mom2[name])
        flip = weights[name].shape[2] % LANE != 0
        if flip:
            operands = tuple(a.T if a.ndim == 2 else a.transpose(0, 2, 1) for a in operands)
        outs = _adamw_sharded(*operands, name="adamw_" + name)
        if flip:
            outs = [o.transpose(0, 2, 1) for o in outs]
        grads[name], delta[name], new_m[name], new_v[name] = outs
    for name in WEIGHTS:
        if name in BIG:
            continue
        shape = weights[name].shape
        flat = (shape[-2], shape[-1]) if len(shape) == 3 else (1, shape[-1])
        d, m2, v2 = _adamw(weights[name].reshape(flat), grads[name].reshape(flat), mom1[name].reshape(flat),
                           mom2[name].reshape(flat), name="adamw_" + name)
        delta[name], new_m[name], new_v[name] = d.reshape(shape), m2.reshape(shape), v2.reshape(shape)

    return (loss, dx.reshape(1, S, D), *[grads[n] for n in WEIGHTS], *[delta[n] for n in WEIGHTS],
            *[new_m[n] for n in WEIGHTS], *[new_v[n] for n in WEIGHTS])
```

```python
import functools
import math

import jax
import jax.numpy as jnp
from jax import lax
from jax.experimental import pallas as pl
from jax.experimental.pallas import tpu as pltpu

F32 = jnp.float32
BF16 = jnp.bfloat16

LANE = 128
VMEM_LIMIT = 56 * 1024 * 1024
ROWWISE_BLOCK_BYTES = 8 * 1024 * 1024
ROWWISE_MAX_ROWS = 2048
MM_TILE = 1024
MM_TILE_N = 1408
MM_TILE_CONTRACT = 2816
MM_TILE_CONTRACT_N = 2816

N_CHIP = 4
MESH = pl.DeviceIdType.MESH

CHUNK = 64
RET_HEADS = 8
MLA_HEADS = 16
MLA_NOPE = 128
MLA_ROPE = 64
MLA_QK_PAD = 256
PLE_DIM = 256
CONV_WIDTH = 3
ROPE_BASE = 10000.0
EPS = 1e-6
RET_BLOCK = 256

ADAM_LR = 0.001
ADAM_B1 = 0.9
ADAM_B2 = 0.999
ADAM_EPS = 1e-08
ADAM_WD = 0.01
ADAM_STEP = 10

WEIGHTS = ['w_in', 'g_attn', 'g_q_lora', 'g_kv_lora', 'w_uq', 'w_ukv', 'w_o', 'g_ffn', 'w_ffn_gate', 'w_ffn_up',
           'conv_w', 'conv_b', 'w_ffn_down', 'g_ple', 'w_ple_gate', 'w_ple_proj', 'g_final']
COL_SHARDED = ('w_in', 'w_uq', 'w_ukv', 'w_ffn_gate', 'w_ffn_up', 'w_ple_proj')
ROW_SHARDED = ('w_o', 'w_ffn_down', 'w_ple_gate')
BIG = COL_SHARDED + ROW_SHARDED
SMALL_REPLICATED = ('g_attn', 'g_q_lora', 'g_kv_lora', 'g_ffn', 'conv_b', 'g_ple', 'g_final')


def _round_up(n, m):
    return (n + m - 1) // m * m


def _tile(dim, cap, align=LANE):
    if dim <= cap:
        return dim
    t = cap // align * align
    while t >= align:
        if dim % t == 0:
            return t
        t -= align
    return dim


def _params(sem):
    return pltpu.CompilerParams(dimension_semantics=sem, vmem_limit_bytes=VMEM_LIMIT)


HBM = pl.BlockSpec(memory_space=pltpu.HBM)


class _Side:
    def __init__(self, ins, out_shape, sems, start, finish):
        self.ins, self.out_shape, self.sems, self.start, self.finish = ins, out_shape, sems, start, finish


def _run_side(side, name):
    n_in, n_out = len(side.ins), len(side.out_shape)

    def body(*refs):
        ins, outs, sems = refs[:n_in], refs[n_in:n_in + n_out], refs[n_in + n_out:]
        side.start(ins, outs, sems)
        side.finish(ins, outs, sems)

    return pl.pallas_call(
        body, name=name, in_specs=[HBM] * n_in, out_specs=[HBM] * n_out, out_shape=side.out_shape,
        scratch_shapes=side.sems,
    )(*side.ins)


def _hosted_call(body, side, *, name, grid, in_specs, out_specs, out_shape, scratch_shapes, operands, semantics):
    if side is None:
        res = pl.pallas_call(
            body, name=name, grid=grid, in_specs=in_specs, out_specs=out_specs, out_shape=out_shape,
            scratch_shapes=scratch_shapes, compiler_params=_params(semantics))(*operands)
        return res, []
    n_in, n_out, n_scr = len(in_specs), len(out_specs), len(scratch_shapes)
    s_in, s_out = len(side.ins), len(side.out_shape)

    def hosted(*refs):
        refs = list(refs)
        ins, refs = refs[:n_in], refs[n_in:]
        side_ins, refs = refs[:s_in], refs[s_in:]
        outs, refs = refs[:n_out], refs[n_out:]
        side_outs, refs = refs[:s_out], refs[s_out:]
        scratch, sems = refs[:n_scr], refs[n_scr:]
        ids = [pl.program_id(a) for a in range(len(grid))]
        first = functools.reduce(jnp.logical_and, [i == 0 for i in ids])
        last = functools.reduce(jnp.logical_and, [i == g - 1 for i, g in zip(ids, grid)])

        @pl.when(first)
        def _():
            side.start(side_ins, side_outs, sems)

        body(*ins, *outs, *scratch)

        @pl.when(last)
        def _():
            side.finish(side_ins, side_outs, sems)

    res = pl.pallas_call(
        hosted, name=name, grid=grid, in_specs=list(in_specs) + [HBM] * s_in,
        out_specs=list(out_specs) + [HBM] * s_out, out_shape=list(out_shape) + list(side.out_shape),
        scratch_shapes=list(scratch_shapes) + list(side.sems),
        compiler_params=_params(("arbitrary",) * len(grid)))(*operands, *side.ins)
    return res[:n_out], res[n_out:]


def _mm_call(name, dims, grid, in_specs, out_specs, out_shape, acc_shape, operands, has_res, side=None):
    nsteps = grid[2]
    n_out = len(out_shape)

    def body(*refs):
        a_ref, b_ref = refs[0], refs[1]
        res_ref = refs[2] if has_res else None
        outs = refs[2 + has_res:2 + has_res + n_out]
        acc = refs[2 + has_res + n_out]
        k = pl.program_id(2)

        @pl.when(k == 0)
        def _():
            acc[...] = jnp.zeros_like(acc)

        acc[...] += lax.dot_general(a_ref[...], b_ref[...], (dims, ((), ())), preferred_element_type=F32)

        @pl.when(k == nsteps - 1)
        def _():
            r = acc[...]
            if has_res:
                r = r + res_ref[...]
            for o in outs:
                o[...] = r.astype(o.dtype)

    outs, side_outs = _hosted_call(
        body, side, name=name, grid=grid, in_specs=in_specs, out_specs=out_specs, out_shape=out_shape,
        scratch_shapes=[pltpu.VMEM(acc_shape, F32)], operands=operands,
        semantics=("parallel", "parallel", "arbitrary"))
    return (outs, side_outs) if side is not None else outs


def _mm_nn(a, w, *, name, res=None, out_dtype=F32, side=None):
    M, K = a.shape
    G, _, n = w.shape
    tm, tn, tk = _tile(M, MM_TILE), _tile(n, MM_TILE_N), _tile(K, MM_TILE_CONTRACT)
    npg = n // tn
    grid = (M // tm, G * npg, K // tk)
    in_specs = [pl.BlockSpec((tm, tk), lambda i, j, k: (i, k)),
                pl.BlockSpec((None, tk, tn), lambda i, j, k: (j // npg, k, j % npg))]
    operands = [a, w]
    if res is not None:
        in_specs.append(pl.BlockSpec((tm, tn), lambda i, j, k: (i, j)))
        operands.append(res)
    out_specs = [pl.BlockSpec((tm, tn), lambda i, j, k: (i, j))]
    out_shape = [jax.ShapeDtypeStruct((M, G * n), out_dtype)]
    got = _mm_call(name, ((1,), (0,)), grid, in_specs, out_specs, out_shape, (tm, tn), operands, res is not None, side)
    return (got[0][0], got[1]) if side is not None else got[0]


def _mm_nt(g, w, *, name, res=None, out_dtype=F32, side=None):
    M, _ = g.shape
    G, K, n = w.shape
    tm, tko, tn = _tile(M, MM_TILE), _tile(K, MM_TILE), _tile(n, MM_TILE_CONTRACT_N)
    npg = n // tn
    grid = (M // tm, K // tko, G * npg)
    in_specs = [pl.BlockSpec((tm, tn), lambda i, j, k: (i, k)),
                pl.BlockSpec((None, tko, tn), lambda i, j, k: (k // npg, j, k % npg))]
    operands = [g, w]
    if res is not None:
        in_specs.append(pl.BlockSpec((tm, tko), lambda i, j, k: (i, j)))
        operands.append(res)
    out_specs = [pl.BlockSpec((tm, tko), lambda i, j, k: (i, j))]
    out_shape = [jax.ShapeDtypeStruct((M, K), out_dtype)]
    got = _mm_call(name, ((1,), (1,)), grid, in_specs, out_specs, out_shape, (tm, tko), operands, res is not None, side)
    return (got[0][0], got[1]) if side is not None else got[0]


def _mm_tn(a, g, groups, *, name):
    M, K = a.shape
    n = g.shape[1] // groups
    tm, tko, tn = _tile(M, MM_TILE_CONTRACT), _tile(K, MM_TILE), _tile(n, MM_TILE_N)
    npg = n // tn
    grid = (K // tko, groups * npg, M // tm)
    in_specs = [pl.BlockSpec((tm, tko), lambda i, j, k: (k, i)),
                pl.BlockSpec((tm, tn), lambda i, j, k: (k, j))]
    out_spec = pl.BlockSpec((None, tko, tn), lambda i, j, k: (j // npg, i, j % npg))
    out_shape = [jax.ShapeDtypeStruct((groups, K, n), F32), jax.ShapeDtypeStruct((groups, K, n), BF16)]
    return _mm_call(name, ((0,), (0,)), grid, in_specs, [out_spec, out_spec], out_shape, (tko, tn), [a, g], False)


def _rowwise(fn, ins, outs, accs=(), *, name):
    R = next(a.shape[0] for a, kind in ins if kind == 'row')
    row_bytes = sum(a.shape[1] * a.dtype.itemsize for a, kind in ins if kind != 'bcast')
    row_bytes += sum(w * jnp.dtype(dt).itemsize for w, dt in outs)
    ts = 8
    while ts * 2 <= ROWWISE_MAX_ROWS and ts * 2 * row_bytes <= ROWWISE_BLOCK_BYTES:
        ts *= 2
    ts = min(ts, R)
    for a, kind in ins:
        if kind == 'per':
            ts = math.gcd(ts, a.shape[0])
    while R % ts:
        ts //= 2
    in_specs = []
    for a, kind in ins:
        w = a.shape[1]
        if kind == 'row':
            in_specs.append(pl.BlockSpec((ts, w), lambda i: (i, 0)))
        elif kind == 'bcast':
            in_specs.append(pl.BlockSpec((1, w), lambda i: (0, 0)))
        else:
            nper = a.shape[0] // ts
            in_specs.append(pl.BlockSpec((ts, w), lambda i, nper=nper: (i % nper, 0)))
    out_specs = [pl.BlockSpec((ts, w), lambda i: (i, 0)) for w, _ in outs]
    out_specs += [pl.BlockSpec((1, w), lambda i: (0, 0)) for w in accs]
    out_shape = [jax.ShapeDtypeStruct((R, w), dt) for w, dt in outs]
    out_shape += [jax.ShapeDtypeStruct((1, w), F32) for w in accs]
    n_in, n_out = len(ins), len(outs)

    def body(*refs):
        vals = [r[...] for r in refs[:n_in]]
        res = fn(*vals)
        for r, v in zip(refs[n_in:n_in + n_out], res[:n_out]):
            r[...] = v.astype(r.dtype)
        if accs:
            first = pl.program_id(0) == 0
            for r, v in zip(refs[n_in + n_out:], res[n_out:]):
                @pl.when(first)
                def _(r=r, v=v):
                    r[...] = v

                @pl.when(jnp.logical_not(first))
                def _(r=r, v=v):
                    r[...] += v

    return pl.pallas_call(
        body, name=name, grid=(R // ts,), in_specs=in_specs, out_specs=out_specs, out_shape=out_shape,
        compiler_params=_params(("arbitrary",) if accs else ("parallel",)),
    )(*[a for a, _ in ins])


def _mean(v):
    return jnp.mean(v, axis=-1, keepdims=True)


def _colsum(v):
    return jnp.sum(v, axis=0, keepdims=True)


def _sigmoid(v):
    return 1.0 / (1.0 + jnp.exp(-v))


def _rms_fwd(x, g, *, name):
    def fn(xv, gv):
        r = lax.rsqrt(_mean(xv * xv) + EPS)
        return [xv * r * gv, r]
    w = x.shape[1]
    return _rowwise(fn, [(x, 'row'), (g, 'bcast')], [(w, BF16), (1, F32)], name=name)


def _rms_bwd(x, r, dhn, g, dres, *, name):
    def fn(xv, rv, dv, gv, *rest):
        n = xv * rv
        dn = dv * gv
        dx = rv * (dn - n * _mean(dn * n))
        if rest:
            dx = dx + rest[0]
        return [dx, dx, _colsum(dv * n)]
    w = x.shape[1]
    ins = [(x, 'row'), (r, 'row'), (dhn, 'row'), (g, 'bcast')]
    if dres is not None:
        ins.append((dres, 'row'))
    return _rowwise(fn, ins, [(w, F32), (w, BF16)], [w], name=name)


def _rope(x, cos, sin_a, sin_b, shift, scale, out_dtype, *, name):
    w = x.shape[1]
    wr = cos.shape[1]
    lo = w - wr

    def fn(xv, cv, sav, sbv):
        xr = xv[:, lo:] if lo else xv
        y = xr * cv + pltpu.roll(xr, wr - shift, 1) * sav + pltpu.roll(xr, shift, 1) * sbv
        if lo:
            y = jnp.concatenate([xv[:, :lo], y], axis=1)
        return [y * scale]
    return _rowwise(fn, [(x, 'row'), (cos, 'per'), (sin_a, 'per'), (sin_b, 'per')], [(w, out_dtype)], name=name)[0]


def _dot(a, b, ca, cb):
    return lax.dot_general(a, b, (((ca,), (cb,)), ((), ())), preferred_element_type=F32)


def _ret_fwd(q, k, v, dec, dq_, dk_, ds_):
    H, S, d = q.shape
    T = dec.shape[1]
    nC = S // T

    def body(q_ref, k_ref, v_ref, m_ref, qd_ref, kd_ref, sd_ref, o_ref, st_ref, state):
        @pl.when(pl.program_id(1) == 0)
        def _():
            state[...] = jnp.zeros_like(state)

        st = state[...]
        st_ref[...] = st
        qv, kv, vv = q_ref[...], k_ref[...], v_ref[...]
        p = (_dot(qv, kv, 1, 1) * m_ref[...]).astype(BF16)
        qs = (qv.astype(F32) * qd_ref[...]).astype(BF16)
        o_ref[...] = _dot(p, vv, 1, 0) + _dot(qs, st.astype(BF16), 1, 0)
        ks = (kv.astype(F32) * kd_ref[...]).astype(BF16)
        state[...] = st * sd_ref[...] + _dot(ks, vv, 0, 0)

    blk = pl.BlockSpec((None, T, d), lambda h, c: (h, c, 0))
    return pl.pallas_call(
        body, name="ret_fwd", grid=(H, nC),
        in_specs=[blk, blk, blk,
                  pl.BlockSpec((None, T, T), lambda h, c: (h, 0, 0)),
                  pl.BlockSpec((None, T, 1), lambda h, c: (h, 0, 0)),
                  pl.BlockSpec((None, T, 1), lambda h, c: (h, 0, 0)),
                  pl.BlockSpec((None, 1, 1), lambda h, c: (h, 0, 0))],
        out_specs=[blk, pl.BlockSpec((None, None, d, d), lambda h, c: (h, c, 0, 0))],
        out_shape=[jax.ShapeDtypeStruct((H, S, d), F32), jax.ShapeDtypeStruct((H, nC, d, d), F32)],
        scratch_shapes=[pltpu.VMEM((d, d), F32)],
        compiler_params=_params(("parallel", "arbitrary")),
    )(q, k, v, dec, dq_, dk_, ds_)


def _ret_bwd(q, k, v, do, states, dec, dq_, dk_, ds_):
    H, S, d = q.shape
    T = dec.shape[1]
    nC = S // T

    def body(q_ref, k_ref, v_ref, do_ref, st_ref, m_ref, qd_ref, kd_ref, sd_ref, gq_ref, gk_ref, gv_ref, dstate):
        @pl.when(pl.program_id(1) == 0)
        def _():
            dstate[...] = jnp.zeros_like(dstate)

        qv, kv, vv, dov = q_ref[...], k_ref[...], v_ref[...], do_ref[...]
        m = m_ref[...]
        qd, kd = qd_ref[...], kd_ref[...]
        ds = dstate[...]
        dsb = ds.astype(BF16)
        sb = st_ref[...].astype(BF16)
        p = (_dot(qv, kv, 1, 1) * m).astype(BF16)
        da = (_dot(dov, vv, 1, 1) * m).astype(BF16)
        qs = (qv.astype(F32) * qd).astype(BF16)
        ks = (kv.astype(F32) * kd).astype(BF16)
        gq_ref[...] = _dot(da, kv, 1, 0) + _dot(dov, sb, 1, 1) * qd
        gk_ref[...] = _dot(da, qv, 0, 0) + _dot(vv, dsb, 1, 1) * kd
        gv_ref[...] = _dot(p, dov, 0, 0) + _dot(ks, dsb, 1, 0)
        dstate[...] = ds * sd_ref[...] + _dot(qs, dov, 0, 0)

    blk = pl.BlockSpec((None, T, d), lambda h, c: (h, nC - 1 - c, 0))
    out = jax.ShapeDtypeStruct((H, S, d), F32)
    return pl.pallas_call(
        body, name="ret_bwd", grid=(H, nC),
        in_specs=[blk, blk, blk, blk,
                  pl.BlockSpec((None, None, d, d), lambda h, c: (h, nC - 1 - c, 0, 0)),
                  pl.BlockSpec((None, T, T), lambda h, c: (h, 0, 0)),
                  pl.BlockSpec((None, T, 1), lambda h, c: (h, 0, 0)),
                  pl.BlockSpec((None, T, 1), lambda h, c: (h, 0, 0)),
                  pl.BlockSpec((None, 1, 1), lambda h, c: (h, 0, 0))],
        out_specs=[blk, blk, blk], out_shape=[out, out, out],
        scratch_shapes=[pltpu.VMEM((d, d), F32)],
        compiler_params=_params(("parallel", "arbitrary")),
    )(q, k, v, do, states, dec, dq_, dk_, ds_)


def _ret_tables(T, d):
    h = jnp.arange(RET_HEADS, dtype=F32)
    log_g = jnp.log1p(-jnp.exp2(-5.0 - h))
    idx = jnp.arange(T, dtype=F32)
    diff = idx[:, None] - idx[None, :]
    same = (jnp.arange(T)[:, None] // CHUNK) == (jnp.arange(T)[None, :] // CHUNK)
    earlier = (jnp.arange(T)[None, :] // CHUNK) < (jnp.arange(T)[:, None] // CHUNK)
    expo = jnp.where(same, jnp.abs(diff), diff)
    dec = jnp.where(same | earlier, jnp.exp(log_g[:, None, None] * expo[None]), 0.0)
    q_dec = jnp.exp(log_g[:, None] * (idx + 1.0))[..., None]
    k_dec = jnp.exp(log_g[:, None] * (T - 1.0 - idx))[..., None]
    s_dec = jnp.exp(log_g * T)[:, None, None]
    return dec.astype(F32), q_dec, k_dec, s_dec


NEG = -1e30


ROW_GROUP = 512


def _scores(q, kv, r, diagonal):
    s = _dot(q, kv, 1, 1)
    if diagonal:
        rg, T = s.shape
        qc = (r * rg + lax.broadcasted_iota(jnp.int32, (rg, T), 0)) // CHUNK
        kc = lax.broadcasted_iota(jnp.int32, (rg, T), 1) // CHUNK
        s = jnp.where(kc <= qc, s, NEG)
    return s


def _by_query_block(p, n):
    i = sum((p >= t * (t + 1) // 2).astype(jnp.int32) for t in range(1, n))
    return i, p - i * (i + 1) // 2


def _by_key_block(p, n):
    j = sum((p >= t * n - t * (t - 1) // 2).astype(jnp.int32) for t in range(1, n))
    return j, j + p - (j * n - j * (j - 1) // 2)


def _flash_fwd(q, k, v, side=None):
    H, S, dk = q.shape
    dv = v.shape[2]
    T = _tile(S, 512)
    n = S // T
    rg = min(ROW_GROUP, T)

    def body(q_ref, k_ref, v_ref, o_ref, lse_ref, m_s, l_s, acc):
        qi, ki = _by_query_block(pl.program_id(1), n)

        @pl.when(ki == 0)
        def _():
            m_s[...] = jnp.full_like(m_s, NEG)
            l_s[...] = jnp.zeros_like(l_s)
            acc[...] = jnp.zeros_like(acc)

        def step(diagonal):
            kv, vv = k_ref[...], v_ref[...]
            for r in range(T // rg):
                rows = pl.ds(r * rg, rg)
                s = _scores(q_ref[rows, :], kv, r, diagonal)
                m_old = m_s[rows, :]
                m_new = jnp.maximum(m_old, jnp.max(s, axis=1, keepdims=True))
                p = jnp.exp(s - m_new)
                alpha = jnp.exp(m_old - m_new)
                l_s[rows, :] = alpha * l_s[rows, :] + jnp.sum(p, axis=1, keepdims=True)
                acc[rows, :] = alpha * acc[rows, :] + _dot(p.astype(BF16), vv, 1, 0)
                m_s[rows, :] = m_new

        @pl.when(ki < qi)
        def _():
            step(False)

        @pl.when(ki == qi)
        def _():
            step(True)
            o_ref[...] = acc[...] / l_s[...]
            lse_ref[...] = m_s[...] + jnp.log(l_s[...])

    q_map = lambda h, p: (h, _by_query_block(p, n)[0], 0)
    kv_map = lambda h, p: (h, _by_query_block(p, n)[1], 0)
    return _hosted_call(
        body, side, name="mla_fwd", grid=(H, n * (n + 1) // 2),
        in_specs=[pl.BlockSpec((None, T, dk), q_map), pl.BlockSpec((None, T, dk), kv_map),
                  pl.BlockSpec((None, T, dv), kv_map)],
        out_specs=[pl.BlockSpec((None, T, dv), q_map), pl.BlockSpec((None, T, 1), q_map)],
        out_shape=[jax.ShapeDtypeStruct((H, S, dv), F32), jax.ShapeDtypeStruct((H, S, 1), F32)],
        scratch_shapes=[pltpu.VMEM((T, 1), F32), pltpu.VMEM((T, 1), F32), pltpu.VMEM((T, dv), F32)],
        operands=[q, k, v], semantics=("parallel", "arbitrary"))


def _flash_dq(q, k, v, do, lse, dlt, side=None):
    H, S, dk = q.shape
    dv = v.shape[2]
    T = _tile(S, 512)
    n = S // T
    rg = min(ROW_GROUP, T)

    def body(q_ref, k_ref, v_ref, do_ref, lse_ref, dlt_ref, dq_ref, acc):
        qi, ki = _by_query_block(pl.program_id(1), n)

        @pl.when(ki == 0)
        def _():
            acc[...] = jnp.zeros_like(acc)

        def step(diagonal):
            kv, vv = k_ref[...], v_ref[...]
            for r in range(T // rg):
                rows = pl.ds(r * rg, rg)
                p = jnp.exp(_scores(q_ref[rows, :], kv, r, diagonal) - lse_ref[rows, :])
                dp = _dot(do_ref[rows, :], vv, 1, 1)
                ds = (p * (dp - dlt_ref[rows, :])).astype(BF16)
                acc[rows, :] += _dot(ds, kv, 1, 0)

        @pl.when(ki < qi)
        def _():
            step(False)

        @pl.when(ki == qi)
        def _():
            step(True)
            dq_ref[...] = acc[...]

    q_map = lambda h, p: (h, _by_query_block(p, n)[0], 0)
    kv_map = lambda h, p: (h, _by_query_block(p, n)[1], 0)
    return _hosted_call(
        body, side, name="mla_dq", grid=(H, n * (n + 1) // 2),
        in_specs=[pl.BlockSpec((None, T, dk), q_map), pl.BlockSpec((None, T, dk), kv_map),
                  pl.BlockSpec((None, T, dv), kv_map), pl.BlockSpec((None, T, dv), q_map),
                  pl.BlockSpec((None, T, 1), q_map), pl.BlockSpec((None, T, 1), q_map)],
        out_specs=[pl.BlockSpec((None, T, dk), q_map)],
        out_shape=[jax.ShapeDtypeStruct((H, S, dk), F32)],
        scratch_shapes=[pltpu.VMEM((T, dk), F32)],
        operands=[q, k, v, do, lse, dlt], semantics=("parallel", "arbitrary"))


def _flash_dkv(q, k, v, do, lse, dlt, side=None):
    H, S, dk = q.shape
    dv = v.shape[2]
    T = _tile(S, 512)
    n = S // T
    rg = min(ROW_GROUP, T)

    def body(q_ref, k_ref, v_ref, do_ref, lse_ref, dlt_ref, dk_ref, dv_ref, acc_k, acc_v):
        ki, qi = _by_key_block(pl.program_id(1), n)

        @pl.when(qi == ki)
        def _():
            acc_k[...] = jnp.zeros_like(acc_k)
            acc_v[...] = jnp.zeros_like(acc_v)

        def step(diagonal):
            kv, vv = k_ref[...], v_ref[...]
            for r in range(T // rg):
                rows = pl.ds(r * rg, rg)
                qv, dov = q_ref[rows, :], do_ref[rows, :]
                p = jnp.exp(_scores(qv, kv, r, diagonal) - lse_ref[rows, :])
                acc_v[...] += _dot(p.astype(BF16), dov, 0, 0)
                dp = _dot(dov, vv, 1, 1)
                ds = (p * (dp - dlt_ref[rows, :])).astype(BF16)
                acc_k[...] += _dot(ds, qv, 0, 0)

        @pl.when(qi > ki)
        def _():
            step(False)

        @pl.when(qi == ki)
        def _():
            step(True)

        @pl.when(qi == n - 1)
        def _():
            dk_ref[...] = acc_k[...]
            dv_ref[...] = acc_v[...]

    q_map = lambda h, p: (h, _by_key_block(p, n)[1], 0)
    kv_map = lambda h, p: (h, _by_key_block(p, n)[0], 0)
    return _hosted_call(
        body, side, name="mla_dkv", grid=(H, n * (n + 1) // 2),
        in_specs=[pl.BlockSpec((None, T, dk), q_map), pl.BlockSpec((None, T, dk), kv_map),
                  pl.BlockSpec((None, T, dv), kv_map), pl.BlockSpec((None, T, dv), q_map),
                  pl.BlockSpec((None, T, 1), q_map), pl.BlockSpec((None, T, 1), q_map)],
        out_specs=[pl.BlockSpec((None, T, dk), kv_map), pl.BlockSpec((None, T, dv), kv_map)],
        out_shape=[jax.ShapeDtypeStruct((H, S, dk), F32), jax.ShapeDtypeStruct((H, S, dv), F32)],
        scratch_shapes=[pltpu.VMEM((T, dk), F32), pltpu.VMEM((T, dv), F32)],
        operands=[q, k, v, do, lse, dlt], semantics=("parallel", "arbitrary"))


EDGE = 8


def _shift_down(cur, prev, by):
    ts, tc = cur.shape
    rows = lax.broadcasted_iota(jnp.int32, cur.shape, 0)
    head = jnp.concatenate([pltpu.roll(prev, by, 0), jnp.zeros((ts - EDGE, tc), cur.dtype)], axis=0)
    return jnp.where(rows < by, head, pltpu.roll(cur, by, 0))


def _shift_up(cur, nxt, by):
    ts, tc = cur.shape
    rows = lax.broadcasted_iota(jnp.int32, cur.shape, 0)
    tail = jnp.concatenate([jnp.zeros((ts - EDGE, tc), cur.dtype), pltpu.roll(nxt, EDGE - by, 0)], axis=0)
    return jnp.where(rows >= ts - by, tail, pltpu.roll(cur, ts - by, 0))


def _ffn_tiles(S, F):
    return _tile(S, 256, 8), _tile(F, 1024)


def _ffn_fwd(G, U, cw, cb):
    S, F = G.shape
    ts, tc = _ffn_tiles(S, F)

    def body(g_ref, gp_ref, u_ref, cw_ref, cb_ref, a_ref, act_ref):
        cur = g_ref[...]
        prev = gp_ref[...] * (pl.program_id(1) > 0).astype(F32)
        a = (cb_ref[...] + cw_ref[0:1, :] * _shift_down(cur, prev, 2) + cw_ref[1:2, :] * _shift_down(cur, prev, 1)
             + cw_ref[2:3, :] * cur)
        a_ref[...] = a
        act_ref[...] = (a * _sigmoid(a) * u_ref[...]).astype(BF16)

    cur_spec = pl.BlockSpec((ts, tc), lambda j, i: (i, j))
    return pl.pallas_call(
        body, name="ffn_act_fwd", grid=(F // tc, S // ts),
        in_specs=[cur_spec, pl.BlockSpec((EDGE, tc), lambda j, i: (jnp.maximum(i * (ts // EDGE) - 1, 0), j)),
                  cur_spec,
                  pl.BlockSpec((CONV_WIDTH, tc), lambda j, i: (0, j)), pl.BlockSpec((1, tc), lambda j, i: (0, j))],
        out_specs=[cur_spec, cur_spec],
        out_shape=[jax.ShapeDtypeStruct((S, F), F32), jax.ShapeDtypeStruct((S, F), BF16)],
        compiler_params=_params(("parallel", "parallel")),
    )(G, G, U, cw, cb)


def _ffn_bwd_act(a, U, dact):
    S, F = a.shape
    ts, tc = _ffn_tiles(S, F)

    def body(a_ref, u_ref, d_ref, da_ref, du_ref, db_ref):
        av, dv = a_ref[...], d_ref[...]
        sg = _sigmoid(av)
        du_ref[...] = (dv * av * sg).astype(BF16)
        da = dv * u_ref[...] * sg * (1.0 + av * (1.0 - sg))
        da_ref[...] = da

        @pl.when(pl.program_id(1) == 0)
        def _():
            db_ref[...] = jnp.zeros_like(db_ref)

        db_ref[...] += _colsum(da)

    cur_spec = pl.BlockSpec((ts, tc), lambda j, i: (i, j))
    return pl.pallas_call(
        body, name="ffn_act_bwd", grid=(F // tc, S // ts),
        in_specs=[cur_spec, cur_spec, cur_spec],
        out_specs=[cur_spec, cur_spec, pl.BlockSpec((1, tc), lambda j, i: (0, j))],
        out_shape=[jax.ShapeDtypeStruct((S, F), F32), jax.ShapeDtypeStruct((S, F), BF16),
                   jax.ShapeDtypeStruct((1, F), F32)],
        compiler_params=_params(("parallel", "arbitrary")),
    )(a, U, dact)


def _ffn_bwd_conv(da, G, cw):
    S, F = da.shape
    ts, tc = _ffn_tiles(S, F)
    n = S // ts

    def body(d_ref, dn_ref, g_ref, gp_ref, cw_ref, dg_ref, dw_ref):
        i = pl.program_id(1)
        dcur = d_ref[...]
        dnxt = dn_ref[...] * (i < n - 1).astype(F32)
        cur = g_ref[...]
        prev = gp_ref[...] * (i > 0).astype(F32)
        dg = (cw_ref[2:3, :] * dcur + cw_ref[1:2, :] * _shift_up(dcur, dnxt, 1)
              + cw_ref[0:1, :] * _shift_up(dcur, dnxt, 2))
        dg_ref[...] = dg.astype(BF16)

        @pl.when(i == 0)
        def _():
            dw_ref[...] = jnp.zeros_like(dw_ref)

        dw_ref[0:1, :] += _colsum(dcur * _shift_down(cur, prev, 2))
        dw_ref[1:2, :] += _colsum(dcur * _shift_down(cur, prev, 1))
        dw_ref[2:3, :] += _colsum(dcur * cur)

    cur_spec = pl.BlockSpec((ts, tc), lambda j, i: (i, j))
    return pl.pallas_call(
        body, name="ffn_conv_bwd", grid=(F // tc, n),
        in_specs=[cur_spec,
                  pl.BlockSpec((EDGE, tc), lambda j, i: (jnp.minimum((i + 1) * (ts // EDGE), S // EDGE - 1), j)),
                  cur_spec, pl.BlockSpec((EDGE, tc), lambda j, i: (jnp.maximum(i * (ts // EDGE) - 1, 0), j)),
                  pl.BlockSpec((CONV_WIDTH, tc), lambda j, i: (0, j))],
        out_specs=[cur_spec, pl.BlockSpec((CONV_WIDTH, tc), lambda j, i: (0, j))],
        out_shape=[jax.ShapeDtypeStruct((S, F), BF16), jax.ShapeDtypeStruct((CONV_WIDTH, F), F32)],
        compiler_params=_params(("parallel", "arbitrary")),
    )(da, da, G, G, cw)


class _Cfg:
    def __init__(self, S, D):
        self.S, self.D = S, D
        self.RD = D // (2 * RET_HEADS)
        self.RW = RET_HEADS * self.RD
        self.MV = (D - self.RW) // MLA_HEADS
        self.QL, self.KVL = D // 4, D // 8
        self.F = ((8 * D // 3 + 255) // 256) * 256
        self.IN = 4 * self.RW + self.QL + self.KVL + MLA_ROPE
        self.INs = self.IN // N_CHIP
        self.INp = _round_up(self.INs, LANE)
        self.Fs = self.F // N_CHIP
        self.Fp = _round_up(self.Fs, LANE)
        self.F4 = N_CHIP * self.Fp
        self.QK = MLA_NOPE + MLA_ROPE
        self.KVH = MLA_NOPE + self.MV


def _head_major(t, H, d):
    S = t.shape[0]
    return t.reshape(S, H, d).transpose(1, 0, 2).reshape(H * S, d)


def _seq_major(t, H, d):
    S = t.shape[0] // H
    return t.reshape(H, S, d).transpose(1, 0, 2).reshape(S, H * d)


def _rope_tables(cfg):
    S = cfg.S

    def cs(dim):
        inv = 1.0 / (ROPE_BASE ** (jnp.arange(0, dim, 2, dtype=F32) / dim))
        ang = jnp.arange(S, dtype=F32)[:, None] * inv[None, :]
        return jnp.cos(ang), jnp.sin(ang)

    c, s = cs(cfg.RD)
    z = jnp.zeros_like(s)
    ret = (jnp.concatenate([c, c], 1), jnp.concatenate([-s, z], 1), jnp.concatenate([z, s], 1))
    c, s = cs(MLA_ROPE)
    tail1 = jnp.ones((S, MLA_QK_PAD - cfg.QK), F32)
    tail0 = jnp.zeros((S, MLA_QK_PAD - cfg.QK), F32)
    z = jnp.zeros_like(s)
    mla = (jnp.concatenate([c, c, tail1], 1), jnp.concatenate([-s, z, tail0], 1), jnp.concatenate([z, s, tail0], 1))
    return ret, mla


GATHER_FIRST = ('w_in',)
GATHER_BEHIND_IN_PROJ = ('w_uq', 'w_ukv', 'w_o')
GATHER_BEHIND_MLA = ('w_ffn_gate', 'w_ffn_up')
GATHER_BEHIND_FFN_GATE = ('w_ffn_down',)
GATHER_BEHIND_FFN_UP = ('w_ple_gate', 'w_ple_proj')
PAIR_BEHIND_FFN_DX = ('w_ple_proj', 'w_ple_gate', 'w_ffn_down', 'w_ffn_gate', 'w_ffn_up')
PAIR_AFTER_OUT_PROJ = ('w_o',)
REDUCE_BEHIND_DQ = ('w_ffn_gate', 'w_ple_gate', 'w_o')
REDUCE_BEHIND_DKV = ('w_ffn_up', 'w_ffn_down', 'w_ple_proj')
REDUCE_LAST = ('w_uq', 'w_ukv', 'w_in')


def _local_step(cfg, x, p, tgt, W, late, sp):
    W = dict(W)
    S, D, RD, RW, MV = cfg.S, cfg.D, cfg.RD, cfg.RW, cfg.MV
    H, MH = RET_HEADS, MLA_HEADS
    (rc, rsa, rsb), (mc, msa, msb) = _rope_tables(cfg)
    dec, q_dec, k_dec, s_dec = _ret_tables(min(RET_BLOCK, S), RD)
    k_scale = RD ** -0.5
    a_scale = cfg.QK ** -0.5
    gb, gs = {}, {}

    hn1, r1 = _rms_fwd(x, sp['g_attn'], name="rms_attn")
    proj, got = _mm_nn(hn1, W['w_in'], name="in_proj",
                       side=_gather_side([late[n] for n in GATHER_BEHIND_IN_PROJ]))
    W.update({n: _as_operand(n, g) for n, g in zip(GATHER_BEHIND_IN_PROJ, got)})
    proj = proj.reshape(S, N_CHIP, cfg.INp)[:, :, :cfg.INs].reshape(S, cfg.IN)
    cuts = [RW, 2 * RW, 3 * RW, 4 * RW, 4 * RW + cfg.QL, 4 * RW + cfg.QL + cfg.KVL]
    rq, rk, rv, rg, cq, ckv, kr = jnp.split(proj, cuts, axis=1)

    rq_h, rk_h, rv_h, rg_h = (_head_major(t, H, RD) for t in (rq, rk, rv, rg))
    q_r = _rope(rq_h, rc, rsa, rsb, RD // 2, 1.0, BF16, name="ret_rope_q").reshape(H, S, RD)
    k_r = _rope(rk_h, rc, rsa, rsb, RD // 2, k_scale, BF16, name="ret_rope_k").reshape(H, S, RD)
    v_r = rv_h.astype(BF16).reshape(H, S, RD)
    o_h, states = _ret_fwd(q_r, k_r, v_r, dec, q_dec, k_dec, s_dec)
    o_h = o_h.reshape(H * S, RD)

    def gate_fn(ov, gv):
        oc = ov - _mean(ov)
        ron = oc * lax.rsqrt(_mean(oc * oc) + EPS)
        return [gv * _sigmoid(gv) * ron]
    ro_h = _rowwise(gate_fn, [(o_h, 'row'), (rg_h, 'row')], [(RD, BF16)], name="ret_gate")[0]
    ro = _seq_major(ro_h, H, RD)

    cqn, rcq = _rms_fwd(cq, sp['g_q_lora'], name="rms_q")
    ckvn, rckv = _rms_fwd(ckv, sp['g_kv_lora'], name="rms_kv")
    Q = _mm_nn(cqn, W['w_uq'], name="q_up")
    KV = _mm_nn(ckvn, W['w_ukv'], name="kv_up")
    pad = jnp.zeros((MH, S, MLA_QK_PAD - cfg.QK), F32)
    q_raw = jnp.concatenate([Q.reshape(S, MH, cfg.QK).transpose(1, 0, 2), pad], 2).reshape(MH * S, MLA_QK_PAD)
    KV3 = KV.reshape(S, MH, cfg.KVH).transpose(1, 0, 2)
    k_raw = jnp.concatenate([KV3[:, :, :MLA_NOPE], jnp.broadcast_to(kr[None], (MH, S, MLA_ROPE)), pad], 2)
    k_raw = k_raw.reshape(MH * S, MLA_QK_PAD)
    v_m = KV3[:, :, MLA_NOPE:].astype(BF16)
    q_m = _rope(q_raw, mc, msa, msb, MLA_ROPE // 2, a_scale, BF16, name="mla_rope_q").reshape(MH, S, MLA_QK_PAD)
    k_m = _rope(k_raw, mc, msa, msb, MLA_ROPE // 2, 1.0, BF16, name="mla_rope_k").reshape(MH, S, MLA_QK_PAD)
    (mo_h, lse), got = _flash_fwd(q_m, k_m, v_m, side=_gather_side([late[n] for n in GATHER_BEHIND_MLA]))
    W.update({n: _as_operand(n, g) for n, g in zip(GATHER_BEHIND_MLA, got)})
    mo = _seq_major(mo_h.reshape(MH * S, MV), MH, MV)

    cat = jnp.concatenate([ro, mo.astype(BF16)], axis=1)
    h1 = _mm_nn(cat, W['w_o'], name="out_proj", res=x)

    hn2, r2 = _rms_fwd(h1, sp['g_ffn'], name="rms_ffn")
    G, got = _mm_nn(hn2, W['w_ffn_gate'], name="ffn_gate",
                    side=_gather_side([late[n] for n in GATHER_BEHIND_FFN_GATE]))
    W.update({n: _as_operand(n, g) for n, g in zip(GATHER_BEHIND_FFN_GATE, got)})
    U, got = _mm_nn(hn2, W['w_ffn_up'], name="ffn_up", side=_gather_side([late[n] for n in GATHER_BEHIND_FFN_UP]))
    W.update({n: _as_operand(n, g) for n, g in zip(GATHER_BEHIND_FFN_UP, got)})
    a, act = _ffn_fwd(G, U, sp['conv_w'], sp['conv_b'])
    h2 = _mm_nn(act, W['w_ffn_down'], name="ffn_down", res=h1)

    hn3, r3 = _rms_fwd(h2, sp['g_ple'], name="rms_ple")
    Z = _mm_nn(hn3, W['w_ple_gate'], name="ple_gate")
    p_b = p.astype(BF16)
    PP = _mm_nn(p_b, W['w_ple_proj'], name="ple_proj")

    def head_fn(h2v, zv, ppv, tv, gv):
        gate = _sigmoid(zv)
        h3 = h2v + gate * ppv
        r4 = lax.rsqrt(_mean(h3 * h3) + EPS)
        n4 = h3 * r4
        e = n4 * gv - tv
        dy = e * (1.0 / D)
        dn = dy * gv
        dh3 = r4 * (dn - n4 * _mean(dn * n4))
        dpp = dh3 * gate
        dz = dh3 * ppv * gate * (1.0 - gate)
        loss = jnp.sum(0.5 * _mean(e * e), axis=0, keepdims=True)
        return [dh3, dz, dpp, _colsum(dy * n4), jnp.broadcast_to(loss, (1, LANE))]
    dh3, dZ, dPP, dgf, loss = _rowwise(
        head_fn, [(h2, 'row'), (Z, 'row'), (PP, 'row'), (tgt, 'row'), (sp['g_final'], 'bcast')],
        [(D, F32), (D, BF16), (D, BF16)], [D, LANE], name="ple_loss_head")
    gs['g_final'] = dgf
    loss = loss[0, 0]

    gb['w_ple_proj'] = _mm_tn(p_b, dPP, N_CHIP, name="ple_proj_dw")
    gb['w_ple_gate'] = _mm_tn(hn3, dZ, 1, name="ple_gate_dw")
    dhn3 = _mm_nt(dZ, W['w_ple_gate'], name="ple_gate_dx")
    dh2, dh2_b, gs['g_ple'] = _rms_bwd(h2, r3, dhn3, sp['g_ple'], dh3, name="rms_ple_bwd")

    dact = _mm_nt(dh2_b, W['w_ffn_down'], name="ffn_down_dx")
    gb['w_ffn_down'] = _mm_tn(act, dh2_b, 1, name="ffn_down_dw")
    da, dU, gs['conv_b'] = _ffn_bwd_act(a, U, dact)
    dG, gs['conv_w'] = _ffn_bwd_conv(da, G, sp['conv_w'])
    gb['w_ffn_gate'] = _mm_tn(hn2, dG, N_CHIP, name="ffn_gate_dw")
    gb['w_ffn_up'] = _mm_tn(hn2, dU, N_CHIP, name="ffn_up_dw")
    def pair_sums(names, from_sibling):
        return {n: _pair_add(_grad_pieces(n, gb[n][0]), r) for n, r in zip(names, from_sibling)}

    def wire(names):
        return [_grad_pieces(n, gb[n][1]) for n in names]

    dhn2, got = _mm_nt(dG, W['w_ffn_gate'], name="ffn_gate_dx", side=_pair_side(wire(PAIR_BEHIND_FFN_DX)))
    pair = pair_sums(PAIR_BEHIND_FFN_DX, got)
    dhn2 = _mm_nt(dU, W['w_ffn_up'], name="ffn_up_dx", res=dhn2)
    dh1, dh1_b, gs['g_ffn'] = _rms_bwd(h1, r2, dhn2, sp['g_ffn'], dh2, name="rms_ffn_bwd")

    dcat = _mm_nt(dh1_b, W['w_o'], name="out_proj_dx")
    gb['w_o'] = _mm_tn(cat, dh1_b, 1, name="out_proj_dw")
    pair.update(pair_sums(PAIR_AFTER_OUT_PROJ, _pair_exchange(wire(PAIR_AFTER_OUT_PROJ))))
    from_chips = {}
    dro_h = _head_major(dcat[:, :RW], H, RD)

    def gate_bwd_fn(ov, gv, dv):
        oc = ov - _mean(ov)
        rs = lax.rsqrt(_mean(oc * oc) + EPS)
        ron = oc * rs
        sg = _sigmoid(gv)
        dron = dv * gv * sg
        drg = dv * ron * sg * (1.0 + gv * (1.0 - sg))
        do = rs * (dron - _mean(dron) - ron * _mean(dron * ron))
        return [do, drg]
    do_h, drg_h = _rowwise(gate_bwd_fn, [(o_h, 'row'), (rg_h, 'row'), (dro_h, 'row')],
                           [(RD, BF16), (RD, F32)], name="ret_gate_bwd")
    gq_r, gk_r, gv_r = _ret_bwd(q_r, k_r, v_r, do_h.reshape(H, S, RD), states, dec, q_dec, k_dec, s_dec)
    drq_h = _rope(gq_r.reshape(H * S, RD), rc, -rsa, -rsb, RD // 2, 1.0, F32, name="ret_rope_q_bwd")
    drk_h = _rope(gk_r.reshape(H * S, RD), rc, -rsa, -rsb, RD // 2, k_scale, F32, name="ret_rope_k_bwd")
    drq, drk, drv, drg = (_seq_major(t, H, RD) for t in (drq_h, drk_h, gv_r.reshape(H * S, RD), drg_h))

    dmo_h = _head_major(dcat[:, RW:], MH, MV)
    dlt = _rowwise(lambda ov, dv: [jnp.sum(ov * dv, axis=1, keepdims=True)],
                   [(mo_h.reshape(MH * S, MV), 'row'), (dmo_h, 'row')], [(1, F32)], name="mla_delta")[0]
    dmo_b = dmo_h.astype(BF16).reshape(MH, S, MV)
    dlt = dlt.reshape(MH, S, 1)
    (gq_m,), got = _flash_dq(q_m, k_m, v_m, dmo_b, lse, dlt,
                             side=_exchange_side([pair[n][1] for n in REDUCE_BEHIND_DQ]))
    from_chips.update(zip(REDUCE_BEHIND_DQ, got))
    (gk_m, gv_m), got = _flash_dkv(q_m, k_m, v_m, dmo_b, lse, dlt,
                                   side=_exchange_side([pair[n][1] for n in REDUCE_BEHIND_DKV]))
    from_chips.update(zip(REDUCE_BEHIND_DKV, got))
    dq_raw = _rope(gq_m.reshape(MH * S, MLA_QK_PAD), mc, -msa, -msb, MLA_ROPE // 2, a_scale, F32,
                   name="mla_rope_q_bwd")
    dk_raw = _rope(gk_m.reshape(MH * S, MLA_QK_PAD), mc, -msa, -msb, MLA_ROPE // 2, 1.0, F32, name="mla_rope_k_bwd")
    dQ = dq_raw.reshape(MH, S, MLA_QK_PAD)[:, :, :cfg.QK].transpose(1, 0, 2).reshape(S, MH * cfg.QK).astype(BF16)
    dk3 = dk_raw.reshape(MH, S, MLA_QK_PAD)
    dKV = jnp.concatenate([dk3[:, :, :MLA_NOPE], gv_m], 2).transpose(1, 0, 2).reshape(S, MH * cfg.KVH).astype(BF16)
    dkr_heads = [(dk3[h, :, MLA_NOPE:cfg.QK], 'row') for h in range(MH)]
    dkr = _rowwise(lambda *v: [functools.reduce(lambda s, t: s + t, v)], dkr_heads, [(MLA_ROPE, F32)],
                   name="mla_rope_k_heads")[0]

    gb['w_uq'] = _mm_tn(cqn, dQ, N_CHIP, name="q_up_dw")
    dcqn = _mm_nt(dQ, W['w_uq'], name="q_up_dx")
    dcq, _, gs['g_q_lora'] = _rms_bwd(cq, rcq, dcqn, sp['g_q_lora'], None, name="rms_q_bwd")
    gb['w_ukv'] = _mm_tn(ckvn, dKV, N_CHIP, name="kv_up_dw")
    dckvn = _mm_nt(dKV, W['w_ukv'], name="kv_up_dx")
    dckv, _, gs['g_kv_lora'] = _rms_bwd(ckv, rckv, dckvn, sp['g_kv_lora'], None, name="rms_kv_bwd")

    dproj = jnp.concatenate([drq, drk, drv, drg, dcq, dckv, dkr], axis=1).astype(BF16)
    dproj = jnp.pad(dproj.reshape(S, N_CHIP, cfg.INs), ((0, 0), (0, 0), (0, cfg.INp - cfg.INs)))
    dproj = dproj.reshape(S, N_CHIP * cfg.INp)
    gb['w_in'] = _mm_tn(hn1, dproj, N_CHIP, name="in_proj_dw")
    pair.update(pair_sums(REDUCE_LAST, _pair_exchange(wire(REDUCE_LAST))))
    dhn1, got = _mm_nt(dproj, W['w_in'], name="in_proj_dx", side=_exchange_side([pair[n][1] for n in REDUCE_LAST]))
    from_chips.update(zip(REDUCE_LAST, got))
    dx, _, gs['g_attn'] = _rms_bwd(x, r1, dhn1, sp['g_attn'], dh1, name="rms_attn_bwd")
    return loss, dx, {n: (pair[n][0], from_chips[n]) for n in BIG}, gs


def _padded_shape(name, shape):
    K, n = shape
    if name in COL_SHARDED:
        return K, _round_up(n, LANE)
    return _round_up(K, LANE), n


def _pad_shard(name, w):
    K, n = _padded_shape(name, w.shape)
    return jnp.pad(w.astype(BF16), ((0, K - w.shape[0]), (0, n - w.shape[1])))


def _as_operand(name, gathered):
    if name in COL_SHARDED:
        return gathered
    return gathered.reshape(1, gathered.shape[0] * gathered.shape[1], gathered.shape[2])


def _grad_pieces(name, g):
    if name in COL_SHARDED:
        return g
    return g.reshape(N_CHIP, g.shape[1] // N_CHIP, g.shape[2])


def _channels_padded(v, cfg):
    r = v.shape[0]
    return jnp.pad(v.reshape(r, N_CHIP, cfg.Fs), ((0, 0), (0, 0), (0, cfg.Fp - cfg.Fs))).reshape(r, cfg.F4)


def _channels_unpadded(v, cfg):
    r = v.shape[0]
    return v.reshape(r, N_CHIP, cfg.Fp)[:, :, :cfg.Fs].reshape(r, cfg.F)


def _mesh_pos():
    return lax.axis_index("x"), lax.axis_index("y"), lax.axis_index("c")


def _other_chips(x, y):
    return [(1 - x, y), (x, 1 - y), (1 - x, 1 - y)]


def _chip_id(cx, cy):
    return 2 * cx + cy


def _half(ref_rows, core):
    half = ref_rows // 2
    return pl.ds(core * half, half)


def _gather_side(shards):
    n = len(shards)

    def copies(srcs, outs, sems):
        ici_send, ici_recv, d2d_send, d2d_recv, own_send, own_recv = sems
        x, y, c = _mesh_pos()
        mine = _chip_id(x, y)
        others = _other_chips(x, y)

        def over_ici(w, j, chip):
            rows = _half(srcs[w].shape[0], c)
            return pltpu.make_async_remote_copy(
                src_ref=srcs[w].at[rows], dst_ref=outs[w].at[chip, rows],
                send_sem=ici_send.at[w, j], recv_sem=ici_recv.at[w, j],
                device_id=(*others[j], c), device_id_type=MESH)

        def over_d2d(w, j, core):
            rows = _half(srcs[w].shape[0], core)
            slab = outs[w].at[_chip_id(*others[j]), rows]
            return pltpu.make_async_remote_copy(
                src_ref=slab, dst_ref=slab, send_sem=d2d_send.at[w, j], recv_sem=d2d_recv.at[w, j],
                device_id=(x, y, 1 - c), device_id_type=MESH)

        def own(w):
            return pltpu.make_async_remote_copy(
                src_ref=srcs[w], dst_ref=outs[w].at[mine], send_sem=own_send.at[w], recv_sem=own_recv.at[w],
                device_id=(x, y, 1 - c), device_id_type=MESH)

        return c, mine, others, over_ici, over_d2d, own

    def start(srcs, outs, sems):
        c, mine, others, over_ici, over_d2d, own = copies(srcs, outs, sems)
        for w in range(n):
            for j in range(3):
                over_ici(w, j, mine).start()
        for w in range(n):
            own(w).start()

    def finish(srcs, outs, sems):
        c, mine, others, over_ici, over_d2d, own = copies(srcs, outs, sems)
        for w in range(n):
            for j in range(3):
                over_ici(w, j, _chip_id(*others[j])).wait_recv()
                over_d2d(w, j, c).start()
        for w in range(n):
            for j in range(3):
                over_d2d(w, j, 1 - c).wait_recv()
        for w in range(n):
            own(w).wait()
            for j in range(3):
                over_ici(w, j, mine).wait_send()
                over_d2d(w, j, c).wait_send()

    out_shape = [jax.ShapeDtypeStruct((N_CHIP,) + s.shape, s.dtype) for s in shards]
    sems = [pltpu.SemaphoreType.DMA((n, 3))] * 4 + [pltpu.SemaphoreType.DMA((n,))] * 2
    return _Side(shards, out_shape, sems, start, finish)


def _gather_weights(shards):
    return _run_side(_gather_side(shards), "gather_weights")


def _pair_side(grads):
    n = len(grads)

    def copies(srcs, outs, sems):
        send, recv = sems
        x, y, c = _mesh_pos()
        return [pltpu.make_async_remote_copy(
            src_ref=srcs[w].at[:, _half(srcs[w].shape[1], 1 - c), :], dst_ref=outs[w],
            send_sem=send.at[w], recv_sem=recv.at[w],
            device_id=(x, y, 1 - c), device_id_type=MESH) for w in range(n)]

    def start(srcs, outs, sems):
        for cp in copies(srcs, outs, sems):
            cp.start()

    def finish(srcs, outs, sems):
        for cp in copies(srcs, outs, sems):
            cp.wait()

    out_shape = [jax.ShapeDtypeStruct((g.shape[0], g.shape[1] // 2, g.shape[2]), g.dtype) for g in grads]
    return _Side(grads, out_shape, [pltpu.SemaphoreType.DMA((n,))] * 2, start, finish)


def _pair_exchange(grads):
    return _run_side(_pair_side(grads), "grad_pair_exchange")


def _exchange_side(parts):
    n = len(parts)

    def copies(srcs, outs, sems):
        send, recv = sems
        x, y, c = _mesh_pos()
        others = _other_chips(x, y)
        return [pltpu.make_async_remote_copy(
            src_ref=srcs[w].at[_chip_id(*others[j])], dst_ref=outs[w].at[j],
            send_sem=send.at[w, j], recv_sem=recv.at[w, j],
            device_id=(*others[j], c), device_id_type=MESH) for w in range(n) for j in range(3)]

    def start(srcs, outs, sems):
        for cp in copies(srcs, outs, sems):
            cp.start()

    def finish(srcs, outs, sems):
        for cp in copies(srcs, outs, sems):
            cp.wait()

    out_shape = [jax.ShapeDtypeStruct((3,) + p.shape[1:], p.dtype) for p in parts]
    return _Side(parts, out_shape, [pltpu.SemaphoreType.DMA((n, 3))] * 2, start, finish)


def _sibling_share(shards):
    n = len(shards)

    def body(*refs):
        outs = refs[n:2 * n]
        send, recv = refs[2 * n:]
        x, y, c = _mesh_pos()

        def half_of(w, core):
            rows = outs[w].at[_half(outs[w].shape[0], core)]
            return pltpu.make_async_remote_copy(
                src_ref=rows, dst_ref=rows, send_sem=send.at[w], recv_sem=recv.at[w],
                device_id=(x, y, 1 - c), device_id_type=MESH)

        for w in range(n):
            half_of(w, c).start()
        for w in range(n):
            half_of(w, 1 - c).wait_recv()
        for w in range(n):
            half_of(w, c).wait_send()

    return pl.pallas_call(
        body, name="grad_sibling_share", in_specs=[HBM] * n, out_specs=[HBM] * n,
        out_shape=[jax.ShapeDtypeStruct(s.shape, s.dtype) for s in shards],
        scratch_shapes=[pltpu.SemaphoreType.DMA((n,))] * 2, input_output_aliases={w: w for w in range(n)},
    )(*shards)


N_DEV = 8


def _gather_small(v):
    r, width = v.shape

    def body(v_ref, out_ref, send_sems, recv_sems, local_sem):
        x, y, c = _mesh_pos()
        me, sibling = (x, y, c), (x, y, 1 - c)
        chips = _other_chips(x, y)

        def rows(px, py, pc):
            return out_ref.at[pl.ds((4 * px + 2 * py + pc) * r, r), :]

        def copy(k, block, to, src=None):
            return pltpu.make_async_remote_copy(
                src_ref=rows(*block) if src is None else src, dst_ref=rows(*block),
                send_sem=send_sems.at[k], recv_sem=recv_sems.at[k], device_id=to, device_id_type=MESH)

        mine = pltpu.make_async_copy(v_ref, rows(*me), local_sem)
        mine.start()
        first = [copy(0, me, sibling, src=v_ref)]
        first += [copy(1 + j, me, (*chip, c), src=v_ref) for j, chip in enumerate(chips)]
        for cp in first:
            cp.start()
        passed = [copy(4 + j, (*chip, c), sibling) for j, chip in enumerate(chips)]
        for j, chip in enumerate(chips):
            copy(1 + j, (*chip, c), me).wait_recv()
            passed[j].start()
        copy(0, sibling, me).wait_recv()
        for j, chip in enumerate(chips):
            copy(4 + j, (*chip, 1 - c), me).wait_recv()
        for cp in first + passed:
            cp.wait_send()
        mine.wait()

    vmem = pl.BlockSpec(memory_space=pltpu.VMEM)
    return pl.pallas_call(
        body, name="gather_small", out_shape=jax.ShapeDtypeStruct((N_DEV * r, width), v.dtype),
        in_specs=[vmem], out_specs=vmem,
        scratch_shapes=[pltpu.SemaphoreType.DMA((7,)), pltpu.SemaphoreType.DMA((7,)), pltpu.SemaphoreType.DMA],
    )(v)


def _pack(arrays):
    flat = jnp.concatenate([a.reshape(-1) for a in arrays])
    size = _round_up(flat.shape[0], 8 * LANE)
    return jnp.pad(flat, (0, size - flat.shape[0])).reshape(size // LANE, LANE)


def _unpack(packed, shapes):
    flat = packed.reshape(-1)
    out, at = [], 0
    for s in shapes:
        size = math.prod(s)
        out.append(flat[at:at + size].reshape(s))
        at += size
    return out


def _sum_devices(gathered):
    r = gathered.shape[0] // N_DEV
    blocks = [(gathered[d * r:(d + 1) * r], 'row') for d in range(N_DEV)]
    return _rowwise(lambda *v: [functools.reduce(lambda s, t: s + t, v)], blocks, [(LANE, F32)],
                    name="small_grad_sum")[0]


def _reduce_tiles(half, n):
    return _tile(half, 128, 8)


def _pair_add(g32, r1):
    G, K, n = g32.shape
    half = K // 2
    tr = _reduce_tiles(half, n)
    nrt = half // tr

    def body(g_ref, r_ref, s32_ref, sb_ref):
        s = g_ref[...] + r_ref[...].astype(F32)
        s32_ref[...] = s
        sb_ref[...] = s.astype(BF16)

    blk = pl.BlockSpec((None, tr, n), lambda k, i: (k, i, 0))
    return pl.pallas_call(
        body, name="grad_pair_add", grid=(G, nrt),
        in_specs=[pl.BlockSpec((None, tr, n), lambda k, i: (k, lax.axis_index("c") * nrt + i, 0)), blk],
        out_specs=[blk, blk],
        out_shape=[jax.ShapeDtypeStruct((G, half, n), F32), jax.ShapeDtypeStruct((G, half, n), BF16)],
        compiler_params=_params(("parallel", "parallel")),
    )(g32, r1)


def _chip_add(s32, r2):
    _, half, n = s32.shape
    tr = _reduce_tiles(half, n)
    nrt = half // tr

    def body(s_ref, a_ref, b_ref, c_ref, o_ref):
        o_ref[...] = ((s_ref[...] + a_ref[...].astype(F32)) + b_ref[...].astype(F32)) + c_ref[...].astype(F32)

    def piece(j):
        return pl.BlockSpec((None, tr, n), lambda i, j=j: (j, i, 0))

    def mine(i):
        return _chip_id(lax.axis_index("x"), lax.axis_index("y")), i, 0

    return pl.pallas_call(
        body, name="grad_chip_add", grid=(nrt,),
        in_specs=[pl.BlockSpec((None, tr, n), mine), piece(0), piece(1), piece(2)],
        out_specs=pl.BlockSpec((tr, n), lambda i: (lax.axis_index("c") * nrt + i, 0)),
        out_shape=jax.ShapeDtypeStruct((2 * half, n), F32),
        compiler_params=_params(("parallel",)),
    )(s32, r2, r2, r2)


def _adamw_math(wv, gv, mv, vv):
    m2 = ADAM_B1 * mv + (1.0 - ADAM_B1) * gv
    v2 = ADAM_B2 * vv + (1.0 - ADAM_B2) * (gv * gv)
    m_hat = m2 / (1.0 - ADAM_B1 ** ADAM_STEP)
    v_hat = v2 / (1.0 - ADAM_B2 ** ADAM_STEP)
    delta = -ADAM_LR * (m_hat / (jnp.sqrt(v_hat) + ADAM_EPS) + ADAM_WD * wv)
    return [delta, m2, v2]


def _adamw(w, g, m, v, *, name):
    width = w.shape[1]
    return _rowwise(_adamw_math, [(w, 'row'), (g, 'row'), (m, 'row'), (v, 'row')], [(width, F32)] * 3, name=name)


def _adamw_sharded(w, g_padded, m, v, *, name):
    _, K, n = w.shape
    n_pad = g_padded.shape[1]
    ts = 8
    while ts * 2 <= 256 and ts * 2 * 8 * n_pad * 4 <= ROWWISE_BLOCK_BYTES and K % (ts * 2) == 0:
        ts *= 2

    def body(w_ref, g_ref, m_ref, v_ref, go_ref, d_ref, mo_ref, vo_ref):
        gv = g_ref[:, :n] if n != n_pad else g_ref[...]
        go_ref[...] = gv
        d_ref[...], mo_ref[...], vo_ref[...] = _adamw_math(w_ref[...], gv, m_ref[...], v_ref[...])

    blk = pl.BlockSpec((None, ts, n), lambda i: (0, i, 0))
    out = jax.ShapeDtypeStruct((1, K, n), F32)
    return pl.pallas_call(
        body, name=name, grid=(K // ts,),
        in_specs=[blk, pl.BlockSpec((ts, n_pad), lambda i: (i, 0)), blk, blk],
        out_specs=[blk] * 4, out_shape=[out] * 4,
        compiler_params=_params(("parallel",)),
    )(w, g_padded, m, v)


def kernel(x, p, w_in, g_attn, g_q_lora, g_kv_lora, w_uq, w_ukv, w_o, g_ffn, w_ffn_gate, w_ffn_up, conv_w, conv_b, w_ffn_down, g_ple, w_ple_gate, w_ple_proj, g_final, loss_target, m_w_in, m_g_attn, m_g_q_lora, m_g_kv_lora, m_w_uq, m_w_ukv, m_w_o, m_g_ffn, m_w_ffn_gate, m_w_ffn_up, m_conv_w, m_conv_b, m_w_ffn_down, m_g_ple, m_w_ple_gate, m_w_ple_proj, m_g_final, v_w_in, v_g_attn, v_g_q_lora, v_g_kv_lora, v_w_uq, v_w_ukv, v_w_o, v_g_ffn, v_w_ffn_gate, v_w_ffn_up, v_conv_w, v_conv_b, v_w_ffn_down, v_g_ple, v_w_ple_gate, v_w_ple_proj, v_g_final):
    weights = dict(w_in=w_in, g_attn=g_attn, g_q_lora=g_q_lora, g_kv_lora=g_kv_lora, w_uq=w_uq, w_ukv=w_ukv, w_o=w_o,
                   g_ffn=g_ffn, w_ffn_gate=w_ffn_gate, w_ffn_up=w_ffn_up, conv_w=conv_w, conv_b=conv_b,
                   w_ffn_down=w_ffn_down, g_ple=g_ple, w_ple_gate=w_ple_gate, w_ple_proj=w_ple_proj, g_final=g_final)
    mom1 = dict(w_in=m_w_in, g_attn=m_g_attn, g_q_lora=m_g_q_lora, g_kv_lora=m_g_kv_lora, w_uq=m_w_uq, w_ukv=m_w_ukv,
                w_o=m_w_o, g_ffn=m_g_ffn, w_ffn_gate=m_w_ffn_gate, w_ffn_up=m_w_ffn_up, conv_w=m_conv_w,
                conv_b=m_conv_b, w_ffn_down=m_w_ffn_down, g_ple=m_g_ple, w_ple_gate=m_w_ple_gate,
                w_ple_proj=m_w_ple_proj, g_final=m_g_final)
    mom2 = dict(w_in=v_w_in, g_attn=v_g_attn, g_q_lora=v_g_q_lora, g_kv_lora=v_g_kv_lora, w_uq=v_w_uq, w_ukv=v_w_ukv,
                w_o=v_w_o, g_ffn=v_g_ffn, w_ffn_gate=v_w_ffn_gate, w_ffn_up=v_w_ffn_up, conv_w=v_conv_w,
                conv_b=v_conv_b, w_ffn_down=v_w_ffn_down, g_ple=v_g_ple, w_ple_gate=v_w_ple_gate,
                w_ple_proj=v_w_ple_proj, g_final=v_g_final)
    _, S, D = x.shape
    cfg = _Cfg(S, D)
    cx, cy, _ = _mesh_pos()

    shards = {name: _pad_shard(name, weights[name][0]) for name in BIG}
    gathered = _gather_weights([shards[name] for name in GATHER_FIRST])
    W = {name: _as_operand(name, g) for name, g in zip(GATHER_FIRST, gathered)}
    cw_all = _gather_small(_pack([jnp.pad(conv_w[0], ((0, 0), (0, cfg.Fp - cfg.Fs)))]))
    r_cw = cw_all.shape[0] // N_DEV
    cw_chips = [_unpack(cw_all[2 * k * r_cw:(2 * k + 1) * r_cw], [(CONV_WIDTH, cfg.Fp)])[0] for k in range(N_CHIP)]
    sp = dict(g_attn=g_attn, g_q_lora=g_q_lora, g_kv_lora=g_kv_lora, g_ffn=g_ffn, g_ple=g_ple,
              g_final=g_final.reshape(1, D), conv_w=jnp.concatenate(cw_chips, axis=1),
              conv_b=_channels_padded(conv_b, cfg))

    loss, dx, parts, gs = _local_step(cfg, x[0], p[0, 0], loss_target[0], W, shards, sp)
    loss = lax.psum(loss, ("x", "y", "c"))

    halves = [_chip_add(*parts[name]) for name in BIG]
    whole = dict(zip(BIG, _sibling_share(halves)))
    grads = {}

    small_names = ['g_attn', 'g_q_lora', 'g_kv_lora', 'g_ffn', 'g_ple', 'g_final', 'conv_b', 'conv_w']
    small_sum = _sum_devices(_gather_small(_pack([gs[name] for name in small_names])))
    for name, g in zip(small_names, _unpack(small_sum, [gs[name].shape for name in small_names])):
        grads[name] = g
    grads['g_final'] = grads['g_final'].reshape(D)
    grads['conv_b'] = _channels_unpadded(grads['conv_b'], cfg)
    mine = _chip_id(cx, cy)
    grads['conv_w'] = lax.dynamic_slice_in_dim(grads['conv_w'], mine * cfg.Fp, cfg.Fp, axis=1)[:, :cfg.Fs]
    grads['conv_w'] = grads['conv_w'].reshape(1, CONV_WIDTH, cfg.Fs)

    delta, new_m, new_v = {}, {}, {}
    for name in BIG:
        operands = (weights[name], whole[name], mom1[name], mom2[name])
        flip = weights[name].shape[2] % LANE != 0
        if flip:
            operands = tuple(a.T if a.ndim == 2 else a.transpose(0, 2, 1) for a in operands)
        outs = _adamw_sharded(*operands, name="adamw_" + name)
        if flip:
            outs = [o.transpose(0, 2, 1) for o in outs]
        grads[name], delta[name], new_m[name], new_v[name] = outs
    for name in WEIGHTS:
        if name in BIG:
            continue
        shape = weights[name].shape
        flat = (shape[-2], shape[-1]) if len(shape) == 3 else (1, shape[-1])
        d, m2, v2 = _adamw(weights[name].reshape(flat), grads[name].reshape(flat), mom1[name].reshape(flat),
                           mom2[name].reshape(flat), name="adamw_" + name)
        delta[name], new_m[name], new_v[name] = d.reshape(shape), m2.reshape(shape), v2.reshape(shape)

    return (loss, dx.reshape(1, S, D), *[grads[n] for n in WEIGHTS], *[delta[n] for n in WEIGHTS],
            *[new_m[n] for n in WEIGHTS], *[new_v[n] for n in WEIGHTS])
```

```python
import functools
import math

import jax
import jax.numpy as jnp
from jax import lax
from jax.experimental import pallas as pl
from jax.experimental.pallas import tpu as pltpu

F32 = jnp.float32
BF16 = jnp.bfloat16

LANE = 128
VMEM_LIMIT = 56 * 1024 * 1024
ROWWISE_BLOCK_BYTES = 8 * 1024 * 1024
ROWWISE_MAX_ROWS = 2048
MM_TILE = 1024
MM_TILE_N = 1408
MM_TILE_CONTRACT = 2816
MM_TILE_CONTRACT_N = 2816

N_CHIP = 4
MESH = pl.DeviceIdType.MESH

CHUNK = 64
RET_HEADS = 8
MLA_HEADS = 16
MLA_NOPE = 128
MLA_ROPE = 64
MLA_QK_PAD = 256
PLE_DIM = 256
CONV_WIDTH = 3
ROPE_BASE = 10000.0
EPS = 1e-6
RET_BLOCK = 256

ADAM_LR = 0.001
ADAM_B1 = 0.9
ADAM_B2 = 0.999
ADAM_EPS = 1e-08
ADAM_WD = 0.01
ADAM_STEP = 10

WEIGHTS = ['w_in', 'g_attn', 'g_q_lora', 'g_kv_lora', 'w_uq', 'w_ukv', 'w_o', 'g_ffn', 'w_ffn_gate', 'w_ffn_up',
           'conv_w', 'conv_b', 'w_ffn_down', 'g_ple', 'w_ple_gate', 'w_ple_proj', 'g_final']
COL_SHARDED = ('w_in', 'w_uq', 'w_ukv', 'w_ffn_gate', 'w_ffn_up', 'w_ple_proj')
ROW_SHARDED = ('w_o', 'w_ffn_down', 'w_ple_gate')
BIG = COL_SHARDED + ROW_SHARDED
SMALL_REPLICATED = ('g_attn', 'g_q_lora', 'g_kv_lora', 'g_ffn', 'conv_b', 'g_ple', 'g_final')


def _round_up(n, m):
    return (n + m - 1) // m * m


def _tile(dim, cap, align=LANE):
    if dim <= cap:
        return dim
    t = cap // align * align
    while t >= align:
        if dim % t == 0:
            return t
        t -= align
    return dim


def _params(sem):
    return pltpu.CompilerParams(dimension_semantics=sem, vmem_limit_bytes=VMEM_LIMIT)


HBM = pl.BlockSpec(memory_space=pltpu.HBM)


class _Side:
    def __init__(self, ins, out_shape, sems, start, finish):
        self.ins, self.out_shape, self.sems, self.start, self.finish = ins, out_shape, sems, start, finish


def _run_side(side, name):
    n_in, n_out = len(side.ins), len(side.out_shape)

    def body(*refs):
        ins, outs, sems = refs[:n_in], refs[n_in:n_in + n_out], refs[n_in + n_out:]
        side.start(ins, outs, sems)
        side.finish(ins, outs, sems)

    return pl.pallas_call(
        body, name=name, in_specs=[HBM] * n_in, out_specs=[HBM] * n_out, out_shape=side.out_shape,
        scratch_shapes=side.sems,
    )(*side.ins)


def _hosted_call(body, side, *, name, grid, in_specs, out_specs, out_shape, scratch_shapes, operands, semantics):
    if side is None:
        res = pl.pallas_call(
            body, name=name, grid=grid, in_specs=in_specs, out_specs=out_specs, out_shape=out_shape,
            scratch_shapes=scratch_shapes, compiler_params=_params(semantics))(*operands)
        return res, []
    n_in, n_out, n_scr = len(in_specs), len(out_specs), len(scratch_shapes)
    s_in, s_out = len(side.ins), len(side.out_shape)

    def hosted(*refs):
        refs = list(refs)
        ins, refs = refs[:n_in], refs[n_in:]
        side_ins, refs = refs[:s_in], refs[s_in:]
        outs, refs = refs[:n_out], refs[n_out:]
        side_outs, refs = refs[:s_out], refs[s_out:]
        scratch, sems = refs[:n_scr], refs[n_scr:]
        ids = [pl.program_id(a) for a in range(len(grid))]
        first = functools.reduce(jnp.logical_and, [i == 0 for i in ids])
        last = functools.reduce(jnp.logical_and, [i == g - 1 for i, g in zip(ids, grid)])

        @pl.when(first)
        def _():
            side.start(side_ins, side_outs, sems)

        body(*ins, *outs, *scratch)

        @pl.when(last)
        def _():
            side.finish(side_ins, side_outs, sems)

    res = pl.pallas_call(
        hosted, name=name, grid=grid, in_specs=list(in_specs) + [HBM] * s_in,
        out_specs=list(out_specs) + [HBM] * s_out, out_shape=list(out_shape) + list(side.out_shape),
        scratch_shapes=list(scratch_shapes) + list(side.sems),
        compiler_params=_params(("arbitrary",) * len(grid)))(*operands, *side.ins)
    return res[:n_out], res[n_out:]


def _mm_call(name, dims, grid, in_specs, out_specs, out_shape, acc_shape, operands, has_res, side=None):
    nsteps = grid[2]
    n_out = len(out_shape)

    def body(*refs):
        a_ref, b_ref = refs[0], refs[1]
        res_ref = refs[2] if has_res else None
        outs = refs[2 + has_res:2 + has_res + n_out]
        acc = refs[2 + has_res + n_out]
        k = pl.program_id(2)

        @pl.when(k == 0)
        def _():
            acc[...] = jnp.zeros_like(acc)

        acc[...] += lax.dot_general(a_ref[...], b_ref[...], (dims, ((), ())), preferred_element_type=F32)

        @pl.when(k == nsteps - 1)
        def _():
            r = acc[...]
            if has_res:
                r = r + res_ref[...]
            for o in outs:
                o[...] = r.astype(o.dtype)

    outs, side_outs = _hosted_call(
        body, side, name=name, grid=grid, in_specs=in_specs, out_specs=out_specs, out_shape=out_shape,
        scratch_shapes=[pltpu.VMEM(acc_shape, F32)], operands=operands,
        semantics=("parallel", "parallel", "arbitrary"))
    return (outs, side_outs) if side is not None else outs


def _mm_nn(a, w, *, name, res=None, out_dtype=F32, side=None):
    M, K = a.shape
    G, _, n = w.shape
    tm, tn, tk = _tile(M, MM_TILE), _tile(n, MM_TILE_N), _tile(K, MM_TILE_CONTRACT)
    npg = n // tn
    grid = (M // tm, G * npg, K // tk)
    in_specs = [pl.BlockSpec((tm, tk), lambda i, j, k: (i, k)),
                pl.BlockSpec((None, tk, tn), lambda i, j, k: (j // npg, k, j % npg))]
    operands = [a, w]
    if res is not None:
        in_specs.append(pl.BlockSpec((tm, tn), lambda i, j, k: (i, j)))
        operands.append(res)
    out_specs = [pl.BlockSpec((tm, tn), lambda i, j, k: (i, j))]
    out_shape = [jax.ShapeDtypeStruct((M, G * n), out_dtype)]
    got = _mm_call(name, ((1,), (0,)), grid, in_specs, out_specs, out_shape, (tm, tn), operands, res is not None, side)
    return (got[0][0], got[1]) if side is not None else got[0]


def _mm_nt(g, w, *, name, res=None, out_dtype=F32, side=None):
    M, _ = g.shape
    G, K, n = w.shape
    tm, tko, tn = _tile(M, MM_TILE), _tile(K, MM_TILE), _tile(n, MM_TILE_CONTRACT_N)
    npg = n // tn
    grid = (M // tm, K // tko, G * npg)
    in_specs = [pl.BlockSpec((tm, tn), lambda i, j, k: (i, k)),
                pl.BlockSpec((None, tko, tn), lambda i, j, k: (k // npg, j, k % npg))]
    operands = [g, w]
    if res is not None:
        in_specs.append(pl.BlockSpec((tm, tko), lambda i, j, k: (i, j)))
        operands.append(res)
    out_specs = [pl.BlockSpec((tm, tko), lambda i, j, k: (i, j))]
    out_shape = [jax.ShapeDtypeStruct((M, K), out_dtype)]
    got = _mm_call(name, ((1,), (1,)), grid, in_specs, out_specs, out_shape, (tm, tko), operands, res is not None, side)
    return (got[0][0], got[1]) if side is not None else got[0]


def _mm_tn(a, g, groups, *, name):
    M, K = a.shape
    n = g.shape[1] // groups
    tm, tko, tn = _tile(M, MM_TILE_CONTRACT), _tile(K, MM_TILE), _tile(n, MM_TILE_N)
    npg = n // tn
    grid = (K // tko, groups * npg, M // tm)
    in_specs = [pl.BlockSpec((tm, tko), lambda i, j, k: (k, i)),
                pl.BlockSpec((tm, tn), lambda i, j, k: (k, j))]
    out_spec = pl.BlockSpec((None, tko, tn), lambda i, j, k: (j // npg, i, j % npg))
    out_shape = [jax.ShapeDtypeStruct((groups, K, n), F32), jax.ShapeDtypeStruct((groups, K, n), BF16)]
    return _mm_call(name, ((0,), (0,)), grid, in_specs, [out_spec, out_spec], out_shape, (tko, tn), [a, g], False)


def _rowwise(fn, ins, outs, accs=(), *, name):
    R = next(a.shape[0] for a, kind in ins if kind == 'row')
    row_bytes = sum(a.shape[1] * a.dtype.itemsize for a, kind in ins if kind != 'bcast')
    row_bytes += sum(w * jnp.dtype(dt).itemsize for w, dt in outs)
    ts = 8
    while ts * 2 <= ROWWISE_MAX_ROWS and ts * 2 * row_bytes <= ROWWISE_BLOCK_BYTES:
        ts *= 2
    ts = min(ts, R)
    for a, kind in ins:
        if kind == 'per':
            ts = math.gcd(ts, a.shape[0])
    while R % ts:
        ts //= 2
    in_specs = []
    for a, kind in ins:
        w = a.shape[1]
        if kind == 'row':
            in_specs.append(pl.BlockSpec((ts, w), lambda i: (i, 0)))
        elif kind == 'bcast':
            in_specs.append(pl.BlockSpec((1, w), lambda i: (0, 0)))
        else:
            nper = a.shape[0] // ts
            in_specs.append(pl.BlockSpec((ts, w), lambda i, nper=nper: (i % nper, 0)))
    out_specs = [pl.BlockSpec((ts, w), lambda i: (i, 0)) for w, _ in outs]
    out_specs += [pl.BlockSpec((1, w), lambda i: (0, 0)) for w in accs]
    out_shape = [jax.ShapeDtypeStruct((R, w), dt) for w, dt in outs]
    out_shape += [jax.ShapeDtypeStruct((1, w), F32) for w in accs]
    n_in, n_out = len(ins), len(outs)

    def body(*refs):
        vals = [r[...] for r in refs[:n_in]]
        res = fn(*vals)
        for r, v in zip(refs[n_in:n_in + n_out], res[:n_out]):
            r[...] = v.astype(r.dtype)
        if accs:
            first = pl.program_id(0) == 0
            for r, v in zip(refs[n_in + n_out:], res[n_out:]):
                @pl.when(first)
                def _(r=r, v=v):
                    r[...] = v

                @pl.when(jnp.logical_not(first))
                def _(r=r, v=v):
                    r[...] += v

    return pl.pallas_call(
        body, name=name, grid=(R // ts,), in_specs=in_specs, out_specs=out_specs, out_shape=out_shape,
        compiler_params=_params(("arbitrary",) if accs else ("parallel",)),
    )(*[a for a, _ in ins])


def _mean(v):
    return jnp.mean(v, axis=-1, keepdims=True)


def _colsum(v):
    return jnp.sum(v, axis=0, keepdims=True)


def _sigmoid(v):
    return 1.0 / (1.0 + jnp.exp(-v))


def _rms_fwd(x, g, *, name):
    def fn(xv, gv):
        r = lax.rsqrt(_mean(xv * xv) + EPS)
        return [xv * r * gv, r]
    w = x.shape[1]
    return _rowwise(fn, [(x, 'row'), (g, 'bcast')], [(w, BF16), (1, F32)], name=name)


def _rms_bwd(x, r, dhn, g, dres, *, name):
    def fn(xv, rv, dv, gv, *rest):
        n = xv * rv
        dn = dv * gv
        dx = rv * (dn - n * _mean(dn * n))
        if rest:
            dx = dx + rest[0]
        return [dx, dx, _colsum(dv * n)]
    w = x.shape[1]
    ins = [(x, 'row'), (r, 'row'), (dhn, 'row'), (g, 'bcast')]
    if dres is not None:
        ins.append((dres, 'row'))
    return _rowwise(fn, ins, [(w, F32), (w, BF16)], [w], name=name)


def _rope(x, cos, sin_a, sin_b, shift, scale, out_dtype, *, name):
    w = x.shape[1]
    wr = cos.shape[1]
    lo = w - wr

    def fn(xv, cv, sav, sbv):
        xr = xv[:, lo:] if lo else xv
        y = xr * cv + pltpu.roll(xr, wr - shift, 1) * sav + pltpu.roll(xr, shift, 1) * sbv
        if lo:
            y = jnp.concatenate([xv[:, :lo], y], axis=1)
        return [y * scale]
    return _rowwise(fn, [(x, 'row'), (cos, 'per'), (sin_a, 'per'), (sin_b, 'per')], [(w, out_dtype)], name=name)[0]


def _dot(a, b, ca, cb):
    return lax.dot_general(a, b, (((ca,), (cb,)), ((), ())), preferred_element_type=F32)


def _ret_fwd(q, k, v, dec, dq_, dk_, ds_):
    H, S, d = q.shape
    T = dec.shape[1]
    nC = S // T

    def body(q_ref, k_ref, v_ref, m_ref, qd_ref, kd_ref, sd_ref, o_ref, st_ref, state):
        @pl.when(pl.program_id(1) == 0)
        def _():
            state[...] = jnp.zeros_like(state)

        st = state[...]
        st_ref[...] = st
        qv, kv, vv = q_ref[...], k_ref[...], v_ref[...]
        p = (_dot(qv, kv, 1, 1) * m_ref[...]).astype(BF16)
        qs = (qv.astype(F32) * qd_ref[...]).astype(BF16)
        o_ref[...] = _dot(p, vv, 1, 0) + _dot(qs, st.astype(BF16), 1, 0)
        ks = (kv.astype(F32) * kd_ref[...]).astype(BF16)
        state[...] = st * sd_ref[...] + _dot(ks, vv, 0, 0)

    blk = pl.BlockSpec((None, T, d), lambda h, c: (h, c, 0))
    return pl.pallas_call(
        body, name="ret_fwd", grid=(H, nC),
        in_specs=[blk, blk, blk,
                  pl.BlockSpec((None, T, T), lambda h, c: (h, 0, 0)),
                  pl.BlockSpec((None, T, 1), lambda h, c: (h, 0, 0)),
                  pl.BlockSpec((None, T, 1), lambda h, c: (h, 0, 0)),
                  pl.BlockSpec((None, 1, 1), lambda h, c: (h, 0, 0))],
        out_specs=[blk, pl.BlockSpec((None, None, d, d), lambda h, c: (h, c, 0, 0))],
        out_shape=[jax.ShapeDtypeStruct((H, S, d), F32), jax.ShapeDtypeStruct((H, nC, d, d), F32)],
        scratch_shapes=[pltpu.VMEM((d, d), F32)],
        compiler_params=_params(("parallel", "arbitrary")),
    )(q, k, v, dec, dq_, dk_, ds_)


def _ret_bwd(q, k, v, do, states, dec, dq_, dk_, ds_):
    H, S, d = q.shape
    T = dec.shape[1]
    nC = S // T

    def body(q_ref, k_ref, v_ref, do_ref, st_ref, m_ref, qd_ref, kd_ref, sd_ref, gq_ref, gk_ref, gv_ref, dstate):
        @pl.when(pl.program_id(1) == 0)
        def _():
            dstate[...] = jnp.zeros_like(dstate)

        qv, kv, vv, dov = q_ref[...], k_ref[...], v_ref[...], do_ref[...]
        m = m_ref[...]
        qd, kd = qd_ref[...], kd_ref[...]
        ds = dstate[...]
        dsb = ds.astype(BF16)
        sb = st_ref[...].astype(BF16)
        p = (_dot(qv, kv, 1, 1) * m).astype(BF16)
        da = (_dot(dov, vv, 1, 1) * m).astype(BF16)
        qs = (qv.astype(F32) * qd).astype(BF16)
        ks = (kv.astype(F32) * kd).astype(BF16)
        gq_ref[...] = _dot(da, kv, 1, 0) + _dot(dov, sb, 1, 1) * qd
        gk_ref[...] = _dot(da, qv, 0, 0) + _dot(vv, dsb, 1, 1) * kd
        gv_ref[...] = _dot(p, dov, 0, 0) + _dot(ks, dsb, 1, 0)
        dstate[...] = ds * sd_ref[...] + _dot(qs, dov, 0, 0)

    blk = pl.BlockSpec((None, T, d), lambda h, c: (h, nC - 1 - c, 0))
    out = jax.ShapeDtypeStruct((H, S, d), F32)
    return pl.pallas_call(
        body, name="ret_bwd", grid=(H, nC),
        in_specs=[blk, blk, blk, blk,
                  pl.BlockSpec((None, None, d, d), lambda h, c: (h, nC - 1 - c, 0, 0)),
                  pl.BlockSpec((None, T, T), lambda h, c: (h, 0, 0)),
                  pl.BlockSpec((None, T, 1), lambda h, c: (h, 0, 0)),
                  pl.BlockSpec((None, T, 1), lambda h, c: (h, 0, 0)),
                  pl.BlockSpec((None, 1, 1), lambda h, c: (h, 0, 0))],
        out_specs=[blk, blk, blk], out_shape=[out, out, out],
        scratch_shapes=[pltpu.VMEM((d, d), F32)],
        compiler_params=_params(("parallel", "arbitrary")),
    )(q, k, v, do, states, dec, dq_, dk_, ds_)


def _ret_tables(T, d):
    h = jnp.arange(RET_HEADS, dtype=F32)
    log_g = jnp.log1p(-jnp.exp2(-5.0 - h))
    idx = jnp.arange(T, dtype=F32)
    diff = idx[:, None] - idx[None, :]
    same = (jnp.arange(T)[:, None] // CHUNK) == (jnp.arange(T)[None, :] // CHUNK)
    earlier = (jnp.arange(T)[None, :] // CHUNK) < (jnp.arange(T)[:, None] // CHUNK)
    expo = jnp.where(same, jnp.abs(diff), diff)
    dec = jnp.where(same | earlier, jnp.exp(log_g[:, None, None] * expo[None]), 0.0)
    q_dec = jnp.exp(log_g[:, None] * (idx + 1.0))[..., None]
    k_dec = jnp.exp(log_g[:, None] * (T - 1.0 - idx))[..., None]
    s_dec = jnp.exp(log_g * T)[:, None, None]
    return dec.astype(F32), q_dec, k_dec, s_dec


NEG = -1e30


ROW_GROUP = 512


def _scores(q, kv, r, diagonal):
    s = _dot(q, kv, 1, 1)
    if diagonal:
        rg, T = s.shape
        qc = (r * rg + lax.broadcasted_iota(jnp.int32, (rg, T), 0)) // CHUNK
        kc = lax.broadcasted_iota(jnp.int32, (rg, T), 1) // CHUNK
        s = jnp.where(kc <= qc, s, NEG)
    return s


def _by_query_block(p, n):
    i = sum((p >= t * (t + 1) // 2).astype(jnp.int32) for t in range(1, n))
    return i, p - i * (i + 1) // 2


def _by_key_block(p, n):
    j = sum((p >= t * n - t * (t - 1) // 2).astype(jnp.int32) for t in range(1, n))
    return j, j + p - (j * n - j * (j - 1) // 2)


def _flash_fwd(q, k, v, side=None):
    H, S, dk = q.shape
    dv = v.shape[2]
    T = _tile(S, 512)
    n = S // T
    rg = min(ROW_GROUP, T)

    def body(q_ref, k_ref, v_ref, o_ref, lse_ref, m_s, l_s, acc):
        qi, ki = _by_query_block(pl.program_id(1), n)

        @pl.when(ki == 0)
        def _():
            m_s[...] = jnp.full_like(m_s, NEG)
            l_s[...] = jnp.zeros_like(l_s)
            acc[...] = jnp.zeros_like(acc)

        def step(diagonal):
            kv, vv = k_ref[...], v_ref[...]
            for r in range(T // rg):
                rows = pl.ds(r * rg, rg)
                s = _scores(q_ref[rows, :], kv, r, diagonal)
                m_old = m_s[rows, :]
                m_new = jnp.maximum(m_old, jnp.max(s, axis=1, keepdims=True))
                p = jnp.exp(s - m_new)
                alpha = jnp.exp(m_old - m_new)
                l_s[rows, :] = alpha * l_s[rows, :] + jnp.sum(p, axis=1, keepdims=True)
                acc[rows, :] = alpha * acc[rows, :] + _dot(p.astype(BF16), vv, 1, 0)
                m_s[rows, :] = m_new

        @pl.when(ki < qi)
        def _():
            step(False)

        @pl.when(ki == qi)
        def _():
            step(True)
            o_ref[...] = acc[...] / l_s[...]
            lse_ref[...] = m_s[...] + jnp.log(l_s[...])

    q_map = lambda h, p: (h, _by_query_block(p, n)[0], 0)
    kv_map = lambda h, p: (h, _by_query_block(p, n)[1], 0)
    return _hosted_call(
        body, side, name="mla_fwd", grid=(H, n * (n + 1) // 2),
        in_specs=[pl.BlockSpec((None, T, dk), q_map), pl.BlockSpec((None, T, dk), kv_map),
                  pl.BlockSpec((None, T, dv), kv_map)],
        out_specs=[pl.BlockSpec((None, T, dv), q_map), pl.BlockSpec((None, T, 1), q_map)],
        out_shape=[jax.ShapeDtypeStruct((H, S, dv), F32), jax.ShapeDtypeStruct((H, S, 1), F32)],
        scratch_shapes=[pltpu.VMEM((T, 1), F32), pltpu.VMEM((T, 1), F32), pltpu.VMEM((T, dv), F32)],
        operands=[q, k, v], semantics=("parallel", "arbitrary"))


def _flash_dq(q, k, v, do, lse, dlt, side=None):
    H, S, dk = q.shape
    dv = v.shape[2]
    T = _tile(S, 512)
    n = S // T
    rg = min(ROW_GROUP, T)

    def body(q_ref, k_ref, v_ref, do_ref, lse_ref, dlt_ref, dq_ref, acc):
        qi, ki = _by_query_block(pl.program_id(1), n)

        @pl.when(ki == 0)
        def _():
            acc[...] = jnp.zeros_like(acc)

        def step(diagonal):
            kv, vv = k_ref[...], v_ref[...]
            for r in range(T // rg):
                rows = pl.ds(r * rg, rg)
                p = jnp.exp(_scores(q_ref[rows, :], kv, r, diagonal) - lse_ref[rows, :])
                dp = _dot(do_ref[rows, :], vv, 1, 1)
                ds = (p * (dp - dlt_ref[rows, :])).astype(BF16)
                acc[rows, :] += _dot(ds, kv, 1, 0)

        @pl.when(ki < qi)
        def _():
            step(False)

        @pl.when(ki == qi)
        def _():
            step(True)
            dq_ref[...] = acc[...]

    q_map = lambda h, p: (h, _by_query_block(p, n)[0], 0)
    kv_map = lambda h, p: (h, _by_query_block(p, n)[1], 0)
    return _hosted_call(
        body, side, name="mla_dq", grid=(H, n * (n + 1) // 2),
        in_specs=[pl.BlockSpec((None, T, dk), q_map), pl.BlockSpec((None, T, dk), kv_map),
                  pl.BlockSpec((None, T, dv), kv_map), pl.BlockSpec((None, T, dv), q_map),
                  pl.BlockSpec((None, T, 1), q_map), pl.BlockSpec((None, T, 1), q_map)],
        out_specs=[pl.BlockSpec((None, T, dk), q_map)],
        out_shape=[jax.ShapeDtypeStruct((H, S, dk), F32)],
        scratch_shapes=[pltpu.VMEM((T, dk), F32)],
        operands=[q, k, v, do, lse, dlt], semantics=("parallel", "arbitrary"))


def _flash_dkv(q, k, v, do, lse, dlt, side=None):
    H, S, dk = q.shape
    dv = v.shape[2]
    T = _tile(S, 512)
    n = S // T
    rg = min(ROW_GROUP, T)

    def body(q_ref, k_ref, v_ref, do_ref, lse_ref, dlt_ref, dk_ref, dv_ref, acc_k, acc_v):
        ki, qi = _by_key_block(pl.program_id(1), n)

        @pl.when(qi == ki)
        def _():
            acc_k[...] = jnp.zeros_like(acc_k)
            acc_v[...] = jnp.zeros_like(acc_v)

        def step(diagonal):
            kv, vv = k_ref[...], v_ref[...]
            for r in range(T // rg):
                rows = pl.ds(r * rg, rg)
                qv, dov = q_ref[rows, :], do_ref[rows, :]
                p = jnp.exp(_scores(qv, kv, r, diagonal) - lse_ref[rows, :])
                acc_v[...] += _dot(p.astype(BF16), dov, 0, 0)
                dp = _dot(dov, vv, 1, 1)
                ds = (p * (dp - dlt_ref[rows, :])).astype(BF16)
                acc_k[...] += _dot(ds, qv, 0, 0)

        @pl.when(qi > ki)
        def _():
            step(False)

        @pl.when(qi == ki)
        def _():
            step(True)

        @pl.when(qi == n - 1)
        def _():
            dk_ref[...] = acc_k[...]
            dv_ref[...] = acc_v[...]

    q_map = lambda h, p: (h, _by_key_block(p, n)[1], 0)
    kv_map = lambda h, p: (h, _by_key_block(p, n)[0], 0)
    return _hosted_call(
        body, side, name="mla_dkv", grid=(H, n * (n + 1) // 2),
        in_specs=[pl.BlockSpec((None, T, dk), q_map), pl.BlockSpec((None, T, dk), kv_map),
                  pl.BlockSpec((None, T, dv), kv_map), pl.BlockSpec((None, T, dv), q_map),
                  pl.BlockSpec((None, T, 1), q_map), pl.BlockSpec((None, T, 1), q_map)],
        out_specs=[pl.BlockSpec((None, T, dk), kv_map), pl.BlockSpec((None, T, dv), kv_map)],
        out_shape=[jax.ShapeDtypeStruct((H, S, dk), F32), jax.ShapeDtypeStruct((H, S, dv), F32)],
        scratch_shapes=[pltpu.VMEM((T, dk), F32), pltpu.VMEM((T, dv), F32)],
        operands=[q, k, v, do, lse, dlt], semantics=("parallel", "arbitrary"))


EDGE = 8


def _shift_down(cur, prev, by):
    ts, tc = cur.shape
    rows = lax.broadcasted_iota(jnp.int32, cur.shape, 0)
    head = jnp.concatenate([pltpu.roll(prev, by, 0), jnp.zeros((ts - EDGE, tc), cur.dtype)], axis=0)
    return jnp.where(rows < by, head, pltpu.roll(cur, by, 0))


def _shift_up(cur, nxt, by):
    ts, tc = cur.shape
    rows = lax.broadcasted_iota(jnp.int32, cur.shape, 0)
    tail = jnp.concatenate([jnp.zeros((ts - EDGE, tc), cur.dtype), pltpu.roll(nxt, EDGE - by, 0)], axis=0)
    return jnp.where(rows >= ts - by, tail, pltpu.roll(cur, ts - by, 0))


def _ffn_tiles(S, F):
    return _tile(S, 512, 8), _tile(F, 1024)


def _ffn_fwd(G, U, cw, cb):
    S, F = G.shape
    ts, tc = _ffn_tiles(S, F)

    def body(g_ref, gp_ref, u_ref, cw_ref, cb_ref, a_ref, act_ref):
        cur = g_ref[...]
        prev = gp_ref[...] * (pl.program_id(1) > 0).astype(F32)
        a = (cb_ref[...] + cw_ref[0:1, :] * _shift_down(cur, prev, 2) + cw_ref[1:2, :] * _shift_down(cur, prev, 1)
             + cw_ref[2:3, :] * cur)
        a_ref[...] = a
        act_ref[...] = (a * _sigmoid(a) * u_ref[...]).astype(BF16)

    cur_spec = pl.BlockSpec((ts, tc), lambda j, i: (i, j))
    return pl.pallas_call(
        body, name="ffn_act_fwd", grid=(F // tc, S // ts),
        in_specs=[cur_spec, pl.BlockSpec((EDGE, tc), lambda j, i: (jnp.maximum(i * (ts // EDGE) - 1, 0), j)),
                  cur_spec,
                  pl.BlockSpec((CONV_WIDTH, tc), lambda j, i: (0, j)), pl.BlockSpec((1, tc), lambda j, i: (0, j))],
        out_specs=[cur_spec, cur_spec],
        out_shape=[jax.ShapeDtypeStruct((S, F), F32), jax.ShapeDtypeStruct((S, F), BF16)],
        compiler_params=_params(("parallel", "parallel")),
    )(G, G, U, cw, cb)


def _ffn_bwd_act(a, U, dact):
    S, F = a.shape
    ts, tc = _ffn_tiles(S, F)

    def body(a_ref, u_ref, d_ref, da_ref, du_ref, db_ref):
        av, dv = a_ref[...], d_ref[...]
        sg = _sigmoid(av)
        du_ref[...] = (dv * av * sg).astype(BF16)
        da = dv * u_ref[...] * sg * (1.0 + av * (1.0 - sg))
        da_ref[...] = da

        @pl.when(pl.program_id(1) == 0)
        def _():
            db_ref[...] = jnp.zeros_like(db_ref)

        db_ref[...] += _colsum(da)

    cur_spec = pl.BlockSpec((ts, tc), lambda j, i: (i, j))
    return pl.pallas_call(
        body, name="ffn_act_bwd", grid=(F // tc, S // ts),
        in_specs=[cur_spec, cur_spec, cur_spec],
        out_specs=[cur_spec, cur_spec, pl.BlockSpec((1, tc), lambda j, i: (0, j))],
        out_shape=[jax.ShapeDtypeStruct((S, F), F32), jax.ShapeDtypeStruct((S, F), BF16),
                   jax.ShapeDtypeStruct((1, F), F32)],
        compiler_params=_params(("parallel", "arbitrary")),
    )(a, U, dact)


def _ffn_bwd_conv(da, G, cw):
    S, F = da.shape
    ts, tc = _ffn_tiles(S, F)
    n = S // ts

    def body(d_ref, dn_ref, g_ref, gp_ref, cw_ref, dg_ref, dw_ref):
        i = pl.program_id(1)
        dcur = d_ref[...]
        dnxt = dn_ref[...] * (i < n - 1).astype(F32)
        cur = g_ref[...]
        prev = gp_ref[...] * (i > 0).astype(F32)
        dg = (cw_ref[2:3, :] * dcur + cw_ref[1:2, :] * _shift_up(dcur, dnxt, 1)
              + cw_ref[0:1, :] * _shift_up(dcur, dnxt, 2))
        dg_ref[...] = dg.astype(BF16)

        @pl.when(i == 0)
        def _():
            dw_ref[...] = jnp.zeros_like(dw_ref)

        dw_ref[0:1, :] += _colsum(dcur * _shift_down(cur, prev, 2))
        dw_ref[1:2, :] += _colsum(dcur * _shift_down(cur, prev, 1))
        dw_ref[2:3, :] += _colsum(dcur * cur)

    cur_spec = pl.BlockSpec((ts, tc), lambda j, i: (i, j))
    return pl.pallas_call(
        body, name="ffn_conv_bwd", grid=(F // tc, n),
        in_specs=[cur_spec,
                  pl.BlockSpec((EDGE, tc), lambda j, i: (jnp.minimum((i + 1) * (ts // EDGE), S // EDGE - 1), j)),
                  cur_spec, pl.BlockSpec((EDGE, tc), lambda j, i: (jnp.maximum(i * (ts // EDGE) - 1, 0), j)),
                  pl.BlockSpec((CONV_WIDTH, tc), lambda j, i: (0, j))],
        out_specs=[cur_spec, pl.BlockSpec((CONV_WIDTH, tc), lambda j, i: (0, j))],
        out_shape=[jax.ShapeDtypeStruct((S, F), BF16), jax.ShapeDtypeStruct((CONV_WIDTH, F), F32)],
        compiler_params=_params(("parallel", "arbitrary")),
    )(da, da, G, G, cw)


class _Cfg:
    def __init__(self, S, D):
        self.S, self.D = S, D
        self.RD = D // (2 * RET_HEADS)
        self.RW = RET_HEADS * self.RD
        self.MV = (D - self.RW) // MLA_HEADS
        self.QL, self.KVL = D // 4, D // 8
        self.F = ((8 * D // 3 + 255) // 256) * 256
        self.IN = 4 * self.RW + self.QL + self.KVL + MLA_ROPE
        self.INs = self.IN // N_CHIP
        self.INp = _round_up(self.INs, LANE)
        self.Fs = self.F // N_CHIP
        self.Fp = _round_up(self.Fs, LANE)
        self.F4 = N_CHIP * self.Fp
        self.QK = MLA_NOPE + MLA_ROPE
        self.KVH = MLA_NOPE + self.MV


def _head_major(t, H, d):
    S = t.shape[0]
    return t.reshape(S, H, d).transpose(1, 0, 2).reshape(H * S, d)


def _seq_major(t, H, d):
    S = t.shape[0] // H
    return t.reshape(H, S, d).transpose(1, 0, 2).reshape(S, H * d)


def _rope_tables(cfg):
    S = cfg.S

    def cs(dim):
        inv = 1.0 / (ROPE_BASE ** (jnp.arange(0, dim, 2, dtype=F32) / dim))
        ang = jnp.arange(S, dtype=F32)[:, None] * inv[None, :]
        return jnp.cos(ang), jnp.sin(ang)

    c, s = cs(cfg.RD)
    z = jnp.zeros_like(s)
    ret = (jnp.concatenate([c, c], 1), jnp.concatenate([-s, z], 1), jnp.concatenate([z, s], 1))
    c, s = cs(MLA_ROPE)
    tail1 = jnp.ones((S, MLA_QK_PAD - cfg.QK), F32)
    tail0 = jnp.zeros((S, MLA_QK_PAD - cfg.QK), F32)
    z = jnp.zeros_like(s)
    mla = (jnp.concatenate([c, c, tail1], 1), jnp.concatenate([-s, z, tail0], 1), jnp.concatenate([z, s, tail0], 1))
    return ret, mla


GATHER_FIRST = ('w_in',)
GATHER_BEHIND_IN_PROJ = ('w_uq', 'w_ukv', 'w_o')
GATHER_BEHIND_MLA = ('w_ffn_gate', 'w_ffn_up')
GATHER_BEHIND_FFN_GATE = ('w_ffn_down',)
GATHER_BEHIND_FFN_UP = ('w_ple_gate', 'w_ple_proj')
PAIR_BEHIND_FFN_DX = ('w_ple_proj', 'w_ple_gate', 'w_ffn_down', 'w_ffn_gate', 'w_ffn_up')
PAIR_AFTER_OUT_PROJ = ('w_o',)
REDUCE_BEHIND_DQ = ('w_ffn_gate', 'w_ple_gate', 'w_o')
REDUCE_BEHIND_DKV = ('w_ffn_up', 'w_ffn_down', 'w_ple_proj')
REDUCE_LAST = ('w_uq', 'w_ukv', 'w_in')


def _local_step(cfg, x, p, tgt, W, late, sp):
    W = dict(W)
    S, D, RD, RW, MV = cfg.S, cfg.D, cfg.RD, cfg.RW, cfg.MV
    H, MH = RET_HEADS, MLA_HEADS
    (rc, rsa, rsb), (mc, msa, msb) = _rope_tables(cfg)
    dec, q_dec, k_dec, s_dec = _ret_tables(min(RET_BLOCK, S), RD)
    k_scale = RD ** -0.5
    a_scale = cfg.QK ** -0.5
    gb, gs = {}, {}

    hn1, r1 = _rms_fwd(x, sp['g_attn'], name="rms_attn")
    proj, got = _mm_nn(hn1, W['w_in'], name="in_proj",
                       side=_gather_side([late[n] for n in GATHER_BEHIND_IN_PROJ]))
    W.update({n: _as_operand(n, g) for n, g in zip(GATHER_BEHIND_IN_PROJ, got)})
    proj = proj.reshape(S, N_CHIP, cfg.INp)[:, :, :cfg.INs].reshape(S, cfg.IN)
    cuts = [RW, 2 * RW, 3 * RW, 4 * RW, 4 * RW + cfg.QL, 4 * RW + cfg.QL + cfg.KVL]
    rq, rk, rv, rg, cq, ckv, kr = jnp.split(proj, cuts, axis=1)

    rq_h, rk_h, rv_h, rg_h = (_head_major(t, H, RD) for t in (rq, rk, rv, rg))
    q_r = _rope(rq_h, rc, rsa, rsb, RD // 2, 1.0, BF16, name="ret_rope_q").reshape(H, S, RD)
    k_r = _rope(rk_h, rc, rsa, rsb, RD // 2, k_scale, BF16, name="ret_rope_k").reshape(H, S, RD)
    v_r = rv_h.astype(BF16).reshape(H, S, RD)
    o_h, states = _ret_fwd(q_r, k_r, v_r, dec, q_dec, k_dec, s_dec)
    o_h = o_h.reshape(H * S, RD)

    def gate_fn(ov, gv):
        oc = ov - _mean(ov)
        ron = oc * lax.rsqrt(_mean(oc * oc) + EPS)
        return [gv * _sigmoid(gv) * ron]
    ro_h = _rowwise(gate_fn, [(o_h, 'row'), (rg_h, 'row')], [(RD, BF16)], name="ret_gate")[0]
    ro = _seq_major(ro_h, H, RD)

    cqn, rcq = _rms_fwd(cq, sp['g_q_lora'], name="rms_q")
    ckvn, rckv = _rms_fwd(ckv, sp['g_kv_lora'], name="rms_kv")
    Q = _mm_nn(cqn, W['w_uq'], name="q_up")
    KV = _mm_nn(ckvn, W['w_ukv'], name="kv_up")
    pad = jnp.zeros((MH, S, MLA_QK_PAD - cfg.QK), F32)
    q_raw = jnp.concatenate([Q.reshape(S, MH, cfg.QK).transpose(1, 0, 2), pad], 2).reshape(MH * S, MLA_QK_PAD)
    KV3 = KV.reshape(S, MH, cfg.KVH).transpose(1, 0, 2)
    k_raw = jnp.concatenate([KV3[:, :, :MLA_NOPE], jnp.broadcast_to(kr[None], (MH, S, MLA_ROPE)), pad], 2)
    k_raw = k_raw.reshape(MH * S, MLA_QK_PAD)
    v_m = KV3[:, :, MLA_NOPE:].astype(BF16)
    q_m = _rope(q_raw, mc, msa, msb, MLA_ROPE // 2, a_scale, BF16, name="mla_rope_q").reshape(MH, S, MLA_QK_PAD)
    k_m = _rope(k_raw, mc, msa, msb, MLA_ROPE // 2, 1.0, BF16, name="mla_rope_k").reshape(MH, S, MLA_QK_PAD)
    (mo_h, lse), got = _flash_fwd(q_m, k_m, v_m, side=_gather_side([late[n] for n in GATHER_BEHIND_MLA]))
    W.update({n: _as_operand(n, g) for n, g in zip(GATHER_BEHIND_MLA, got)})
    mo = _seq_major(mo_h.reshape(MH * S, MV), MH, MV)

    cat = jnp.concatenate([ro, mo.astype(BF16)], axis=1)
    h1 = _mm_nn(cat, W['w_o'], name="out_proj", res=x)

    hn2, r2 = _rms_fwd(h1, sp['g_ffn'], name="rms_ffn")
    G, got = _mm_nn(hn2, W['w_ffn_gate'], name="ffn_gate",
                    side=_gather_side([late[n] for n in GATHER_BEHIND_FFN_GATE]))
    W.update({n: _as_operand(n, g) for n, g in zip(GATHER_BEHIND_FFN_GATE, got)})
    U, got = _mm_nn(hn2, W['w_ffn_up'], name="ffn_up", side=_gather_side([late[n] for n in GATHER_BEHIND_FFN_UP]))
    W.update({n: _as_operand(n, g) for n, g in zip(GATHER_BEHIND_FFN_UP, got)})
    a, act = _ffn_fwd(G, U, sp['conv_w'], sp['conv_b'])
    h2 = _mm_nn(act, W['w_ffn_down'], name="ffn_down", res=h1)

    hn3, r3 = _rms_fwd(h2, sp['g_ple'], name="rms_ple")
    Z = _mm_nn(hn3, W['w_ple_gate'], name="ple_gate")
    p_b = p.astype(BF16)
    PP = _mm_nn(p_b, W['w_ple_proj'], name="ple_proj")

    def head_fn(h2v, zv, ppv, tv, gv):
        gate = _sigmoid(zv)
        h3 = h2v + gate * ppv
        r4 = lax.rsqrt(_mean(h3 * h3) + EPS)
        n4 = h3 * r4
        e = n4 * gv - tv
        dy = e * (1.0 / D)
        dn = dy * gv
        dh3 = r4 * (dn - n4 * _mean(dn * n4))
        dpp = dh3 * gate
        dz = dh3 * ppv * gate * (1.0 - gate)
        loss = jnp.sum(0.5 * _mean(e * e), axis=0, keepdims=True)
        return [dh3, dz, dpp, _colsum(dy * n4), jnp.broadcast_to(loss, (1, LANE))]
    dh3, dZ, dPP, dgf, loss = _rowwise(
        head_fn, [(h2, 'row'), (Z, 'row'), (PP, 'row'), (tgt, 'row'), (sp['g_final'], 'bcast')],
        [(D, F32), (D, BF16), (D, BF16)], [D, LANE], name="ple_loss_head")
    gs['g_final'] = dgf
    loss = loss[0, 0]

    gb['w_ple_proj'] = _mm_tn(p_b, dPP, N_CHIP, name="ple_proj_dw")
    gb['w_ple_gate'] = _mm_tn(hn3, dZ, 1, name="ple_gate_dw")
    dhn3 = _mm_nt(dZ, W['w_ple_gate'], name="ple_gate_dx")
    dh2, dh2_b, gs['g_ple'] = _rms_bwd(h2, r3, dhn3, sp['g_ple'], dh3, name="rms_ple_bwd")

    dact = _mm_nt(dh2_b, W['w_ffn_down'], name="ffn_down_dx")
    gb['w_ffn_down'] = _mm_tn(act, dh2_b, 1, name="ffn_down_dw")
    da, dU, gs['conv_b'] = _ffn_bwd_act(a, U, dact)
    dG, gs['conv_w'] = _ffn_bwd_conv(da, G, sp['conv_w'])
    gb['w_ffn_gate'] = _mm_tn(hn2, dG, N_CHIP, name="ffn_gate_dw")
    gb['w_ffn_up'] = _mm_tn(hn2, dU, N_CHIP, name="ffn_up_dw")
    def pair_sums(names, from_sibling):
        return {n: _pair_add(_grad_pieces(n, gb[n][0]), r) for n, r in zip(names, from_sibling)}

    def wire(names):
        return [_grad_pieces(n, gb[n][1]) for n in names]

    dhn2, got = _mm_nt(dG, W['w_ffn_gate'], name="ffn_gate_dx", side=_pair_side(wire(PAIR_BEHIND_FFN_DX)))
    pair = pair_sums(PAIR_BEHIND_FFN_DX, got)
    dhn2 = _mm_nt(dU, W['w_ffn_up'], name="ffn_up_dx", res=dhn2)
    dh1, dh1_b, gs['g_ffn'] = _rms_bwd(h1, r2, dhn2, sp['g_ffn'], dh2, name="rms_ffn_bwd")

    dcat = _mm_nt(dh1_b, W['w_o'], name="out_proj_dx")
    gb['w_o'] = _mm_tn(cat, dh1_b, 1, name="out_proj_dw")
    pair.update(pair_sums(PAIR_AFTER_OUT_PROJ, _pair_exchange(wire(PAIR_AFTER_OUT_PROJ))))
    from_chips = {}
    dro_h = _head_major(dcat[:, :RW], H, RD)

    def gate_bwd_fn(ov, gv, dv):
        oc = ov - _mean(ov)
        rs = lax.rsqrt(_mean(oc * oc) + EPS)
        ron = oc * rs
        sg = _sigmoid(gv)
        dron = dv * gv * sg
        drg = dv * ron * sg * (1.0 + gv * (1.0 - sg))
        do = rs * (dron - _mean(dron) - ron * _mean(dron * ron))
        return [do, drg]
    do_h, drg_h = _rowwise(gate_bwd_fn, [(o_h, 'row'), (rg_h, 'row'), (dro_h, 'row')],
                           [(RD, BF16), (RD, F32)], name="ret_gate_bwd")
    gq_r, gk_r, gv_r = _ret_bwd(q_r, k_r, v_r, do_h.reshape(H, S, RD), states, dec, q_dec, k_dec, s_dec)
    drq_h = _rope(gq_r.reshape(H * S, RD), rc, -rsa, -rsb, RD // 2, 1.0, F32, name="ret_rope_q_bwd")
    drk_h = _rope(gk_r.reshape(H * S, RD), rc, -rsa, -rsb, RD // 2, k_scale, F32, name="ret_rope_k_bwd")
    drq, drk, drv, drg = (_seq_major(t, H, RD) for t in (drq_h, drk_h, gv_r.reshape(H * S, RD), drg_h))

    dmo_h = _head_major(dcat[:, RW:], MH, MV)
    dlt = _rowwise(lambda ov, dv: [jnp.sum(ov * dv, axis=1, keepdims=True)],
                   [(mo_h.reshape(MH * S, MV), 'row'), (dmo_h, 'row')], [(1, F32)], name="mla_delta")[0]
    dmo_b = dmo_h.astype(BF16).reshape(MH, S, MV)
    dlt = dlt.reshape(MH, S, 1)
    (gq_m,), got = _flash_dq(q_m, k_m, v_m, dmo_b, lse, dlt,
                             side=_exchange_side([pair[n][1] for n in REDUCE_BEHIND_DQ]))
    from_chips.update(zip(REDUCE_BEHIND_DQ, got))
    (gk_m, gv_m), got = _flash_dkv(q_m, k_m, v_m, dmo_b, lse, dlt,
                                   side=_exchange_side([pair[n][1] for n in REDUCE_BEHIND_DKV]))
    from_chips.update(zip(REDUCE_BEHIND_DKV, got))
    dq_raw = _rope(gq_m.reshape(MH * S, MLA_QK_PAD), mc, -msa, -msb, MLA_ROPE // 2, a_scale, F32,
                   name="mla_rope_q_bwd")
    dk_raw = _rope(gk_m.reshape(MH * S, MLA_QK_PAD), mc, -msa, -msb, MLA_ROPE // 2, 1.0, F32, name="mla_rope_k_bwd")
    dQ = dq_raw.reshape(MH, S, MLA_QK_PAD)[:, :, :cfg.QK].transpose(1, 0, 2).reshape(S, MH * cfg.QK).astype(BF16)
    dk3 = dk_raw.reshape(MH, S, MLA_QK_PAD)
    dKV = jnp.concatenate([dk3[:, :, :MLA_NOPE], gv_m], 2).transpose(1, 0, 2).reshape(S, MH * cfg.KVH).astype(BF16)
    dkr_heads = [(dk3[h, :, MLA_NOPE:cfg.QK], 'row') for h in range(MH)]
    dkr = _rowwise(lambda *v: [functools.reduce(lambda s, t: s + t, v)], dkr_heads, [(MLA_ROPE, F32)],
                   name="mla_rope_k_heads")[0]

    gb['w_uq'] = _mm_tn(cqn, dQ, N_CHIP, name="q_up_dw")
    dcqn = _mm_nt(dQ, W['w_uq'], name="q_up_dx")
    dcq, _, gs['g_q_lora'] = _rms_bwd(cq, rcq, dcqn, sp['g_q_lora'], None, name="rms_q_bwd")
    gb['w_ukv'] = _mm_tn(ckvn, dKV, N_CHIP, name="kv_up_dw")
    dckvn = _mm_nt(dKV, W['w_ukv'], name="kv_up_dx")
    dckv, _, gs['g_kv_lora'] = _rms_bwd(ckv, rckv, dckvn, sp['g_kv_lora'], None, name="rms_kv_bwd")

    dproj = jnp.concatenate([drq, drk, drv, drg, dcq, dckv, dkr], axis=1).astype(BF16)
    dproj = jnp.pad(dproj.reshape(S, N_CHIP, cfg.INs), ((0, 0), (0, 0), (0, cfg.INp - cfg.INs)))
    dproj = dproj.reshape(S, N_CHIP * cfg.INp)
    gb['w_in'] = _mm_tn(hn1, dproj, N_CHIP, name="in_proj_dw")
    pair.update(pair_sums(REDUCE_LAST, _pair_exchange(wire(REDUCE_LAST))))
    dhn1, got = _mm_nt(dproj, W['w_in'], name="in_proj_dx", side=_exchange_side([pair[n][1] for n in REDUCE_LAST]))
    from_chips.update(zip(REDUCE_LAST, got))
    dx, _, gs['g_attn'] = _rms_bwd(x, r1, dhn1, sp['g_attn'], dh1, name="rms_attn_bwd")
    return loss, dx, {n: (pair[n][0], from_chips[n]) for n in BIG}, gs


def _padded_shape(name, shape):
    K, n = shape
    if name in COL_SHARDED:
        return K, _round_up(n, LANE)
    return _round_up(K, LANE), n


def _pad_shard(name, w):
    K, n = _padded_shape(name, w.shape)
    return jnp.pad(w.astype(BF16), ((0, K - w.shape[0]), (0, n - w.shape[1])))


def _as_operand(name, gathered):
    if name in COL_SHARDED:
        return gathered
    return gathered.reshape(1, gathered.shape[0] * gathered.shape[1], gathered.shape[2])


def _grad_pieces(name, g):
    if name in COL_SHARDED:
        return g
    return g.reshape(N_CHIP, g.shape[1] // N_CHIP, g.shape[2])


def _channels_padded(v, cfg):
    r = v.shape[0]
    return jnp.pad(v.reshape(r, N_CHIP, cfg.Fs), ((0, 0), (0, 0), (0, cfg.Fp - cfg.Fs))).reshape(r, cfg.F4)


def _channels_unpadded(v, cfg):
    r = v.shape[0]
    return v.reshape(r, N_CHIP, cfg.Fp)[:, :, :cfg.Fs].reshape(r, cfg.F)


def _mesh_pos():
    return lax.axis_index("x"), lax.axis_index("y"), lax.axis_index("c")


def _other_chips(x, y):
    return [(1 - x, y), (x, 1 - y), (1 - x, 1 - y)]


def _chip_id(cx, cy):
    return 2 * cx + cy


def _half(ref_rows, core):
    half = ref_rows // 2
    return pl.ds(core * half, half)


def _gather_side(shards):
    n = len(shards)

    def copies(srcs, outs, sems):
        ici_send, ici_recv, d2d_send, d2d_recv, own_send, own_recv = sems
        x, y, c = _mesh_pos()
        mine = _chip_id(x, y)
        others = _other_chips(x, y)

        def over_ici(w, j, chip):
            rows = _half(srcs[w].shape[0], c)
            return pltpu.make_async_remote_copy(
                src_ref=srcs[w].at[rows], dst_ref=outs[w].at[chip, rows],
                send_sem=ici_send.at[w, j], recv_sem=ici_recv.at[w, j],
                device_id=(*others[j], c), device_id_type=MESH)

        def over_d2d(w, j, core):
            rows = _half(srcs[w].shape[0], core)
            slab = outs[w].at[_chip_id(*others[j]), rows]
            return pltpu.make_async_remote_copy(
                src_ref=slab, dst_ref=slab, send_sem=d2d_send.at[w, j], recv_sem=d2d_recv.at[w, j],
                device_id=(x, y, 1 - c), device_id_type=MESH)

        def own(w):
            return pltpu.make_async_remote_copy(
                src_ref=srcs[w], dst_ref=outs[w].at[mine], send_sem=own_send.at[w], recv_sem=own_recv.at[w],
                device_id=(x, y, 1 - c), device_id_type=MESH)

        return c, mine, others, over_ici, over_d2d, own

    def start(srcs, outs, sems):
        c, mine, others, over_ici, over_d2d, own = copies(srcs, outs, sems)
        for w in range(n):
            for j in range(3):
                over_ici(w, j, mine).start()
        for w in range(n):
            own(w).start()

    def finish(srcs, outs, sems):
        c, mine, others, over_ici, over_d2d, own = copies(srcs, outs, sems)
        for w in range(n):
            for j in range(3):
                over_ici(w, j, _chip_id(*others[j])).wait_recv()
                over_d2d(w, j, c).start()
        for w in range(n):
            for j in range(3):
                over_d2d(w, j, 1 - c).wait_recv()
        for w in range(n):
            own(w).wait()
            for j in range(3):
                over_ici(w, j, mine).wait_send()
                over_d2d(w, j, c).wait_send()

    out_shape = [jax.ShapeDtypeStruct((N_CHIP,) + s.shape, s.dtype) for s in shards]
    sems = [pltpu.SemaphoreType.DMA((n, 3))] * 4 + [pltpu.SemaphoreType.DMA((n,))] * 2
    return _Side(shards, out_shape, sems, start, finish)


def _gather_weights(shards):
    return _run_side(_gather_side(shards), "gather_weights")


def _pair_side(grads):
    n = len(grads)

    def copies(srcs, outs, sems):
        send, recv = sems
        x, y, c = _mesh_pos()
        return [pltpu.make_async_remote_copy(
            src_ref=srcs[w].at[:, _half(srcs[w].shape[1], 1 - c), :], dst_ref=outs[w],
            send_sem=send.at[w], recv_sem=recv.at[w],
            device_id=(x, y, 1 - c), device_id_type=MESH) for w in range(n)]

    def start(srcs, outs, sems):
        for cp in copies(srcs, outs, sems):
            cp.start()

    def finish(srcs, outs, sems):
        for cp in copies(srcs, outs, sems):
            cp.wait()

    out_shape = [jax.ShapeDtypeStruct((g.shape[0], g.shape[1] // 2, g.shape[2]), g.dtype) for g in grads]
    return _Side(grads, out_shape, [pltpu.SemaphoreType.DMA((n,))] * 2, start, finish)


def _pair_exchange(grads):
    return _run_side(_pair_side(grads), "grad_pair_exchange")


def _exchange_side(parts):
    n = len(parts)

    def copies(srcs, outs, sems):
        send, recv = sems
        x, y, c = _mesh_pos()
        others = _other_chips(x, y)
        return [pltpu.make_async_remote_copy(
            src_ref=srcs[w].at[_chip_id(*others[j])], dst_ref=outs[w].at[j],
            send_sem=send.at[w, j], recv_sem=recv.at[w, j],
            device_id=(*others[j], c), device_id_type=MESH) for w in range(n) for j in range(3)]

    def start(srcs, outs, sems):
        for cp in copies(srcs, outs, sems):
            cp.start()

    def finish(srcs, outs, sems):
        for cp in copies(srcs, outs, sems):
            cp.wait()

    out_shape = [jax.ShapeDtypeStruct((3,) + p.shape[1:], p.dtype) for p in parts]
    return _Side(parts, out_shape, [pltpu.SemaphoreType.DMA((n, 3))] * 2, start, finish)


def _sibling_share(shards):
    n = len(shards)

    def body(*refs):
        outs = refs[n:2 * n]
        send, recv = refs[2 * n:]
        x, y, c = _mesh_pos()

        def half_of(w, core):
            rows = outs[w].at[_half(outs[w].shape[0], core)]
            return pltpu.make_async_remote_copy(
                src_ref=rows, dst_ref=rows, send_sem=send.at[w], recv_sem=recv.at[w],
                device_id=(x, y, 1 - c), device_id_type=MESH)

        for w in range(n):
            half_of(w, c).start()
        for w in range(n):
            half_of(w, 1 - c).wait_recv()
        for w in range(n):
            half_of(w, c).wait_send()

    return pl.pallas_call(
        body, name="grad_sibling_share", in_specs=[HBM] * n, out_specs=[HBM] * n,
        out_shape=[jax.ShapeDtypeStruct(s.shape, s.dtype) for s in shards],
        scratch_shapes=[pltpu.SemaphoreType.DMA((n,))] * 2, input_output_aliases={w: w for w in range(n)},
    )(*shards)


N_DEV = 8


def _gather_small(v):
    r, width = v.shape

    def body(v_ref, out_ref, send_sems, recv_sems, local_sem):
        x, y, c = _mesh_pos()
        me, sibling = (x, y, c), (x, y, 1 - c)
        chips = _other_chips(x, y)

        def rows(px, py, pc):
            return out_ref.at[pl.ds((4 * px + 2 * py + pc) * r, r), :]

        def copy(k, block, to, src=None):
            return pltpu.make_async_remote_copy(
                src_ref=rows(*block) if src is None else src, dst_ref=rows(*block),
                send_sem=send_sems.at[k], recv_sem=recv_sems.at[k], device_id=to, device_id_type=MESH)

        mine = pltpu.make_async_copy(v_ref, rows(*me), local_sem)
        mine.start()
        first = [copy(0, me, sibling, src=v_ref)]
        first += [copy(1 + j, me, (*chip, c), src=v_ref) for j, chip in enumerate(chips)]
        for cp in first:
            cp.start()
        passed = [copy(4 + j, (*chip, c), sibling) for j, chip in enumerate(chips)]
        for j, chip in enumerate(chips):
            copy(1 + j, (*chip, c), me).wait_recv()
            passed[j].start()
        copy(0, sibling, me).wait_recv()
        for j, chip in enumerate(chips):
            copy(4 + j, (*chip, 1 - c), me).wait_recv()
        for cp in first + passed:
            cp.wait_send()
        mine.wait()

    vmem = pl.BlockSpec(memory_space=pltpu.VMEM)
    return pl.pallas_call(
        body, name="gather_small", out_shape=jax.ShapeDtypeStruct((N_DEV * r, width), v.dtype),
        in_specs=[vmem], out_specs=vmem,
        scratch_shapes=[pltpu.SemaphoreType.DMA((7,)), pltpu.SemaphoreType.DMA((7,)), pltpu.SemaphoreType.DMA],
    )(v)


def _pack(arrays):
    flat = jnp.concatenate([a.reshape(-1) for a in arrays])
    size = _round_up(flat.shape[0], 8 * LANE)
    return jnp.pad(flat, (0, size - flat.shape[0])).reshape(size // LANE, LANE)


def _unpack(packed, shapes):
    flat = packed.reshape(-1)
    out, at = [], 0
    for s in shapes:
        size = math.prod(s)
        out.append(flat[at:at + size].reshape(s))
        at += size
    return out


def _sum_devices(gathered):
    r = gathered.shape[0] // N_DEV
    blocks = [(gathered[d * r:(d + 1) * r], 'row') for d in range(N_DEV)]
    return _rowwise(lambda *v: [functools.reduce(lambda s, t: s + t, v)], blocks, [(LANE, F32)],
                    name="small_grad_sum")[0]


def _reduce_tiles(half, n):
    return _tile(half, 256, 8)


def _pair_add(g32, r1):
    G, K, n = g32.shape
    half = K // 2
    tr = _reduce_tiles(half, n)
    nrt = half // tr

    def body(g_ref, r_ref, s32_ref, sb_ref):
        s = g_ref[...] + r_ref[...].astype(F32)
        s32_ref[...] = s
        sb_ref[...] = s.astype(BF16)

    blk = pl.BlockSpec((None, tr, n), lambda k, i: (k, i, 0))
    return pl.pallas_call(
        body, name="grad_pair_add", grid=(G, nrt),
        in_specs=[pl.BlockSpec((None, tr, n), lambda k, i: (k, lax.axis_index("c") * nrt + i, 0)), blk],
        out_specs=[blk, blk],
        out_shape=[jax.ShapeDtypeStruct((G, half, n), F32), jax.ShapeDtypeStruct((G, half, n), BF16)],
        compiler_params=_params(("parallel", "parallel")),
    )(g32, r1)


def _chip_add(s32, r2):
    _, half, n = s32.shape
    tr = _reduce_tiles(half, n)
    nrt = half // tr

    def body(s_ref, a_ref, b_ref, c_ref, o_ref):
        o_ref[...] = ((s_ref[...] + a_ref[...].astype(F32)) + b_ref[...].astype(F32)) + c_ref[...].astype(F32)

    def piece(j):
        return pl.BlockSpec((None, tr, n), lambda i, j=j: (j, i, 0))

    def mine(i):
        return _chip_id(lax.axis_index("x"), lax.axis_index("y")), i, 0

    return pl.pallas_call(
        body, name="grad_chip_add", grid=(nrt,),
        in_specs=[pl.BlockSpec((None, tr, n), mine), piece(0), piece(1), piece(2)],
        out_specs=pl.BlockSpec((tr, n), lambda i: (lax.axis_index("c") * nrt + i, 0)),
        out_shape=jax.ShapeDtypeStruct((2 * half, n), F32),
        compiler_params=_params(("parallel",)),
    )(s32, r2, r2, r2)


def _adamw_math(wv, gv, mv, vv):
    m2 = ADAM_B1 * mv + (1.0 - ADAM_B1) * gv
    v2 = ADAM_B2 * vv + (1.0 - ADAM_B2) * (gv * gv)
    m_hat = m2 / (1.0 - ADAM_B1 ** ADAM_STEP)
    v_hat = v2 / (1.0 - ADAM_B2 ** ADAM_STEP)
    delta = -ADAM_LR * (m_hat / (jnp.sqrt(v_hat) + ADAM_EPS) + ADAM_WD * wv)
    return [delta, m2, v2]


def _adamw(w, g, m, v, *, name):
    width = w.shape[1]
    return _rowwise(_adamw_math, [(w, 'row'), (g, 'row'), (m, 'row'), (v, 'row')], [(width, F32)] * 3, name=name)


def _adamw_sharded(w, g_padded, m, v, *, name):
    _, K, n = w.shape
    n_pad = g_padded.shape[1]
    ts = 8
    while ts * 2 <= 256 and ts * 2 * 8 * n_pad * 4 <= ROWWISE_BLOCK_BYTES and K % (ts * 2) == 0:
        ts *= 2

    def body(w_ref, g_ref, m_ref, v_ref, go_ref, d_ref, mo_ref, vo_ref):
        gv = g_ref[:, :n] if n != n_pad else g_ref[...]
        go_ref[...] = gv
        d_ref[...], mo_ref[...], vo_ref[...] = _adamw_math(w_ref[...], gv, m_ref[...], v_ref[...])

    blk = pl.BlockSpec((None, ts, n), lambda i: (0, i, 0))
    out = jax.ShapeDtypeStruct((1, K, n), F32)
    return pl.pallas_call(
        body, name=name, grid=(K // ts,),
        in_specs=[blk, pl.BlockSpec((ts, n_pad), lambda i: (i, 0)), blk, blk],
        out_specs=[blk] * 4, out_shape=[out] * 4,
        compiler_params=_params(("parallel",)),
    )(w, g_padded, m, v)


def kernel(x, p, w_in, g_attn, g_q_lora, g_kv_lora, w_uq, w_ukv, w_o, g_ffn, w_ffn_gate, w_ffn_up, conv_w, conv_b, w_ffn_down, g_ple, w_ple_gate, w_ple_proj, g_final, loss_target, m_w_in, m_g_attn, m_g_q_lora, m_g_kv_lora, m_w_uq, m_w_ukv, m_w_o, m_g_ffn, m_w_ffn_gate, m_w_ffn_up, m_conv_w, m_conv_b, m_w_ffn_down, m_g_ple, m_w_ple_gate, m_w_ple_proj, m_g_final, v_w_in, v_g_attn, v_g_q_lora, v_g_kv_lora, v_w_uq, v_w_ukv, v_w_o, v_g_ffn, v_w_ffn_gate, v_w_ffn_up, v_conv_w, v_conv_b, v_w_ffn_down, v_g_ple, v_w_ple_gate, v_w_ple_proj, v_g_final):
    weights = dict(w_in=w_in, g_attn=g_attn, g_q_lora=g_q_lora, g_kv_lora=g_kv_lora, w_uq=w_uq, w_ukv=w_ukv, w_o=w_o,
                   g_ffn=g_ffn, w_ffn_gate=w_ffn_gate, w_ffn_up=w_ffn_up, conv_w=conv_w, conv_b=conv_b,
                   w_ffn_down=w_ffn_down, g_ple=g_ple, w_ple_gate=w_ple_gate, w_ple_proj=w_ple_proj, g_final=g_final)
    mom1 = dict(w_in=m_w_in, g_attn=m_g_attn, g_q_lora=m_g_q_lora, g_kv_lora=m_g_kv_lora, w_uq=m_w_uq, w_ukv=m_w_ukv,
                w_o=m_w_o, g_ffn=m_g_ffn, w_ffn_gate=m_w_ffn_gate, w_ffn_up=m_w_ffn_up, conv_w=m_conv_w,
                conv_b=m_conv_b, w_ffn_down=m_w_ffn_down, g_ple=m_g_ple, w_ple_gate=m_w_ple_gate,
                w_ple_proj=m_w_ple_proj, g_final=m_g_final)
    mom2 = dict(w_in=v_w_in, g_attn=v_g_attn, g_q_lora=v_g_q_lora, g_kv_lora=v_g_kv_lora, w_uq=v_w_uq, w_ukv=v_w_ukv,
                w_o=v_w_o, g_ffn=v_g_ffn, w_ffn_gate=v_w_ffn_gate, w_ffn_up=v_w_ffn_up, conv_w=v_conv_w,
                conv_b=v_conv_b, w_ffn_down=v_w_ffn_down, g_ple=v_g_ple, w_ple_gate=v_w_ple_gate,
                w_ple_proj=v_w_ple_proj, g_final=v_g_final)
    _, S, D = x.shape
    cfg = _Cfg(S, D)
    cx, cy, _ = _mesh_pos()

    shards = {name: _pad_shard(name, weights[name][0]) for name in BIG}
    gathered = _gather_weights([shards[name] for name in GATHER_FIRST])
    W = {name: _as_operand(name, g) for name, g in zip(GATHER_FIRST, gathered)}
    cw_all = _gather_small(_pack([jnp.pad(conv_w[0], ((0, 0), (0, cfg.Fp - cfg.Fs)))]))
    r_cw = cw_all.shape[0] // N_DEV
    cw_chips = [_unpack(cw_all[2 * k * r_cw:(2 * k + 1) * r_cw], [(CONV_WIDTH, cfg.Fp)])[0] for k in range(N_CHIP)]
    sp = dict(g_attn=g_attn, g_q_lora=g_q_lora, g_kv_lora=g_kv_lora, g_ffn=g_ffn, g_ple=g_ple,
              g_final=g_final.reshape(1, D), conv_w=jnp.concatenate(cw_chips, axis=1),
              conv_b=_channels_padded(conv_b, cfg))

    loss, dx, parts, gs = _local_step(cfg, x[0], p[0, 0], loss_target[0], W, shards, sp)
    loss = lax.psum(loss, ("x", "y", "c"))

    halves = [_chip_add(*parts[name]) for name in BIG]
    whole = dict(zip(BIG, _sibling_share(halves)))
    grads = {}

    small_names = ['g_attn', 'g_q_lora', 'g_kv_lora', 'g_ffn', 'g_ple', 'g_final', 'conv_b', 'conv_w']
    small_sum = _sum_devices(_gather_small(_pack([gs[name] for name in small_names])))
    for name, g in zip(small_names, _unpack(small_sum, [gs[name].shape for name in small_names])):
        grads[name] = g
    grads['g_final'] = grads['g_final'].reshape(D)
    grads['conv_b'] = _channels_unpadded(grads['conv_b'], cfg)
    mine = _chip_id(cx, cy)
    grads['conv_w'] = lax.dynamic_slice_in_dim(grads['conv_w'], mine * cfg.Fp, cfg.Fp, axis=1)[:, :cfg.Fs]
    grads['conv_w'] = grads['conv_w'].reshape(1, CONV_WIDTH, cfg.Fs)

    delta, new_m, new_v = {}, {}, {}
    for name in BIG:
        operands = (weights[name], whole[name], mom1[name], mom2[name])
        flip = weights[name].shape[2] % LANE != 0
        if flip:
            operands = tuple(a.T if a.ndim == 2 else a.transpose(0, 2, 1) for a in operands)
        outs = _adamw_sharded(*operands, name="adamw_" + name)
        if flip:
            outs = [o.transpose(0, 2, 1) for o in outs]
        grads[name], delta[name], new_m[name], new_v[name] = outs
    for name in WEIGHTS:
        if name in BIG:
            continue
        shape = weights[name].shape
        flat = (shape[-2], shape[-1]) if len(shape) == 3 else (1, shape[-1])
        d, m2, v2 = _adamw(weights[name].reshape(flat), grads[name].reshape(flat), mom1[name].reshape(flat),
                           mom2[name].reshape(flat), name="adamw_" + name)
        delta[name], new_m[name], new_v[name] = d.reshape(shape), m2.reshape(shape), v2.reshape(shape)

    return (loss, dx.reshape(1, S, D), *[grads[n] for n in WEIGHTS], *[delta[n] for n in WEIGHTS],
            *[new_m[n] for n in WEIGHTS], *[new_v[n] for n in WEIGHTS])
```

```python
import functools
import math

import jax
import jax.numpy as jnp
from jax import lax
from jax.experimental import pallas as pl
from jax.experimental.pallas import tpu as pltpu

F32 = jnp.float32
BF16 = jnp.bfloat16

LANE = 128
VMEM_LIMIT = 56 * 1024 * 1024
ROWWISE_BLOCK_BYTES = 8 * 1024 * 1024
ROWWISE_MAX_ROWS = 2048
MM_TILE = 1024
MM_TILE_N = 1408
MM_TILE_CONTRACT = 2816
MM_TILE_CONTRACT_N = 2816

N_CHIP = 4
MESH = pl.DeviceIdType.MESH

CHUNK = 64
RET_HEADS = 8
MLA_HEADS = 16
MLA_NOPE = 128
MLA_ROPE = 64
MLA_QK_PAD = 256
PLE_DIM = 256
CONV_WIDTH = 3
ROPE_BASE = 10000.0
EPS = 1e-6
RET_BLOCK = 256

ADAM_LR = 0.001
ADAM_B1 = 0.9
ADAM_B2 = 0.999
ADAM_EPS = 1e-08
ADAM_WD = 0.01
ADAM_STEP = 10

WEIGHTS = ['w_in', 'g_attn', 'g_q_lora', 'g_kv_lora', 'w_uq', 'w_ukv', 'w_o', 'g_ffn', 'w_ffn_gate', 'w_ffn_up',
           'conv_w', 'conv_b', 'w_ffn_down', 'g_ple', 'w_ple_gate', 'w_ple_proj', 'g_final']
COL_SHARDED = ('w_in', 'w_uq', 'w_ukv', 'w_ffn_gate', 'w_ffn_up', 'w_ple_proj')
ROW_SHARDED = ('w_o', 'w_ffn_down', 'w_ple_gate')
BIG = COL_SHARDED + ROW_SHARDED
SMALL_REPLICATED = ('g_attn', 'g_q_lora', 'g_kv_lora', 'g_ffn', 'conv_b', 'g_ple', 'g_final')


def _round_up(n, m):
    return (n + m - 1) // m * m


def _tile(dim, cap, align=LANE):
    if dim <= cap:
        return dim
    t = cap // align * align
    while t >= align:
        if dim % t == 0:
            return t
        t -= align
    return dim


def _params(sem):
    return pltpu.CompilerParams(dimension_semantics=sem, vmem_limit_bytes=VMEM_LIMIT)


HBM = pl.BlockSpec(memory_space=pltpu.HBM)


class _Side:
    def __init__(self, ins, out_shape, sems, start, finish):
        self.ins, self.out_shape, self.sems, self.start, self.finish = ins, out_shape, sems, start, finish


def _run_side(side, name):
    n_in, n_out = len(side.ins), len(side.out_shape)

    def body(*refs):
        ins, outs, sems = refs[:n_in], refs[n_in:n_in + n_out], refs[n_in + n_out:]
        side.start(ins, outs, sems)
        side.finish(ins, outs, sems)

    return pl.pallas_call(
        body, name=name, in_specs=[HBM] * n_in, out_specs=[HBM] * n_out, out_shape=side.out_shape,
        scratch_shapes=side.sems,
    )(*side.ins)


def _hosted_call(body, side, *, name, grid, in_specs, out_specs, out_shape, scratch_shapes, operands, semantics):
    if side is None:
        res = pl.pallas_call(
            body, name=name, grid=grid, in_specs=in_specs, out_specs=out_specs, out_shape=out_shape,
            scratch_shapes=scratch_shapes, compiler_params=_params(semantics))(*operands)
        return res, []
    n_in, n_out, n_scr = len(in_specs), len(out_specs), len(scratch_shapes)
    s_in, s_out = len(side.ins), len(side.out_shape)

    def hosted(*refs):
        refs = list(refs)
        ins, refs = refs[:n_in], refs[n_in:]
        side_ins, refs = refs[:s_in], refs[s_in:]
        outs, refs = refs[:n_out], refs[n_out:]
        side_outs, refs = refs[:s_out], refs[s_out:]
        scratch, sems = refs[:n_scr], refs[n_scr:]
        ids = [pl.program_id(a) for a in range(len(grid))]
        first = functools.reduce(jnp.logical_and, [i == 0 for i in ids])
        last = functools.reduce(jnp.logical_and, [i == g - 1 for i, g in zip(ids, grid)])

        @pl.when(first)
        def _():
            side.start(side_ins, side_outs, sems)

        body(*ins, *outs, *scratch)

        @pl.when(last)
        def _():
            side.finish(side_ins, side_outs, sems)

    res = pl.pallas_call(
        hosted, name=name, grid=grid, in_specs=list(in_specs) + [HBM] * s_in,
        out_specs=list(out_specs) + [HBM] * s_out, out_shape=list(out_shape) + list(side.out_shape),
        scratch_shapes=list(scratch_shapes) + list(side.sems),
        compiler_params=_params(("arbitrary",) * len(grid)))(*operands, *side.ins)
    return res[:n_out], res[n_out:]


def _mm_call(name, dims, grid, in_specs, out_specs, out_shape, acc_shape, operands, has_res, side=None):
    nsteps = grid[2]
    n_out = len(out_shape)

    def body(*refs):
        a_ref, b_ref = refs[0], refs[1]
        res_ref = refs[2] if has_res else None
        outs = refs[2 + has_res:2 + has_res + n_out]
        acc = refs[2 + has_res + n_out]
        k = pl.program_id(2)

        @pl.when(k == 0)
        def _():
            acc[...] = jnp.zeros_like(acc)

        acc[...] += lax.dot_general(a_ref[...], b_ref[...], (dims, ((), ())), preferred_element_type=F32)

        @pl.when(k == nsteps - 1)
        def _():
            r = acc[...]
            if has_res:
                r = r + res_ref[...]
            for o in outs:
                o[...] = r.astype(o.dtype)

    outs, side_outs = _hosted_call(
        body, side, name=name, grid=grid, in_specs=in_specs, out_specs=out_specs, out_shape=out_shape,
        scratch_shapes=[pltpu.VMEM(acc_shape, F32)], operands=operands,
        semantics=("parallel", "parallel", "arbitrary"))
    return (outs, side_outs) if side is not None else outs


def _mm_nn(a, w, *, name, res=None, out_dtype=F32, side=None):
    M, K = a.shape
    G, _, n = w.shape
    tm, tn, tk = _tile(M, MM_TILE), _tile(n, MM_TILE_N), _tile(K, MM_TILE_CONTRACT)
    npg = n // tn
    grid = (M // tm, G * npg, K // tk)
    in_specs = [pl.BlockSpec((tm, tk), lambda i, j, k: (i, k)),
                pl.BlockSpec((None, tk, tn), lambda i, j, k: (j // npg, k, j % npg))]
    operands = [a, w]
    if res is not None:
        in_specs.append(pl.BlockSpec((tm, tn), lambda i, j, k: (i, j)))
        operands.append(res)
    out_specs = [pl.BlockSpec((tm, tn), lambda i, j, k: (i, j))]
    out_shape = [jax.ShapeDtypeStruct((M, G * n), out_dtype)]
    got = _mm_call(name, ((1,), (0,)), grid, in_specs, out_specs, out_shape, (tm, tn), operands, res is not None, side)
    return (got[0][0], got[1]) if side is not None else got[0]


def _mm_nt(g, w, *, name, res=None, out_dtype=F32, side=None):
    M, _ = g.shape
    G, K, n = w.shape
    tm, tko, tn = _tile(M, MM_TILE), _tile(K, MM_TILE), _tile(n, MM_TILE_CONTRACT_N)
    npg = n // tn
    grid = (M // tm, K // tko, G * npg)
    in_specs = [pl.BlockSpec((tm, tn), lambda i, j, k: (i, k)),
                pl.BlockSpec((None, tko, tn), lambda i, j, k: (k // npg, j, k % npg))]
    operands = [g, w]
    if res is not None:
        in_specs.append(pl.BlockSpec((tm, tko), lambda i, j, k: (i, j)))
        operands.append(res)
    out_specs = [pl.BlockSpec((tm, tko), lambda i, j, k: (i, j))]
    out_shape = [jax.ShapeDtypeStruct((M, K), out_dtype)]
    got = _mm_call(name, ((1,), (1,)), grid, in_specs, out_specs, out_shape, (tm, tko), operands, res is not None, side)
    return (got[0][0], got[1]) if side is not None else got[0]


def _mm_tn(a, g, groups, *, name):
    M, K = a.shape
    n = g.shape[1] // groups
    tm, tko, tn = _tile(M, MM_TILE_CONTRACT), _tile(K, MM_TILE), _tile(n, MM_TILE_N)
    npg = n // tn
    grid = (K // tko, groups * npg, M // tm)
    in_specs = [pl.BlockSpec((tm, tko), lambda i, j, k: (k, i)),
                pl.BlockSpec((tm, tn), lambda i, j, k: (k, j))]
    out_spec = pl.BlockSpec((None, tko, tn), lambda i, j, k: (j // npg, i, j % npg))
    out_shape = [jax.ShapeDtypeStruct((groups, K, n), F32), jax.ShapeDtypeStruct((groups, K, n), BF16)]
    return _mm_call(name, ((0,), (0,)), grid, in_specs, [out_spec, out_spec], out_shape, (tko, tn), [a, g], False)


def _rowwise(fn, ins, outs, accs=(), *, name):
    R = next(a.shape[0] for a, kind in ins if kind == 'row')
    row_bytes = sum(a.shape[1] * a.dtype.itemsize for a, kind in ins if kind != 'bcast')
    row_bytes += sum(w * jnp.dtype(dt).itemsize for w, dt in outs)
    ts = 8
    while ts * 2 <= ROWWISE_MAX_ROWS and ts * 2 * row_bytes <= ROWWISE_BLOCK_BYTES:
        ts *= 2
    ts = min(ts, R)
    for a, kind in ins:
        if kind == 'per':
            ts = math.gcd(ts, a.shape[0])
    while R % ts:
        ts //= 2
    in_specs = []
    for a, kind in ins:
        w = a.shape[1]
        if kind == 'row':
            in_specs.append(pl.BlockSpec((ts, w), lambda i: (i, 0)))
        elif kind == 'bcast':
            in_specs.append(pl.BlockSpec((1, w), lambda i: (0, 0)))
        else:
            nper = a.shape[0] // ts
            in_specs.append(pl.BlockSpec((ts, w), lambda i, nper=nper: (i % nper, 0)))
    out_specs = [pl.BlockSpec((ts, w), lambda i: (i, 0)) for w, _ in outs]
    out_specs += [pl.BlockSpec((1, w), lambda i: (0, 0)) for w in accs]
    out_shape = [jax.ShapeDtypeStruct((R, w), dt) for w, dt in outs]
    out_shape += [jax.ShapeDtypeStruct((1, w), F32) for w in accs]
    n_in, n_out = len(ins), len(outs)

    def body(*refs):
        vals = [r[...] for r in refs[:n_in]]
        res = fn(*vals)
        for r, v in zip(refs[n_in:n_in + n_out], res[:n_out]):
            r[...] = v.astype(r.dtype)
        if accs:
            first = pl.program_id(0) == 0
            for r, v in zip(refs[n_in + n_out:], res[n_out:]):
                @pl.when(first)
                def _(r=r, v=v):
                    r[...] = v

                @pl.when(jnp.logical_not(first))
                def _(r=r, v=v):
                    r[...] += v

    return pl.pallas_call(
        body, name=name, grid=(R // ts,), in_specs=in_specs, out_specs=out_specs, out_shape=out_shape,
        compiler_params=_params(("arbitrary",) if accs else ("parallel",)),
    )(*[a for a, _ in ins])


def _mean(v):
    return jnp.mean(v, axis=-1, keepdims=True)


def _colsum(v):
    return jnp.sum(v, axis=0, keepdims=True)


def _sigmoid(v):
    return 1.0 / (1.0 + jnp.exp(-v))


def _rms_fwd(x, g, *, name):
    def fn(xv, gv):
        r = lax.rsqrt(_mean(xv * xv) + EPS)
        return [xv * r * gv, r]
    w = x.shape[1]
    return _rowwise(fn, [(x, 'row'), (g, 'bcast')], [(w, BF16), (1, F32)], name=name)


def _rms_bwd(x, r, dhn, g, dres, *, name):
    def fn(xv, rv, dv, gv, *rest):
        n = xv * rv
        dn = dv * gv
        dx = rv * (dn - n * _mean(dn * n))
        if rest:
            dx = dx + rest[0]
        return [dx, dx, _colsum(dv * n)]
    w = x.shape[1]
    ins = [(x, 'row'), (r, 'row'), (dhn, 'row'), (g, 'bcast')]
    if dres is not None:
        ins.append((dres, 'row'))
    return _rowwise(fn, ins, [(w, F32), (w, BF16)], [w], name=name)


def _rope(x, cos, sin_a, sin_b, shift, scale, out_dtype, *, name):
    w = x.shape[1]
    wr = cos.shape[1]
    lo = w - wr

    def fn(xv, cv, sav, sbv):
        xr = xv[:, lo:] if lo else xv
        y = xr * cv + pltpu.roll(xr, wr - shift, 1) * sav + pltpu.roll(xr, shift, 1) * sbv
        if lo:
            y = jnp.concatenate([xv[:, :lo], y], axis=1)
        return [y * scale]
    return _rowwise(fn, [(x, 'row'), (cos, 'per'), (sin_a, 'per'), (sin_b, 'per')], [(w, out_dtype)], name=name)[0]


def _dot(a, b, ca, cb):
    return lax.dot_general(a, b, (((ca,), (cb,)), ((), ())), preferred_element_type=F32)


def _ret_fwd(q, k, v, dec, dq_, dk_, ds_):
    H, S, d = q.shape
    T = dec.shape[1]
    nC = S // T

    def body(q_ref, k_ref, v_ref, m_ref, qd_ref, kd_ref, sd_ref, o_ref, st_ref, state):
        @pl.when(pl.program_id(1) == 0)
        def _():
            state[...] = jnp.zeros_like(state)

        st = state[...]
        st_ref[...] = st
        qv, kv, vv = q_ref[...], k_ref[...], v_ref[...]
        p = (_dot(qv, kv, 1, 1) * m_ref[...]).astype(BF16)
        qs = (qv.astype(F32) * qd_ref[...]).astype(BF16)
        o_ref[...] = _dot(p, vv, 1, 0) + _dot(qs, st.astype(BF16), 1, 0)
        ks = (kv.astype(F32) * kd_ref[...]).astype(BF16)
        state[...] = st * sd_ref[...] + _dot(ks, vv, 0, 0)

    blk = pl.BlockSpec((None, T, d), lambda h, c: (h, c, 0))
    return pl.pallas_call(
        body, name="ret_fwd", grid=(H, nC),
        in_specs=[blk, blk, blk,
                  pl.BlockSpec((None, T, T), lambda h, c: (h, 0, 0)),
                  pl.BlockSpec((None, T, 1), lambda h, c: (h, 0, 0)),
                  pl.BlockSpec((None, T, 1), lambda h, c: (h, 0, 0)),
                  pl.BlockSpec((None, 1, 1), lambda h, c: (h, 0, 0))],
        out_specs=[blk, pl.BlockSpec((None, None, d, d), lambda h, c: (h, c, 0, 0))],
        out_shape=[jax.ShapeDtypeStruct((H, S, d), F32), jax.ShapeDtypeStruct((H, nC, d, d), F32)],
        scratch_shapes=[pltpu.VMEM((d, d), F32)],
        compiler_params=_params(("parallel", "arbitrary")),
    )(q, k, v, dec, dq_, dk_, ds_)


def _ret_bwd(q, k, v, do, states, dec, dq_, dk_, ds_):
    H, S, d = q.shape
    T = dec.shape[1]
    nC = S // T

    def body(q_ref, k_ref, v_ref, do_ref, st_ref, m_ref, qd_ref, kd_ref, sd_ref, gq_ref, gk_ref, gv_ref, dstate):
        @pl.when(pl.program_id(1) == 0)
        def _():
            dstate[...] = jnp.zeros_like(dstate)

        qv, kv, vv, dov = q_ref[...], k_ref[...], v_ref[...], do_ref[...]
        m = m_ref[...]
        qd, kd = qd_ref[...], kd_ref[...]
        ds = dstate[...]
        dsb = ds.astype(BF16)
        sb = st_ref[...].astype(BF16)
        p = (_dot(qv, kv, 1, 1) * m).astype(BF16)
        da = (_dot(dov, vv, 1, 1) * m).astype(BF16)
        qs = (qv.astype(F32) * qd).astype(BF16)
        ks = (kv.astype(F32) * kd).astype(BF16)
        gq_ref[...] = _dot(da, kv, 1, 0) + _dot(dov, sb, 1, 1) * qd
        gk_ref[...] = _dot(da, qv, 0, 0) + _dot(vv, dsb, 1, 1) * kd
        gv_ref[...] = _dot(p, dov, 0, 0) + _dot(ks, dsb, 1, 0)
        dstate[...] = ds * sd_ref[...] + _dot(qs, dov, 0, 0)

    blk = pl.BlockSpec((None, T, d), lambda h, c: (h, nC - 1 - c, 0))
    out = jax.ShapeDtypeStruct((H, S, d), F32)
    return pl.pallas_call(
        body, name="ret_bwd", grid=(H, nC),
        in_specs=[blk, blk, blk, blk,
                  pl.BlockSpec((None, None, d, d), lambda h, c: (h, nC - 1 - c, 0, 0)),
                  pl.BlockSpec((None, T, T), lambda h, c: (h, 0, 0)),
                  pl.BlockSpec((None, T, 1), lambda h, c: (h, 0, 0)),
                  pl.BlockSpec((None, T, 1), lambda h, c: (h, 0, 0)),
                  pl.BlockSpec((None, 1, 1), lambda h, c: (h, 0, 0))],
        out_specs=[blk, blk, blk], out_shape=[out, out, out],
        scratch_shapes=[pltpu.VMEM((d, d), F32)],
        compiler_params=_params(("parallel", "arbitrary")),
    )(q, k, v, do, states, dec, dq_, dk_, ds_)


def _ret_tables(T, d):
    h = jnp.arange(RET_HEADS, dtype=F32)
    log_g = jnp.log1p(-jnp.exp2(-5.0 - h))
    idx = jnp.arange(T, dtype=F32)
    diff = idx[:, None] - idx[None, :]
    same = (jnp.arange(T)[:, None] // CHUNK) == (jnp.arange(T)[None, :] // CHUNK)
    earlier = (jnp.arange(T)[None, :] // CHUNK) < (jnp.arange(T)[:, None] // CHUNK)
    expo = jnp.where(same, jnp.abs(diff), diff)
    dec = jnp.where(same | earlier, jnp.exp(log_g[:, None, None] * expo[None]), 0.0)
    q_dec = jnp.exp(log_g[:, None] * (idx + 1.0))[..., None]
    k_dec = jnp.exp(log_g[:, None] * (T - 1.0 - idx))[..., None]
    s_dec = jnp.exp(log_g * T)[:, None, None]
    return dec.astype(F32), q_dec, k_dec, s_dec


NEG = -1e30


ROW_GROUP = 512


def _scores(q, kv, r, diagonal):
    s = _dot(q, kv, 1, 1)
    if diagonal:
        rg, T = s.shape
        qc = (r * rg + lax.broadcasted_iota(jnp.int32, (rg, T), 0)) // CHUNK
        kc = lax.broadcasted_iota(jnp.int32, (rg, T), 1) // CHUNK
        s = jnp.where(kc <= qc, s, NEG)
    return s


def _by_query_block(p, n):
    i = sum((p >= t * (t + 1) // 2).astype(jnp.int32) for t in range(1, n))
    return i, p - i * (i + 1) // 2


def _by_key_block(p, n):
    j = sum((p >= t * n - t * (t - 1) // 2).astype(jnp.int32) for t in range(1, n))
    return j, j + p - (j * n - j * (j - 1) // 2)


def _flash_fwd(q, k, v_ones, side=None):
    H, S, dk = q.shape
    dv2 = v_ones.shape[2]
    dv = dv2 // 2
    T = _tile(S, 512)
    n = S // T
    rg = min(ROW_GROUP, T)

    def body(q_ref, k_ref, v_ref, o_ref, lse_ref, m_s, acc):
        qi, ki = _by_query_block(pl.program_id(1), n)

        @pl.when(ki == 0)
        def _():
            m_s[...] = jnp.full_like(m_s, NEG)
            acc[...] = jnp.zeros_like(acc)

        def step(diagonal):
            kv, vv = k_ref[...], v_ref[...]
            for r in range(T // rg):
                rows = pl.ds(r * rg, rg)
                s = _scores(q_ref[rows, :], kv, r, diagonal)
                m_old = m_s[rows, :]
                m_new = jnp.maximum(m_old, jnp.max(s, axis=1, keepdims=True))
                p = jnp.exp(s - m_new)
                alpha = jnp.exp(m_old - m_new)
                acc[rows, :] = alpha * acc[rows, :] + _dot(p.astype(BF16), vv, 1, 0)
                m_s[rows, :] = m_new

        @pl.when(ki < qi)
        def _():
            step(False)

        @pl.when(ki == qi)
        def _():
            step(True)
            total = acc[...]
            denom = total[:, dv:dv + 1]
            o_ref[...] = total[:, :dv] / denom
            lse_ref[...] = m_s[...] + jnp.log(denom)

    q_map = lambda h, p: (h, _by_query_block(p, n)[0], 0)
    kv_map = lambda h, p: (h, _by_query_block(p, n)[1], 0)
    return _hosted_call(
        body, side, name="mla_fwd", grid=(H, n * (n + 1) // 2),
        in_specs=[pl.BlockSpec((None, T, dk), q_map), pl.BlockSpec((None, T, dk), kv_map),
                  pl.BlockSpec((None, T, dv2), kv_map)],
        out_specs=[pl.BlockSpec((None, T, dv), q_map), pl.BlockSpec((None, T, 1), q_map)],
        out_shape=[jax.ShapeDtypeStruct((H, S, dv), F32), jax.ShapeDtypeStruct((H, S, 1), F32)],
        scratch_shapes=[pltpu.VMEM((T, 1), F32), pltpu.VMEM((T, dv2), F32)],
        operands=[q, k, v_ones], semantics=("parallel", "arbitrary"))


def _flash_dq(q, k, v, do, lse, dlt, side=None):
    H, S, dk = q.shape
    dv = v.shape[2]
    T = _tile(S, 512)
    n = S // T
    rg = min(ROW_GROUP, T)

    def body(q_ref, k_ref, v_ref, do_ref, lse_ref, dlt_ref, dq_ref, acc):
        qi, ki = _by_query_block(pl.program_id(1), n)

        @pl.when(ki == 0)
        def _():
            acc[...] = jnp.zeros_like(acc)

        def step(diagonal):
            kv, vv = k_ref[...], v_ref[...]
            for r in range(T // rg):
                rows = pl.ds(r * rg, rg)
                p = jnp.exp(_scores(q_ref[rows, :], kv, r, diagonal) - lse_ref[rows, :])
                dp = _dot(do_ref[rows, :], vv, 1, 1)
                ds = (p * (dp - dlt_ref[rows, :])).astype(BF16)
                acc[rows, :] += _dot(ds, kv, 1, 0)

        @pl.when(ki < qi)
        def _():
            step(False)

        @pl.when(ki == qi)
        def _():
            step(True)
            dq_ref[...] = acc[...]

    q_map = lambda h, p: (h, _by_query_block(p, n)[0], 0)
    kv_map = lambda h, p: (h, _by_query_block(p, n)[1], 0)
    return _hosted_call(
        body, side, name="mla_dq", grid=(H, n * (n + 1) // 2),
        in_specs=[pl.BlockSpec((None, T, dk), q_map), pl.BlockSpec((None, T, dk), kv_map),
                  pl.BlockSpec((None, T, dv), kv_map), pl.BlockSpec((None, T, dv), q_map),
                  pl.BlockSpec((None, T, 1), q_map), pl.BlockSpec((None, T, 1), q_map)],
        out_specs=[pl.BlockSpec((None, T, dk), q_map)],
        out_shape=[jax.ShapeDtypeStruct((H, S, dk), F32)],
        scratch_shapes=[pltpu.VMEM((T, dk), F32)],
        operands=[q, k, v, do, lse, dlt], semantics=("parallel", "arbitrary"))


def _flash_dkv(q, k, v, do, lse, dlt, side=None):
    H, S, dk = q.shape
    dv = v.shape[2]
    T = _tile(S, 512)
    n = S // T
    rg = min(ROW_GROUP, T)

    def body(q_ref, k_ref, v_ref, do_ref, lse_ref, dlt_ref, dk_ref, dv_ref, acc_k, acc_v):
        ki, qi = _by_key_block(pl.program_id(1), n)

        @pl.when(qi == ki)
        def _():
            acc_k[...] = jnp.zeros_like(acc_k)
            acc_v[...] = jnp.zeros_like(acc_v)

        def step(diagonal):
            kv, vv = k_ref[...], v_ref[...]
            for r in range(T // rg):
                rows = pl.ds(r * rg, rg)
                qv, dov = q_ref[rows, :], do_ref[rows, :]
                p = jnp.exp(_scores(qv, kv, r, diagonal) - lse_ref[rows, :])
                acc_v[...] += _dot(p.astype(BF16), dov, 0, 0)
                dp = _dot(dov, vv, 1, 1)
                ds = (p * (dp - dlt_ref[rows, :])).astype(BF16)
                acc_k[...] += _dot(ds, qv, 0, 0)

        @pl.when(qi > ki)
        def _():
            step(False)

        @pl.when(qi == ki)
        def _():
            step(True)

        @pl.when(qi == n - 1)
        def _():
            dk_ref[...] = acc_k[...]
            dv_ref[...] = acc_v[...]

    q_map = lambda h, p: (h, _by_key_block(p, n)[1], 0)
    kv_map = lambda h, p: (h, _by_key_block(p, n)[0], 0)
    return _hosted_call(
        body, side, name="mla_dkv", grid=(H, n * (n + 1) // 2),
        in_specs=[pl.BlockSpec((None, T, dk), q_map), pl.BlockSpec((None, T, dk), kv_map),
                  pl.BlockSpec((None, T, dv), kv_map), pl.BlockSpec((None, T, dv), q_map),
                  pl.BlockSpec((None, T, 1), q_map), pl.BlockSpec((None, T, 1), q_map)],
        out_specs=[pl.BlockSpec((None, T, dk), kv_map), pl.BlockSpec((None, T, dv), kv_map)],
        out_shape=[jax.ShapeDtypeStruct((H, S, dk), F32), jax.ShapeDtypeStruct((H, S, dv), F32)],
        scratch_shapes=[pltpu.VMEM((T, dk), F32), pltpu.VMEM((T, dv), F32)],
        operands=[q, k, v, do, lse, dlt], semantics=("parallel", "arbitrary"))


EDGE = 8


def _shift_down(cur, prev, by):
    ts, tc = cur.shape
    rows = lax.broadcasted_iota(jnp.int32, cur.shape, 0)
    head = jnp.concatenate([pltpu.roll(prev, by, 0), jnp.zeros((ts - EDGE, tc), cur.dtype)], axis=0)
    return jnp.where(rows < by, head, pltpu.roll(cur, by, 0))


def _shift_up(cur, nxt, by):
    ts, tc = cur.shape
    rows = lax.broadcasted_iota(jnp.int32, cur.shape, 0)
    tail = jnp.concatenate([jnp.zeros((ts - EDGE, tc), cur.dtype), pltpu.roll(nxt, EDGE - by, 0)], axis=0)
    return jnp.where(rows >= ts - by, tail, pltpu.roll(cur, ts - by, 0))


def _ffn_tiles(S, F):
    return _tile(S, 512, 8), _tile(F, 1024)


def _ffn_fwd(G, U, cw, cb):
    S, F = G.shape
    ts, tc = _ffn_tiles(S, F)

    def body(g_ref, gp_ref, u_ref, cw_ref, cb_ref, a_ref, act_ref):
        cur = g_ref[...]
        prev = gp_ref[...] * (pl.program_id(1) > 0).astype(F32)
        a = (cb_ref[...] + cw_ref[0:1, :] * _shift_down(cur, prev, 2) + cw_ref[1:2, :] * _shift_down(cur, prev, 1)
             + cw_ref[2:3, :] * cur)
        a_ref[...] = a
        act_ref[...] = (a * _sigmoid(a) * u_ref[...]).astype(BF16)

    cur_spec = pl.BlockSpec((ts, tc), lambda j, i: (i, j))
    return pl.pallas_call(
        body, name="ffn_act_fwd", grid=(F // tc, S // ts),
        in_specs=[cur_spec, pl.BlockSpec((EDGE, tc), lambda j, i: (jnp.maximum(i * (ts // EDGE) - 1, 0), j)),
                  cur_spec,
                  pl.BlockSpec((CONV_WIDTH, tc), lambda j, i: (0, j)), pl.BlockSpec((1, tc), lambda j, i: (0, j))],
        out_specs=[cur_spec, cur_spec],
        out_shape=[jax.ShapeDtypeStruct((S, F), F32), jax.ShapeDtypeStruct((S, F), BF16)],
        compiler_params=_params(("parallel", "parallel")),
    )(G, G, U, cw, cb)


def _ffn_bwd_act(a, U, dact):
    S, F = a.shape
    ts, tc = _ffn_tiles(S, F)

    def body(a_ref, u_ref, d_ref, da_ref, du_ref, db_ref):
        av, dv = a_ref[...], d_ref[...]
        sg = _sigmoid(av)
        du_ref[...] = (dv * av * sg).astype(BF16)
        da = dv * u_ref[...] * sg * (1.0 + av * (1.0 - sg))
        da_ref[...] = da

        @pl.when(pl.program_id(1) == 0)
        def _():
            db_ref[...] = jnp.zeros_like(db_ref)

        db_ref[...] += _colsum(da)

    cur_spec = pl.BlockSpec((ts, tc), lambda j, i: (i, j))
    return pl.pallas_call(
        body, name="ffn_act_bwd", grid=(F // tc, S // ts),
        in_specs=[cur_spec, cur_spec, cur_spec],
        out_specs=[cur_spec, cur_spec, pl.BlockSpec((1, tc), lambda j, i: (0, j))],
        out_shape=[jax.ShapeDtypeStruct((S, F), F32), jax.ShapeDtypeStruct((S, F), BF16),
                   jax.ShapeDtypeStruct((1, F), F32)],
        compiler_params=_params(("parallel", "arbitrary")),
    )(a, U, dact)


def _ffn_bwd_conv(da, G, cw):
    S, F = da.shape
    ts, tc = _ffn_tiles(S, F)
    n = S // ts

    def body(d_ref, dn_ref, g_ref, gp_ref, cw_ref, dg_ref, dw_ref):
        i = pl.program_id(1)
        dcur = d_ref[...]
        dnxt = dn_ref[...] * (i < n - 1).astype(F32)
        cur = g_ref[...]
        prev = gp_ref[...] * (i > 0).astype(F32)
        dg = (cw_ref[2:3, :] * dcur + cw_ref[1:2, :] * _shift_up(dcur, dnxt, 1)
              + cw_ref[0:1, :] * _shift_up(dcur, dnxt, 2))
        dg_ref[...] = dg.astype(BF16)

        @pl.when(i == 0)
        def _():
            dw_ref[...] = jnp.zeros_like(dw_ref)

        dw_ref[0:1, :] += _colsum(dcur * _shift_down(cur, prev, 2))
        dw_ref[1:2, :] += _colsum(dcur * _shift_down(cur, prev, 1))
        dw_ref[2:3, :] += _colsum(dcur * cur)

    cur_spec = pl.BlockSpec((ts, tc), lambda j, i: (i, j))
    return pl.pallas_call(
        body, name="ffn_conv_bwd", grid=(F // tc, n),
        in_specs=[cur_spec,
                  pl.BlockSpec((EDGE, tc), lambda j, i: (jnp.minimum((i + 1) * (ts // EDGE), S // EDGE - 1), j)),
                  cur_spec, pl.BlockSpec((EDGE, tc), lambda j, i: (jnp.maximum(i * (ts // EDGE) - 1, 0), j)),
                  pl.BlockSpec((CONV_WIDTH, tc), lambda j, i: (0, j))],
        out_specs=[cur_spec, pl.BlockSpec((CONV_WIDTH, tc), lambda j, i: (0, j))],
        out_shape=[jax.ShapeDtypeStruct((S, F), BF16), jax.ShapeDtypeStruct((CONV_WIDTH, F), F32)],
        compiler_params=_params(("parallel", "arbitrary")),
    )(da, da, G, G, cw)


class _Cfg:
    def __init__(self, S, D):
        self.S, self.D = S, D
        self.RD = D // (2 * RET_HEADS)
        self.RW = RET_HEADS * self.RD
        self.MV = (D - self.RW) // MLA_HEADS
        self.QL, self.KVL = D // 4, D // 8
        self.F = ((8 * D // 3 + 255) // 256) * 256
        self.IN = 4 * self.RW + self.QL + self.KVL + MLA_ROPE
        self.INs = self.IN // N_CHIP
        self.INp = _round_up(self.INs, LANE)
        self.Fs = self.F // N_CHIP
        self.Fp = _round_up(self.Fs, LANE)
        self.F4 = N_CHIP * self.Fp
        self.QK = MLA_NOPE + MLA_ROPE
        self.KVH = MLA_NOPE + self.MV


def _head_major(t, H, d):
    S = t.shape[0]
    return t.reshape(S, H, d).transpose(1, 0, 2).reshape(H * S, d)


def _seq_major(t, H, d):
    S = t.shape[0] // H
    return t.reshape(H, S, d).transpose(1, 0, 2).reshape(S, H * d)


def _rope_tables(cfg):
    S = cfg.S

    def cs(dim):
        inv = 1.0 / (ROPE_BASE ** (jnp.arange(0, dim, 2, dtype=F32) / dim))
        ang = jnp.arange(S, dtype=F32)[:, None] * inv[None, :]
        return jnp.cos(ang), jnp.sin(ang)

    c, s = cs(cfg.RD)
    z = jnp.zeros_like(s)
    ret = (jnp.concatenate([c, c], 1), jnp.concatenate([-s, z], 1), jnp.concatenate([z, s], 1))
    c, s = cs(MLA_ROPE)
    tail1 = jnp.ones((S, MLA_QK_PAD - cfg.QK), F32)
    tail0 = jnp.zeros((S, MLA_QK_PAD - cfg.QK), F32)
    z = jnp.zeros_like(s)
    mla = (jnp.concatenate([c, c, tail1], 1), jnp.concatenate([-s, z, tail0], 1), jnp.concatenate([z, s, tail0], 1))
    return ret, mla


GATHER_FIRST = ('w_in',)
GATHER_BEHIND_IN_PROJ = ('w_uq', 'w_ukv', 'w_o')
GATHER_BEHIND_MLA = ('w_ffn_gate', 'w_ffn_up')
GATHER_BEHIND_FFN_GATE = ('w_ffn_down',)
GATHER_BEHIND_FFN_UP = ('w_ple_gate', 'w_ple_proj')
PAIR_BEHIND_FFN_DX = ('w_ple_proj', 'w_ple_gate', 'w_ffn_down', 'w_ffn_gate', 'w_ffn_up')
PAIR_AFTER_OUT_PROJ = ('w_o',)
REDUCE_BEHIND_DQ = ('w_ffn_gate', 'w_ple_gate', 'w_o')
REDUCE_BEHIND_DKV = ('w_ffn_up', 'w_ffn_down', 'w_ple_proj')
REDUCE_LAST = ('w_uq', 'w_ukv', 'w_in')


def _local_step(cfg, x, p, tgt, W, late, sp):
    W = dict(W)
    S, D, RD, RW, MV = cfg.S, cfg.D, cfg.RD, cfg.RW, cfg.MV
    H, MH = RET_HEADS, MLA_HEADS
    (rc, rsa, rsb), (mc, msa, msb) = _rope_tables(cfg)
    dec, q_dec, k_dec, s_dec = _ret_tables(min(RET_BLOCK, S), RD)
    k_scale = RD ** -0.5
    a_scale = cfg.QK ** -0.5
    gb, gs = {}, {}

    hn1, r1 = _rms_fwd(x, sp['g_attn'], name="rms_attn")
    proj, got = _mm_nn(hn1, W['w_in'], name="in_proj",
                       side=_gather_side([late[n] for n in GATHER_BEHIND_IN_PROJ]))
    W.update({n: _as_operand(n, g) for n, g in zip(GATHER_BEHIND_IN_PROJ, got)})
    proj = proj.reshape(S, N_CHIP, cfg.INp)[:, :, :cfg.INs].reshape(S, cfg.IN)
    cuts = [RW, 2 * RW, 3 * RW, 4 * RW, 4 * RW + cfg.QL, 4 * RW + cfg.QL + cfg.KVL]
    rq, rk, rv, rg, cq, ckv, kr = jnp.split(proj, cuts, axis=1)

    rq_h, rk_h, rv_h, rg_h = (_head_major(t, H, RD) for t in (rq, rk, rv, rg))
    q_r = _rope(rq_h, rc, rsa, rsb, RD // 2, 1.0, BF16, name="ret_rope_q").reshape(H, S, RD)
    k_r = _rope(rk_h, rc, rsa, rsb, RD // 2, k_scale, BF16, name="ret_rope_k").reshape(H, S, RD)
    v_r = rv_h.astype(BF16).reshape(H, S, RD)
    o_h, states = _ret_fwd(q_r, k_r, v_r, dec, q_dec, k_dec, s_dec)
    o_h = o_h.reshape(H * S, RD)

    def gate_fn(ov, gv):
        oc = ov - _mean(ov)
        ron = oc * lax.rsqrt(_mean(oc * oc) + EPS)
        return [gv * _sigmoid(gv) * ron]
    ro_h = _rowwise(gate_fn, [(o_h, 'row'), (rg_h, 'row')], [(RD, BF16)], name="ret_gate")[0]
    ro = _seq_major(ro_h, H, RD)

    cqn, rcq = _rms_fwd(cq, sp['g_q_lora'], name="rms_q")
    ckvn, rckv = _rms_fwd(ckv, sp['g_kv_lora'], name="rms_kv")
    Q = _mm_nn(cqn, W['w_uq'], name="q_up")
    KV = _mm_nn(ckvn, W['w_ukv'], name="kv_up")
    pad = jnp.zeros((MH, S, MLA_QK_PAD - cfg.QK), F32)
    q_raw = jnp.concatenate([Q.reshape(S, MH, cfg.QK).transpose(1, 0, 2), pad], 2).reshape(MH * S, MLA_QK_PAD)
    KV3 = KV.reshape(S, MH, cfg.KVH).transpose(1, 0, 2)
    k_raw = jnp.concatenate([KV3[:, :, :MLA_NOPE], jnp.broadcast_to(kr[None], (MH, S, MLA_ROPE)), pad], 2)
    k_raw = k_raw.reshape(MH * S, MLA_QK_PAD)
    v_m = KV3[:, :, MLA_NOPE:].astype(BF16)
    q_m = _rope(q_raw, mc, msa, msb, MLA_ROPE // 2, a_scale, BF16, name="mla_rope_q").reshape(MH, S, MLA_QK_PAD)
    k_m = _rope(k_raw, mc, msa, msb, MLA_ROPE // 2, 1.0, BF16, name="mla_rope_k").reshape(MH, S, MLA_QK_PAD)
    v_ones = jnp.concatenate([v_m, jnp.ones_like(v_m)], axis=2)
    (mo_h, lse), got = _flash_fwd(q_m, k_m, v_ones, side=_gather_side([late[n] for n in GATHER_BEHIND_MLA]))
    W.update({n: _as_operand(n, g) for n, g in zip(GATHER_BEHIND_MLA, got)})
    mo = _seq_major(mo_h.reshape(MH * S, MV), MH, MV)

    cat = jnp.concatenate([ro, mo.astype(BF16)], axis=1)
    h1 = _mm_nn(cat, W['w_o'], name="out_proj", res=x)

    hn2, r2 = _rms_fwd(h1, sp['g_ffn'], name="rms_ffn")
    G, got = _mm_nn(hn2, W['w_ffn_gate'], name="ffn_gate",
                    side=_gather_side([late[n] for n in GATHER_BEHIND_FFN_GATE]))
    W.update({n: _as_operand(n, g) for n, g in zip(GATHER_BEHIND_FFN_GATE, got)})
    U, got = _mm_nn(hn2, W['w_ffn_up'], name="ffn_up", side=_gather_side([late[n] for n in GATHER_BEHIND_FFN_UP]))
    W.update({n: _as_operand(n, g) for n, g in zip(GATHER_BEHIND_FFN_UP, got)})
    a, act = _ffn_fwd(G, U, sp['conv_w'], sp['conv_b'])
    h2 = _mm_nn(act, W['w_ffn_down'], name="ffn_down", res=h1)

    hn3, r3 = _rms_fwd(h2, sp['g_ple'], name="rms_ple")
    Z = _mm_nn(hn3, W['w_ple_gate'], name="ple_gate")
    p_b = p.astype(BF16)
    PP = _mm_nn(p_b, W['w_ple_proj'], name="ple_proj")

    def head_fn(h2v, zv, ppv, tv, gv):
        gate = _sigmoid(zv)
        h3 = h2v + gate * ppv
        r4 = lax.rsqrt(_mean(h3 * h3) + EPS)
        n4 = h3 * r4
        e = n4 * gv - tv
        dy = e * (1.0 / D)
        dn = dy * gv
        dh3 = r4 * (dn - n4 * _mean(dn * n4))
        dpp = dh3 * gate
        dz = dh3 * ppv * gate * (1.0 - gate)
        loss = jnp.sum(0.5 * _mean(e * e), axis=0, keepdims=True)
        return [dh3, dz, dpp, _colsum(dy * n4), jnp.broadcast_to(loss, (1, LANE))]
    dh3, dZ, dPP, dgf, loss = _rowwise(
        head_fn, [(h2, 'row'), (Z, 'row'), (PP, 'row'), (tgt, 'row'), (sp['g_final'], 'bcast')],
        [(D, F32), (D, BF16), (D, BF16)], [D, LANE], name="ple_loss_head")
    gs['g_final'] = dgf
    loss = loss[0, 0]

    gb['w_ple_proj'] = _mm_tn(p_b, dPP, N_CHIP, name="ple_proj_dw")
    gb['w_ple_gate'] = _mm_tn(hn3, dZ, 1, name="ple_gate_dw")
    dhn3 = _mm_nt(dZ, W['w_ple_gate'], name="ple_gate_dx")
    dh2, dh2_b, gs['g_ple'] = _rms_bwd(h2, r3, dhn3, sp['g_ple'], dh3, name="rms_ple_bwd")

    dact = _mm_nt(dh2_b, W['w_ffn_down'], name="ffn_down_dx")
    gb['w_ffn_down'] = _mm_tn(act, dh2_b, 1, name="ffn_down_dw")
    da, dU, gs['conv_b'] = _ffn_bwd_act(a, U, dact)
    dG, gs['conv_w'] = _ffn_bwd_conv(da, G, sp['conv_w'])
    gb['w_ffn_gate'] = _mm_tn(hn2, dG, N_CHIP, name="ffn_gate_dw")
    gb['w_ffn_up'] = _mm_tn(hn2, dU, N_CHIP, name="ffn_up_dw")
    def pair_sums(names, from_sibling):
        return {n: _pair_add(_grad_pieces(n, gb[n][0]), r) for n, r in zip(names, from_sibling)}

    def wire(names):
        return [_grad_pieces(n, gb[n][1]) for n in names]

    dhn2, got = _mm_nt(dG, W['w_ffn_gate'], name="ffn_gate_dx", side=_pair_side(wire(PAIR_BEHIND_FFN_DX)))
    pair = pair_sums(PAIR_BEHIND_FFN_DX, got)
    dhn2 = _mm_nt(dU, W['w_ffn_up'], name="ffn_up_dx", res=dhn2)
    dh1, dh1_b, gs['g_ffn'] = _rms_bwd(h1, r2, dhn2, sp['g_ffn'], dh2, name="rms_ffn_bwd")

    dcat = _mm_nt(dh1_b, W['w_o'], name="out_proj_dx")
    gb['w_o'] = _mm_tn(cat, dh1_b, 1, name="out_proj_dw")
    pair.update(pair_sums(PAIR_AFTER_OUT_PROJ, _pair_exchange(wire(PAIR_AFTER_OUT_PROJ))))
    from_chips = {}
    dro_h = _head_major(dcat[:, :RW], H, RD)

    def gate_bwd_fn(ov, gv, dv):
        oc = ov - _mean(ov)
        rs = lax.rsqrt(_mean(oc * oc) + EPS)
        ron = oc * rs
        sg = _sigmoid(gv)
        dron = dv * gv * sg
        drg = dv * ron * sg * (1.0 + gv * (1.0 - sg))
        do = rs * (dron - _mean(dron) - ron * _mean(dron * ron))
        return [do, drg]
    do_h, drg_h = _rowwise(gate_bwd_fn, [(o_h, 'row'), (rg_h, 'row'), (dro_h, 'row')],
                           [(RD, BF16), (RD, F32)], name="ret_gate_bwd")
    gq_r, gk_r, gv_r = _ret_bwd(q_r, k_r, v_r, do_h.reshape(H, S, RD), states, dec, q_dec, k_dec, s_dec)
    drq_h = _rope(gq_r.reshape(H * S, RD), rc, -rsa, -rsb, RD // 2, 1.0, F32, name="ret_rope_q_bwd")
    drk_h = _rope(gk_r.reshape(H * S, RD), rc, -rsa, -rsb, RD // 2, k_scale, F32, name="ret_rope_k_bwd")
    drq, drk, drv, drg = (_seq_major(t, H, RD) for t in (drq_h, drk_h, gv_r.reshape(H * S, RD), drg_h))

    dmo_h = _head_major(dcat[:, RW:], MH, MV)
    dlt = _rowwise(lambda ov, dv: [jnp.sum(ov * dv, axis=1, keepdims=True)],
                   [(mo_h.reshape(MH * S, MV), 'row'), (dmo_h, 'row')], [(1, F32)], name="mla_delta")[0]
    dmo_b = dmo_h.astype(BF16).reshape(MH, S, MV)
    dlt = dlt.reshape(MH, S, 1)
    (gq_m,), got = _flash_dq(q_m, k_m, v_m, dmo_b, lse, dlt,
                             side=_exchange_side([pair[n][1] for n in REDUCE_BEHIND_DQ]))
    from_chips.update(zip(REDUCE_BEHIND_DQ, got))
    (gk_m, gv_m), got = _flash_dkv(q_m, k_m, v_m, dmo_b, lse, dlt,
                                   side=_exchange_side([pair[n][1] for n in REDUCE_BEHIND_DKV]))
    from_chips.update(zip(REDUCE_BEHIND_DKV, got))
    dq_raw = _rope(gq_m.reshape(MH * S, MLA_QK_PAD), mc, -msa, -msb, MLA_ROPE // 2, a_scale, F32,
                   name="mla_rope_q_bwd")
    dk_raw = _rope(gk_m.reshape(MH * S, MLA_QK_PAD), mc, -msa, -msb, MLA_ROPE // 2, 1.0, F32, name="mla_rope_k_bwd")
    dQ = dq_raw.reshape(MH, S, MLA_QK_PAD)[:, :, :cfg.QK].transpose(1, 0, 2).reshape(S, MH * cfg.QK).astype(BF16)
    dk3 = dk_raw.reshape(MH, S, MLA_QK_PAD)
    dKV = jnp.concatenate([dk3[:, :, :MLA_NOPE], gv_m], 2).transpose(1, 0, 2).reshape(S, MH * cfg.KVH).astype(BF16)
    dkr_heads = [(dk3[h, :, MLA_NOPE:cfg.QK], 'row') for h in range(MH)]
    dkr = _rowwise(lambda *v: [functools.reduce(lambda s, t: s + t, v)], dkr_heads, [(MLA_ROPE, F32)],
                   name="mla_rope_k_heads")[0]

    gb['w_uq'] = _mm_tn(cqn, dQ, N_CHIP, name="q_up_dw")
    dcqn = _mm_nt(dQ, W['w_uq'], name="q_up_dx")
    dcq, _, gs['g_q_lora'] = _rms_bwd(cq, rcq, dcqn, sp['g_q_lora'], None, name="rms_q_bwd")
    gb['w_ukv'] = _mm_tn(ckvn, dKV, N_CHIP, name="kv_up_dw")
    dckvn = _mm_nt(dKV, W['w_ukv'], name="kv_up_dx")
    dckv, _, gs['g_kv_lora'] = _rms_bwd(ckv, rckv, dckvn, sp['g_kv_lora'], None, name="rms_kv_bwd")

    dproj = jnp.concatenate([drq, drk, drv, drg, dcq, dckv, dkr], axis=1).astype(BF16)
    dproj = jnp.pad(dproj.reshape(S, N_CHIP, cfg.INs), ((0, 0), (0, 0), (0, cfg.INp - cfg.INs)))
    dproj = dproj.reshape(S, N_CHIP * cfg.INp)
    gb['w_in'] = _mm_tn(hn1, dproj, N_CHIP, name="in_proj_dw")
    pair.update(pair_sums(REDUCE_LAST, _pair_exchange(wire(REDUCE_LAST))))
    dhn1, got = _mm_nt(dproj, W['w_in'], name="in_proj_dx", side=_exchange_side([pair[n][1] for n in REDUCE_LAST]))
    from_chips.update(zip(REDUCE_LAST, got))
    dx, _, gs['g_attn'] = _rms_bwd(x, r1, dhn1, sp['g_attn'], dh1, name="rms_attn_bwd")
    return loss, dx, {n: (pair[n][0], from_chips[n]) for n in BIG}, gs


def _padded_shape(name, shape):
    K, n = shape
    if name in COL_SHARDED:
        return K, _round_up(n, LANE)
    return _round_up(K, LANE), n


def _pad_shard(name, w):
    K, n = _padded_shape(name, w.shape)
    return jnp.pad(w.astype(BF16), ((0, K - w.shape[0]), (0, n - w.shape[1])))


def _as_operand(name, gathered):
    if name in COL_SHARDED:
        return gathered
    return gathered.reshape(1, gathered.shape[0] * gathered.shape[1], gathered.shape[2])


def _grad_pieces(name, g):
    if name in COL_SHARDED:
        return g
    return g.reshape(N_CHIP, g.shape[1] // N_CHIP, g.shape[2])


def _channels_padded(v, cfg):
    r = v.shape[0]
    return jnp.pad(v.reshape(r, N_CHIP, cfg.Fs), ((0, 0), (0, 0), (0, cfg.Fp - cfg.Fs))).reshape(r, cfg.F4)


def _channels_unpadded(v, cfg):
    r = v.shape[0]
    return v.reshape(r, N_CHIP, cfg.Fp)[:, :, :cfg.Fs].reshape(r, cfg.F)


def _mesh_pos():
    return lax.axis_index("x"), lax.axis_index("y"), lax.axis_index("c")


def _other_chips(x, y):
    return [(1 - x, y), (x, 1 - y), (1 - x, 1 - y)]


def _chip_id(cx, cy):
    return 2 * cx + cy


def _half(ref_rows, core):
    half = ref_rows // 2
    return pl.ds(core * half, half)


def _gather_side(shards):
    n = len(shards)

    def copies(srcs, outs, sems):
        ici_send, ici_recv, d2d_send, d2d_recv, own_send, own_recv = sems
        x, y, c = _mesh_pos()
        mine = _chip_id(x, y)
        others = _other_chips(x, y)

        def over_ici(w, j, chip):
            rows = _half(srcs[w].shape[0], c)
            return pltpu.make_async_remote_copy(
                src_ref=srcs[w].at[rows], dst_ref=outs[w].at[chip, rows],
                send_sem=ici_send.at[w, j], recv_sem=ici_recv.at[w, j],
                device_id=(*others[j], c), device_id_type=MESH)

        def over_d2d(w, j, core):
            rows = _half(srcs[w].shape[0], core)
            slab = outs[w].at[_chip_id(*others[j]), rows]
            return pltpu.make_async_remote_copy(
                src_ref=slab, dst_ref=slab, send_sem=d2d_send.at[w, j], recv_sem=d2d_recv.at[w, j],
                device_id=(x, y, 1 - c), device_id_type=MESH)

        def own(w):
            return pltpu.make_async_remote_copy(
                src_ref=srcs[w], dst_ref=outs[w].at[mine], send_sem=own_send.at[w], recv_sem=own_recv.at[w],
                device_id=(x, y, 1 - c), device_id_type=MESH)

        return c, mine, others, over_ici, over_d2d, own

    def start(srcs, outs, sems):
        c, mine, others, over_ici, over_d2d, own = copies(srcs, outs, sems)
        for w in range(n):
            for j in range(3):
                over_ici(w, j, mine).start()
        for w in range(n):
            own(w).start()

    def finish(srcs, outs, sems):
        c, mine, others, over_ici, over_d2d, own = copies(srcs, outs, sems)
        for w in range(n):
            for j in range(3):
                over_ici(w, j, _chip_id(*others[j])).wait_recv()
                over_d2d(w, j, c).start()
        for w in range(n):
            for j in range(3):
                over_d2d(w, j, 1 - c).wait_recv()
        for w in range(n):
            own(w).wait()
            for j in range(3):
                over_ici(w, j, mine).wait_send()
                over_d2d(w, j, c).wait_send()

    out_shape = [jax.ShapeDtypeStruct((N_CHIP,) + s.shape, s.dtype) for s in shards]
    sems = [pltpu.SemaphoreType.DMA((n, 3))] * 4 + [pltpu.SemaphoreType.DMA((n,))] * 2
    return _Side(shards, out_shape, sems, start, finish)


def _gather_weights(shards):
    return _run_side(_gather_side(shards), "gather_weights")


def _pair_side(grads):
    n = len(grads)

    def copies(srcs, outs, sems):
        send, recv = sems
        x, y, c = _mesh_pos()
        return [pltpu.make_async_remote_copy(
            src_ref=srcs[w].at[:, _half(srcs[w].shape[1], 1 - c), :], dst_ref=outs[w],
            send_sem=send.at[w], recv_sem=recv.at[w],
            device_id=(x, y, 1 - c), device_id_type=MESH) for w in range(n)]

    def start(srcs, outs, sems):
        for cp in copies(srcs, outs, sems):
            cp.start()

    def finish(srcs, outs, sems):
        for cp in copies(srcs, outs, sems):
            cp.wait()

    out_shape = [jax.ShapeDtypeStruct((g.shape[0], g.shape[1] // 2, g.shape[2]), g.dtype) for g in grads]
    return _Side(grads, out_shape, [pltpu.SemaphoreType.DMA((n,))] * 2, start, finish)


def _pair_exchange(grads):
    return _run_side(_pair_side(grads), "grad_pair_exchange")


def _exchange_side(parts):
    n = len(parts)

    def copies(srcs, outs, sems):
        send, recv = sems
        x, y, c = _mesh_pos()
        others = _other_chips(x, y)
        return [pltpu.make_async_remote_copy(
            src_ref=srcs[w].at[_chip_id(*others[j])], dst_ref=outs[w].at[j],
            send_sem=send.at[w, j], recv_sem=recv.at[w, j],
            device_id=(*others[j], c), device_id_type=MESH) for w in range(n) for j in range(3)]

    def start(srcs, outs, sems):
        for cp in copies(srcs, outs, sems):
            cp.start()

    def finish(srcs, outs, sems):
        for cp in copies(srcs, outs, sems):
            cp.wait()

    out_shape = [jax.ShapeDtypeStruct((3,) + p.shape[1:], p.dtype) for p in parts]
    return _Side(parts, out_shape, [pltpu.SemaphoreType.DMA((n, 3))] * 2, start, finish)


def _sibling_share(shards):
    n = len(shards)

    def body(*refs):
        outs = refs[n:2 * n]
        send, recv = refs[2 * n:]
        x, y, c = _mesh_pos()

        def half_of(w, core):
            rows = outs[w].at[_half(outs[w].shape[0], core)]
            return pltpu.make_async_remote_copy(
                src_ref=rows, dst_ref=rows, send_sem=send.at[w], recv_sem=recv.at[w],
                device_id=(x, y, 1 - c), device_id_type=MESH)

        for w in range(n):
            half_of(w, c).start()
        for w in range(n):
            half_of(w, 1 - c).wait_recv()
        for w in range(n):
            half_of(w, c).wait_send()

    return pl.pallas_call(
        body, name="grad_sibling_share", in_specs=[HBM] * n, out_specs=[HBM] * n,
        out_shape=[jax.ShapeDtypeStruct(s.shape, s.dtype) for s in shards],
        scratch_shapes=[pltpu.SemaphoreType.DMA((n,))] * 2, input_output_aliases={w: w for w in range(n)},
    )(*shards)


N_DEV = 8


def _gather_small(v):
    r, width = v.shape

    def body(v_ref, out_ref, send_sems, recv_sems, local_sem):
        x, y, c = _mesh_pos()
        me, sibling = (x, y, c), (x, y, 1 - c)
        chips = _other_chips(x, y)

        def rows(px, py, pc):
            return out_ref.at[pl.ds((4 * px + 2 * py + pc) * r, r), :]

        def copy(k, block, to, src=None):
            return pltpu.make_async_remote_copy(
                src_ref=rows(*block) if src is None else src, dst_ref=rows(*block),
                send_sem=send_sems.at[k], recv_sem=recv_sems.at[k], device_id=to, device_id_type=MESH)

        mine = pltpu.make_async_copy(v_ref, rows(*me), local_sem)
        mine.start()
        first = [copy(0, me, sibling, src=v_ref)]
        first += [copy(1 + j, me, (*chip, c), src=v_ref) for j, chip in enumerate(chips)]
        for cp in first:
            cp.start()
        passed = [copy(4 + j, (*chip, c), sibling) for j, chip in enumerate(chips)]
        for j, chip in enumerate(chips):
            copy(1 + j, (*chip, c), me).wait_recv()
            passed[j].start()
        copy(0, sibling, me).wait_recv()
        for j, chip in enumerate(chips):
            copy(4 + j, (*chip, 1 - c), me).wait_recv()
        for cp in first + passed:
            cp.wait_send()
        mine.wait()

    vmem = pl.BlockSpec(memory_space=pltpu.VMEM)
    return pl.pallas_call(
        body, name="gather_small", out_shape=jax.ShapeDtypeStruct((N_DEV * r, width), v.dtype),
        in_specs=[vmem], out_specs=vmem,
        scratch_shapes=[pltpu.SemaphoreType.DMA((7,)), pltpu.SemaphoreType.DMA((7,)), pltpu.SemaphoreType.DMA],
    )(v)


def _pack(arrays):
    flat = jnp.concatenate([a.reshape(-1) for a in arrays])
    size = _round_up(flat.shape[0], 8 * LANE)
    return jnp.pad(flat, (0, size - flat.shape[0])).reshape(size // LANE, LANE)


def _unpack(packed, shapes):
    flat = packed.reshape(-1)
    out, at = [], 0
    for s in shapes:
        size = math.prod(s)
        out.append(flat[at:at + size].reshape(s))
        at += size
    return out


def _sum_devices(gathered):
    r = gathered.shape[0] // N_DEV
    blocks = [(gathered[d * r:(d + 1) * r], 'row') for d in range(N_DEV)]
    return _rowwise(lambda *v: [functools.reduce(lambda s, t: s + t, v)], blocks, [(LANE, F32)],
                    name="small_grad_sum")[0]


def _reduce_tiles(half, n):
    return _tile(half, 256, 8)


def _pair_add(g32, r1):
    G, K, n = g32.shape
    half = K // 2
    tr = _reduce_tiles(half, n)
    nrt = half // tr

    def body(g_ref, r_ref, s32_ref, sb_ref):
        s = g_ref[...] + r_ref[...].astype(F32)
        s32_ref[...] = s
        sb_ref[...] = s.astype(BF16)

    blk = pl.BlockSpec((None, tr, n), lambda k, i: (k, i, 0))
    return pl.pallas_call(
        body, name="grad_pair_add", grid=(G, nrt),
        in_specs=[pl.BlockSpec((None, tr, n), lambda k, i: (k, lax.axis_index("c") * nrt + i, 0)), blk],
        out_specs=[blk, blk],
        out_shape=[jax.ShapeDtypeStruct((G, half, n), F32), jax.ShapeDtypeStruct((G, half, n), BF16)],
        compiler_params=_params(("parallel", "parallel")),
    )(g32, r1)


def _chip_add(s32, r2):
    _, half, n = s32.shape
    tr = _reduce_tiles(half, n)
    nrt = half // tr

    def body(s_ref, a_ref, b_ref, c_ref, o_ref):
        o_ref[...] = ((s_ref[...] + a_ref[...].astype(F32)) + b_ref[...].astype(F32)) + c_ref[...].astype(F32)

    def piece(j):
        return pl.BlockSpec((None, tr, n), lambda i, j=j: (j, i, 0))

    def mine(i):
        return _chip_id(lax.axis_index("x"), lax.axis_index("y")), i, 0

    return pl.pallas_call(
        body, name="grad_chip_add", grid=(nrt,),
        in_specs=[pl.BlockSpec((None, tr, n), mine), piece(0), piece(1), piece(2)],
        out_specs=pl.BlockSpec((tr, n), lambda i: (lax.axis_index("c") * nrt + i, 0)),
        out_shape=jax.ShapeDtypeStruct((2 * half, n), F32),
        compiler_params=_params(("parallel",)),
    )(s32, r2, r2, r2)


def _adamw_math(wv, gv, mv, vv):
    m2 = ADAM_B1 * mv + (1.0 - ADAM_B1) * gv
    v2 = ADAM_B2 * vv + (1.0 - ADAM_B2) * (gv * gv)
    m_hat = m2 / (1.0 - ADAM_B1 ** ADAM_STEP)
    v_hat = v2 / (1.0 - ADAM_B2 ** ADAM_STEP)
    delta = -ADAM_LR * (m_hat / (jnp.sqrt(v_hat) + ADAM_EPS) + ADAM_WD * wv)
    return [delta, m2, v2]


def _adamw(w, g, m, v, *, name):
    width = w.shape[1]
    return _rowwise(_adamw_math, [(w, 'row'), (g, 'row'), (m, 'row'), (v, 'row')], [(width, F32)] * 3, name=name)


def _adamw_sharded(w, g_padded, m, v, *, name):
    _, K, n = w.shape
    n_pad = g_padded.shape[1]
    ts = 8
    while ts * 2 <= 256 and ts * 2 * 8 * n_pad * 4 <= ROWWISE_BLOCK_BYTES and K % (ts * 2) == 0:
        ts *= 2

    def body(w_ref, g_ref, m_ref, v_ref, go_ref, d_ref, mo_ref, vo_ref):
        gv = g_ref[:, :n] if n != n_pad else g_ref[...]
        go_ref[...] = gv
        d_ref[...], mo_ref[...], vo_ref[...] = _adamw_math(w_ref[...], gv, m_ref[...], v_ref[...])

    blk = pl.BlockSpec((None, ts, n), lambda i: (0, i, 0))
    out = jax.ShapeDtypeStruct((1, K, n), F32)
    return pl.pallas_call(
        body, name=name, grid=(K // ts,),
        in_specs=[blk, pl.BlockSpec((ts, n_pad), lambda i: (i, 0)), blk, blk],
        out_specs=[blk] * 4, out_shape=[out] * 4,
        compiler_params=_params(("parallel",)),
    )(w, g_padded, m, v)


def kernel(x, p, w_in, g_attn, g_q_lora, g_kv_lora, w_uq, w_ukv, w_o, g_ffn, w_ffn_gate, w_ffn_up, conv_w, conv_b, w_ffn_down, g_ple, w_ple_gate, w_ple_proj, g_final, loss_target, m_w_in, m_g_attn, m_g_q_lora, m_g_kv_lora, m_w_uq, m_w_ukv, m_w_o, m_g_ffn, m_w_ffn_gate, m_w_ffn_up, m_conv_w, m_conv_b, m_w_ffn_down, m_g_ple, m_w_ple_gate, m_w_ple_proj, m_g_final, v_w_in, v_g_attn, v_g_q_lora, v_g_kv_lora, v_w_uq, v_w_ukv, v_w_o, v_g_ffn, v_w_ffn_gate, v_w_ffn_up, v_conv_w, v_conv_b, v_w_ffn_down, v_g_ple, v_w_ple_gate, v_w_ple_proj, v_g_final):
    weights = dict(w_in=w_in, g_attn=g_attn, g_q_lora=g_q_lora, g_kv_lora=g_kv_lora, w_uq=w_uq, w_ukv=w_ukv, w_o=w_o,
                   g_ffn=g_ffn, w_ffn_gate=w_ffn_gate, w_ffn_up=w_ffn_up, conv_w=conv_w, conv_b=conv_b,
                   w_ffn_down=w_ffn_down, g_ple=g_ple, w_ple_gate=w_ple_gate, w_ple_proj=w_ple_proj, g_final=g_final)
    mom1 = dict(w_in=m_w_in, g_attn=m_g_attn, g_q_lora=m_g_q_lora, g_kv_lora=m_g_kv_lora, w_uq=m_w_uq, w_ukv=m_w_ukv,
                w_o=m_w_o, g_ffn=m_g_ffn, w_ffn_gate=m_w_ffn_gate, w_ffn_up=m_w_ffn_up, conv_w=m_conv_w,
                conv_b=m_conv_b, w_ffn_down=m_w_ffn_down, g_ple=m_g_ple, w_ple_gate=m_w_ple_gate,
                w_ple_proj=m_w_ple_proj, g_final=m_g_final)
    mom2 = dict(w_in=v_w_in, g_attn=v_g_attn, g_q_lora=v_g_q_lora, g_kv_lora=v_g_kv_lora, w_uq=v_w_uq, w_ukv=v_w_ukv,
                w_o=v_w_o, g_ffn=v_g_ffn, w_ffn_gate=v_w_ffn_gate, w_ffn_up=v_w_ffn_up, conv_w=v_conv_w,
                conv_b=v_conv_b, w_ffn_down=v_w_ffn_down, g_ple=v_g_ple, w_ple_gate=v_w_ple_gate,
                w_ple_proj=v_w_ple_proj, g_final=v_g_final)
    _, S, D = x.shape
    cfg = _Cfg(S, D)
    cx, cy, _ = _mesh_pos()

    shards = {name: _pad_shard(name, weights[name][0]) for name in BIG}
    gathered = _gather_weights([shards[name] for name in GATHER_FIRST])
    W = {name: _as_operand(name, g) for name, g in zip(GATHER_FIRST, gathered)}
    cw_all = _gather_small(_pack([jnp.pad(conv_w[0], ((0, 0), (0, cfg.Fp - cfg.Fs)))]))
    r_cw = cw_all.shape[0] // N_DEV
    cw_chips = [_unpack(cw_all[2 * k * r_cw:(2 * k + 1) * r_cw], [(CONV_WIDTH, cfg.Fp)])[0] for k in range(N_CHIP)]
    sp = dict(g_attn=g_attn, g_q_lora=g_q_lora, g_kv_lora=g_kv_lora, g_ffn=g_ffn, g_ple=g_ple,
              g_final=g_final.reshape(1, D), conv_w=jnp.concatenate(cw_chips, axis=1),
              conv_b=_channels_padded(conv_b, cfg))

    loss, dx, parts, gs = _local_step(cfg, x[0], p[0, 0], loss_target[0], W, shards, sp)
    loss = lax.psum(loss, ("x", "y", "c"))

    halves = [_chip_add(*parts[name]) for name in BIG]
    whole = dict(zip(BIG, _sibling_share(halves)))
    grads = {}

    small_names = ['g_attn', 'g_q_lora', 'g_kv_lora', 'g_ffn', 'g_ple', 'g_final', 'conv_b', 'conv_w']
    small_sum = _sum_devices(_gather_small(_pack([gs[name] for name in small_names])))
    for name, g in zip(small_names, _unpack(small_sum, [gs[name].shape for name in small_names])):
        grads[name] = g
    grads['g_final'] = grads['g_final'].reshape(D)
    grads['conv_b'] = _channels_unpadded(grads['conv_b'], cfg)
    mine = _chip_id(cx, cy)
    grads['conv_w'] = lax.dynamic_slice_in_dim(grads['conv_w'], mine * cfg.Fp, cfg.Fp, axis=1)[:, :cfg.Fs]
    grads['conv_w'] = grads['conv_w'].reshape(1, CONV_WIDTH, cfg.Fs)

    delta, new_m, new_v = {}, {}, {}
    for name in BIG:
        operands = (weights[name], whole[name], mom1[name], mom2[name])
        flip = weights[name].shape[2] % LANE != 0
        if flip:
            operands = tuple(a.T if a.ndim == 2 else a.transpose(0, 2, 1) for a in operands)
        outs = _adamw_sharded(*operands, name="adamw_" + name)
        if flip:
            outs = [o.transpose(0, 2, 1) for o in outs]
        grads[name], delta[name], new_m[name], new_v[name] = outs
    for name in WEIGHTS:
        if name in BIG:
            continue
        shape = weights[name].shape
        flat = (shape[-2], shape[-1]) if len(shape) == 3 else (1, shape[-1])
        d, m2, v2 = _adamw(weights[name].reshape(flat), grads[name].reshape(flat), mom1[name].reshape(flat),
                           mom2[name].reshape(flat), name="adamw_" + name)
        delta[name], new_m[name], new_v[name] = d.reshape(shape), m2.reshape(shape), v2.reshape(shape)

    return (loss, dx.reshape(1, S, D), *[grads[n] for n in WEIGHTS], *[delta[n] for n in WEIGHTS],
            *[new_m[n] for n in WEIGHTS], *[new_v[n] for n in WEIGHTS])
```
